```python
import math
import jax, jax.numpy as jnp
from jax import lax
import numpy as np

D_MODEL = 1024
BATCH = 8
SEQ = 2048
DEPTH = 4

HEAD_DIM = 64
ROPE_THETA = 10000.0
LN_EPS = 1e-5
A_HEADS = 8
IDX_HEADS = 8
IDX_DIM = 32
DSA_TOPK = 256
DSA_Q_BLOCK = 128
B_HEADS = 8
MOBA_BLOCK = 256
MOBA_TOPK = 3
MOBA_Q_CHUNK = 32
C_HEADS = 16
DILATED_CFG = ((128, 1), (512, 4), (2048, 16))
BAND_BLOCK = 128
D_FF = 2816
N_EXPERTS = 8
TOP_K = 2
D_FF_EXPERT = 3584
MOE_BLOCK = 256
DEEPNORM_ALPHA = (2 * DEPTH) ** 0.25
DEEPNORM_BETA = (8 * DEPTH) ** -0.25
N_EVEN = (DEPTH + 1) // 2
N_ODD = DEPTH // 2
EVEN_SPLITS = (A_HEADS * HEAD_DIM, HEAD_DIM, HEAD_DIM, IDX_HEADS * IDX_DIM, IDX_DIM, IDX_HEADS,
               B_HEADS * HEAD_DIM, B_HEADS * HEAD_DIM, B_HEADS * HEAD_DIM)
EVEN_IN = sum(EVEN_SPLITS)
EVEN_MIX = (A_HEADS + B_HEADS) * HEAD_DIM
ODD_MIX = C_HEADS * HEAD_DIM
ODD_IN = 3 * ODD_MIX

kernel_name = 'hybrid_dsa_moba_dilated_deepnorm_moe'

F32 = jnp.float32


def split_cols(h, sizes):
    idx = [int(v) for v in np.cumsum(sizes)[:-1]]
    return jnp.split(h, idx, axis=-1)


def layer_norm(x, g, b):
    xf = x.astype(F32)
    mu = xf.mean(-1, keepdims=True)
    var = jnp.mean(jnp.square(xf - mu), -1, keepdims=True)
    return ((xf - mu) * lax.rsqrt(var + LN_EPS) * g.astype(F32) + b.astype(F32)).astype(x.dtype)


def rope_tables(seq, dim):
    inv = ROPE_THETA ** (-jnp.arange(0, dim, 2, dtype=F32) / dim)
    ang = jnp.arange(seq, dtype=F32)[:, None] * inv[None, :]
    return jnp.cos(ang), jnp.sin(ang)


def apply_rope(x, cos, sin):
    half = x.shape[-1] // 2
    xf = x.astype(F32)
    x1, x2 = xf[..., :half], xf[..., half:]
    c, s = cos[None, :, None, :], sin[None, :, None, :]
    return jnp.concatenate([x1 * c - x2 * s, x1 * s + x2 * c], -1).astype(x.dtype)


def dsa_attention(q, k, v, iq, ik, iw):
    B, S, H, Dh = q.shape
    topk = min(DSA_TOPK, S // 4)
    qb = min(DSA_Q_BLOCK, S)
    nq = S // qb

    def chunk(t):
        return jnp.moveaxis(t.reshape((B, nq, qb) + t.shape[2:]), 1, 0)

    ikf = ik.astype(F32)
    key_pos = jnp.arange(S)
    take = jax.vmap(lambda rows, idx: rows[idx])

    def block(args):
        qc, iqc, iwc, start = args
        qpos = start + jnp.arange(qb)
        rel = jax.nn.relu(jnp.einsum('bqhi,bsi->bqhs', iqc.astype(F32), ikf) * IDX_DIM ** -0.5)
        score = jnp.einsum('bqhs,bqh->bqs', rel, iwc.astype(F32) * IDX_HEADS ** -0.5)
        causal = key_pos[None, :] <= qpos[:, None]
        score = jnp.where(causal[None], score, -jnp.inf)
        _, sel = lax.top_k(score, topk)
        valid = sel <= qpos[None, :, None]
        kg = take(k, sel)
        vg = take(v, sel)
        s = jnp.einsum('bqhd,bqkd->bqhk', qc, kg, preferred_element_type=F32) * Dh ** -0.5
        s = jnp.where(valid[:, :, None, :], s, -jnp.inf)
        p = jax.nn.softmax(s, axis=-1).astype(v.dtype)
        return jnp.einsum('bqhk,bqkd->bqhd', p, vg)

    starts = jnp.arange(nq, dtype=jnp.int32) * qb
    o = lax.map(block, (chunk(q), chunk(iq), chunk(iw), starts))
    return jnp.moveaxis(o, 0, 1).reshape(B, S, H, Dh)


def moba_attention(q, k, v):
    B, S, H, Dh = q.shape
    bs = MOBA_BLOCK
    nb = -(-S // bs)
    Sp = nb * bs
    scale = Dh ** -0.5
    pad = ((0, 0), (0, Sp - S), (0, 0), (0, 0))
    qb = jnp.pad(q, pad).reshape(B, nb, bs, H, Dh)
    kb = jnp.pad(k, pad).reshape(B, nb, bs, H, Dh)
    vb = jnp.pad(v, pad).reshape(B, nb, bs, H, Dh)
    s_own = jnp.einsum('bnqhd,bnkhd->bnhqk', qb, kb, preferred_element_type=F32) * scale
    tri = jnp.tril(jnp.ones((bs, bs), dtype=bool))
    s_own = jnp.where(tri, s_own, -jnp.inf)
    lse_own = jax.nn.logsumexp(s_own, axis=-1)
    p_own = jnp.exp(s_own - lse_own[..., None]).astype(v.dtype)
    o_own = jnp.einsum('bnhqk,bnkhd->bnqhd', p_own, vb).reshape(B, Sp, H, Dh)[:, :S]
    lse_own = jnp.moveaxis(lse_own, 2, 3).reshape(B, Sp, H)[:, :S]
    kt = min(MOBA_TOPK, nb - 1)
    if kt == 0:
        return o_own
    kmean = kb.astype(F32).mean(axis=2)
    gate = jnp.einsum('bshd,bnhd->bshn', q.astype(F32), kmean)
    qblk = jnp.arange(S) // bs
    past = jnp.arange(nb)[None, :] < qblk[:, None]
    gate = jnp.where(past[None, :, None, :], gate, -jnp.inf)
    _, sel = lax.top_k(gate, kt)
    valid = sel < qblk[None, :, None, None]
    kbh = jnp.transpose(kb, (0, 3, 1, 2, 4))
    vbh = jnp.transpose(vb, (0, 3, 1, 2, 4))
    take = jax.vmap(jax.vmap(lambda blocks, idx: blocks[idx]))
    qc_size = min(MOBA_Q_CHUNK, S)
    nc = S // qc_size

    def chunk(t):
        return jnp.moveaxis(t.reshape((B, nc, qc_size) + t.shape[2:]), 1, 0)

    def block(args):
        qc, selc, validc, oc, lc = args
        qh = jnp.swapaxes(qc, 1, 2)
        selh = jnp.swapaxes(selc, 1, 2)
        validh = jnp.swapaxes(validc, 1, 2)
        kg = take(kbh, selh)
        vg = take(vbh, selh)
        s = jnp.einsum('bhcd,bhcjkd->bhcjk', qh, kg, preferred_element_type=F32) * scale
        s = jnp.where(validh[..., None], s, -jnp.inf)
        m = jnp.where(jnp.any(validh, -1), jnp.max(s, axis=(-2, -1)), 0.0)
        e = jnp.exp(s - m[..., None, None])
        den = e.sum(axis=(-2, -1))
        num = jnp.einsum('bhcjk,bhcjkd->bhcd', e.astype(v.dtype), vg, preferred_element_type=F32)
        lo = jnp.swapaxes(lc, 1, 2)
        oo = jnp.swapaxes(oc, 1, 2).astype(F32)
        mx = jnp.maximum(lo, m)
        a = jnp.exp(lo - mx)
        c = jnp.exp(m - mx)
        out = (a[..., None] * oo + c[..., None] * num) / (a + c * den)[..., None]
        return jnp.swapaxes(out, 1, 2).astype(v.dtype)

    o = lax.map(block, (chunk(q), chunk(sel), chunk(valid), chunk(o_own), chunk(lse_own)))
    return jnp.moveaxis(o, 0, 1).reshape(B, S, H, Dh)


def dilated_branch(q, k, v, window, dil):
    B, S, H, Dh = q.shape
    L = S // dil
    wsub = window // dil
    blk = BAND_BLOCK
    nblk = -(-L // blk)
    Lp = nblk * blk

    def strided(t):
        t = t.reshape(B, L, dil, H, Dh).transpose(0, 2, 1, 3, 4)
        t = jnp.pad(t, ((0, 0), (0, 0), (0, Lp - L), (0, 0), (0, 0)))
        return t.reshape(B, dil, nblk, blk, H, Dh)

    def with_prev(t):
        prev = jnp.pad(t, ((0, 0), (0, 0), (1, 0), (0, 0), (0, 0), (0, 0)))[:, :, :-1]
        return jnp.concatenate([prev, t], axis=3)

    qs = strided(q)
    kk = with_prev(strided(k))
    vv = with_prev(strided(v))
    i = jnp.arange(blk)[:, None]
    j = jnp.arange(2 * blk)[None, :]
    dist = blk + i - j
    key_u = jnp.arange(nblk)[:, None, None] * blk - blk + j[None]
    mask = (dist >= 0)[None] & (dist <= wsub)[None] & (key_u >= 0)
    s = jnp.einsum('bgnqhd,bgnkhd->bgnhqk', qs, kk, preferred_element_type=F32) * Dh ** -0.5
    s = jnp.where(mask[None, None, :, None], s, -jnp.inf)
    lse = jax.nn.logsumexp(s, axis=-1)
    p = jnp.exp(s - lse[..., None]).astype(v.dtype)
    o = jnp.einsum('bgnhqk,bgnkhd->bgnqhd', p, vv)
    o = o.reshape(B, dil, Lp, H, Dh)[:, :, :L].transpose(0, 2, 1, 3, 4).reshape(B, S, H, Dh)
    lse = jnp.moveaxis(lse, 3, 4).reshape(B, dil, Lp, H)[:, :, :L].transpose(0, 2, 1, 3).reshape(B, S, H)
    return o, lse


def dilated_mixture(q, k, v):
    outs, lses = [], []
    for window, dil in DILATED_CFG:
        o, l = dilated_branch(q, k, v, window, dil)
        outs.append(o)
        lses.append(l)
    wts = jax.nn.softmax(jnp.stack(lses, 0), axis=0)
    o = jnp.sum(wts[..., None] * jnp.stack(outs, 0).astype(F32), axis=0)
    return o.astype(v.dtype)


def even_mixer(x, w_in, w_out, cos64, sin64, cos32, sin32):
    B, S, _ = x.shape
    qa, ka, va, iq, ik, iw, qb, kb, vb = split_cols(x @ w_in, EVEN_SPLITS)
    qa = apply_rope(qa.reshape(B, S, A_HEADS, HEAD_DIM), cos64, sin64)
    ka = apply_rope(ka.reshape(B, S, 1, HEAD_DIM), cos64, sin64)[:, :, 0]
    iq = apply_rope(iq.reshape(B, S, IDX_HEADS, IDX_DIM), cos32, sin32)
    ik = apply_rope(ik.reshape(B, S, 1, IDX_DIM), cos32, sin32)[:, :, 0]
    shp = (B, S, B_HEADS, HEAD_DIM)
    qb = apply_rope(qb.reshape(shp), cos64, sin64)
    kb = apply_rope(kb.reshape(shp), cos64, sin64)
    o_a = dsa_attention(qa, ka, va, iq, ik, iw)
    o_b = moba_attention(qb, kb, vb.reshape(shp))
    o = jnp.concatenate([o_a.reshape(B, S, -1), o_b.reshape(B, S, -1)], axis=-1)
    return o @ w_out


def odd_mixer(x, w_in, w_out, cos64, sin64):
    B, S, _ = x.shape
    q, k, v = split_cols(x @ w_in, (ODD_MIX, ODD_MIX, ODD_MIX))
    shp = (B, S, C_HEADS, HEAD_DIM)
    q = apply_rope(q.reshape(shp), cos64, sin64)
    k = apply_rope(k.reshape(shp), cos64, sin64)
    o = dilated_mixture(q, k, v.reshape(shp))
    return o.reshape(B, S, ODD_MIX) @ w_out


def swiglu(x, w1, w3, w2):
    return (jax.nn.silu(x @ w1) * (x @ w3)) @ w2


def moe_swiglu(x2d, router, w1, w3, w2):
    n_tok, dm = x2d.shape
    logits = (x2d @ router).astype(F32)
    top_val, top_exp = lax.top_k(logits, TOP_K)
    gates = jax.nn.softmax(top_val, axis=-1)
    n_asg = n_tok * TOP_K
    eid = top_exp.reshape(n_asg)
    tok = jnp.repeat(jnp.arange(n_tok, dtype=jnp.int32), TOP_K)
    gate = gates.reshape(n_asg)
    order = jnp.argsort(eid)
    es, ts, gs = eid[order], tok[order], gate[order]
    counts = jnp.bincount(eid, length=N_EXPERTS)
    starts = jnp.cumsum(counts) - counts
    padded = (counts + MOE_BLOCK - 1) // MOE_BLOCK * MOE_BLOCK
    ends = jnp.cumsum(padded)
    pstarts = ends - padded
    dest = pstarts[es] + jnp.arange(n_asg) - starts[es]
    n_blocks = -(-n_asg // MOE_BLOCK) + N_EXPERTS
    xbuf = jnp.zeros((n_blocks * MOE_BLOCK, dm), x2d.dtype).at[dest].set(x2d[ts])
    block_exp = jnp.clip(jnp.searchsorted(ends, jnp.arange(n_blocks) * MOE_BLOCK, side='right'), 0, N_EXPERTS - 1)

    def expert_block(args):
        xb, e = args
        return (jax.nn.silu(xb @ w1[e]) * (xb @ w3[e])) @ w2[e]

    ybuf = lax.map(expert_block, (xbuf.reshape(n_blocks, MOE_BLOCK, dm), block_exp)).reshape(-1, dm)
    contrib = ybuf[dest] * gs[:, None].astype(x2d.dtype)
    return jnp.zeros_like(x2d).at[ts].add(contrib)


def setup_inputs(seed: int = 0) -> dict:
    key = jax.random.key(seed)
    ks = jax.random.split(key, 20)

    def nrm(k, shape, scale):
        return jax.random.normal(k, shape, F32) * scale

    beta = DEEPNORM_BETA
    hd = HEAD_DIM
    even_col = jnp.concatenate([
        jnp.ones((A_HEADS * hd + hd,), F32), jnp.full((hd,), beta, F32),
        jnp.ones((IDX_HEADS * IDX_DIM + IDX_DIM + IDX_HEADS + 2 * B_HEADS * hd,), F32),
        jnp.full((B_HEADS * hd,), beta, F32)])
    odd_col = jnp.concatenate([jnp.ones((2 * ODD_MIX,), F32), jnp.full((ODD_MIX,), beta, F32)])
    d = D_MODEL
    return {
        'x': nrm(ks[0], (BATCH, SEQ, d), 1.0),
        'even_w_in': nrm(ks[1], (N_EVEN, d, EVEN_IN), d ** -0.5) * even_col,
        'even_w_out': nrm(ks[2], (N_EVEN, EVEN_MIX, d), EVEN_MIX ** -0.5 * beta),
        'even_ln1_g': 1.0 + nrm(ks[3], (N_EVEN, d), 0.02),
        'even_ln1_b': nrm(ks[4], (N_EVEN, d), 0.02),
        'even_w1': nrm(ks[5], (N_EVEN, d, D_FF), d ** -0.5),
        'even_w3': nrm(ks[6], (N_EVEN, d, D_FF), d ** -0.5),
        'even_w2': nrm(ks[7], (N_EVEN, D_FF, d), D_FF ** -0.5 * beta),
        'even_ln2_g': 1.0 + nrm(ks[8], (N_EVEN, d), 0.02),
        'even_ln2_b': nrm(ks[9], (N_EVEN, d), 0.02),
        'odd_w_in': nrm(ks[10], (N_ODD, d, ODD_IN), d ** -0.5) * odd_col,
        'odd_w_out': nrm(ks[11], (N_ODD, ODD_MIX, d), ODD_MIX ** -0.5 * beta),
        'odd_ln1_g': 1.0 + nrm(ks[12], (N_ODD, d), 0.02),
        'odd_ln1_b': nrm(ks[13], (N_ODD, d), 0.02),
        'odd_router': nrm(ks[14], (N_ODD, d, N_EXPERTS), d ** -0.5),
        'odd_w1': nrm(ks[15], (N_ODD, N_EXPERTS, d, D_FF_EXPERT), d ** -0.5),
        'odd_w3': nrm(ks[16], (N_ODD, N_EXPERTS, d, D_FF_EXPERT), d ** -0.5),
        'odd_w2': nrm(ks[17], (N_ODD, N_EXPERTS, D_FF_EXPERT, d), D_FF_EXPERT ** -0.5 * beta),
        'odd_ln2_g': 1.0 + nrm(ks[18], (N_ODD, d), 0.02),
        'odd_ln2_b': nrm(ks[19], (N_ODD, d), 0.02),
    }


def reference(x, even_w_in, even_w_out, even_ln1_g, even_ln1_b, even_w1, even_w3, even_w2,
              even_ln2_g, even_ln2_b, odd_w_in, odd_w_out, odd_ln1_g, odd_ln1_b, odd_router,
              odd_w1, odd_w3, odd_w2, odd_ln2_g, odd_ln2_b):
    B, S, D = x.shape
    cos64, sin64 = rope_tables(S, HEAD_DIM)
    cos32, sin32 = rope_tables(S, IDX_DIM)
    a = DEEPNORM_ALPHA
    for layer in range(DEPTH):
        i = layer // 2
        if layer % 2 == 0:
            mix = even_mixer(x, even_w_in[i], even_w_out[i], cos64, sin64, cos32, sin32)
            x = layer_norm(a * x + mix, even_ln1_g[i], even_ln1_b[i])
            ffn = swiglu(x, even_w1[i], even_w3[i], even_w2[i])
            x = layer_norm(a * x + ffn, even_ln2_g[i], even_ln2_b[i])
        else:
            mix = odd_mixer(x, odd_w_in[i], odd_w_out[i], cos64, sin64)
            x = layer_norm(a * x + mix, odd_ln1_g[i], odd_ln1_b[i])
            ffn = moe_swiglu(x.reshape(B * S, D), odd_router[i], odd_w1[i], odd_w3[i], odd_w2[i]).reshape(B, S, D)
            x = layer_norm(a * x + ffn, odd_ln2_g[i], odd_ln2_b[i])
    return x
```

```python
import functools

import jax
import jax.numpy as jnp
from jax import lax
from jax.experimental import pallas as pl
from jax.experimental.pallas import tpu as pltpu

F32 = jnp.float32
BF16 = jnp.bfloat16
I32 = jnp.int32

DEPTH = 4
HEAD_DIM = 64
ROPE_THETA = 10000.0
LN_EPS = 1e-5
A_HEADS = 8
IDX_HEADS = 8
IDX_DIM = 32
DSA_TOPK = 256
DSA_Q_BLOCK = 128
B_HEADS = 8
MOBA_BLOCK = 256
MOBA_TOPK = 3
C_HEADS = 16
DILATED_CFG = ((128, 1), (512, 4), (2048, 16))
BAND_BLOCK = 128
N_EXPERTS = 8
DEEPNORM_ALPHA = (2 * DEPTH) ** 0.25

LANES = 128
VMEM_LIMIT_BYTES = 52 * 1024 * 1024

PROJ_ROWS = 512
FFN_ROWS = 512
ROUTER_ROWS = 512
MOE_GROUP = 2048
MOE_SUB = 256
MOE_FF_CHUNK = 512

NEG_INF = float("-inf")
NEG_INF_KEY = -2139095041


def _params(semantics):
    return pltpu.CompilerParams(dimension_semantics=semantics, vmem_limit_bytes=VMEM_LIMIT_BYTES)


def _dot(a, b):
    return jnp.dot(a, b, preferred_element_type=F32)


def _dot_nt(a, b):
    return lax.dot_general(a, b, (((1,), (1,)), ((), ())), preferred_element_type=F32)


def _lane_group(shape, group, idx):
    lane = lax.broadcasted_iota(I32, shape, len(shape) - 1)
    return (lane // group) == idx


def _rope_lanes(t, cosf, sinf, half):
    lane = lax.broadcasted_iota(I32, t.shape, 1)
    first = (lane % (2 * half)) < half
    swapped = jnp.where(first, pltpu.roll(t, LANES - half, 1), pltpu.roll(t, half, 1))
    return t * cosf + swapped * sinf


def _proj_kernel(x_ref, w_ref, c64_ref, s64_ref, c32_ref, s32_ref, *out_refs, sections):
    x = x_ref[...].astype(BF16)
    for o_ref, (start, width, rope) in zip(out_refs, sections):
        h = _dot(x, w_ref[:, start:start + width])
        if rope is not None:
            cosf = (c64_ref if rope == HEAD_DIM else c32_ref)[...]
            sinf = (s64_ref if rope == HEAD_DIM else s32_ref)[...]
            parts = [_rope_lanes(h[:, g * LANES:(g + 1) * LANES], cosf, sinf, rope // 2)
                     for g in range(width // LANES)]
            h = parts[0] if len(parts) == 1 else jnp.concatenate(parts, axis=1)
        o_ref[...] = h.astype(o_ref.dtype)


def _project(x2d, w, tables, sections, out_dtypes, seq):
    n, d = x2d.shape
    tm = PROJ_ROWS
    pos_blocks = seq // tm
    tab_spec = pl.BlockSpec((tm, LANES), lambda i: (i % pos_blocks, 0))
    return pl.pallas_call(
        functools.partial(_proj_kernel, sections=sections),
        grid=(n // tm,),
        in_specs=[pl.BlockSpec((tm, d), lambda i: (i, 0)),
                  pl.BlockSpec(w.shape, lambda i: (0, 0)),
                  tab_spec, tab_spec, tab_spec, tab_spec],
        out_specs=[pl.BlockSpec((tm, width), lambda i: (i, 0)) for (_, width, _) in sections],
        out_shape=[jax.ShapeDtypeStruct((n, width), dt) for (_, width, _), dt in zip(sections, out_dtypes)],
        compiler_params=_params(("parallel",)),
    )(x2d, w, *tables)


def _rope_tables(seq):
    out = []
    for dim in (HEAD_DIM, IDX_DIM):
        inv = ROPE_THETA ** (-jnp.arange(0, dim, 2, dtype=F32) / dim)
        ang = jnp.arange(seq, dtype=F32)[:, None] * inv[None, :]
        cos, sin = jnp.cos(ang), jnp.sin(ang)
        reps = LANES // dim
        out.append(jnp.tile(jnp.concatenate([cos, cos], -1), (1, reps)))
        out.append(jnp.tile(jnp.concatenate([-sin, sin], -1), (1, reps)))
    return out


def _dsa_kernel(iq_ref, iw_ref, ik_ref, q_ref, kk_ref, vv_ref, o_ref, key_ref, sel_ref, *, seq, topk):
    qb = DSA_Q_BLOCK
    i = pl.program_id(1)
    qpos = i * qb + lax.broadcasted_iota(I32, (qb, 1), 0)
    kpos = lax.broadcasted_iota(I32, (1, seq), 1)
    causal = kpos <= qpos

    ik = ik_ref[...]
    iw = iw_ref[...] * (IDX_HEADS ** -0.5)
    heads_per_group = LANES // IDX_DIM
    score = jnp.zeros((qb, seq), F32)
    for h in range(IDX_HEADS):
        grp, j = divmod(h, heads_per_group)
        iqg = iq_ref[:, grp * LANES:(grp + 1) * LANES]
        iqm = jnp.where(_lane_group((1, LANES), IDX_DIM, j), iqg, jnp.zeros_like(iqg))
        rel = jnp.maximum(_dot_nt(iqm, ik) * (IDX_DIM ** -0.5), 0.0)
        score = score + rel * iw[:, h:h + 1]
    score = jnp.where(score == 0.0, 0.0, score)
    score = jnp.where(causal, score, NEG_INF)
    bits = pltpu.bitcast(score, I32)
    key_ref[...] = bits ^ ((bits >> 31) & 0x7FFFFFFF)

    def count_ge(t):
        return jnp.sum((key_ref[...] >= t).astype(I32), axis=1, keepdims=True)

    int_min = jnp.full((qb, 1), -2 ** 31, I32)
    thr0 = jnp.where(count_ge(jnp.zeros((qb, 1), I32)) >= topk, 0, int_min)

    def search(it, thr):
        cand = thr | jnp.left_shift(jnp.int32(1), 30 - it)
        return jnp.where(count_ge(cand) >= topk, cand, thr)

    thr = lax.fori_loop(0, 31, search, thr0)

    key = key_ref[...]
    gt = key > thr
    eq = key == thr
    need = topk - jnp.sum(gt.astype(I32), axis=1, keepdims=True)
    n_eq = jnp.sum(eq.astype(I32), axis=1, keepdims=True)
    sel_ref[...] = jnp.where((gt | eq) & causal, 1.0, 0.0)
    tie_rows = (n_eq > need) & (thr != NEG_INF_KEY)

    @pl.when(jnp.max(tie_rows.astype(I32)) > 0)
    def _break_ties():
        upper = (lax.broadcasted_iota(I32, (LANES, LANES), 0)
                 < lax.broadcasted_iota(I32, (LANES, LANES), 1))
        upper = jnp.where(upper, 1.0, 0.0).astype(BF16)
        before = jnp.zeros((qb, 1), F32)
        needf = need.astype(F32)
        for c in range(seq // LANES):
            sl = slice(c * LANES, (c + 1) * LANES)
            keyc = key_ref[:, sl]
            eqc = keyc == thr
            eqf = jnp.where(eqc, 1.0, 0.0)
            rank = _dot(eqf.astype(BF16), upper) + before
            take = (keyc > thr) | (eqc & (rank < needf))
            sel_ref[:, sl] = jnp.where(take & causal[:, sl], 1.0, 0.0)
            before = before + jnp.sum(eqf, axis=1, keepdims=True)

    selected = sel_ref[...] > 0.5
    kk = kk_ref[...]
    vv = vv_ref[...]
    for hp in range(A_HEADS // 2):
        q2 = q_ref[:, hp * LANES:(hp + 1) * LANES]
        outs = []
        for half in range(2):
            qm = jnp.where(_lane_group((1, LANES), HEAD_DIM, half), q2, jnp.zeros_like(q2))
            s = jnp.where(selected, _dot_nt(qm, kk) * (HEAD_DIM ** -0.5), NEG_INF)
            e = jnp.exp(s - jnp.max(s, axis=1, keepdims=True))
            den = jnp.sum(e, axis=1, keepdims=True)
            outs.append(_dot(e.astype(BF16), vv) / den)
        o2 = jnp.where(_lane_group((1, LANES), HEAD_DIM, 0), outs[0], outs[1])
        o_ref[:, hp * LANES:(hp + 1) * LANES] = o2.astype(o_ref.dtype)


def _dsa_attention(iq, iw, ik4, qa, kk, vv, batch, seq):
    n = batch * seq
    qb = DSA_Q_BLOCK
    nq = seq // qb
    topk = min(DSA_TOPK, seq // 4)
    rows = lambda width: pl.BlockSpec((qb, width), lambda b, i: (b * nq + i, 0))
    whole = pl.BlockSpec((seq, LANES), lambda b, i: (b, 0))
    return pl.pallas_call(
        functools.partial(_dsa_kernel, seq=seq, topk=topk),
        grid=(batch, nq),
        in_specs=[rows(iq.shape[1]), rows(LANES), whole, rows(qa.shape[1]), whole, whole],
        out_specs=rows(qa.shape[1]),
        out_shape=jax.ShapeDtypeStruct((n, qa.shape[1]), BF16),
        scratch_shapes=[pltpu.VMEM((qb, seq), I32), pltpu.VMEM((qb, seq), F32)],
        compiler_params=_params(("parallel", "parallel")),
    )(iq, iw, ik4, qa, kk, vv)


def _moba_kernel(q_ref, k_ref, v_ref, o_ref, *, seq):
    bs = MOBA_BLOCK
    nb = seq // bs
    kt = min(MOBA_TOPK, nb - 1)
    qi = pl.program_id(2)
    q2 = q_ref[...]
    k2 = k_ref[...]
    v2 = v_ref[...]
    blk_row = lax.broadcasted_iota(I32, (nb, seq), 0)
    blk_of_key = lax.broadcasted_iota(I32, (nb, seq), 1) // bs
    member = jnp.where(blk_row == blk_of_key, 1.0, 0.0).astype(BF16)
    kmean = (_dot(member, k2) * (1.0 / bs)).astype(BF16)
    n_iota = lax.broadcasted_iota(I32, (1, nb), 1)
    past = n_iota < qi
    qrow = lax.broadcasted_iota(I32, (bs, 1), 0)
    kcol = lax.broadcasted_iota(I32, (1, seq), 1)
    own = (kcol >= qi * bs) & (kcol <= qi * bs + qrow)
    outs = []
    for half in range(2):
        qm = jnp.where(_lane_group((1, LANES), HEAD_DIM, half), q2, jnp.zeros_like(q2))
        gate = jnp.where(past, _dot_nt(qm, kmean), NEG_INF)
        rank = jnp.zeros((bs, nb), I32)
        for m in range(nb):
            gm = gate[:, m:m + 1]
            beats = (gm > gate) | ((gm == gate) & (m < n_iota))
            rank = rank + beats.astype(I32)
        chosen = jnp.where(past & (rank < kt), 1.0, 0.0).astype(BF16)
        mask = (_dot(chosen, member) > 0.5) | own
        s = jnp.where(mask, _dot_nt(qm, k2) * (HEAD_DIM ** -0.5), NEG_INF)
        e = jnp.exp(s - jnp.max(s, axis=1, keepdims=True))
        den = jnp.sum(e, axis=1, keepdims=True)
        outs.append(_dot(e.astype(BF16), v2) / den)
    o2 = jnp.where(_lane_group((1, LANES), HEAD_DIM, 0), outs[0], outs[1])
    o_ref[...] = o2.astype(o_ref.dtype)


def _moba_attention(q, k, v, batch, seq):
    n, width = q.shape
    bs = MOBA_BLOCK
    nb = seq // bs
    rows = pl.BlockSpec((bs, LANES), lambda b, hp, qi: (b * nb + qi, hp))
    whole = pl.BlockSpec((seq, LANES), lambda b, hp, qi: (b, hp))
    return pl.pallas_call(
        functools.partial(_moba_kernel, seq=seq),
        grid=(batch, width // LANES, nb),
        in_specs=[rows, whole, whole],
        out_specs=rows,
        out_shape=jax.ShapeDtypeStruct((n, width), BF16),
        compiler_params=_params(("parallel", "parallel", "parallel")),
    )(q, k, v)


def _rows(start, size, stride):
    return pl.ds(start, size) if stride == 1 else pl.ds(start, size, stride=stride)


def _dilated_kernel(q_ref, k_ref, v_ref, o_ref, m_ref, l_ref, acc_ref, *, seq):
    blk = BAND_BLOCK
    first_half = _lane_group((1, LANES), HEAD_DIM, 0)

    def unit(qstart, kstart, nk, stride, merge):
        qsl = _rows(qstart, blk, stride)
        ksl = _rows(kstart, nk, stride)
        q2 = q_ref[qsl, :].astype(BF16)
        k2 = k_ref[ksl, :].astype(BF16)
        v2 = v_ref[ksl, :].astype(BF16)
        i = lax.broadcasted_iota(I32, (blk, 1), 0)
        j = lax.broadcasted_iota(I32, (1, nk), 1)
        if nk == blk:
            mask = j <= i
        else:
            mask = (j >= i) & (j <= i + blk)
        ms, ls, accs = [], [], []
        for half in range(2):
            qm = jnp.where(_lane_group((1, LANES), HEAD_DIM, half), q2, jnp.zeros_like(q2))
            s = jnp.where(mask, _dot_nt(qm, k2) * (HEAD_DIM ** -0.5), NEG_INF)
            mx = jnp.max(s, axis=1, keepdims=True)
            e = jnp.exp(s - mx)
            ms.append(mx)
            ls.append(jnp.sum(e, axis=1, keepdims=True))
            accs.append(_dot(e.astype(BF16), v2))
        m_new = jnp.where(first_half, ms[0], ms[1])
        l_new = jnp.where(first_half, ls[0], ls[1])
        a_new = jnp.where(first_half, accs[0], accs[1])
        if merge:
            m_old = m_ref[qsl, :]
            m_tot = jnp.maximum(m_old, m_new)
            w_old = jnp.exp(m_old - m_tot)
            w_new = jnp.exp(m_new - m_tot)
            l_ref[qsl, :] = w_old * l_ref[qsl, :] + w_new * l_new
            acc_ref[qsl, :] = w_old * acc_ref[qsl, :] + w_new * a_new
            m_ref[qsl, :] = m_tot
        else:
            m_ref[qsl, :] = m_new
            l_ref[qsl, :] = l_new
            acc_ref[qsl, :] = a_new

    for bi, (window, dil) in enumerate(DILATED_CFG):
        assert window // dil == blk
        merge = bi > 0
        nblk = seq // dil // blk

        def residue_class(r, carry, dil=dil, merge=merge, nblk=nblk):
            unit(r, r, blk, dil, merge)

            def later_block(nbi, c):
                qstart = r + dil * blk * nbi
                unit(qstart, qstart - dil * blk, 2 * blk, dil, merge)
                return c

            if nblk > 1:
                lax.fori_loop(1, nblk, later_block, 0)
            return carry

        if dil <= 4:
            for r in range(dil):
                residue_class(r, 0)
        else:
            lax.fori_loop(0, dil, residue_class, 0)

    o_ref[...] = (acc_ref[...] / l_ref[...]).astype(o_ref.dtype)


def _dilated_attention(q, k, v, batch, seq):
    n, width = q.shape
    for window, dil in DILATED_CFG:
        assert seq % (dil * BAND_BLOCK) == 0
    whole = pl.BlockSpec((seq, LANES), lambda b, hp: (b, hp))
    return pl.pallas_call(
        functools.partial(_dilated_kernel, seq=seq),
        grid=(batch, width // LANES),
        in_specs=[whole, whole, whole],
        out_specs=whole,
        out_shape=jax.ShapeDtypeStruct((n, width), BF16),
        scratch_shapes=[pltpu.VMEM((seq, LANES), F32)] * 3,
        compiler_params=_params(("parallel", "parallel")),
    )(q, k, v)


def _deepnorm(x, f, g, b):
    y = DEEPNORM_ALPHA * x + f
    mu = jnp.mean(y, axis=1, keepdims=True)
    yc = y - mu
    var = jnp.mean(yc * yc, axis=1, keepdims=True)
    return yc * lax.rsqrt(var + LN_EPS) * g + b


def _outproj_ln_kernel(*refs, n_parts):
    part_refs = refs[:n_parts]
    w_ref, x_ref, g_ref, b_ref, o_ref, oh_ref = refs[n_parts:]
    mix = None
    start = 0
    for p_ref in part_refs:
        width = p_ref.shape[1]
        t = _dot(p_ref[...], w_ref[start:start + width, :])
        mix = t if mix is None else mix + t
        start += width
    y = _deepnorm(x_ref[...], mix, g_ref[...], b_ref[...])
    o_ref[...] = y
    oh_ref[...] = y.astype(BF16)


def _outproj_ln(parts, w, x2d, g, b):
    n, d = x2d.shape
    tm = PROJ_ROWS
    row = lambda width: pl.BlockSpec((tm, width), lambda i: (i, 0))
    vec = pl.BlockSpec((1, d), lambda i: (0, 0))
    return pl.pallas_call(
        functools.partial(_outproj_ln_kernel, n_parts=len(parts)),
        grid=(n // tm,),
        in_specs=[row(p.shape[1]) for p in parts] + [pl.BlockSpec(w.shape, lambda i: (0, 0)), row(d), vec, vec],
        out_specs=[row(d), row(d)],
        out_shape=[jax.ShapeDtypeStruct((n, d), F32), jax.ShapeDtypeStruct((n, d), BF16)],
        compiler_params=_params(("parallel",)),
    )(*parts, w, x2d, g.reshape(1, d), b.reshape(1, d))


def _silu(h):
    return h * (1.0 / (1.0 + jnp.exp(-h)))


def _swiglu_ln_kernel(xh_ref, x_ref, w1_ref, w3_ref, w2_ref, g_ref, b_ref, o_ref, acc_ref):
    f = pl.program_id(1)
    xh = xh_ref[...]
    hid = (_silu(_dot(xh, w1_ref[...])) * _dot(xh, w3_ref[...])).astype(BF16)
    y = _dot(hid, w2_ref[...])

    @pl.when(f == 0)
    def _():
        acc_ref[...] = y

    @pl.when(f > 0)
    def _():
        acc_ref[...] += y

    @pl.when(f == pl.num_programs(1) - 1)
    def _():
        o_ref[...] = _deepnorm(x_ref[...], acc_ref[...], g_ref[...], b_ref[...])


def _swiglu_ln(xh, x2d, w1, w3, w2, g, b):
    n, d = x2d.shape
    dff = w1.shape[1]
    tm = FFN_ROWS
    nf = 2 if dff % (2 * LANES) == 0 else 1
    fc = dff // nf
    row = lambda: pl.BlockSpec((tm, d), lambda i, f: (i, 0))
    vec = pl.BlockSpec((1, d), lambda i, f: (0, 0))
    return pl.pallas_call(
        _swiglu_ln_kernel,
        grid=(n // tm, nf),
        in_specs=[row(), row(),
                  pl.BlockSpec((d, fc), lambda i, f: (0, f)),
                  pl.BlockSpec((d, fc), lambda i, f: (0, f)),
                  pl.BlockSpec((fc, d), lambda i, f: (f, 0)),
                  vec, vec],
        out_specs=row(),
        out_shape=jax.ShapeDtypeStruct((n, d), F32),
        scratch_shapes=[pltpu.VMEM((tm, d), F32)],
        compiler_params=_params(("parallel", "arbitrary")),
    )(xh, x2d, w1, w3, w2, g.reshape(1, d), b.reshape(1, d))


def _add_ln_kernel(x_ref, f_ref, g_ref, b_ref, o_ref):
    o_ref[...] = _deepnorm(x_ref[...], f_ref[...], g_ref[...], b_ref[...])


def _add_ln(x2d, f2d, g, b):
    n, d = x2d.shape
    tm = PROJ_ROWS
    row = pl.BlockSpec((tm, d), lambda i: (i, 0))
    vec = pl.BlockSpec((1, d), lambda i: (0, 0))
    return pl.pallas_call(
        _add_ln_kernel,
        grid=(n // tm,),
        in_specs=[row, row, vec, vec],
        out_specs=row,
        out_shape=jax.ShapeDtypeStruct((n, d), F32),
        compiler_params=_params(("parallel",)),
    )(x2d, f2d, g.reshape(1, d), b.reshape(1, d))


def _router_kernel(xh_ref, rw_ref, gate_ref, lposc_ref, lpost_ref, cum_ref, carry_ref, *, tiles_per_group):
    tm = xh_ref.shape[0]
    i = pl.program_id(0)

    @pl.when(i % tiles_per_group == 0)
    def _():
        carry_ref[...] = jnp.zeros_like(carry_ref)

    lane = lax.broadcasted_iota(I32, (1, LANES), 1)
    logits = jnp.where(lane < N_EXPERTS, _dot(xh_ref[...], rw_ref[...]), NEG_INF)
    m1 = jnp.max(logits, axis=1, keepdims=True)
    i1 = jnp.min(jnp.where(logits == m1, lane, LANES), axis=1, keepdims=True)
    rest = jnp.where(lane == i1, NEG_INF, logits)
    m2 = jnp.max(rest, axis=1, keepdims=True)
    i2 = jnp.min(jnp.where(rest == m2, lane, LANES), axis=1, keepdims=True)
    e2 = jnp.exp(m2 - m1)
    g1 = 1.0 / (1.0 + e2)
    g2 = e2 / (1.0 + e2)
    sel1 = lane == i1
    sel2 = lane == i2
    routed = sel1 | sel2
    gate_ref[...] = jnp.where(sel1, g1, jnp.where(sel2, g2, 0.0))
    routedf = jnp.where(routed, 1.0, 0.0)
    earlier = (lax.broadcasted_iota(I32, (tm, tm), 1) < lax.broadcasted_iota(I32, (tm, tm), 0))
    earlier = jnp.where(earlier, 1.0, 0.0).astype(BF16)
    carry = carry_ref[...]
    lpos = jnp.where(routed, _dot(earlier, routedf.astype(BF16)) + carry, -1.0)
    lposc_ref[...] = lpos
    lpost_ref[...] = jnp.transpose(lpos)[0:N_EXPERTS, :]
    carry = carry + jnp.sum(routedf, axis=0, keepdims=True)
    carry_ref[...] = carry
    cum_ref[...] = carry.astype(I32).reshape(1, 1, LANES)


def _router(xh, rw):
    n, d = xh.shape
    tm = ROUTER_ROWS
    nt = n // tm
    row = pl.BlockSpec((tm, LANES), lambda i: (i, 0))
    return pl.pallas_call(
        functools.partial(_router_kernel, tiles_per_group=MOE_GROUP // tm),
        grid=(nt,),
        in_specs=[pl.BlockSpec((tm, d), lambda i: (i, 0)), pl.BlockSpec(rw.shape, lambda i: (0, 0))],
        out_specs=[row, row,
                   pl.BlockSpec((N_EXPERTS, tm), lambda i: (0, i)),
                   pl.BlockSpec((1, 1, LANES), lambda i: (i, 0, 0))],
        out_shape=[jax.ShapeDtypeStruct((n, LANES), F32), jax.ShapeDtypeStruct((n, LANES), F32),
                   jax.ShapeDtypeStruct((N_EXPERTS, n), F32), jax.ShapeDtypeStruct((nt, 1, LANES), I32)],
        scratch_shapes=[pltpu.VMEM((1, LANES), F32)],
        compiler_params=_params(("arbitrary",)),
    )(xh, rw)


def _moe_kernel(cum_ref, xh_ref, lpost_ref, lposc_ref, gatec_ref, w1_ref, w3_ref, w2_ref,
                o_ref, xg_ref, yacc_ref):
    tg = xh_ref.shape[0]
    sub, ch = MOE_SUB, ROUTER_ROWS
    nsub, nch = tg // sub, tg // ch
    g, e, f = pl.program_id(0), pl.program_id(1), pl.program_id(2)
    nf = pl.num_programs(2)
    bounds = [0] + [cum_ref[(g * nch + c) * N_EXPERTS + e] for c in range(nch)]
    cnt = bounds[nch]

    def overlaps(s, c):
        return (bounds[c] < (s + 1) * sub) & (bounds[c + 1] > s * sub)

    @pl.when((e == 0) & (f == 0))
    def _():
        o_ref[...] = jnp.zeros_like(o_ref)

    @pl.when(f == 0)
    def _gather():
        for s in range(nsub):
            @pl.when(s * sub < cnt)
            def _(s=s):
                rows = slice(s * sub, (s + 1) * sub)
                xg_ref[rows, :] = jnp.zeros((sub, xg_ref.shape[1]), xg_ref.dtype)
                yacc_ref[rows, :] = jnp.zeros((sub, yacc_ref.shape[1]), F32)
                want = (lax.broadcasted_iota(I32, (sub, 1), 0) + s * sub).astype(F32)
                for c in range(nch):
                    @pl.when(overlaps(s, c))
                    def _(c=c):
                        lp = lpost_ref[pl.ds(e, 1), c * ch:(c + 1) * ch]
                        pick = jnp.where(lp == want, 1.0, 0.0).astype(BF16)
                        got = _dot(pick, xh_ref[c * ch:(c + 1) * ch, :])
                        xg_ref[rows, :] += got.astype(xg_ref.dtype)

    for s in range(nsub):
        @pl.when(s * sub < cnt)
        def _(s=s):
            rows = slice(s * sub, (s + 1) * sub)
            xs = xg_ref[rows, :]
            hid = (_silu(_dot(xs, w1_ref[0])) * _dot(xs, w3_ref[0])).astype(BF16)
            yacc_ref[rows, :] += _dot(hid, w2_ref[0])

    @pl.when(f == nf - 1)
    def _scatter():
        lane = lax.broadcasted_iota(I32, (1, LANES), 1)
        for s in range(nsub):
            @pl.when(s * sub < cnt)
            def _(s=s):
                y = yacc_ref[s * sub:(s + 1) * sub, :]
                y_hi = y.astype(BF16)
                y_lo = (y - y_hi.astype(F32)).astype(BF16)
                want = (lax.broadcasted_iota(I32, (1, sub), 1) + s * sub).astype(F32)
                for c in range(nch):
                    @pl.when(overlaps(s, c))
                    def _(c=c):
                        toks = slice(c * ch, (c + 1) * ch)
                        lp = jnp.sum(jnp.where(lane == e, lposc_ref[toks, :], 0.0), axis=1, keepdims=True)
                        gt = jnp.sum(jnp.where(lane == e, gatec_ref[toks, :], 0.0), axis=1, keepdims=True)
                        place = jnp.where(lp == want, 1.0, 0.0).astype(BF16)
                        o_ref[toks, :] += gt * (_dot(place, y_hi) + _dot(place, y_lo))


def _moe(cum, xh, lpost, lposc, gatec, w1, w3, w2):
    n, d = xh.shape
    tg = MOE_GROUP
    dff = w1.shape[2]
    fc = MOE_FF_CHUNK
    grid_spec = pltpu.PrefetchScalarGridSpec(
        num_scalar_prefetch=1,
        grid=(n // tg, N_EXPERTS, dff // fc),
        in_specs=[pl.BlockSpec((tg, d), lambda g, e, f, c: (g, 0)),
                  pl.BlockSpec((N_EXPERTS, tg), lambda g, e, f, c: (0, g)),
                  pl.BlockSpec((tg, LANES), lambda g, e, f, c: (g, 0)),
                  pl.BlockSpec((tg, LANES), lambda g, e, f, c: (g, 0)),
                  pl.BlockSpec((1, d, fc), lambda g, e, f, c: (e, 0, f)),
                  pl.BlockSpec((1, d, fc), lambda g, e, f, c: (e, 0, f)),
                  pl.BlockSpec((1, fc, d), lambda g, e, f, c: (e, f, 0))],
        out_specs=pl.BlockSpec((tg, d), lambda g, e, f, c: (g, 0)),
        scratch_shapes=[pltpu.VMEM((tg, d), BF16), pltpu.VMEM((tg, d), F32)],
    )
    return pl.pallas_call(
        _moe_kernel,
        grid_spec=grid_spec,
        out_shape=jax.ShapeDtypeStruct((n, d), F32),
        compiler_params=_params(("parallel", "arbitrary", "arbitrary")),
    )(cum, xh, lpost, lposc, gatec, w1, w3, w2)


def _even_weight_layout(w_in):
    hd = HEAD_DIM
    widths = (A_HEADS * hd, hd, hd, IDX_HEADS * IDX_DIM, IDX_DIM, IDX_HEADS, B_HEADS * hd, B_HEADS * hd, B_HEADS * hd)
    offs = [0]
    for wd in widths:
        offs.append(offs[-1] + wd)
    qa, ka, va, iq, ik, iw, qb, kb, vb = (w_in[:, offs[j]:offs[j + 1]] for j in range(9))
    iw_pad = jnp.pad(iw, ((0, 0), (0, LANES - IDX_HEADS)))
    w = jnp.concatenate([qa, qb, kb, vb, ka, ka, va, va, iq, ik, ik, ik, ik, iw_pad], axis=1)
    mixw = B_HEADS * hd
    sections = ((0, A_HEADS * hd, hd),
                (512, mixw, hd), (1024, mixw, hd),
                (1536, mixw, None),
                (2048, LANES, hd),
                (2176, LANES, None),
                (2304, IDX_HEADS * IDX_DIM, IDX_DIM),
                (2560, LANES, IDX_DIM),
                (2688, LANES, None))
    dtypes = (BF16,) * 8 + (F32,)
    return w.astype(BF16), sections, dtypes


def kernel(x, even_w_in, even_w_out, even_ln1_g, even_ln1_b, even_w1, even_w3, even_w2, even_ln2_g, even_ln2_b, odd_w_in, odd_w_out, odd_ln1_g, odd_ln1_b, odd_router, odd_w1, odd_w3, odd_w2, odd_ln2_g, odd_ln2_b):
    batch, seq, d = x.shape
    n = batch * seq
    tables = _rope_tables(seq)
    x2d = x.reshape(n, d)
    for layer in range(DEPTH):
        i = layer // 2
        if layer % 2 == 0:
            w, sections, dtypes = _even_weight_layout(even_w_in[i])
            qa, qb, kb, vb, kk, vv, iq, ik4, iw = _project(x2d, w, tables, sections, dtypes, seq)
            o_a = _dsa_attention(iq, iw, ik4, qa, kk, vv, batch, seq)
            o_b = _moba_attention(qb, kb, vb, batch, seq)
            x2d, xh = _outproj_ln([o_a, o_b], even_w_out[i].astype(BF16), x2d, even_ln1_g[i], even_ln1_b[i])
            x2d = _swiglu_ln(xh, x2d, even_w1[i].astype(BF16), even_w3[i].astype(BF16), even_w2[i].astype(BF16),
                             even_ln2_g[i], even_ln2_b[i])
        else:
            mix = C_HEADS * HEAD_DIM
            sections = ((0, mix, HEAD_DIM), (mix, mix, HEAD_DIM), (2 * mix, mix, None))
            q, k, v = _project(x2d, odd_w_in[i].astype(BF16), tables, sections, (F32, F32, F32), seq)
            o = _dilated_attention(q, k, v, batch, seq)
            x2d, xh = _outproj_ln([o], odd_w_out[i].astype(BF16), x2d, odd_ln1_g[i], odd_ln1_b[i])
            rw = jnp.pad(odd_router[i], ((0, 0), (0, LANES - N_EXPERTS))).astype(BF16)
            gatec, lposc, lpost, cum = _router(xh, rw)
            cum = cum[:, 0, :N_EXPERTS].reshape(-1)
            ffn = _moe(cum, xh, lpost, lposc, gatec, odd_w1[i].astype(BF16), odd_w3[i].astype(BF16),
                       odd_w2[i].astype(BF16))
            x2d = _add_ln(x2d, ffn, odd_ln2_g[i], odd_ln2_b[i])
    return x2d.reshape(batch, seq, d)
```

```python
import functools

import jax
import jax.numpy as jnp
from jax import lax
from jax.experimental import pallas as pl
from jax.experimental.pallas import tpu as pltpu

F32 = jnp.float32
BF16 = jnp.bfloat16
I32 = jnp.int32

DEPTH = 4
HEAD_DIM = 64
ROPE_THETA = 10000.0
LN_EPS = 1e-5
A_HEADS = 8
IDX_HEADS = 8
IDX_DIM = 32
DSA_TOPK = 256
DSA_Q_BLOCK = 128
B_HEADS = 8
MOBA_BLOCK = 256
MOBA_TOPK = 3
C_HEADS = 16
DILATED_CFG = ((128, 1), (512, 4), (2048, 16))
BAND_BLOCK = 128
N_EXPERTS = 8
DEEPNORM_ALPHA = (2 * DEPTH) ** 0.25

LANES = 128
VMEM_LIMIT_BYTES = 52 * 1024 * 1024

PROJ_ROWS = 512
FFN_ROWS = 512
ROUTER_ROWS = 512
MOE_GROUP = 2048
MOE_SUB = 256
MOE_FF_CHUNK = 512
DILATED_UNITS = 4
ATTN_KEY_STEP = 512

NEG_INF = float("-inf")
NEG_INF_KEY = -2139095041


def _params(semantics):
    return pltpu.CompilerParams(dimension_semantics=semantics, vmem_limit_bytes=VMEM_LIMIT_BYTES)


def _dot(a, b):
    return jnp.dot(a, b, preferred_element_type=F32)


def _dot_nt(a, b):
    return lax.dot_general(a, b, (((1,), (1,)), ((), ())), preferred_element_type=F32)


def _lane_group(shape, group, idx):
    lane = lax.broadcasted_iota(I32, shape, len(shape) - 1)
    return (lane // group) == idx


def _rope_lanes(t, cosf, sinf, half):
    lane = lax.broadcasted_iota(I32, t.shape, 1)
    first = (lane % (2 * half)) < half
    swapped = jnp.where(first, pltpu.roll(t, LANES - half, 1), pltpu.roll(t, half, 1))
    return t * cosf + swapped * sinf


def _proj_kernel(x_ref, w_ref, c64_ref, s64_ref, c32_ref, s32_ref, *out_refs, sections):
    x = x_ref[...].astype(BF16)
    for o_ref, (start, width, rope) in zip(out_refs, sections):
        h = _dot(x, w_ref[:, start:start + width])
        if rope is not None:
            cosf = (c64_ref if rope == HEAD_DIM else c32_ref)[...]
            sinf = (s64_ref if rope == HEAD_DIM else s32_ref)[...]
            parts = [_rope_lanes(h[:, g * LANES:(g + 1) * LANES], cosf, sinf, rope // 2)
                     for g in range(width // LANES)]
            h = parts[0] if len(parts) == 1 else jnp.concatenate(parts, axis=1)
        o_ref[...] = h.astype(o_ref.dtype)


def _project(x2d, w, tables, sections, out_dtypes, seq):
    n, d = x2d.shape
    tm = PROJ_ROWS
    pos_blocks = seq // tm
    tab_spec = pl.BlockSpec((tm, LANES), lambda i: (i % pos_blocks, 0))
    return pl.pallas_call(
        functools.partial(_proj_kernel, sections=sections),
        grid=(n // tm,),
        in_specs=[pl.BlockSpec((tm, d), lambda i: (i, 0)),
                  pl.BlockSpec(w.shape, lambda i: (0, 0)),
                  tab_spec, tab_spec, tab_spec, tab_spec],
        out_specs=[pl.BlockSpec((tm, width), lambda i: (i, 0)) for (_, width, _) in sections],
        out_shape=[jax.ShapeDtypeStruct((n, width), dt) for (_, width, _), dt in zip(sections, out_dtypes)],
        compiler_params=_params(("parallel",)),
    )(x2d, w, *tables)


def _rope_tables(seq):
    out = []
    for dim in (HEAD_DIM, IDX_DIM):
        inv = ROPE_THETA ** (-jnp.arange(0, dim, 2, dtype=F32) / dim)
        ang = jnp.arange(seq, dtype=F32)[:, None] * inv[None, :]
        cos, sin = jnp.cos(ang), jnp.sin(ang)
        reps = LANES // dim
        out.append(jnp.tile(jnp.concatenate([cos, cos], -1), (1, reps)))
        out.append(jnp.tile(jnp.concatenate([-sin, sin], -1), (1, reps)))
    return out


def _dsa_block(i, iq_ref, iw_ref, ik_ref, q_ref, kk_ref, vv_ref, o_ref, key_ref, sel_ref, *, kext, topk):
    qb = DSA_Q_BLOCK
    keys = slice(0, kext)
    qpos = i * qb + lax.broadcasted_iota(I32, (qb, 1), 0)
    kpos = lax.broadcasted_iota(I32, (1, kext), 1)
    causal = kpos <= qpos

    ik = ik_ref[keys, :]
    iw = iw_ref[...] * (IDX_HEADS ** -0.5)
    heads_per_group = LANES // IDX_DIM
    score = jnp.zeros((qb, kext), F32)
    for h in range(IDX_HEADS):
        grp, j = divmod(h, heads_per_group)
        iqg = iq_ref[:, grp * LANES:(grp + 1) * LANES]
        iqm = jnp.where(_lane_group((1, LANES), IDX_DIM, j), iqg, jnp.zeros_like(iqg))
        rel = jnp.maximum(_dot_nt(iqm, ik) * (IDX_DIM ** -0.5), 0.0)
        score = score + rel * iw[:, h:h + 1]
    score = jnp.where(score == 0.0, 0.0, score)
    score = jnp.where(causal, score, NEG_INF)
    bits = pltpu.bitcast(score, I32)
    key_ref[:, keys] = bits ^ ((bits >> 31) & 0x7FFFFFFF)

    def count_ge(t):
        return jnp.sum((key_ref[:, keys] >= t).astype(I32), axis=1, keepdims=True)

    int_min = jnp.full((qb, 1), -2 ** 31, I32)
    thr0 = jnp.where(count_ge(jnp.zeros((qb, 1), I32)) >= topk, 0, int_min)

    def search(it, thr):
        cand = thr | jnp.left_shift(jnp.int32(1), 30 - it)
        return jnp.where(count_ge(cand) >= topk, cand, thr)

    thr = lax.fori_loop(0, 31, search, thr0)

    key = key_ref[:, keys]
    gt = key > thr
    eq = key == thr
    need = topk - jnp.sum(gt.astype(I32), axis=1, keepdims=True)
    n_eq = jnp.sum(eq.astype(I32), axis=1, keepdims=True)
    sel_ref[:, keys] = jnp.where((gt | eq) & causal, 1.0, 0.0)
    tie_rows = (n_eq > need) & (thr != NEG_INF_KEY)

    @pl.when(jnp.max(tie_rows.astype(I32)) > 0)
    def _break_ties():
        upper = (lax.broadcasted_iota(I32, (LANES, LANES), 0)
                 < lax.broadcasted_iota(I32, (LANES, LANES), 1))
        upper = jnp.where(upper, 1.0, 0.0).astype(BF16)
        before = jnp.zeros((qb, 1), F32)
        needf = need.astype(F32)
        for c in range(kext // LANES):
            sl = slice(c * LANES, (c + 1) * LANES)
            keyc = key_ref[:, sl]
            eqc = keyc == thr
            eqf = jnp.where(eqc, 1.0, 0.0)
            rank = _dot(eqf.astype(BF16), upper) + before
            take = (keyc > thr) | (eqc & (rank < needf))
            sel_ref[:, sl] = jnp.where(take & causal[:, sl], 1.0, 0.0)
            before = before + jnp.sum(eqf, axis=1, keepdims=True)

    selected = sel_ref[:, keys] > 0.5
    kk = kk_ref[keys, :]
    vv = vv_ref[keys, :]
    for hp in range(A_HEADS // 2):
        q2 = q_ref[:, hp * LANES:(hp + 1) * LANES]
        outs = []
        for half in range(2):
            qm = jnp.where(_lane_group((1, LANES), HEAD_DIM, half), q2, jnp.zeros_like(q2))
            s = jnp.where(selected, _dot_nt(qm, kk) * (HEAD_DIM ** -0.5), NEG_INF)
            e = jnp.exp(s - jnp.max(s, axis=1, keepdims=True))
            den = jnp.sum(e, axis=1, keepdims=True)
            outs.append(_dot(e.astype(BF16), vv) / den)
        o2 = jnp.where(_lane_group((1, LANES), HEAD_DIM, 0), outs[0], outs[1])
        o_ref[:, hp * LANES:(hp + 1) * LANES] = o2.astype(o_ref.dtype)


def _dsa_kernel(*refs, seq, topk):
    i = pl.program_id(1)
    blocks_per_step = ATTN_KEY_STEP // DSA_Q_BLOCK
    for v in range(seq // ATTN_KEY_STEP):
        @pl.when(i // blocks_per_step == v)
        def _(v=v):
            _dsa_block(i, *refs, kext=(v + 1) * ATTN_KEY_STEP, topk=topk)


def _dsa_attention(iq, iw, ik4, qa, kk, vv, batch, seq):
    n = batch * seq
    qb = DSA_Q_BLOCK
    nq = seq // qb
    topk = min(DSA_TOPK, seq // 4)
    rows = lambda width: pl.BlockSpec((qb, width), lambda b, i: (b * nq + i, 0))
    whole = pl.BlockSpec((seq, LANES), lambda b, i: (b, 0))
    return pl.pallas_call(
        functools.partial(_dsa_kernel, seq=seq, topk=topk),
        grid=(batch, nq),
        in_specs=[rows(iq.shape[1]), rows(LANES), whole, rows(qa.shape[1]), whole, whole],
        out_specs=rows(qa.shape[1]),
        out_shape=jax.ShapeDtypeStruct((n, qa.shape[1]), BF16),
        scratch_shapes=[pltpu.VMEM((qb, seq), I32), pltpu.VMEM((qb, seq), F32)],
        compiler_params=_params(("parallel", "parallel")),
    )(iq, iw, ik4, qa, kk, vv)


def _moba_block(qi, q_ref, k_ref, v_ref, o_ref, *, kext, kt):
    bs = MOBA_BLOCK
    nb = kext // bs
    q2 = q_ref[...]
    k2 = k_ref[0:kext, :]
    v2 = v_ref[0:kext, :]
    blk_row = lax.broadcasted_iota(I32, (nb, kext), 0)
    blk_of_key = lax.broadcasted_iota(I32, (nb, kext), 1) // bs
    member = jnp.where(blk_row == blk_of_key, 1.0, 0.0).astype(BF16)
    kmean = (_dot(member, k2) * (1.0 / bs)).astype(BF16)
    n_iota = lax.broadcasted_iota(I32, (1, nb), 1)
    past = n_iota < qi
    qrow = lax.broadcasted_iota(I32, (bs, 1), 0)
    kcol = lax.broadcasted_iota(I32, (1, kext), 1)
    own = (kcol >= qi * bs) & (kcol <= qi * bs + qrow)
    outs = []
    for half in range(2):
        qm = jnp.where(_lane_group((1, LANES), HEAD_DIM, half), q2, jnp.zeros_like(q2))
        gate = jnp.where(past, _dot_nt(qm, kmean), NEG_INF)
        rank = jnp.zeros((bs, nb), I32)
        for m in range(nb):
            gm = gate[:, m:m + 1]
            beats = (gm > gate) | ((gm == gate) & (m < n_iota))
            rank = rank + beats.astype(I32)
        chosen = jnp.where(past & (rank < kt), 1.0, 0.0).astype(BF16)
        mask = (_dot(chosen, member) > 0.5) | own
        s = jnp.where(mask, _dot_nt(qm, k2) * (HEAD_DIM ** -0.5), NEG_INF)
        e = jnp.exp(s - jnp.max(s, axis=1, keepdims=True))
        den = jnp.sum(e, axis=1, keepdims=True)
        outs.append(_dot(e.astype(BF16), v2) / den)
    o2 = jnp.where(_lane_group((1, LANES), HEAD_DIM, 0), outs[0], outs[1])
    o_ref[...] = o2.astype(o_ref.dtype)


def _moba_kernel(*refs, seq):
    qi = pl.program_id(2)
    kt = min(MOBA_TOPK, seq // MOBA_BLOCK - 1)
    blocks_per_step = ATTN_KEY_STEP // MOBA_BLOCK
    for v in range(seq // ATTN_KEY_STEP):
        @pl.when(qi // blocks_per_step == v)
        def _(v=v):
            _moba_block(qi, *refs, kext=(v + 1) * ATTN_KEY_STEP, kt=kt)


def _moba_attention(q, k, v, batch, seq):
    n, width = q.shape
    bs = MOBA_BLOCK
    nb = seq // bs
    rows = pl.BlockSpec((bs, LANES), lambda b, hp, qi: (b * nb + qi, hp))
    whole = pl.BlockSpec((seq, LANES), lambda b, hp, qi: (b, hp))
    return pl.pallas_call(
        functools.partial(_moba_kernel, seq=seq),
        grid=(batch, width // LANES, nb),
        in_specs=[rows, whole, whole],
        out_specs=rows,
        out_shape=jax.ShapeDtypeStruct((n, width), BF16),
        compiler_params=_params(("parallel", "parallel", "parallel")),
    )(q, k, v)


def _rows(start, size, stride):
    return pl.ds(start, size) if stride == 1 else pl.ds(start, size, stride=stride)


def _largest_divisor(n, limit):
    return max(d for d in range(1, limit + 1) if n % d == 0)


def _repeat(trips, body):
    if trips == 1:
        body(0, 0)
    else:
        lax.fori_loop(0, trips, body, 0)


def _dilated_kernel(q_ref, k_ref, v_ref, o_ref, m_ref, l_ref, acc_ref, *, seq):
    blk = BAND_BLOCK
    first_half = _lane_group((1, LANES), HEAD_DIM, 0)

    def compute(qstart, kstart, nk, stride):
        qsl = _rows(qstart, blk, stride)
        ksl = _rows(kstart, nk, stride)
        q2 = q_ref[qsl, :].astype(BF16)
        k2 = k_ref[ksl, :].astype(BF16)
        v2 = v_ref[ksl, :].astype(BF16)
        i = lax.broadcasted_iota(I32, (blk, 1), 0)
        j = lax.broadcasted_iota(I32, (1, nk), 1)
        if nk == blk:
            mask = j <= i
        else:
            mask = (j >= i) & (j <= i + blk)
        ms, ls, accs = [], [], []
        for half in range(2):
            qm = jnp.where(_lane_group((1, LANES), HEAD_DIM, half), q2, jnp.zeros_like(q2))
            s = jnp.where(mask, _dot_nt(qm, k2) * (HEAD_DIM ** -0.5), NEG_INF)
            mx = jnp.max(s, axis=1, keepdims=True)
            e = jnp.exp(s - mx)
            ms.append(mx)
            ls.append(jnp.sum(e, axis=1, keepdims=True))
            accs.append(_dot(e.astype(BF16), v2))
        return (qsl, jnp.where(first_half, ms[0], ms[1]), jnp.where(first_half, ls[0], ls[1]),
                jnp.where(first_half, accs[0], accs[1]))

    def commit(result, merge):
        qsl, m_new, l_new, a_new = result
        if merge:
            m_old = m_ref[qsl, :]
            m_tot = jnp.maximum(m_old, m_new)
            w_old = jnp.exp(m_old - m_tot)
            w_new = jnp.exp(m_new - m_tot)
            l_ref[qsl, :] = w_old * l_ref[qsl, :] + w_new * l_new
            acc_ref[qsl, :] = w_old * acc_ref[qsl, :] + w_new * a_new
            m_ref[qsl, :] = m_tot
        else:
            m_ref[qsl, :] = m_new
            l_ref[qsl, :] = l_new
            acc_ref[qsl, :] = a_new

    def run_units(units, nk, stride, merge):
        results = [compute(qstart, kstart, nk, stride) for qstart, kstart in units]
        for result in results:
            commit(result, merge)

    for bi, (window, dil) in enumerate(DILATED_CFG):
        assert window // dil == blk
        merge = bi > 0
        nblk = seq // dil // blk
        ub = _largest_divisor(dil, DILATED_UNITS)

        def first_blocks(it, carry, dil=dil, merge=merge, ub=ub):
            run_units([(it * ub + u, it * ub + u) for u in range(ub)], blk, dil, merge)
            return carry

        _repeat(dil // ub, first_blocks)
        if nblk > 1:
            ul = _largest_divisor(nblk - 1, DILATED_UNITS)
            per_class = (nblk - 1) // ul

            def later_blocks(it, carry, dil=dil, merge=merge, ul=ul, per_class=per_class):
                r, grp = it // per_class, it % per_class
                qstarts = [r + dil * blk * (1 + grp * ul + u) for u in range(ul)]
                run_units([(qs, qs - dil * blk) for qs in qstarts], 2 * blk, dil, merge)
                return carry

            _repeat(dil * per_class, later_blocks)

    o_ref[...] = (acc_ref[...] / l_ref[...]).astype(o_ref.dtype)


def _dilated_attention(q, k, v, batch, seq):
    n, width = q.shape
    for window, dil in DILATED_CFG:
        assert seq % (dil * BAND_BLOCK) == 0
    whole = pl.BlockSpec((seq, LANES), lambda b, hp: (b, hp))
    return pl.pallas_call(
        functools.partial(_dilated_kernel, seq=seq),
        grid=(batch, width // LANES),
        in_specs=[whole, whole, whole],
        out_specs=whole,
        out_shape=jax.ShapeDtypeStruct((n, width), BF16),
        scratch_shapes=[pltpu.VMEM((seq, LANES), F32)] * 3,
        compiler_params=_params(("parallel", "parallel")),
    )(q, k, v)


def _deepnorm(x, f, g, b):
    y = DEEPNORM_ALPHA * x + f
    mu = jnp.mean(y, axis=1, keepdims=True)
    yc = y - mu
    var = jnp.mean(yc * yc, axis=1, keepdims=True)
    return yc * lax.rsqrt(var + LN_EPS) * g + b


def _outproj_ln_kernel(*refs, n_parts):
    part_refs = refs[:n_parts]
    w_ref, x_ref, g_ref, b_ref, o_ref, oh_ref = refs[n_parts:]
    mix = None
    start = 0
    for p_ref in part_refs:
        width = p_ref.shape[1]
        t = _dot(p_ref[...], w_ref[start:start + width, :])
        mix = t if mix is None else mix + t
        start += width
    y = _deepnorm(x_ref[...], mix, g_ref[...], b_ref[...])
    o_ref[...] = y
    oh_ref[...] = y.astype(BF16)


def _outproj_ln(parts, w, x2d, g, b):
    n, d = x2d.shape
    tm = PROJ_ROWS
    row = lambda width: pl.BlockSpec((tm, width), lambda i: (i, 0))
    vec = pl.BlockSpec((1, d), lambda i: (0, 0))
    return pl.pallas_call(
        functools.partial(_outproj_ln_kernel, n_parts=len(parts)),
        grid=(n // tm,),
        in_specs=[row(p.shape[1]) for p in parts] + [pl.BlockSpec(w.shape, lambda i: (0, 0)), row(d), vec, vec],
        out_specs=[row(d), row(d)],
        out_shape=[jax.ShapeDtypeStruct((n, d), F32), jax.ShapeDtypeStruct((n, d), BF16)],
        compiler_params=_params(("parallel",)),
    )(*parts, w, x2d, g.reshape(1, d), b.reshape(1, d))


def _silu(h):
    return h * (1.0 / (1.0 + jnp.exp(-h)))


def _swiglu_ln_kernel(xh_ref, x_ref, w1_ref, w3_ref, w2_ref, g_ref, b_ref, o_ref, acc_ref):
    f = pl.program_id(1)
    xh = xh_ref[...]
    hid = (_silu(_dot(xh, w1_ref[...])) * _dot(xh, w3_ref[...])).astype(BF16)
    y = _dot(hid, w2_ref[...])

    @pl.when(f == 0)
    def _():
        acc_ref[...] = y

    @pl.when(f > 0)
    def _():
        acc_ref[...] += y

    @pl.when(f == pl.num_programs(1) - 1)
    def _():
        o_ref[...] = _deepnorm(x_ref[...], acc_ref[...], g_ref[...], b_ref[...])


def _swiglu_ln(xh, x2d, w1, w3, w2, g, b):
    n, d = x2d.shape
    dff = w1.shape[1]
    tm = FFN_ROWS
    nf = 2 if dff % (2 * LANES) == 0 else 1
    fc = dff // nf
    row = lambda: pl.BlockSpec((tm, d), lambda i, f: (i, 0))
    vec = pl.BlockSpec((1, d), lambda i, f: (0, 0))
    return pl.pallas_call(
        _swiglu_ln_kernel,
        grid=(n // tm, nf),
        in_specs=[row(), row(),
                  pl.BlockSpec((d, fc), lambda i, f: (0, f)),
                  pl.BlockSpec((d, fc), lambda i, f: (0, f)),
                  pl.BlockSpec((fc, d), lambda i, f: (f, 0)),
                  vec, vec],
        out_specs=row(),
        out_shape=jax.ShapeDtypeStruct((n, d), F32),
        scratch_shapes=[pltpu.VMEM((tm, d), F32)],
        compiler_params=_params(("parallel", "arbitrary")),
    )(xh, x2d, w1, w3, w2, g.reshape(1, d), b.reshape(1, d))


def _add_ln_kernel(x_ref, f_ref, g_ref, b_ref, o_ref):
    o_ref[...] = _deepnorm(x_ref[...], f_ref[...], g_ref[...], b_ref[...])


def _add_ln(x2d, f2d, g, b):
    n, d = x2d.shape
    tm = PROJ_ROWS
    row = pl.BlockSpec((tm, d), lambda i: (i, 0))
    vec = pl.BlockSpec((1, d), lambda i: (0, 0))
    return pl.pallas_call(
        _add_ln_kernel,
        grid=(n // tm,),
        in_specs=[row, row, vec, vec],
        out_specs=row,
        out_shape=jax.ShapeDtypeStruct((n, d), F32),
        compiler_params=_params(("parallel",)),
    )(x2d, f2d, g.reshape(1, d), b.reshape(1, d))


def _router_kernel(xh_ref, rw_ref, gate_ref, lposc_ref, lpost_ref, cum_ref, carry_ref, *, tiles_per_group):
    tm = xh_ref.shape[0]
    i = pl.program_id(0)

    @pl.when(i % tiles_per_group == 0)
    def _():
        carry_ref[...] = jnp.zeros_like(carry_ref)

    lane = lax.broadcasted_iota(I32, (1, LANES), 1)
    logits = jnp.where(lane < N_EXPERTS, _dot(xh_ref[...], rw_ref[...]), NEG_INF)
    m1 = jnp.max(logits, axis=1, keepdims=True)
    i1 = jnp.min(jnp.where(logits == m1, lane, LANES), axis=1, keepdims=True)
    rest = jnp.where(lane == i1, NEG_INF, logits)
    m2 = jnp.max(rest, axis=1, keepdims=True)
    i2 = jnp.min(jnp.where(rest == m2, lane, LANES), axis=1, keepdims=True)
    e2 = jnp.exp(m2 - m1)
    g1 = 1.0 / (1.0 + e2)
    g2 = e2 / (1.0 + e2)
    sel1 = lane == i1
    sel2 = lane == i2
    routed = sel1 | sel2
    gate_ref[...] = jnp.where(sel1, g1, jnp.where(sel2, g2, 0.0))
    routedf = jnp.where(routed, 1.0, 0.0)
    earlier = (lax.broadcasted_iota(I32, (tm, tm), 1) < lax.broadcasted_iota(I32, (tm, tm), 0))
    earlier = jnp.where(earlier, 1.0, 0.0).astype(BF16)
    carry = carry_ref[...]
    lpos = jnp.where(routed, _dot(earlier, routedf.astype(BF16)) + carry, -1.0)
    lposc_ref[...] = lpos
    lpost_ref[...] = jnp.transpose(lpos)[0:N_EXPERTS, :]
    carry = carry + jnp.sum(routedf, axis=0, keepdims=True)
    carry_ref[...] = carry
    cum_ref[...] = carry.astype(I32).reshape(1, 1, LANES)


def _router(xh, rw):
    n, d = xh.shape
    tm = ROUTER_ROWS
    nt = n // tm
    row = pl.BlockSpec((tm, LANES), lambda i: (i, 0))
    return pl.pallas_call(
        functools.partial(_router_kernel, tiles_per_group=MOE_GROUP // tm),
        grid=(nt,),
        in_specs=[pl.BlockSpec((tm, d), lambda i: (i, 0)), pl.BlockSpec(rw.shape, lambda i: (0, 0))],
        out_specs=[row, row,
                   pl.BlockSpec((N_EXPERTS, tm), lambda i: (0, i)),
                   pl.BlockSpec((1, 1, LANES), lambda i: (i, 0, 0))],
        out_shape=[jax.ShapeDtypeStruct((n, LANES), F32), jax.ShapeDtypeStruct((n, LANES), F32),
                   jax.ShapeDtypeStruct((N_EXPERTS, n), F32), jax.ShapeDtypeStruct((nt, 1, LANES), I32)],
        scratch_shapes=[pltpu.VMEM((1, LANES), F32)],
        compiler_params=_params(("arbitrary",)),
    )(xh, rw)


def _moe_kernel(bnd_ref, xh_ref, lpost_ref, lposc_ref, gatec_ref, w1_ref, w3_ref, w2_ref,
                o_ref, xg_ref, yacc_ref):
    tg = xh_ref.shape[0]
    sub, ch = MOE_SUB, ROUTER_ROWS
    nch = tg // ch
    g, e, f = pl.program_id(0), pl.program_id(1), pl.program_id(2)
    nf = pl.num_programs(2)

    def bound(c):
        return bnd_ref[(g * (nch + 1) + c) * N_EXPERTS + e]

    n_active = (bound(nch) + sub - 1) // sub

    def overlaps(s, c):
        return (bound(c) < (s + 1) * sub) & (bound(c + 1) > s * sub)

    def row_block(s):
        return pl.ds(pl.multiple_of(s * sub, sub), sub)

    def tok_chunk(c):
        return pl.ds(pl.multiple_of(c * ch, ch), ch)

    @pl.when((e == 0) & (f == 0))
    def _():
        o_ref[...] = jnp.zeros_like(o_ref)

    @pl.when(f == 0)
    def _gather():
        def per_block(s, carry):
            rows = row_block(s)
            xg_ref[rows, :] = jnp.zeros((sub, xg_ref.shape[1]), xg_ref.dtype)
            yacc_ref[rows, :] = jnp.zeros((sub, yacc_ref.shape[1]), F32)
            want = (lax.broadcasted_iota(I32, (sub, 1), 0) + s * sub).astype(F32)

            def per_chunk(c, carry2):
                @pl.when(overlaps(s, c))
                def _():
                    lp = lpost_ref[pl.ds(e, 1), tok_chunk(c)]
                    pick = jnp.where(lp == want, 1.0, 0.0).astype(BF16)
                    xg_ref[rows, :] += _dot(pick, xh_ref[tok_chunk(c), :]).astype(xg_ref.dtype)
                return carry2

            return lax.fori_loop(0, nch, per_chunk, carry)

        lax.fori_loop(0, n_active, per_block, 0)

    def ffn_block(s, carry):
        rows = row_block(s)
        xs = xg_ref[rows, :]
        hid = (_silu(_dot(xs, w1_ref[0])) * _dot(xs, w3_ref[0])).astype(BF16)
        yacc_ref[rows, :] += _dot(hid, w2_ref[0])
        return carry

    lax.fori_loop(0, n_active, ffn_block, 0)

    @pl.when(f == nf - 1)
    def _scatter():
        lane = lax.broadcasted_iota(I32, (1, LANES), 1)

        def per_block(s, carry):
            y = yacc_ref[row_block(s), :]
            y_hi = y.astype(BF16)
            y_lo = (y - y_hi.astype(F32)).astype(BF16)
            want = (lax.broadcasted_iota(I32, (1, sub), 1) + s * sub).astype(F32)

            def per_chunk(c, carry2):
                @pl.when(overlaps(s, c))
                def _():
                    toks = tok_chunk(c)
                    lp = jnp.sum(jnp.where(lane == e, lposc_ref[toks, :], 0.0), axis=1, keepdims=True)
                    gt = jnp.sum(jnp.where(lane == e, gatec_ref[toks, :], 0.0), axis=1, keepdims=True)
                    place = jnp.where(lp == want, 1.0, 0.0).astype(BF16)
                    o_ref[toks, :] += gt * (_dot(place, y_hi) + _dot(place, y_lo))
                return carry2

            return lax.fori_loop(0, nch, per_chunk, carry)

        lax.fori_loop(0, n_active, per_block, 0)


def _chunk_bounds(cum):
    nch = MOE_GROUP // ROUTER_ROWS
    per_group = cum[:, 0, :N_EXPERTS].reshape(-1, nch, N_EXPERTS)
    return jnp.pad(per_group, ((0, 0), (1, 0), (0, 0))).reshape(-1)


def _moe(bounds, xh, lpost, lposc, gatec, w1, w3, w2):
    n, d = xh.shape
    tg = MOE_GROUP
    dff = w1.shape[2]
    fc = MOE_FF_CHUNK
    grid_spec = pltpu.PrefetchScalarGridSpec(
        num_scalar_prefetch=1,
        grid=(n // tg, N_EXPERTS, dff // fc),
        in_specs=[pl.BlockSpec((tg, d), lambda g, e, f, c: (g, 0)),
                  pl.BlockSpec((N_EXPERTS, tg), lambda g, e, f, c: (0, g)),
                  pl.BlockSpec((tg, LANES), lambda g, e, f, c: (g, 0)),
                  pl.BlockSpec((tg, LANES), lambda g, e, f, c: (g, 0)),
                  pl.BlockSpec((1, d, fc), lambda g, e, f, c: (e, 0, f)),
                  pl.BlockSpec((1, d, fc), lambda g, e, f, c: (e, 0, f)),
                  pl.BlockSpec((1, fc, d), lambda g, e, f, c: (e, f, 0))],
        out_specs=pl.BlockSpec((tg, d), lambda g, e, f, c: (g, 0)),
        scratch_shapes=[pltpu.VMEM((tg, d), BF16), pltpu.VMEM((tg, d), F32)],
    )
    return pl.pallas_call(
        _moe_kernel,
        grid_spec=grid_spec,
        out_shape=jax.ShapeDtypeStruct((n, d), F32),
        compiler_params=_params(("parallel", "arbitrary", "arbitrary")),
    )(bounds, xh, lpost, lposc, gatec, w1, w3, w2)


def _even_weight_layout(w_in):
    hd = HEAD_DIM
    widths = (A_HEADS * hd, hd, hd, IDX_HEADS * IDX_DIM, IDX_DIM, IDX_HEADS, B_HEADS * hd, B_HEADS * hd, B_HEADS * hd)
    offs = [0]
    for wd in widths:
        offs.append(offs[-1] + wd)
    qa, ka, va, iq, ik, iw, qb, kb, vb = (w_in[:, offs[j]:offs[j + 1]] for j in range(9))
    iw_pad = jnp.pad(iw, ((0, 0), (0, LANES - IDX_HEADS)))
    w = jnp.concatenate([qa, qb, kb, vb, ka, ka, va, va, iq, ik, ik, ik, ik, iw_pad], axis=1)
    mixw = B_HEADS * hd
    sections = ((0, A_HEADS * hd, hd),
                (512, mixw, hd), (1024, mixw, hd),
                (1536, mixw, None),
                (2048, LANES, hd),
                (2176, LANES, None),
                (2304, IDX_HEADS * IDX_DIM, IDX_DIM),
                (2560, LANES, IDX_DIM),
                (2688, LANES, None))
    dtypes = (BF16,) * 8 + (F32,)
    return w.astype(BF16), sections, dtypes


def kernel(x, even_w_in, even_w_out, even_ln1_g, even_ln1_b, even_w1, even_w3, even_w2, even_ln2_g, even_ln2_b, odd_w_in, odd_w_out, odd_ln1_g, odd_ln1_b, odd_router, odd_w1, odd_w3, odd_w2, odd_ln2_g, odd_ln2_b):
    batch, seq, d = x.shape
    n = batch * seq
    tables = _rope_tables(seq)
    x2d = x.reshape(n, d)
    for layer in range(DEPTH):
        i = layer // 2
        if layer % 2 == 0:
            w, sections, dtypes = _even_weight_layout(even_w_in[i])
            qa, qb, kb, vb, kk, vv, iq, ik4, iw = _project(x2d, w, tables, sections, dtypes, seq)
            o_a = _dsa_attention(iq, iw, ik4, qa, kk, vv, batch, seq)
            o_b = _moba_attention(qb, kb, vb, batch, seq)
            x2d, xh = _outproj_ln([o_a, o_b], even_w_out[i].astype(BF16), x2d, even_ln1_g[i], even_ln1_b[i])
            x2d = _swiglu_ln(xh, x2d, even_w1[i].astype(BF16), even_w3[i].astype(BF16), even_w2[i].astype(BF16),
                             even_ln2_g[i], even_ln2_b[i])
        else:
            mix = C_HEADS * HEAD_DIM
            sections = ((0, mix, HEAD_DIM), (mix, mix, HEAD_DIM), (2 * mix, mix, None))
            q, k, v = _project(x2d, odd_w_in[i].astype(BF16), tables, sections, (F32, F32, F32), seq)
            o = _dilated_attention(q, k, v, batch, seq)
            x2d, xh = _outproj_ln([o], odd_w_out[i].astype(BF16), x2d, odd_ln1_g[i], odd_ln1_b[i])
            rw = jnp.pad(odd_router[i], ((0, 0), (0, LANES - N_EXPERTS))).astype(BF16)
            gatec, lposc, lpost, cum = _router(xh, rw)
            ffn = _moe(_chunk_bounds(cum), xh, lpost, lposc, gatec, odd_w1[i].astype(BF16), odd_w3[i].astype(BF16),
                       odd_w2[i].astype(BF16))
            x2d = _add_ln(x2d, ffn, odd_ln2_g[i], odd_ln2_b[i])
    return x2d.reshape(batch, seq, d)
```

```python
import functools

import jax
import jax.numpy as jnp
from jax import lax
from jax.experimental import pallas as pl
from jax.experimental.pallas import tpu as pltpu

F32 = jnp.float32
BF16 = jnp.bfloat16
I32 = jnp.int32

DEPTH = 4
HEAD_DIM = 64
ROPE_THETA = 10000.0
LN_EPS = 1e-5
A_HEADS = 8
IDX_HEADS = 8
IDX_DIM = 32
DSA_TOPK = 256
DSA_Q_BLOCK = 128
B_HEADS = 8
MOBA_BLOCK = 256
MOBA_TOPK = 3
C_HEADS = 16
DILATED_CFG = ((128, 1), (512, 4), (2048, 16))
BAND_BLOCK = 128
N_EXPERTS = 8
DEEPNORM_ALPHA = (2 * DEPTH) ** 0.25
QK_SCALE = HEAD_DIM ** -0.5 * 1.4426950408889634

LANES = 128
VMEM_LIMIT_BYTES = 52 * 1024 * 1024

PROJ_ROWS = 512
FFN_ROWS = 512
ROUTER_ROWS = 256
MOE_GROUP = 2048
MOE_SUB = 288
MOE_FF_CHUNK = 512
DILATED_UNITS = 4
DILATED_SLABS = 4
ATTN_KEY_STEP = 512

NEG_INF = float("-inf")
NEG_INF_KEY = -2139095041


def _params(semantics):
    return pltpu.CompilerParams(dimension_semantics=semantics, vmem_limit_bytes=VMEM_LIMIT_BYTES)


def _dot(a, b):
    return jnp.dot(a, b, preferred_element_type=F32)


def _dot_nt(a, b):
    return lax.dot_general(a, b, (((1,), (1,)), ((), ())), preferred_element_type=F32)


def _lane_group(shape, group, idx):
    lane = lax.broadcasted_iota(I32, shape, len(shape) - 1)
    return (lane // group) == idx


def _head_pair_attention(units):
    first = _lane_group((1, LANES), HEAD_DIM, 0)
    halves = (first, jnp.logical_not(first))
    scores = []
    for q2, k2, _, mask in units:
        masks = mask if isinstance(mask, (tuple, list)) else (mask, mask)
        for mine, m in zip(halves, masks):
            qm = jnp.where(mine, q2, jnp.zeros_like(q2))
            scores.append(jnp.where(m, _dot_nt(qm, k2), NEG_INF))
    maxes = [jnp.max(s, axis=1, keepdims=True) for s in scores]
    weights = [jnp.exp2(s - mx).astype(BF16) for s, mx in zip(scores, maxes)]
    results = []
    for u, (_, _, v2, _) in enumerate(units):
        ones = jnp.ones_like(v2)
        pv0 = _dot(weights[2 * u], jnp.where(first, v2, ones))
        pv1 = _dot(weights[2 * u + 1], jnp.where(first, ones, v2))
        den = jnp.where(first, pv0[:, LANES - 1:LANES], pv1[:, 0:1])
        results.append((jnp.where(first, maxes[2 * u], maxes[2 * u + 1]), den, jnp.where(first, pv0, pv1)))
    return results


def _rope_lanes(t, cosf, sinf, half):
    lane = lax.broadcasted_iota(I32, t.shape, 1)
    first = (lane % (2 * half)) < half
    swapped = jnp.where(first, pltpu.roll(t, LANES - half, 1), pltpu.roll(t, half, 1))
    return t * cosf + swapped * sinf


def _proj_kernel(x_ref, w_ref, c64_ref, s64_ref, c32_ref, s32_ref, *out_refs, sections):
    x = x_ref[...].astype(BF16)
    for o_ref, (start, width, rope, scale) in zip(out_refs, sections):
        h = _dot(x, w_ref[:, start:start + width])
        if rope is not None:
            cosf = (c64_ref if rope == HEAD_DIM else c32_ref)[...]
            sinf = (s64_ref if rope == HEAD_DIM else s32_ref)[...]
            parts = [_rope_lanes(h[:, g * LANES:(g + 1) * LANES], cosf, sinf, rope // 2)
                     for g in range(width // LANES)]
            h = parts[0] if len(parts) == 1 else jnp.concatenate(parts, axis=1)
        if scale is not None:
            h = h * scale
        o_ref[...] = h.astype(o_ref.dtype)


def _project(x2d, w, tables, sections, out_dtypes, seq):
    n, d = x2d.shape
    tm = PROJ_ROWS
    pos_blocks = seq // tm
    tab_spec = pl.BlockSpec((tm, LANES), lambda i: (i % pos_blocks, 0))
    return pl.pallas_call(
        functools.partial(_proj_kernel, sections=sections),
        grid=(n // tm,),
        in_specs=[pl.BlockSpec((tm, d), lambda i: (i, 0)),
                  pl.BlockSpec(w.shape, lambda i: (0, 0)),
                  tab_spec, tab_spec, tab_spec, tab_spec],
        out_specs=[pl.BlockSpec((tm, sec[1]), lambda i: (i, 0)) for sec in sections],
        out_shape=[jax.ShapeDtypeStruct((n, sec[1]), dt) for sec, dt in zip(sections, out_dtypes)],
        compiler_params=_params(("parallel",)),
    )(x2d, w, *tables)


def _rope_tables(seq):
    out = []
    for dim in (HEAD_DIM, IDX_DIM):
        inv = ROPE_THETA ** (-jnp.arange(0, dim, 2, dtype=F32) / dim)
        ang = jnp.arange(seq, dtype=F32)[:, None] * inv[None, :]
        cos, sin = jnp.cos(ang), jnp.sin(ang)
        reps = LANES // dim
        out.append(jnp.tile(jnp.concatenate([cos, cos], -1), (1, reps)))
        out.append(jnp.tile(jnp.concatenate([-sin, sin], -1), (1, reps)))
    return out


def _dsa_block(i, iq_ref, iw_ref, ik_ref, q_ref, kk_ref, vv_ref, o_ref, key_ref, sel_ref, *, kext, topk):
    qb = DSA_Q_BLOCK
    keys = slice(0, kext)
    qpos = i * qb + lax.broadcasted_iota(I32, (qb, 1), 0)
    kpos = lax.broadcasted_iota(I32, (1, kext), 1)
    causal = kpos <= qpos

    ik = ik_ref[keys, :]
    iw = iw_ref[...] * (IDX_DIM ** -0.5 * IDX_HEADS ** -0.5)
    heads_per_group = LANES // IDX_DIM
    score = jnp.zeros((qb, kext), F32)
    for h in range(IDX_HEADS):
        grp, j = divmod(h, heads_per_group)
        iqg = iq_ref[:, grp * LANES:(grp + 1) * LANES]
        iqm = jnp.where(_lane_group((1, LANES), IDX_DIM, j), iqg, jnp.zeros_like(iqg))
        rel = jnp.maximum(_dot_nt(iqm, ik), 0.0)
        score = score + rel * iw[:, h:h + 1]
    score = jnp.where(score == 0.0, 0.0, score)
    score = jnp.where(causal, score, NEG_INF)
    bits = pltpu.bitcast(score, I32)
    key_ref[:, keys] = bits ^ ((bits >> 31) & 0x7FFFFFFF)

    def count_ge(t):
        return jnp.sum(jnp.where(key_ref[:, keys] >= t, 1.0, 0.0), axis=1, keepdims=True)

    int_min = jnp.full((qb, 1), -2 ** 31, I32)
    thr0 = jnp.where(count_ge(jnp.zeros((qb, 1), I32)) >= topk, 0, int_min)

    def search(it, thr):
        cand = thr | jnp.left_shift(jnp.int32(1), 30 - it)
        return jnp.where(count_ge(cand) >= topk, cand, thr)

    thr = lax.fori_loop(0, 31, search, thr0)

    key = key_ref[:, keys]
    gt = key > thr
    eq = key == thr
    need = topk - jnp.sum(gt.astype(I32), axis=1, keepdims=True)
    n_eq = jnp.sum(eq.astype(I32), axis=1, keepdims=True)
    sel_ref[:, keys] = jnp.where((gt | eq) & causal, 1.0, 0.0)
    tie_rows = (n_eq > need) & (thr != NEG_INF_KEY)

    @pl.when(jnp.max(tie_rows.astype(I32)) > 0)
    def _break_ties():
        upper = (lax.broadcasted_iota(I32, (LANES, LANES), 0)
                 < lax.broadcasted_iota(I32, (LANES, LANES), 1))
        upper = jnp.where(upper, 1.0, 0.0).astype(BF16)
        before = jnp.zeros((qb, 1), F32)
        needf = need.astype(F32)
        for c in range(kext // LANES):
            sl = slice(c * LANES, (c + 1) * LANES)
            keyc = key_ref[:, sl]
            eqc = keyc == thr
            eqf = jnp.where(eqc, 1.0, 0.0)
            rank = _dot(eqf.astype(BF16), upper) + before
            take = (keyc > thr) | (eqc & (rank < needf))
            sel_ref[:, sl] = jnp.where(take & causal[:, sl], 1.0, 0.0)
            before = before + jnp.sum(eqf, axis=1, keepdims=True)

    selected = sel_ref[:, keys] > 0.5
    kk = kk_ref[keys, :]
    vv = vv_ref[keys, :]
    pairs = [(q_ref[:, hp * LANES:(hp + 1) * LANES], kk, vv, selected) for hp in range(A_HEADS // 2)]
    for hp, (_, den, num) in enumerate(_head_pair_attention(pairs)):
        o_ref[:, hp * LANES:(hp + 1) * LANES] = (num / den).astype(o_ref.dtype)


def _dsa_kernel(*refs, seq, topk):
    i = pl.program_id(1)
    blocks_per_step = ATTN_KEY_STEP // DSA_Q_BLOCK
    for v in range(seq // ATTN_KEY_STEP):
        @pl.when(i // blocks_per_step == v)
        def _(v=v):
            _dsa_block(i, *refs, kext=(v + 1) * ATTN_KEY_STEP, topk=topk)


def _dsa_attention(iq, iw, ik4, qa, kk, vv, batch, seq):
    n = batch * seq
    qb = DSA_Q_BLOCK
    nq = seq // qb
    topk = min(DSA_TOPK, seq // 4)
    rows = lambda width: pl.BlockSpec((qb, width), lambda b, i: (b * nq + i, 0))
    whole = pl.BlockSpec((seq, LANES), lambda b, i: (b, 0))
    return pl.pallas_call(
        functools.partial(_dsa_kernel, seq=seq, topk=topk),
        grid=(batch, nq),
        in_specs=[rows(iq.shape[1]), rows(LANES), whole, rows(qa.shape[1]), whole, whole],
        out_specs=rows(qa.shape[1]),
        out_shape=jax.ShapeDtypeStruct((n, qa.shape[1]), BF16),
        scratch_shapes=[pltpu.VMEM((qb, seq), I32), pltpu.VMEM((qb, seq), F32)],
        compiler_params=_params(("parallel", "parallel")),
    )(iq, iw, ik4, qa, kk, vv)


def _moba_block(qi, q_ref, k_ref, v_ref, o_ref, *, kext, kt):
    bs = MOBA_BLOCK
    nb = kext // bs
    q2 = q_ref[...]
    k2 = k_ref[0:kext, :]
    v2 = v_ref[0:kext, :]
    blk_row = lax.broadcasted_iota(I32, (nb, kext), 0)
    blk_of_key = lax.broadcasted_iota(I32, (nb, kext), 1) // bs
    member = jnp.where(blk_row == blk_of_key, 1.0, 0.0).astype(BF16)
    kmean = (_dot(member, k2) * (1.0 / bs)).astype(BF16)
    n_iota = lax.broadcasted_iota(I32, (1, nb), 1)
    past = n_iota < qi
    qrow = lax.broadcasted_iota(I32, (bs, 1), 0)
    kcol = lax.broadcasted_iota(I32, (1, kext), 1)
    own = (kcol >= qi * bs) & (kcol <= qi * bs + qrow)
    masks = []
    for half in range(2):
        qm = jnp.where(_lane_group((1, LANES), HEAD_DIM, half), q2, jnp.zeros_like(q2))
        gate = jnp.where(past, _dot_nt(qm, kmean), NEG_INF)
        rank = jnp.zeros((bs, nb), I32)
        for m in range(nb):
            gm = gate[:, m:m + 1]
            beats = (gm > gate) | ((gm == gate) & (m < n_iota))
            rank = rank + beats.astype(I32)
        chosen = jnp.where(past & (rank < kt), 1.0, 0.0).astype(BF16)
        masks.append((_dot(chosen, member) > 0.5) | own)
    (_, den, num), = _head_pair_attention([(q2, k2, v2, masks)])
    o_ref[...] = (num / den).astype(o_ref.dtype)


def _moba_kernel(*refs, seq):
    qi = pl.program_id(2)
    kt = min(MOBA_TOPK, seq // MOBA_BLOCK - 1)
    blocks_per_step = ATTN_KEY_STEP // MOBA_BLOCK
    for v in range(seq // ATTN_KEY_STEP):
        @pl.when(qi // blocks_per_step == v)
        def _(v=v):
            _moba_block(qi, *refs, kext=(v + 1) * ATTN_KEY_STEP, kt=kt)


def _moba_attention(q, k, v, batch, seq):
    n, width = q.shape
    bs = MOBA_BLOCK
    nb = seq // bs
    rows = pl.BlockSpec((bs, LANES), lambda b, hp, qi: (b * nb + qi, hp))
    whole = pl.BlockSpec((seq, LANES), lambda b, hp, qi: (b, hp))
    return pl.pallas_call(
        functools.partial(_moba_kernel, seq=seq),
        grid=(batch, width // LANES, nb),
        in_specs=[rows, whole, whole],
        out_specs=rows,
        out_shape=jax.ShapeDtypeStruct((n, width), BF16),
        compiler_params=_params(("parallel", "parallel", "parallel")),
    )(q, k, v)


def _rows(start, size, stride):
    return pl.ds(start, size) if stride == 1 else pl.ds(start, size, stride=stride)


def _largest_divisor(n, limit):
    return max(d for d in range(1, limit + 1) if n % d == 0)


def _repeat(trips, body):
    if trips == 1:
        body(0, 0)
    else:
        lax.fori_loop(0, trips, body, 0)


def _dilated_kernel(q_ref, k_ref, v_ref, o_ref, qp_ref, kp_ref, vp_ref, nat_ref, slab_ref, *, seq):
    blk = BAND_BLOCK
    ns = DILATED_SLABS
    slab_len = seq // ns

    for s in range(ns):
        rows = slice(s * slab_len, (s + 1) * slab_len)
        qp_ref[rows, :] = q_ref[pl.ds(s, slab_len, stride=ns), :]
        kp_ref[rows, :] = k_ref[pl.ds(s, slab_len, stride=ns), :]
        vp_ref[rows, :] = v_ref[pl.ds(s, slab_len, stride=ns), :]

    def operands(srcs, qstart, kstart, nk, stride):
        ksl = _rows(kstart, nk, stride)
        i = lax.broadcasted_iota(I32, (blk, 1), 0)
        j = lax.broadcasted_iota(I32, (1, nk), 1)
        if nk == blk:
            mask = j <= i
        else:
            mask = (j >= i) & (j <= i + blk)
        return (srcs[0][_rows(qstart, blk, stride), :].astype(BF16), srcs[1][ksl, :].astype(BF16),
                srcs[2][ksl, :].astype(BF16), mask)

    def merged(old, new):
        (m_old, l_old, a_old), (m_new, l_new, a_new) = old, new
        m_tot = jnp.maximum(m_old, m_new)
        w_old = jnp.exp2(m_old - m_tot)
        w_new = jnp.exp2(m_new - m_tot)
        return m_tot, w_old * l_old + w_new * l_new, w_old * a_old + w_new * a_new

    def run_units(srcs, state_ref, units, nk, stride, merge):
        results = _head_pair_attention([operands(srcs, qstart, kstart, nk, stride) for qstart, kstart in units])
        for (qstart, _), new in zip(units, results):
            qsl = _rows(qstart, blk, stride)
            if merge:
                new = merged(tuple(state_ref[t, qsl, :] for t in range(3)), new)
            for t in range(3):
                state_ref[t, qsl, :] = new[t]

    written = set()
    for window, dil in DILATED_CFG:
        assert window // dil == blk
        if dil % ns == 0:
            srcs, state_ref, stride, group_len = (qp_ref, kp_ref, vp_ref), slab_ref, dil // ns, slab_len
        else:
            srcs, state_ref, stride, group_len = (q_ref, k_ref, v_ref), nat_ref, dil, seq
        merge = id(state_ref) in written
        written.add(id(state_ref))
        nblk = seq // dil // blk
        ub = _largest_divisor(dil, DILATED_UNITS)

        def class_start(c, stride=stride, group_len=group_len):
            return (c // stride) * group_len + c % stride if group_len != seq else c

        def first_blocks(it, carry, srcs=srcs, state_ref=state_ref, stride=stride, merge=merge, ub=ub,
                         class_start=class_start):
            starts = [class_start(it * ub + u) for u in range(ub)]
            run_units(srcs, state_ref, [(st, st) for st in starts], blk, stride, merge)
            return carry

        _repeat(dil // ub, first_blocks)
        if nblk > 1:
            ul = _largest_divisor(nblk - 1, DILATED_UNITS)
            per_class = (nblk - 1) // ul

            def later_blocks(it, carry, srcs=srcs, state_ref=state_ref, stride=stride, merge=merge, ul=ul,
                             per_class=per_class, class_start=class_start):
                base, grp = class_start(it // per_class), it % per_class
                qstarts = [base + stride * blk * (1 + grp * ul + u) for u in range(ul)]
                run_units(srcs, state_ref, [(qs, qs - stride * blk) for qs in qstarts], 2 * blk, stride, merge)
                return carry

            _repeat(dil * per_class, later_blocks)

    for s in range(ns):
        nat_rows = pl.ds(s, slab_len, stride=ns)
        slab_rows = slice(s * slab_len, (s + 1) * slab_len)
        _, den, num = merged(tuple(nat_ref[t, nat_rows, :] for t in range(3)),
                             tuple(slab_ref[t, slab_rows, :] for t in range(3)))
        nat_ref[2, nat_rows, :] = num / den
    o_ref[...] = nat_ref[2].astype(o_ref.dtype)


def _dilated_attention(q, k, v, batch, seq):
    n, width = q.shape
    for window, dil in DILATED_CFG:
        assert seq % (dil * BAND_BLOCK) == 0
    assert {dil % DILATED_SLABS == 0 for _, dil in DILATED_CFG} == {True, False}
    whole = pl.BlockSpec((seq, LANES), lambda b, hp: (b, hp))
    return pl.pallas_call(
        functools.partial(_dilated_kernel, seq=seq),
        grid=(batch, width // LANES),
        in_specs=[whole, whole, whole],
        out_specs=whole,
        out_shape=jax.ShapeDtypeStruct((n, width), BF16),
        scratch_shapes=[pltpu.VMEM((seq, LANES), F32)] * 3 + [pltpu.VMEM((3, seq, LANES), F32)] * 2,
        compiler_params=_params(("parallel", "parallel")),
    )(q, k, v)


def _deepnorm(x, f, g, b):
    y = DEEPNORM_ALPHA * x + f
    mu = jnp.mean(y, axis=1, keepdims=True)
    yc = y - mu
    var = jnp.mean(yc * yc, axis=1, keepdims=True)
    return yc * lax.rsqrt(var + LN_EPS) * g + b


def _outproj_ln_kernel(*refs, n_parts):
    part_refs = refs[:n_parts]
    w_ref, x_ref, g_ref, b_ref, o_ref, oh_ref = refs[n_parts:]
    mix = None
    start = 0
    for p_ref in part_refs:
        width = p_ref.shape[1]
        t = _dot(p_ref[...], w_ref[start:start + width, :])
        mix = t if mix is None else mix + t
        start += width
    y = _deepnorm(x_ref[...], mix, g_ref[...], b_ref[...])
    o_ref[...] = y
    oh_ref[...] = y.astype(BF16)


def _outproj_ln(parts, w, x2d, g, b):
    n, d = x2d.shape
    tm = PROJ_ROWS
    row = lambda width: pl.BlockSpec((tm, width), lambda i: (i, 0))
    vec = pl.BlockSpec((1, d), lambda i: (0, 0))
    return pl.pallas_call(
        functools.partial(_outproj_ln_kernel, n_parts=len(parts)),
        grid=(n // tm,),
        in_specs=[row(p.shape[1]) for p in parts] + [pl.BlockSpec(w.shape, lambda i: (0, 0)), row(d), vec, vec],
        out_specs=[row(d), row(d)],
        out_shape=[jax.ShapeDtypeStruct((n, d), F32), jax.ShapeDtypeStruct((n, d), BF16)],
        compiler_params=_params(("parallel",)),
    )(*parts, w, x2d, g.reshape(1, d), b.reshape(1, d))


def _silu(h):
    return h * (1.0 / (1.0 + jnp.exp(-h)))


def _swiglu_ln_kernel(xh_ref, x_ref, w1_ref, w3_ref, w2_ref, g_ref, b_ref, o_ref, acc_ref):
    f = pl.program_id(1)
    xh = xh_ref[...]
    hid = (_silu(_dot(xh, w1_ref[...])) * _dot(xh, w3_ref[...])).astype(BF16)
    y = _dot(hid, w2_ref[...])

    @pl.when(f == 0)
    def _():
        acc_ref[...] = y

    @pl.when(f > 0)
    def _():
        acc_ref[...] += y

    @pl.when(f == pl.num_programs(1) - 1)
    def _():
        o_ref[...] = _deepnorm(x_ref[...], acc_ref[...], g_ref[...], b_ref[...])


def _swiglu_ln(xh, x2d, w1, w3, w2, g, b):
    n, d = x2d.shape
    dff = w1.shape[1]
    tm = FFN_ROWS
    nf = 2 if dff % (2 * LANES) == 0 else 1
    fc = dff // nf
    row = lambda: pl.BlockSpec((tm, d), lambda i, f: (i, 0))
    vec = pl.BlockSpec((1, d), lambda i, f: (0, 0))
    return pl.pallas_call(
        _swiglu_ln_kernel,
        grid=(n // tm, nf),
        in_specs=[row(), row(),
                  pl.BlockSpec((d, fc), lambda i, f: (0, f)),
                  pl.BlockSpec((d, fc), lambda i, f: (0, f)),
                  pl.BlockSpec((fc, d), lambda i, f: (f, 0)),
                  vec, vec],
        out_specs=row(),
        out_shape=jax.ShapeDtypeStruct((n, d), F32),
        scratch_shapes=[pltpu.VMEM((tm, d), F32)],
        compiler_params=_params(("parallel", "arbitrary")),
    )(xh, x2d, w1, w3, w2, g.reshape(1, d), b.reshape(1, d))


def _add_ln_kernel(x_ref, f_ref, g_ref, b_ref, o_ref):
    o_ref[...] = _deepnorm(x_ref[...], f_ref[...], g_ref[...], b_ref[...])


def _add_ln(x2d, f2d, g, b):
    n, d = x2d.shape
    tm = PROJ_ROWS
    row = pl.BlockSpec((tm, d), lambda i: (i, 0))
    vec = pl.BlockSpec((1, d), lambda i: (0, 0))
    return pl.pallas_call(
        _add_ln_kernel,
        grid=(n // tm,),
        in_specs=[row, row, vec, vec],
        out_specs=row,
        out_shape=jax.ShapeDtypeStruct((n, d), F32),
        compiler_params=_params(("parallel",)),
    )(x2d, f2d, g.reshape(1, d), b.reshape(1, d))


def _router_kernel(xh_ref, rw_ref, gate_ref, lposc_ref, lpost_ref, cum_ref, carry_ref, *, tiles_per_group):
    tm = xh_ref.shape[0]
    i = pl.program_id(0)

    @pl.when(i % tiles_per_group == 0)
    def _():
        carry_ref[...] = jnp.zeros_like(carry_ref)

    lane = lax.broadcasted_iota(I32, (1, LANES), 1)
    logits = jnp.where(lane < N_EXPERTS, _dot(xh_ref[...], rw_ref[...]), NEG_INF)
    m1 = jnp.max(logits, axis=1, keepdims=True)
    i1 = jnp.min(jnp.where(logits == m1, lane, LANES), axis=1, keepdims=True)
    rest = jnp.where(lane == i1, NEG_INF, logits)
    m2 = jnp.max(rest, axis=1, keepdims=True)
    i2 = jnp.min(jnp.where(rest == m2, lane, LANES), axis=1, keepdims=True)
    e2 = jnp.exp(m2 - m1)
    g1 = 1.0 / (1.0 + e2)
    g2 = e2 / (1.0 + e2)
    sel1 = lane == i1
    sel2 = lane == i2
    routed = sel1 | sel2
    gate_ref[...] = jnp.where(sel1, g1, jnp.where(sel2, g2, 0.0))
    routedf = jnp.where(routed, 1.0, 0.0)
    earlier = (lax.broadcasted_iota(I32, (tm, tm), 1) < lax.broadcasted_iota(I32, (tm, tm), 0))
    earlier = jnp.where(earlier, 1.0, 0.0).astype(BF16)
    carry = carry_ref[...]
    lpos = jnp.where(routed, _dot(earlier, routedf.astype(BF16)) + carry, -1.0)
    lposc_ref[...] = lpos
    lpost_ref[...] = jnp.transpose(lpos)[0:N_EXPERTS, :]
    carry = carry + jnp.sum(routedf, axis=0, keepdims=True)
    carry_ref[...] = carry
    cum_ref[...] = carry.astype(I32).reshape(1, 1, LANES)


def _router(xh, rw):
    n, d = xh.shape
    tm = ROUTER_ROWS
    nt = n // tm
    row = pl.BlockSpec((tm, LANES), lambda i: (i, 0))
    return pl.pallas_call(
        functools.partial(_router_kernel, tiles_per_group=MOE_GROUP // tm),
        grid=(nt,),
        in_specs=[pl.BlockSpec((tm, d), lambda i: (i, 0)), pl.BlockSpec(rw.shape, lambda i: (0, 0))],
        out_specs=[row, row,
                   pl.BlockSpec((N_EXPERTS, tm), lambda i: (0, i)),
                   pl.BlockSpec((1, 1, LANES), lambda i: (i, 0, 0))],
        out_shape=[jax.ShapeDtypeStruct((n, LANES), F32), jax.ShapeDtypeStruct((n, LANES), F32),
                   jax.ShapeDtypeStruct((N_EXPERTS, n), F32), jax.ShapeDtypeStruct((nt, 1, LANES), I32)],
        scratch_shapes=[pltpu.VMEM((1, LANES), F32)],
        compiler_params=_params(("arbitrary",)),
    )(xh, rw)


def _moe_kernel(bnd_ref, xh_ref, lpost_ref, lposc_ref, gatec_ref, w1_ref, w3_ref, w2_ref,
                o_ref, xg_ref, yacc_ref):
    tg = xh_ref.shape[0]
    sub, ch = MOE_SUB, ROUTER_ROWS
    nch = tg // ch
    g, e, f = pl.program_id(0), pl.program_id(1), pl.program_id(2)
    nf = pl.num_programs(2)

    def bound(c):
        return bnd_ref[(g * (nch + 1) + c) * N_EXPERTS + e]

    n_active = (bound(nch) + sub - 1) // sub

    def overlaps(s, c):
        return (bound(c) < (s + 1) * sub) & (bound(c + 1) > s * sub)

    def row_block(s):
        return pl.ds(pl.multiple_of(s * sub, sub), sub)

    def tok_chunk(c):
        return pl.ds(pl.multiple_of(c * ch, ch), ch)

    @pl.when((e == 0) & (f == 0))
    def _():
        o_ref[...] = jnp.zeros_like(o_ref)

    @pl.when(f == 0)
    def _gather():
        def per_block(s, carry):
            rows = row_block(s)
            xg_ref[rows, :] = jnp.zeros((sub, xg_ref.shape[1]), xg_ref.dtype)
            yacc_ref[rows, :] = jnp.zeros((sub, yacc_ref.shape[1]), F32)
            want = (lax.broadcasted_iota(I32, (sub, 1), 0) + s * sub).astype(F32)

            def per_chunk(c, carry2):
                @pl.when(overlaps(s, c))
                def _():
                    lp = lpost_ref[pl.ds(e, 1), tok_chunk(c)]
                    pick = jnp.where(lp == want, 1.0, 0.0).astype(BF16)
                    xg_ref[rows, :] += _dot(pick, xh_ref[tok_chunk(c), :]).astype(xg_ref.dtype)
                return carry2

            return lax.fori_loop(0, nch, per_chunk, carry)

        lax.fori_loop(0, n_active, per_block, 0)

    def ffn_block(s, carry):
        rows = row_block(s)
        xs = xg_ref[rows, :]
        hid = (_silu(_dot(xs, w1_ref[0])) * _dot(xs, w3_ref[0])).astype(BF16)
        yacc_ref[rows, :] += _dot(hid, w2_ref[0])
        return carry

    lax.fori_loop(0, n_active, ffn_block, 0)

    @pl.when(f == nf - 1)
    def _scatter():
        lane = lax.broadcasted_iota(I32, (1, LANES), 1)

        def per_block(s, carry):
            y = yacc_ref[row_block(s), :]
            y_hi = y.astype(BF16)
            y_lo = (y - y_hi.astype(F32)).astype(BF16)
            want = (lax.broadcasted_iota(I32, (1, sub), 1) + s * sub).astype(F32)

            def per_chunk(c, carry2):
                @pl.when(overlaps(s, c))
                def _():
                    toks = tok_chunk(c)
                    lp = jnp.sum(jnp.where(lane == e, lposc_ref[toks, :], 0.0), axis=1, keepdims=True)
                    gt = jnp.sum(jnp.where(lane == e, gatec_ref[toks, :], 0.0), axis=1, keepdims=True)
                    place = jnp.where(lp == want, 1.0, 0.0).astype(BF16)
                    o_ref[toks, :] += gt * (_dot(place, y_hi) + _dot(place, y_lo))
                return carry2

            return lax.fori_loop(0, nch, per_chunk, carry)

        lax.fori_loop(0, n_active, per_block, 0)


def _chunk_bounds(cum):
    nch = MOE_GROUP // ROUTER_ROWS
    per_group = cum[:, 0, :N_EXPERTS].reshape(-1, nch, N_EXPERTS)
    return jnp.pad(per_group, ((0, 0), (1, 0), (0, 0))).reshape(-1)


def _moe(bounds, xh, lpost, lposc, gatec, w1, w3, w2):
    n, d = xh.shape
    tg = MOE_GROUP
    dff = w1.shape[2]
    fc = MOE_FF_CHUNK
    cap = -(-tg // MOE_SUB) * MOE_SUB
    grid_spec = pltpu.PrefetchScalarGridSpec(
        num_scalar_prefetch=1,
        grid=(n // tg, N_EXPERTS, dff // fc),
        in_specs=[pl.BlockSpec((tg, d), lambda g, e, f, c: (g, 0)),
                  pl.BlockSpec((N_EXPERTS, tg), lambda g, e, f, c: (0, g)),
                  pl.BlockSpec((tg, LANES), lambda g, e, f, c: (g, 0)),
                  pl.BlockSpec((tg, LANES), lambda g, e, f, c: (g, 0)),
                  pl.BlockSpec((1, d, fc), lambda g, e, f, c: (e, 0, f)),
                  pl.BlockSpec((1, d, fc), lambda g, e, f, c: (e, 0, f)),
                  pl.BlockSpec((1, fc, d), lambda g, e, f, c: (e, f, 0))],
        out_specs=pl.BlockSpec((tg, d), lambda g, e, f, c: (g, 0)),
        scratch_shapes=[pltpu.VMEM((cap, d), BF16), pltpu.VMEM((cap, d), F32)],
    )
    return pl.pallas_call(
        _moe_kernel,
        grid_spec=grid_spec,
        out_shape=jax.ShapeDtypeStruct((n, d), F32),
        compiler_params=_params(("parallel", "arbitrary", "arbitrary")),
    )(bounds, xh, lpost, lposc, gatec, w1, w3, w2)


def _even_weight_layout(w_in):
    hd = HEAD_DIM
    widths = (A_HEADS * hd, hd, hd, IDX_HEADS * IDX_DIM, IDX_DIM, IDX_HEADS, B_HEADS * hd, B_HEADS * hd, B_HEADS * hd)
    offs = [0]
    for wd in widths:
        offs.append(offs[-1] + wd)
    qa, ka, va, iq, ik, iw, qb, kb, vb = (w_in[:, offs[j]:offs[j + 1]] for j in range(9))
    iw_pad = jnp.pad(iw, ((0, 0), (0, LANES - IDX_HEADS)))
    w = jnp.concatenate([qa, qb, kb, vb, ka, ka, va, va, iq, ik, ik, ik, ik, iw_pad], axis=1)
    mixw = B_HEADS * hd
    sections = ((0, A_HEADS * hd, hd, QK_SCALE),
                (512, mixw, hd, QK_SCALE),
                (1024, mixw, hd, None),
                (1536, mixw, None, None),
                (2048, LANES, hd, None),
                (2176, LANES, None, None),
                (2304, IDX_HEADS * IDX_DIM, IDX_DIM, None),
                (2560, LANES, IDX_DIM, None),
                (2688, LANES, None, None))
    dtypes = (BF16,) * 8 + (F32,)
    return w.astype(BF16), sections, dtypes


def kernel(x, even_w_in, even_w_out, even_ln1_g, even_ln1_b, even_w1, even_w3, even_w2, even_ln2_g, even_ln2_b, odd_w_in, odd_w_out, odd_ln1_g, odd_ln1_b, odd_router, odd_w1, odd_w3, odd_w2, odd_ln2_g, odd_ln2_b):
    batch, seq, d = x.shape
    n = batch * seq
    tables = _rope_tables(seq)
    x2d = x.reshape(n, d)
    for layer in range(DEPTH):
        i = layer // 2
        if layer % 2 == 0:
            w, sections, dtypes = _even_weight_layout(even_w_in[i])
            qa, qb, kb, vb, kk, vv, iq, ik4, iw = _project(x2d, w, tables, sections, dtypes, seq)
            o_a = _dsa_attention(iq, iw, ik4, qa, kk, vv, batch, seq)
            o_b = _moba_attention(qb, kb, vb, batch, seq)
            x2d, xh = _outproj_ln([o_a, o_b], even_w_out[i].astype(BF16), x2d, even_ln1_g[i], even_ln1_b[i])
            x2d = _swiglu_ln(xh, x2d, even_w1[i].astype(BF16), even_w3[i].astype(BF16), even_w2[i].astype(BF16),
                             even_ln2_g[i], even_ln2_b[i])
        else:
            mix = C_HEADS * HEAD_DIM
            sections = ((0, mix, HEAD_DIM, QK_SCALE), (mix, mix, HEAD_DIM, None), (2 * mix, mix, None, None))
            q, k, v = _project(x2d, odd_w_in[i].astype(BF16), tables, sections, (F32, F32, F32), seq)
            o = _dilated_attention(q, k, v, batch, seq)
            x2d, xh = _outproj_ln([o], odd_w_out[i].astype(BF16), x2d, odd_ln1_g[i], odd_ln1_b[i])
            rw = jnp.pad(odd_router[i], ((0, 0), (0, LANES - N_EXPERTS))).astype(BF16)
            gatec, lposc, lpost, cum = _router(xh, rw)
            ffn = _moe(_chunk_bounds(cum), xh, lpost, lposc, gatec, odd_w1[i].astype(BF16), odd_w3[i].astype(BF16),
                       odd_w2[i].astype(BF16))
            x2d = _add_ln(x2d, ffn, odd_ln2_g[i], odd_ln2_b[i])
    return x2d.reshape(batch, seq, d)
```

```python
import functools

import jax
import jax.numpy as jnp
from jax import lax
from jax.experimental import pallas as pl
from jax.experimental.pallas import tpu as pltpu

F32 = jnp.float32
BF16 = jnp.bfloat16
I32 = jnp.int32

DEPTH = 4
HEAD_DIM = 64
ROPE_THETA = 10000.0
LN_EPS = 1e-5
A_HEADS = 8
IDX_HEADS = 8
IDX_DIM = 32
DSA_TOPK = 256
DSA_Q_BLOCK = 128
B_HEADS = 8
MOBA_BLOCK = 256
MOBA_TOPK = 3
C_HEADS = 16
DILATED_CFG = ((128, 1), (512, 4), (2048, 16))
BAND_BLOCK = 128
N_EXPERTS = 8
DEEPNORM_ALPHA = (2 * DEPTH) ** 0.25
QK_SCALE = HEAD_DIM ** -0.5 * 1.4426950408889634

LANES = 128
VMEM_LIMIT_BYTES = 52 * 1024 * 1024

PROJ_ROWS = 512
FFN_ROWS = 512
ROUTER_ROWS = 512
MOE_GROUP = 2048
MOE_SUB = 256
MOE_FF_CHUNK = 512
MOBA_GATE_ROWS = 16
DILATED_UNITS = 4
DILATED_SLABS = 4
ATTN_KEY_STEP = 512

NEG_INF = float("-inf")
NEG_INF_KEY = -2139095041


def _params(semantics):
    return pltpu.CompilerParams(dimension_semantics=semantics, vmem_limit_bytes=VMEM_LIMIT_BYTES)


def _dot(a, b):
    return jnp.dot(a, b, preferred_element_type=F32)


def _dot_nt(a, b):
    return lax.dot_general(a, b, (((1,), (1,)), ((), ())), preferred_element_type=F32)


def _lane_group(shape, group, idx):
    lane = lax.broadcasted_iota(I32, shape, len(shape) - 1)
    return (lane // group) == idx


def _head_pair_attention(units):
    first = _lane_group((1, LANES), HEAD_DIM, 0)
    halves = (first, jnp.logical_not(first))
    raw = [_dot_nt(jnp.where(mine, q2, jnp.zeros_like(q2)), k2) for q2, k2, _, _ in units for mine in halves]
    scores = []
    for u, (_, _, _, mask) in enumerate(units):
        if callable(mask):
            mask = mask()
        masks = mask if isinstance(mask, (tuple, list)) else (mask, mask)
        scores += [jnp.where(masks[h], raw[2 * u + h], NEG_INF) for h in range(2)]
    maxes = [jnp.max(s, axis=1, keepdims=True) for s in scores]
    weights = [jnp.exp2(s - mx).astype(BF16) for s, mx in zip(scores, maxes)]
    results = []
    for u, (_, _, v2, _) in enumerate(units):
        ones = jnp.ones_like(v2)
        pv0 = _dot(weights[2 * u], jnp.where(first, v2, ones))
        pv1 = _dot(weights[2 * u + 1], jnp.where(first, ones, v2))
        den = jnp.where(first, pv0[:, LANES - 1:LANES], pv1[:, 0:1])
        results.append((jnp.where(first, maxes[2 * u], maxes[2 * u + 1]), den, jnp.where(first, pv0, pv1)))
    return results


def _rope_lanes(t, cosf, sinf, half):
    lane = lax.broadcasted_iota(I32, t.shape, 1)
    first = (lane % (2 * half)) < half
    swapped = jnp.where(first, pltpu.roll(t, LANES - half, 1), pltpu.roll(t, half, 1))
    return t * cosf + swapped * sinf


def _proj_kernel(x_ref, w_ref, c64_ref, s64_ref, c32_ref, s32_ref, *out_refs, sections):
    x = x_ref[...].astype(BF16)
    for o_ref, (start, width, rope, scale) in zip(out_refs, sections):
        h = _dot(x, w_ref[:, start:start + width])
        if rope is not None:
            cosf = (c64_ref if rope == HEAD_DIM else c32_ref)[...]
            sinf = (s64_ref if rope == HEAD_DIM else s32_ref)[...]
            parts = [_rope_lanes(h[:, g * LANES:(g + 1) * LANES], cosf, sinf, rope // 2)
                     for g in range(width // LANES)]
            h = parts[0] if len(parts) == 1 else jnp.concatenate(parts, axis=1)
        if scale is not None:
            h = h * scale
        o_ref[...] = h.astype(o_ref.dtype)


def _project(x2d, w, tables, sections, out_dtypes, seq):
    n, d = x2d.shape
    tm = PROJ_ROWS
    pos_blocks = seq // tm
    tab_spec = pl.BlockSpec((tm, LANES), lambda i: (i % pos_blocks, 0))
    return pl.pallas_call(
        functools.partial(_proj_kernel, sections=sections),
        grid=(n // tm,),
        in_specs=[pl.BlockSpec((tm, d), lambda i: (i, 0)),
                  pl.BlockSpec(w.shape, lambda i: (0, 0)),
                  tab_spec, tab_spec, tab_spec, tab_spec],
        out_specs=[pl.BlockSpec((tm, sec[1]), lambda i: (i, 0)) for sec in sections],
        out_shape=[jax.ShapeDtypeStruct((n, sec[1]), dt) for sec, dt in zip(sections, out_dtypes)],
        compiler_params=_params(("parallel",)),
    )(x2d, w, *tables)


def _rope_tables(seq):
    out = []
    for dim in (HEAD_DIM, IDX_DIM):
        inv = ROPE_THETA ** (-jnp.arange(0, dim, 2, dtype=F32) / dim)
        ang = jnp.arange(seq, dtype=F32)[:, None] * inv[None, :]
        cos, sin = jnp.cos(ang), jnp.sin(ang)
        reps = LANES // dim
        out.append(jnp.tile(jnp.concatenate([cos, cos], -1), (1, reps)))
        out.append(jnp.tile(jnp.concatenate([-sin, sin], -1), (1, reps)))
    return out


def _dsa_block(i, iq_ref, iw_ref, ik_ref, q_ref, kk_ref, vv_ref, o_ref, key_ref, sel_ref, *, kext, topk):
    qb = DSA_Q_BLOCK
    keys = slice(0, kext)
    qpos = i * qb + lax.broadcasted_iota(I32, (qb, 1), 0)
    kpos = lax.broadcasted_iota(I32, (1, kext), 1)
    causal = kpos <= qpos

    ik = ik_ref[keys, :]
    iw = iw_ref[...] * (IDX_DIM ** -0.5 * IDX_HEADS ** -0.5)
    heads_per_group = LANES // IDX_DIM
    score = jnp.zeros((qb, kext), F32)
    for h in range(IDX_HEADS):
        grp, j = divmod(h, heads_per_group)
        iqg = iq_ref[:, grp * LANES:(grp + 1) * LANES]
        iqm = jnp.where(_lane_group((1, LANES), IDX_DIM, j), iqg, jnp.zeros_like(iqg))
        rel = jnp.maximum(_dot_nt(iqm, ik), 0.0)
        score = score + rel * iw[:, h:h + 1]
    score = jnp.where(score == 0.0, 0.0, score)
    score = jnp.where(causal, score, NEG_INF)
    bits = pltpu.bitcast(score, I32)
    key_ref[:, keys] = bits ^ ((bits >> 31) & 0x7FFFFFFF)

    def count_ge(t):
        return jnp.sum(jnp.where(key_ref[:, keys] >= t, 1.0, 0.0), axis=1, keepdims=True)

    int_min = jnp.full((qb, 1), -2 ** 31, I32)
    thr0 = jnp.where(count_ge(jnp.zeros((qb, 1), I32)) >= topk, 0, int_min)

    def search(it, thr):
        cand = thr | jnp.left_shift(jnp.int32(1), 30 - it)
        return jnp.where(count_ge(cand) >= topk, cand, thr)

    thr = lax.fori_loop(0, 31, search, thr0)

    key = key_ref[:, keys]
    gt = key > thr
    eq = key == thr
    need = topk - jnp.sum(gt.astype(I32), axis=1, keepdims=True)
    n_eq = jnp.sum(eq.astype(I32), axis=1, keepdims=True)
    sel_ref[:, keys] = jnp.where((gt | eq) & causal, 1.0, 0.0)
    tie_rows = (n_eq > need) & (thr != NEG_INF_KEY)

    @pl.when(jnp.max(tie_rows.astype(I32)) > 0)
    def _break_ties():
        upper = (lax.broadcasted_iota(I32, (LANES, LANES), 0)
                 < lax.broadcasted_iota(I32, (LANES, LANES), 1))
        upper = jnp.where(upper, 1.0, 0.0).astype(BF16)
        before = jnp.zeros((qb, 1), F32)
        needf = need.astype(F32)
        for c in range(kext // LANES):
            sl = slice(c * LANES, (c + 1) * LANES)
            keyc = key_ref[:, sl]
            eqc = keyc == thr
            eqf = jnp.where(eqc, 1.0, 0.0)
            rank = _dot(eqf.astype(BF16), upper) + before
            take = (keyc > thr) | (eqc & (rank < needf))
            sel_ref[:, sl] = jnp.where(take & causal[:, sl], 1.0, 0.0)
            before = before + jnp.sum(eqf, axis=1, keepdims=True)

    selected = sel_ref[:, keys] > 0.5
    kk = kk_ref[keys, :]
    vv = vv_ref[keys, :]
    pairs = [(q_ref[:, hp * LANES:(hp + 1) * LANES], kk, vv, selected) for hp in range(A_HEADS // 2)]
    for hp, (_, den, num) in enumerate(_head_pair_attention(pairs)):
        o_ref[:, hp * LANES:(hp + 1) * LANES] = (num / den).astype(o_ref.dtype)


def _dsa_kernel(*refs, seq, topk):
    i = pl.program_id(1)
    blocks_per_step = ATTN_KEY_STEP // DSA_Q_BLOCK
    for v in range(seq // ATTN_KEY_STEP):
        @pl.when(i // blocks_per_step == v)
        def _(v=v):
            _dsa_block(i, *refs, kext=(v + 1) * ATTN_KEY_STEP, topk=topk)


def _dsa_attention(iq, iw, ik4, qa, kk, vv, batch, seq):
    n = batch * seq
    qb = DSA_Q_BLOCK
    nq = seq // qb
    topk = min(DSA_TOPK, seq // 4)
    rows = lambda width: pl.BlockSpec((qb, width), lambda b, i: (b * nq + i, 0))
    whole = pl.BlockSpec((seq, LANES), lambda b, i: (b, 0))
    return pl.pallas_call(
        functools.partial(_dsa_kernel, seq=seq, topk=topk),
        grid=(batch, nq),
        in_specs=[rows(iq.shape[1]), rows(LANES), whole, rows(qa.shape[1]), whole, whole],
        out_specs=rows(qa.shape[1]),
        out_shape=jax.ShapeDtypeStruct((n, qa.shape[1]), BF16),
        scratch_shapes=[pltpu.VMEM((qb, seq), I32), pltpu.VMEM((qb, seq), F32)],
        compiler_params=_params(("parallel", "parallel")),
    )(iq, iw, ik4, qa, kk, vv)


def _moba_block(qi, q_ref, k_ref, v_ref, o_ref, *, kext, kt):
    bs = MOBA_BLOCK
    nb = kext // bs
    q2 = q_ref[...]
    k2 = k_ref[0:kext, :]
    v2 = v_ref[0:kext, :]
    nbp = max(nb, MOBA_GATE_ROWS)
    blk_row = lax.broadcasted_iota(I32, (nbp, kext), 0)
    blk_of_key = lax.broadcasted_iota(I32, (nbp, kext), 1) // bs
    member = jnp.where(blk_row == blk_of_key, 1.0, 0.0).astype(BF16)
    kmean = (_dot(member, k2) * (1.0 / bs)).astype(BF16)
    n_iota = lax.broadcasted_iota(I32, (nbp, 1), 0)
    past = n_iota < qi
    qrow = lax.broadcasted_iota(I32, (bs, 1), 0)
    kcol = lax.broadcasted_iota(I32, (1, kext), 1)
    own = (kcol >= qi * bs) & (kcol <= qi * bs + qrow)
    def block_masks():
        gates = [jnp.where(past, _dot_nt(kmean, jnp.where(_lane_group((1, LANES), HEAD_DIM, half), q2,
                                                          jnp.zeros_like(q2))), NEG_INF)
                 for half in range(2)]
        chosen = []
        for gate in gates:
            rank = jnp.zeros((nbp, bs), I32)
            for m in range(nb):
                gm = gate[m:m + 1, :]
                beats = (gm > gate) | ((gm == gate) & (m < n_iota))
                rank = rank + beats.astype(I32)
            chosen.append(jnp.transpose(jnp.where(past & (rank < kt), 1.0, 0.0)).astype(BF16))
        return [(_dot(c, member) > 0.5) | own for c in chosen]

    (_, den, num), = _head_pair_attention([(q2, k2, v2, block_masks)])
    o_ref[...] = (num / den).astype(o_ref.dtype)


def _moba_kernel(*refs, seq):
    qi = pl.program_id(2)
    kt = min(MOBA_TOPK, seq // MOBA_BLOCK - 1)
    blocks_per_step = ATTN_KEY_STEP // MOBA_BLOCK
    for v in range(seq // ATTN_KEY_STEP):
        @pl.when(qi // blocks_per_step == v)
        def _(v=v):
            _moba_block(qi, *refs, kext=(v + 1) * ATTN_KEY_STEP, kt=kt)


def _moba_attention(q, k, v, batch, seq):
    n, width = q.shape
    bs = MOBA_BLOCK
    nb = seq // bs
    rows = pl.BlockSpec((bs, LANES), lambda b, hp, qi: (b * nb + qi, hp))
    whole = pl.BlockSpec((seq, LANES), lambda b, hp, qi: (b, hp))
    return pl.pallas_call(
        functools.partial(_moba_kernel, seq=seq),
        grid=(batch, width // LANES, nb),
        in_specs=[rows, whole, whole],
        out_specs=rows,
        out_shape=jax.ShapeDtypeStruct((n, width), BF16),
        compiler_params=_params(("parallel", "parallel", "parallel")),
    )(q, k, v)


def _rows(start, size, stride):
    return pl.ds(start, size) if stride == 1 else pl.ds(start, size, stride=stride)


def _largest_divisor(n, limit):
    return max(d for d in range(1, limit + 1) if n % d == 0)


def _repeat(trips, body):
    if trips == 1:
        body(0, 0)
    else:
        lax.fori_loop(0, trips, body, 0)


def _dilated_kernel(q_ref, k_ref, v_ref, o_ref, qp_ref, kp_ref, vp_ref, nat_ref, slab_ref, *, seq):
    blk = BAND_BLOCK
    ns = DILATED_SLABS
    slab_len = seq // ns

    for s in range(ns):
        rows = slice(s * slab_len, (s + 1) * slab_len)
        qp_ref[rows, :] = q_ref[pl.ds(s, slab_len, stride=ns), :]
        kp_ref[rows, :] = k_ref[pl.ds(s, slab_len, stride=ns), :]
        vp_ref[rows, :] = v_ref[pl.ds(s, slab_len, stride=ns), :]

    def operands(srcs, qstart, kstart, nk, stride):
        ksl = _rows(kstart, nk, stride)
        i = lax.broadcasted_iota(I32, (blk, 1), 0)
        j = lax.broadcasted_iota(I32, (1, nk), 1)
        if nk == blk:
            mask = j <= i
        else:
            mask = (j >= i) & (j <= i + blk)
        return (srcs[0][_rows(qstart, blk, stride), :].astype(BF16), srcs[1][ksl, :].astype(BF16),
                srcs[2][ksl, :].astype(BF16), mask)

    def merged(old, new):
        (m_old, l_old, a_old), (m_new, l_new, a_new) = old, new
        m_tot = jnp.maximum(m_old, m_new)
        w_old = jnp.exp2(m_old - m_tot)
        w_new = jnp.exp2(m_new - m_tot)
        return m_tot, w_old * l_old + w_new * l_new, w_old * a_old + w_new * a_new

    def run_units(srcs, state_ref, units, nk, stride, merge):
        results = _head_pair_attention([operands(srcs, qstart, kstart, nk, stride) for qstart, kstart in units])
        for (qstart, _), new in zip(units, results):
            qsl = _rows(qstart, blk, stride)
            if merge:
                new = merged(tuple(state_ref[t, qsl, :] for t in range(3)), new)
            for t in range(3):
                state_ref[t, qsl, :] = new[t]

    written = set()
    for window, dil in DILATED_CFG:
        assert window // dil == blk
        if dil % ns == 0:
            srcs, state_ref, stride, group_len = (qp_ref, kp_ref, vp_ref), slab_ref, dil // ns, slab_len
        else:
            srcs, state_ref, stride, group_len = (q_ref, k_ref, v_ref), nat_ref, dil, seq
        merge = id(state_ref) in written
        written.add(id(state_ref))
        nblk = seq // dil // blk
        ub = _largest_divisor(dil, DILATED_UNITS)

        def class_start(c, stride=stride, group_len=group_len):
            return (c // stride) * group_len + c % stride if group_len != seq else c

        def first_blocks(it, carry, srcs=srcs, state_ref=state_ref, stride=stride, merge=merge, ub=ub,
                         class_start=class_start):
            starts = [class_start(it * ub + u) for u in range(ub)]
            run_units(srcs, state_ref, [(st, st) for st in starts], blk, stride, merge)
            return carry

        _repeat(dil // ub, first_blocks)
        if nblk > 1:
            ul = _largest_divisor(nblk - 1, DILATED_UNITS)
            per_class = (nblk - 1) // ul

            def later_blocks(it, carry, srcs=srcs, state_ref=state_ref, stride=stride, merge=merge, ul=ul,
                             per_class=per_class, class_start=class_start):
                base, grp = class_start(it // per_class), it % per_class
                qstarts = [base + stride * blk * (1 + grp * ul + u) for u in range(ul)]
                run_units(srcs, state_ref, [(qs, qs - stride * blk) for qs in qstarts], 2 * blk, stride, merge)
                return carry

            _repeat(dil * per_class, later_blocks)

    for s in range(ns):
        nat_rows = pl.ds(s, slab_len, stride=ns)
        slab_rows = slice(s * slab_len, (s + 1) * slab_len)
        _, den, num = merged(tuple(nat_ref[t, nat_rows, :] for t in range(3)),
                             tuple(slab_ref[t, slab_rows, :] for t in range(3)))
        nat_ref[2, nat_rows, :] = num / den
    o_ref[...] = nat_ref[2].astype(o_ref.dtype)


def _dilated_attention(q, k, v, batch, seq):
    n, width = q.shape
    for window, dil in DILATED_CFG:
        assert seq % (dil * BAND_BLOCK) == 0
    assert {dil % DILATED_SLABS == 0 for _, dil in DILATED_CFG} == {True, False}
    whole = pl.BlockSpec((seq, LANES), lambda b, hp: (b, hp))
    return pl.pallas_call(
        functools.partial(_dilated_kernel, seq=seq),
        grid=(batch, width // LANES),
        in_specs=[whole, whole, whole],
        out_specs=whole,
        out_shape=jax.ShapeDtypeStruct((n, width), BF16),
        scratch_shapes=[pltpu.VMEM((seq, LANES), F32)] * 3 + [pltpu.VMEM((3, seq, LANES), F32)] * 2,
        compiler_params=_params(("parallel", "parallel")),
    )(q, k, v)


def _deepnorm(x, f, g, b):
    y = DEEPNORM_ALPHA * x + f
    mu = jnp.mean(y, axis=1, keepdims=True)
    yc = y - mu
    var = jnp.mean(yc * yc, axis=1, keepdims=True)
    return yc * lax.rsqrt(var + LN_EPS) * g + b


def _outproj_ln_kernel(*refs, n_parts):
    part_refs = refs[:n_parts]
    w_ref, x_ref, g_ref, b_ref, o_ref, oh_ref = refs[n_parts:]
    mix = None
    start = 0
    for p_ref in part_refs:
        width = p_ref.shape[1]
        t = _dot(p_ref[...], w_ref[start:start + width, :])
        mix = t if mix is None else mix + t
        start += width
    y = _deepnorm(x_ref[...], mix, g_ref[...], b_ref[...])
    o_ref[...] = y
    oh_ref[...] = y.astype(BF16)


def _outproj_ln(parts, w, x2d, g, b):
    n, d = x2d.shape
    tm = PROJ_ROWS
    row = lambda width: pl.BlockSpec((tm, width), lambda i: (i, 0))
    vec = pl.BlockSpec((1, d), lambda i: (0, 0))
    return pl.pallas_call(
        functools.partial(_outproj_ln_kernel, n_parts=len(parts)),
        grid=(n // tm,),
        in_specs=[row(p.shape[1]) for p in parts] + [pl.BlockSpec(w.shape, lambda i: (0, 0)), row(d), vec, vec],
        out_specs=[row(d), row(d)],
        out_shape=[jax.ShapeDtypeStruct((n, d), F32), jax.ShapeDtypeStruct((n, d), BF16)],
        compiler_params=_params(("parallel",)),
    )(*parts, w, x2d, g.reshape(1, d), b.reshape(1, d))


def _silu(h):
    return h * (1.0 / (1.0 + jnp.exp(-h)))


def _swiglu_ln_kernel(xh_ref, x_ref, w1_ref, w3_ref, w2_ref, g_ref, b_ref, o_ref, acc_ref):
    f = pl.program_id(1)
    xh = xh_ref[...]
    hid = (_silu(_dot(xh, w1_ref[...])) * _dot(xh, w3_ref[...])).astype(BF16)
    y = _dot(hid, w2_ref[...])

    @pl.when(f == 0)
    def _():
        acc_ref[...] = y

    @pl.when(f > 0)
    def _():
        acc_ref[...] += y

    @pl.when(f == pl.num_programs(1) - 1)
    def _():
        o_ref[...] = _deepnorm(x_ref[...], acc_ref[...], g_ref[...], b_ref[...])


def _swiglu_ln(xh, x2d, w1, w3, w2, g, b):
    n, d = x2d.shape
    dff = w1.shape[1]
    tm = FFN_ROWS
    nf = 2 if dff % (2 * LANES) == 0 else 1
    fc = dff // nf
    row = lambda: pl.BlockSpec((tm, d), lambda i, f: (i, 0))
    vec = pl.BlockSpec((1, d), lambda i, f: (0, 0))
    return pl.pallas_call(
        _swiglu_ln_kernel,
        grid=(n // tm, nf),
        in_specs=[row(), row(),
                  pl.BlockSpec((d, fc), lambda i, f: (0, f)),
                  pl.BlockSpec((d, fc), lambda i, f: (0, f)),
                  pl.BlockSpec((fc, d), lambda i, f: (f, 0)),
                  vec, vec],
        out_specs=row(),
        out_shape=jax.ShapeDtypeStruct((n, d), F32),
        scratch_shapes=[pltpu.VMEM((tm, d), F32)],
        compiler_params=_params(("parallel", "arbitrary")),
    )(xh, x2d, w1, w3, w2, g.reshape(1, d), b.reshape(1, d))


def _add_ln_kernel(x_ref, f_ref, g_ref, b_ref, o_ref):
    o_ref[...] = _deepnorm(x_ref[...], f_ref[...], g_ref[...], b_ref[...])


def _add_ln(x2d, f2d, g, b):
    n, d = x2d.shape
    tm = PROJ_ROWS
    row = pl.BlockSpec((tm, d), lambda i: (i, 0))
    vec = pl.BlockSpec((1, d), lambda i: (0, 0))
    return pl.pallas_call(
        _add_ln_kernel,
        grid=(n // tm,),
        in_specs=[row, row, vec, vec],
        out_specs=row,
        out_shape=jax.ShapeDtypeStruct((n, d), F32),
        compiler_params=_params(("parallel",)),
    )(x2d, f2d, g.reshape(1, d), b.reshape(1, d))


def _router_kernel(xh_ref, rw_ref, gate_ref, lposc_ref, lpost_ref, cum_ref, carry_ref, *, tiles_per_group):
    tm = xh_ref.shape[0]
    i = pl.program_id(0)

    @pl.when(i % tiles_per_group == 0)
    def _():
        carry_ref[...] = jnp.zeros_like(carry_ref)

    lane = lax.broadcasted_iota(I32, (1, LANES), 1)
    logits = jnp.where(lane < N_EXPERTS, _dot(xh_ref[...], rw_ref[...]), NEG_INF)
    m1 = jnp.max(logits, axis=1, keepdims=True)
    i1 = jnp.min(jnp.where(logits == m1, lane, LANES), axis=1, keepdims=True)
    rest = jnp.where(lane == i1, NEG_INF, logits)
    m2 = jnp.max(rest, axis=1, keepdims=True)
    i2 = jnp.min(jnp.where(rest == m2, lane, LANES), axis=1, keepdims=True)
    e2 = jnp.exp(m2 - m1)
    g1 = 1.0 / (1.0 + e2)
    g2 = e2 / (1.0 + e2)
    sel1 = lane == i1
    sel2 = lane == i2
    routed = sel1 | sel2
    gate_ref[...] = jnp.where(sel1, g1, jnp.where(sel2, g2, 0.0))
    routedf = jnp.where(routed, 1.0, 0.0)
    earlier = (lax.broadcasted_iota(I32, (tm, tm), 1) < lax.broadcasted_iota(I32, (tm, tm), 0))
    earlier = jnp.where(earlier, 1.0, 0.0).astype(BF16)
    carry = carry_ref[...]
    lpos = jnp.where(routed, _dot(earlier, routedf.astype(BF16)) + carry, -1.0)
    lposc_ref[...] = lpos
    lpost_ref[...] = jnp.transpose(lpos)[0:N_EXPERTS, :]
    carry = carry + jnp.sum(routedf, axis=0, keepdims=True)
    carry_ref[...] = carry
    cum_ref[...] = carry.astype(I32).reshape(1, 1, LANES)


def _router(xh, rw):
    n, d = xh.shape
    tm = ROUTER_ROWS
    nt = n // tm
    row = pl.BlockSpec((tm, LANES), lambda i: (i, 0))
    return pl.pallas_call(
        functools.partial(_router_kernel, tiles_per_group=MOE_GROUP // tm),
        grid=(nt,),
        in_specs=[pl.BlockSpec((tm, d), lambda i: (i, 0)), pl.BlockSpec(rw.shape, lambda i: (0, 0))],
        out_specs=[row, row,
                   pl.BlockSpec((N_EXPERTS, tm), lambda i: (0, i)),
                   pl.BlockSpec((1, 1, LANES), lambda i: (i, 0, 0))],
        out_shape=[jax.ShapeDtypeStruct((n, LANES), F32), jax.ShapeDtypeStruct((n, LANES), F32),
                   jax.ShapeDtypeStruct((N_EXPERTS, n), F32), jax.ShapeDtypeStruct((nt, 1, LANES), I32)],
        scratch_shapes=[pltpu.VMEM((1, LANES), F32)],
        compiler_params=_params(("arbitrary",)),
    )(xh, rw)


def _moe_kernel(bnd_ref, xh_ref, lpost_ref, lposc_ref, gatec_ref, w1_ref, w3_ref, w2_ref,
                o_ref, xg_ref, yacc_ref):
    tg = xh_ref.shape[0]
    sub, ch = MOE_SUB, ROUTER_ROWS
    nch = tg // ch
    g, e, f = pl.program_id(0), pl.program_id(1), pl.program_id(2)
    nf = pl.num_programs(2)

    def bound(c):
        return bnd_ref[(g * (nch + 1) + c) * N_EXPERTS + e]

    n_active = (bound(nch) + sub - 1) // sub

    def overlaps(s, c):
        return (bound(c) < (s + 1) * sub) & (bound(c + 1) > s * sub)

    def row_block(s):
        return pl.ds(pl.multiple_of(s * sub, sub), sub)

    def tok_chunk(c):
        return pl.ds(pl.multiple_of(c * ch, ch), ch)

    @pl.when((e == 0) & (f == 0))
    def _():
        o_ref[...] = jnp.zeros_like(o_ref)

    @pl.when(f == 0)
    def _gather():
        def per_block(s, carry):
            rows = row_block(s)
            xg_ref[rows, :] = jnp.zeros((sub, xg_ref.shape[1]), xg_ref.dtype)
            yacc_ref[rows, :] = jnp.zeros((sub, yacc_ref.shape[1]), F32)
            want = (lax.broadcasted_iota(I32, (sub, 1), 0) + s * sub).astype(F32)

            def per_chunk(c, carry2):
                @pl.when(overlaps(s, c))
                def _():
                    lp = lpost_ref[pl.ds(e, 1), tok_chunk(c)]
                    pick = jnp.where(lp == want, 1.0, 0.0).astype(BF16)
                    xg_ref[rows, :] += _dot(pick, xh_ref[tok_chunk(c), :]).astype(xg_ref.dtype)
                return carry2

            return lax.fori_loop(0, nch, per_chunk, carry)

        lax.fori_loop(0, n_active, per_block, 0)

    def ffn_block(s, carry):
        rows = row_block(s)
        xs = xg_ref[rows, :]
        hid = (_silu(_dot(xs, w1_ref[...])) * _dot(xs, w3_ref[...])).astype(BF16)
        yacc_ref[rows, :] += _dot(hid, w2_ref[...])
        return carry

    lax.fori_loop(0, n_active, ffn_block, 0)

    @pl.when(f == nf - 1)
    def _scatter():
        lane = lax.broadcasted_iota(I32, (1, LANES), 1)

        def per_block(s, carry):
            y = yacc_ref[row_block(s), :]
            y_hi = y.astype(BF16)
            y_lo = (y - y_hi.astype(F32)).astype(BF16)
            want = (lax.broadcasted_iota(I32, (1, sub), 1) + s * sub).astype(F32)

            def per_chunk(c, carry2):
                @pl.when(overlaps(s, c))
                def _():
                    toks = tok_chunk(c)
                    lp = jnp.sum(jnp.where(lane == e, lposc_ref[toks, :], 0.0), axis=1, keepdims=True)
                    gt = jnp.sum(jnp.where(lane == e, gatec_ref[toks, :], 0.0), axis=1, keepdims=True)
                    place = jnp.where(lp == want, 1.0, 0.0).astype(BF16)
                    o_ref[toks, :] += gt * (_dot(place, y_hi) + _dot(place, y_lo))
                return carry2

            return lax.fori_loop(0, nch, per_chunk, carry)

        lax.fori_loop(0, n_active, per_block, 0)


def _chunk_bounds(cum):
    nch = MOE_GROUP // ROUTER_ROWS
    per_group = cum[:, 0, :N_EXPERTS].reshape(-1, nch, N_EXPERTS)
    return jnp.pad(per_group, ((0, 0), (1, 0), (0, 0))).reshape(-1)


def _moe(bounds, xh, lpost, lposc, gatec, w1, w3, w2, layer):
    n, d = xh.shape
    tg = MOE_GROUP
    dff = w1.shape[3]
    fc = MOE_FF_CHUNK
    cap = -(-tg // MOE_SUB) * MOE_SUB
    grid_spec = pltpu.PrefetchScalarGridSpec(
        num_scalar_prefetch=1,
        grid=(n // tg, N_EXPERTS, dff // fc),
        in_specs=[pl.BlockSpec((tg, d), lambda g, e, f, c: (g, 0)),
                  pl.BlockSpec((N_EXPERTS, tg), lambda g, e, f, c: (0, g)),
                  pl.BlockSpec((tg, LANES), lambda g, e, f, c: (g, 0)),
                  pl.BlockSpec((tg, LANES), lambda g, e, f, c: (g, 0)),
                  pl.BlockSpec((None, None, d, fc), lambda g, e, f, c: (layer, e, 0, f)),
                  pl.BlockSpec((None, None, d, fc), lambda g, e, f, c: (layer, e, 0, f)),
                  pl.BlockSpec((None, None, fc, d), lambda g, e, f, c: (layer, e, f, 0))],
        out_specs=pl.BlockSpec((tg, d), lambda g, e, f, c: (g, 0)),
        scratch_shapes=[pltpu.VMEM((cap, d), BF16), pltpu.VMEM((cap, d), F32)],
    )
    return pl.pallas_call(
        _moe_kernel,
        grid_spec=grid_spec,
        out_shape=jax.ShapeDtypeStruct((n, d), F32),
        compiler_params=_params(("parallel", "arbitrary", "arbitrary")),
    )(bounds, xh, lpost, lposc, gatec, w1, w3, w2)


def _even_weight_layout(w_in):
    hd = HEAD_DIM
    widths = (A_HEADS * hd, hd, hd, IDX_HEADS * IDX_DIM, IDX_DIM, IDX_HEADS, B_HEADS * hd, B_HEADS * hd, B_HEADS * hd)
    offs = [0]
    for wd in widths:
        offs.append(offs[-1] + wd)
    qa, ka, va, iq, ik, iw, qb, kb, vb = (w_in[:, offs[j]:offs[j + 1]] for j in range(9))
    iw_pad = jnp.pad(iw, ((0, 0), (0, LANES - IDX_HEADS)))
    w = jnp.concatenate([qa, qb, kb, vb, ka, ka, va, va, iq, ik, ik, ik, ik, iw_pad], axis=1)
    mixw = B_HEADS * hd
    sections = ((0, A_HEADS * hd, hd, QK_SCALE),
                (512, mixw, hd, QK_SCALE),
                (1024, mixw, hd, None),
                (1536, mixw, None, None),
                (2048, LANES, hd, None),
                (2176, LANES, None, None),
                (2304, IDX_HEADS * IDX_DIM, IDX_DIM, None),
                (2560, LANES, IDX_DIM, None),
                (2688, LANES, None, None))
    dtypes = (BF16,) * 8 + (F32,)
    return w.astype(BF16), sections, dtypes


def kernel(x, even_w_in, even_w_out, even_ln1_g, even_ln1_b, even_w1, even_w3, even_w2, even_ln2_g, even_ln2_b, odd_w_in, odd_w_out, odd_ln1_g, odd_ln1_b, odd_router, odd_w1, odd_w3, odd_w2, odd_ln2_g, odd_ln2_b):
    batch, seq, d = x.shape
    n = batch * seq
    tables = _rope_tables(seq)
    x2d = x.reshape(n, d)
    odd_w1h, odd_w3h, odd_w2h = odd_w1.astype(BF16), odd_w3.astype(BF16), odd_w2.astype(BF16)
    for layer in range(DEPTH):
        i = layer // 2
        if layer % 2 == 0:
            w, sections, dtypes = _even_weight_layout(even_w_in[i])
            qa, qb, kb, vb, kk, vv, iq, ik4, iw = _project(x2d, w, tables, sections, dtypes, seq)
            o_a = _dsa_attention(iq, iw, ik4, qa, kk, vv, batch, seq)
            o_b = _moba_attention(qb, kb, vb, batch, seq)
            x2d, xh = _outproj_ln([o_a, o_b], even_w_out[i].astype(BF16), x2d, even_ln1_g[i], even_ln1_b[i])
            x2d = _swiglu_ln(xh, x2d, even_w1[i].astype(BF16), even_w3[i].astype(BF16), even_w2[i].astype(BF16),
                             even_ln2_g[i], even_ln2_b[i])
        else:
            mix = C_HEADS * HEAD_DIM
            sections = ((0, mix, HEAD_DIM, QK_SCALE), (mix, mix, HEAD_DIM, None), (2 * mix, mix, None, None))
            q, k, v = _project(x2d, odd_w_in[i].astype(BF16), tables, sections, (F32, F32, F32), seq)
            o = _dilated_attention(q, k, v, batch, seq)
            x2d, xh = _outproj_ln([o], odd_w_out[i].astype(BF16), x2d, odd_ln1_g[i], odd_ln1_b[i])
            rw = jnp.pad(odd_router[i], ((0, 0), (0, LANES - N_EXPERTS))).astype(BF16)
            gatec, lposc, lpost, cum = _router(xh, rw)
            ffn = _moe(_chunk_bounds(cum), xh, lpost, lposc, gatec, odd_w1h, odd_w3h, odd_w2h, i)
            x2d = _add_ln(x2d, ffn, odd_ln2_g[i], odd_ln2_b[i])
    return x2d.reshape(batch, seq, d)
```

```python
import functools

import jax
import jax.numpy as jnp
from jax import lax
from jax.experimental import pallas as pl
from jax.experimental.pallas import tpu as pltpu

F32 = jnp.float32
BF16 = jnp.bfloat16
I32 = jnp.int32

DEPTH = 4
HEAD_DIM = 64
ROPE_THETA = 10000.0
LN_EPS = 1e-5
A_HEADS = 8
IDX_HEADS = 8
IDX_DIM = 32
DSA_TOPK = 256
DSA_Q_BLOCK = 128
B_HEADS = 8
MOBA_BLOCK = 256
MOBA_TOPK = 3
C_HEADS = 16
DILATED_CFG = ((128, 1), (512, 4), (2048, 16))
BAND_BLOCK = 128
N_EXPERTS = 8
DEEPNORM_ALPHA = (2 * DEPTH) ** 0.25
QK_SCALE = HEAD_DIM ** -0.5 * 1.4426950408889634

LANES = 128
VMEM_LIMIT_BYTES = 52 * 1024 * 1024

PROJ_ROWS = 512
FFN_ROWS = 512
ROUTER_ROWS = 512
MOE_GROUP = 2048
MOE_SUB = 256
MOE_FF_CHUNK = 512
DSA_SEARCH_GROUPS = 4
DSA_SEARCH_UNROLL = 4
MOBA_GATE_ROWS = 16
DILATED_UNITS = 4
DILATED_SLABS = 4
ATTN_KEY_STEP = 512

NEG_INF = float("-inf")
NEG_INF_KEY = -2139095041


def _params(semantics):
    return pltpu.CompilerParams(dimension_semantics=semantics, vmem_limit_bytes=VMEM_LIMIT_BYTES)


def _dot(a, b):
    return jnp.dot(a, b, preferred_element_type=F32)


def _dot_nt(a, b):
    return lax.dot_general(a, b, (((1,), (1,)), ((), ())), preferred_element_type=F32)


def _lane_group(shape, group, idx):
    lane = lax.broadcasted_iota(I32, shape, len(shape) - 1)
    return (lane // group) == idx


def _head_pair_attention(units):
    first = _lane_group((1, LANES), HEAD_DIM, 0)
    halves = (first, jnp.logical_not(first))
    raw = [_dot_nt(jnp.where(mine, q2, jnp.zeros_like(q2)), k2) for q2, k2, _, _ in units for mine in halves]
    scores = []
    for u, (_, _, _, mask) in enumerate(units):
        if callable(mask):
            mask = mask()
        masks = mask if isinstance(mask, (tuple, list)) else (mask, mask)
        for h in range(2):
            if masks[h].dtype == jnp.bool_:
                scores.append(jnp.where(masks[h], raw[2 * u + h], NEG_INF))
            else:
                scores.append(raw[2 * u + h] + masks[h])
    maxes = [jnp.max(s, axis=1, keepdims=True) for s in scores]
    weights = [jnp.exp2(s - mx).astype(BF16) for s, mx in zip(scores, maxes)]
    results = []
    for u, (_, _, v2, _) in enumerate(units):
        ones = jnp.ones_like(v2)
        pv0 = _dot(weights[2 * u], jnp.where(first, v2, ones))
        pv1 = _dot(weights[2 * u + 1], jnp.where(first, ones, v2))
        den = jnp.where(first, pv0[:, LANES - 1:LANES], pv1[:, 0:1])
        results.append((jnp.where(first, maxes[2 * u], maxes[2 * u + 1]), den, jnp.where(first, pv0, pv1)))
    return results


def _rope_lanes(t, cosf, sinf, half):
    lane = lax.broadcasted_iota(I32, t.shape, 1)
    first = (lane % (2 * half)) < half
    swapped = jnp.where(first, pltpu.roll(t, LANES - half, 1), pltpu.roll(t, half, 1))
    return t * cosf + swapped * sinf


def _proj_kernel(x_ref, w_ref, c64_ref, s64_ref, c32_ref, s32_ref, *out_refs, sections):
    x = x_ref[...].astype(BF16)
    for o_ref, (start, width, rope, scale) in zip(out_refs, sections):
        h = _dot(x, w_ref[:, start:start + width])
        if rope is not None:
            cosf = (c64_ref if rope == HEAD_DIM else c32_ref)[...]
            sinf = (s64_ref if rope == HEAD_DIM else s32_ref)[...]
            parts = [_rope_lanes(h[:, g * LANES:(g + 1) * LANES], cosf, sinf, rope // 2)
                     for g in range(width // LANES)]
            h = parts[0] if len(parts) == 1 else jnp.concatenate(parts, axis=1)
        if scale is not None:
            h = h * scale
        o_ref[...] = h.astype(o_ref.dtype)


def _project(x2d, w, tables, sections, out_dtypes, seq):
    n, d = x2d.shape
    tm = PROJ_ROWS
    pos_blocks = seq // tm
    tab_spec = pl.BlockSpec((tm, LANES), lambda i: (i % pos_blocks, 0))
    return pl.pallas_call(
        functools.partial(_proj_kernel, sections=sections),
        grid=(n // tm,),
        in_specs=[pl.BlockSpec((tm, d), lambda i: (i, 0)),
                  pl.BlockSpec(w.shape, lambda i: (0, 0)),
                  tab_spec, tab_spec, tab_spec, tab_spec],
        out_specs=[pl.BlockSpec((tm, sec[1]), lambda i: (i, 0)) for sec in sections],
        out_shape=[jax.ShapeDtypeStruct((n, sec[1]), dt) for sec, dt in zip(sections, out_dtypes)],
        compiler_params=_params(("parallel",)),
    )(x2d, w, *tables)


def _rope_tables(seq):
    out = []
    for dim in (HEAD_DIM, IDX_DIM):
        inv = ROPE_THETA ** (-jnp.arange(0, dim, 2, dtype=F32) / dim)
        ang = jnp.arange(seq, dtype=F32)[:, None] * inv[None, :]
        cos, sin = jnp.cos(ang), jnp.sin(ang)
        reps = LANES // dim
        out.append(jnp.tile(jnp.concatenate([cos, cos], -1), (1, reps)))
        out.append(jnp.tile(jnp.concatenate([-sin, sin], -1), (1, reps)))
    return out


def _dsa_block(i, iq_ref, iw_ref, ik_ref, q_ref, kk_ref, vv_ref, o_ref, key_ref, sel_ref, *, kext, topk):
    qb = DSA_Q_BLOCK
    keys = slice(0, kext)
    qpos = i * qb + lax.broadcasted_iota(I32, (qb, 1), 0)
    kpos = lax.broadcasted_iota(I32, (1, kext), 1)
    causal = kpos <= qpos

    ik = ik_ref[keys, :]
    iw = iw_ref[...] * (IDX_DIM ** -0.5 * IDX_HEADS ** -0.5)
    heads_per_group = LANES // IDX_DIM
    score = jnp.zeros((qb, kext), F32)
    for h in range(IDX_HEADS):
        grp, j = divmod(h, heads_per_group)
        iqg = iq_ref[:, grp * LANES:(grp + 1) * LANES]
        iqm = jnp.where(_lane_group((1, LANES), IDX_DIM, j), iqg, jnp.zeros_like(iqg))
        rel = jnp.maximum(_dot_nt(iqm, ik), 0.0)
        score = score + rel * iw[:, h:h + 1]
    bits = pltpu.bitcast(score, I32)
    neg = bits >> 31
    int_min = jnp.int32(-2 ** 31)
    key_ref[:, keys] = jnp.where(causal, (bits ^ (neg & 0x7FFFFFFF)) - neg, int_min)

    grp_rows = qb // DSA_SEARCH_GROUPS
    groups = [slice(g * grp_rows, (g + 1) * grp_rows) for g in range(DSA_SEARCH_GROUPS)]

    def count_ge(rows, t):
        return jnp.sum(jnp.where(key_ref[rows, keys] >= t, 1.0, 0.0), axis=1, keepdims=True)

    def search(it, thrs):
        bit = jnp.left_shift(jnp.int32(1), 30 - it)
        return tuple(jnp.where(count_ge(rows, thr | bit) >= topk, thr | bit, thr)
                     for rows, thr in zip(groups, thrs))

    zero = jnp.zeros((grp_rows, 1), I32)
    thrs = tuple(jnp.where(count_ge(rows, zero) >= topk, zero, int_min) for rows in groups)
    thrs = lax.fori_loop(0, 31, search, thrs, unroll=DSA_SEARCH_UNROLL)
    thr = jnp.concatenate(thrs, axis=0)

    floor = jnp.maximum(thr, int_min + 1)
    key = key_ref[:, keys]
    sel_ref[:, keys] = jnp.where(key >= floor, 0.0, NEG_INF)
    n_ge = jnp.sum(jnp.where(key >= thr, 1.0, 0.0), axis=1, keepdims=True)
    tie_rows = (n_ge > topk) & (thr > int_min)

    @pl.when(jnp.max(tie_rows.astype(I32)) > 0)
    def _break_ties():
        upper = (lax.broadcasted_iota(I32, (LANES, LANES), 0)
                 < lax.broadcasted_iota(I32, (LANES, LANES), 1))
        upper = jnp.where(upper, 1.0, 0.0).astype(BF16)
        n_gt = jnp.sum(jnp.where(key_ref[:, keys] > thr, 1.0, 0.0), axis=1, keepdims=True)
        need = topk - n_gt
        before = jnp.zeros((qb, 1), F32)
        for c in range(kext // LANES):
            sl = slice(c * LANES, (c + 1) * LANES)
            keyc = key_ref[:, sl]
            eqc = keyc == thr
            eqf = jnp.where(eqc, 1.0, 0.0)
            rank = _dot(eqf.astype(BF16), upper) + before
            take = ((keyc > thr) | (eqc & (rank < need))) & (keyc > int_min)
            sel_ref[:, sl] = jnp.where(take, 0.0, NEG_INF)
            before = before + jnp.sum(eqf, axis=1, keepdims=True)

    selected = sel_ref[:, keys]
    kk = kk_ref[keys, :]
    vv = vv_ref[keys, :]
    pairs = [(q_ref[:, hp * LANES:(hp + 1) * LANES], kk, vv, selected) for hp in range(A_HEADS // 2)]
    for hp, (_, den, num) in enumerate(_head_pair_attention(pairs)):
        o_ref[:, hp * LANES:(hp + 1) * LANES] = (num / den).astype(o_ref.dtype)


def _dsa_kernel(*refs, seq, topk):
    i = pl.program_id(1)
    blocks_per_step = ATTN_KEY_STEP // DSA_Q_BLOCK
    for v in range(seq // ATTN_KEY_STEP):
        @pl.when(i // blocks_per_step == v)
        def _(v=v):
            _dsa_block(i, *refs, kext=(v + 1) * ATTN_KEY_STEP, topk=topk)


def _dsa_attention(iq, iw, ik4, qa, kk, vv, batch, seq):
    n = batch * seq
    qb = DSA_Q_BLOCK
    nq = seq // qb
    topk = min(DSA_TOPK, seq // 4)
    rows = lambda width: pl.BlockSpec((qb, width), lambda b, i: (b * nq + i, 0))
    whole = pl.BlockSpec((seq, LANES), lambda b, i: (b, 0))
    return pl.pallas_call(
        functools.partial(_dsa_kernel, seq=seq, topk=topk),
        grid=(batch, nq),
        in_specs=[rows(iq.shape[1]), rows(LANES), whole, rows(qa.shape[1]), whole, whole],
        out_specs=rows(qa.shape[1]),
        out_shape=jax.ShapeDtypeStruct((n, qa.shape[1]), BF16),
        scratch_shapes=[pltpu.VMEM((qb, seq), I32), pltpu.VMEM((qb, seq), F32)],
        compiler_params=_params(("parallel", "parallel")),
    )(iq, iw, ik4, qa, kk, vv)


def _moba_block(qi, q_ref, k_ref, v_ref, o_ref, *, kext, kt):
    bs = MOBA_BLOCK
    nb = kext // bs
    q2 = q_ref[...]
    k2 = k_ref[0:kext, :]
    v2 = v_ref[0:kext, :]
    nbp = max(nb, MOBA_GATE_ROWS)
    blk_row = lax.broadcasted_iota(I32, (nbp, kext), 0)
    blk_of_key = lax.broadcasted_iota(I32, (nbp, kext), 1) // bs
    member = jnp.where(blk_row == blk_of_key, 1.0, 0.0).astype(BF16)
    kmean = (_dot(member, k2) * (1.0 / bs)).astype(BF16)
    n_iota = lax.broadcasted_iota(I32, (nbp, 1), 0)
    past = n_iota < qi
    qrow = lax.broadcasted_iota(I32, (bs, 1), 0)
    kcol = lax.broadcasted_iota(I32, (1, kext), 1)
    own = (kcol >= qi * bs) & (kcol <= qi * bs + qrow)
    def block_masks():
        gates = [jnp.where(past, _dot_nt(kmean, jnp.where(_lane_group((1, LANES), HEAD_DIM, half), q2,
                                                          jnp.zeros_like(q2))), NEG_INF)
                 for half in range(2)]
        chosen = []
        for gate in gates:
            rank = jnp.zeros((nbp, bs), I32)
            for m in range(nb):
                gm = gate[m:m + 1, :]
                beats = (gm > gate) | ((gm == gate) & (m < n_iota))
                rank = rank + beats.astype(I32)
            chosen.append(jnp.transpose(jnp.where(past & (rank < kt), 1.0, 0.0)).astype(BF16))
        return [(_dot(c, member) > 0.5) | own for c in chosen]

    (_, den, num), = _head_pair_attention([(q2, k2, v2, block_masks)])
    o_ref[...] = (num / den).astype(o_ref.dtype)


def _moba_kernel(*refs, seq):
    qi = pl.program_id(2)
    kt = min(MOBA_TOPK, seq // MOBA_BLOCK - 1)
    blocks_per_step = ATTN_KEY_STEP // MOBA_BLOCK
    for v in range(seq // ATTN_KEY_STEP):
        @pl.when(qi // blocks_per_step == v)
        def _(v=v):
            _moba_block(qi, *refs, kext=(v + 1) * ATTN_KEY_STEP, kt=kt)


def _moba_attention(q, k, v, batch, seq):
    n, width = q.shape
    bs = MOBA_BLOCK
    nb = seq // bs
    rows = pl.BlockSpec((bs, LANES), lambda b, hp, qi: (b * nb + qi, hp))
    whole = pl.BlockSpec((seq, LANES), lambda b, hp, qi: (b, hp))
    return pl.pallas_call(
        functools.partial(_moba_kernel, seq=seq),
        grid=(batch, width // LANES, nb),
        in_specs=[rows, whole, whole],
        out_specs=rows,
        out_shape=jax.ShapeDtypeStruct((n, width), BF16),
        compiler_params=_params(("parallel", "parallel", "parallel")),
    )(q, k, v)


def _rows(start, size, stride):
    return pl.ds(start, size) if stride == 1 else pl.ds(start, size, stride=stride)


def _largest_divisor(n, limit):
    return max(d for d in range(1, limit + 1) if n % d == 0)


def _repeat(trips, body):
    if trips == 1:
        body(0, 0)
    else:
        lax.fori_loop(0, trips, body, 0)


def _dilated_kernel(q_ref, k_ref, v_ref, o_ref, qp_ref, kp_ref, vp_ref, nat_ref, slab_ref, *, seq):
    blk = BAND_BLOCK
    ns = DILATED_SLABS
    slab_len = seq // ns

    for s in range(ns):
        rows = slice(s * slab_len, (s + 1) * slab_len)
        qp_ref[rows, :] = q_ref[pl.ds(s, slab_len, stride=ns), :]
        kp_ref[rows, :] = k_ref[pl.ds(s, slab_len, stride=ns), :]
        vp_ref[rows, :] = v_ref[pl.ds(s, slab_len, stride=ns), :]

    def operands(srcs, qstart, kstart, nk, stride):
        ksl = _rows(kstart, nk, stride)
        i = lax.broadcasted_iota(I32, (blk, 1), 0)
        j = lax.broadcasted_iota(I32, (1, nk), 1)
        if nk == blk:
            mask = j <= i
        else:
            mask = (j >= i) & (j <= i + blk)
        return (srcs[0][_rows(qstart, blk, stride), :].astype(BF16), srcs[1][ksl, :].astype(BF16),
                srcs[2][ksl, :].astype(BF16), mask)

    def merged(old, new):
        (m_old, l_old, a_old), (m_new, l_new, a_new) = old, new
        m_tot = jnp.maximum(m_old, m_new)
        w_old = jnp.exp2(m_old - m_tot)
        w_new = jnp.exp2(m_new - m_tot)
        return m_tot, w_old * l_old + w_new * l_new, w_old * a_old + w_new * a_new

    def run_units(srcs, state_ref, units, nk, stride, merge):
        results = _head_pair_attention([operands(srcs, qstart, kstart, nk, stride) for qstart, kstart in units])
        for (qstart, _), new in zip(units, results):
            qsl = _rows(qstart, blk, stride)
            if merge:
                new = merged(tuple(state_ref[t, qsl, :] for t in range(3)), new)
            for t in range(3):
                state_ref[t, qsl, :] = new[t]

    written = set()
    for window, dil in DILATED_CFG:
        assert window // dil == blk
        if dil % ns == 0:
            srcs, state_ref, stride, group_len = (qp_ref, kp_ref, vp_ref), slab_ref, dil // ns, slab_len
        else:
            srcs, state_ref, stride, group_len = (q_ref, k_ref, v_ref), nat_ref, dil, seq
        merge = id(state_ref) in written
        written.add(id(state_ref))
        nblk = seq // dil // blk
        ub = _largest_divisor(dil, DILATED_UNITS)

        def class_start(c, stride=stride, group_len=group_len):
            return (c // stride) * group_len + c % stride if group_len != seq else c

        def first_blocks(it, carry, srcs=srcs, state_ref=state_ref, stride=stride, merge=merge, ub=ub,
                         class_start=class_start):
            starts = [class_start(it * ub + u) for u in range(ub)]
            run_units(srcs, state_ref, [(st, st) for st in starts], blk, stride, merge)
            return carry

        _repeat(dil // ub, first_blocks)
        if nblk > 1:
            ul = _largest_divisor(nblk - 1, DILATED_UNITS)
            per_class = (nblk - 1) // ul

            def later_blocks(it, carry, srcs=srcs, state_ref=state_ref, stride=stride, merge=merge, ul=ul,
                             per_class=per_class, class_start=class_start):
                base, grp = class_start(it // per_class), it % per_class
                qstarts = [base + stride * blk * (1 + grp * ul + u) for u in range(ul)]
                run_units(srcs, state_ref, [(qs, qs - stride * blk) for qs in qstarts], 2 * blk, stride, merge)
                return carry

            _repeat(dil * per_class, later_blocks)

    for s in range(ns):
        nat_rows = pl.ds(s, slab_len, stride=ns)
        slab_rows = slice(s * slab_len, (s + 1) * slab_len)
        _, den, num = merged(tuple(nat_ref[t, nat_rows, :] for t in range(3)),
                             tuple(slab_ref[t, slab_rows, :] for t in range(3)))
        nat_ref[2, nat_rows, :] = num / den
    o_ref[...] = nat_ref[2].astype(o_ref.dtype)


def _dilated_attention(q, k, v, batch, seq):
    n, width = q.shape
    for window, dil in DILATED_CFG:
        assert seq % (dil * BAND_BLOCK) == 0
    assert {dil % DILATED_SLABS == 0 for _, dil in DILATED_CFG} == {True, False}
    whole = pl.BlockSpec((seq, LANES), lambda b, hp: (b, hp))
    return pl.pallas_call(
        functools.partial(_dilated_kernel, seq=seq),
        grid=(batch, width // LANES),
        in_specs=[whole, whole, whole],
        out_specs=whole,
        out_shape=jax.ShapeDtypeStruct((n, width), BF16),
        scratch_shapes=[pltpu.VMEM((seq, LANES), F32)] * 3 + [pltpu.VMEM((3, seq, LANES), F32)] * 2,
        compiler_params=_params(("parallel", "parallel")),
    )(q, k, v)


def _deepnorm(x, f, g, b):
    y = DEEPNORM_ALPHA * x + f
    mu = jnp.mean(y, axis=1, keepdims=True)
    yc = y - mu
    var = jnp.mean(yc * yc, axis=1, keepdims=True)
    return yc * lax.rsqrt(var + LN_EPS) * g + b


def _outproj_ln_kernel(*refs, n_parts):
    part_refs = refs[:n_parts]
    w_ref, x_ref, g_ref, b_ref, o_ref, oh_ref = refs[n_parts:]
    mix = None
    start = 0
    for p_ref in part_refs:
        width = p_ref.shape[1]
        t = _dot(p_ref[...], w_ref[start:start + width, :])
        mix = t if mix is None else mix + t
        start += width
    y = _deepnorm(x_ref[...], mix, g_ref[...], b_ref[...])
    o_ref[...] = y
    oh_ref[...] = y.astype(BF16)


def _outproj_ln(parts, w, x2d, g, b):
    n, d = x2d.shape
    tm = PROJ_ROWS
    row = lambda width: pl.BlockSpec((tm, width), lambda i: (i, 0))
    vec = pl.BlockSpec((1, d), lambda i: (0, 0))
    return pl.pallas_call(
        functools.partial(_outproj_ln_kernel, n_parts=len(parts)),
        grid=(n // tm,),
        in_specs=[row(p.shape[1]) for p in parts] + [pl.BlockSpec(w.shape, lambda i: (0, 0)), row(d), vec, vec],
        out_specs=[row(d), row(d)],
        out_shape=[jax.ShapeDtypeStruct((n, d), F32), jax.ShapeDtypeStruct((n, d), BF16)],
        compiler_params=_params(("parallel",)),
    )(*parts, w, x2d, g.reshape(1, d), b.reshape(1, d))


def _silu(h):
    return h * (1.0 / (1.0 + jnp.exp(-h)))


def _swiglu_ln_kernel(xh_ref, x_ref, w1_ref, w3_ref, w2_ref, g_ref, b_ref, o_ref, acc_ref):
    f = pl.program_id(1)
    xh = xh_ref[...]
    hid = (_silu(_dot(xh, w1_ref[...])) * _dot(xh, w3_ref[...])).astype(BF16)
    y = _dot(hid, w2_ref[...])

    @pl.when(f == 0)
    def _():
        acc_ref[...] = y

    @pl.when(f > 0)
    def _():
        acc_ref[...] += y

    @pl.when(f == pl.num_programs(1) - 1)
    def _():
        o_ref[...] = _deepnorm(x_ref[...], acc_ref[...], g_ref[...], b_ref[...])


def _swiglu_ln(xh, x2d, w1, w3, w2, g, b):
    n, d = x2d.shape
    dff = w1.shape[1]
    tm = FFN_ROWS
    nf = 2 if dff % (2 * LANES) == 0 else 1
    fc = dff // nf
    row = lambda: pl.BlockSpec((tm, d), lambda i, f: (i, 0))
    vec = pl.BlockSpec((1, d), lambda i, f: (0, 0))
    return pl.pallas_call(
        _swiglu_ln_kernel,
        grid=(n // tm, nf),
        in_specs=[row(), row(),
                  pl.BlockSpec((d, fc), lambda i, f: (0, f)),
                  pl.BlockSpec((d, fc), lambda i, f: (0, f)),
                  pl.BlockSpec((fc, d), lambda i, f: (f, 0)),
                  vec, vec],
        out_specs=row(),
        out_shape=jax.ShapeDtypeStruct((n, d), F32),
        scratch_shapes=[pltpu.VMEM((tm, d), F32)],
        compiler_params=_params(("parallel", "arbitrary")),
    )(xh, x2d, w1, w3, w2, g.reshape(1, d), b.reshape(1, d))


def _add_ln_kernel(x_ref, f_ref, g_ref, b_ref, o_ref):
    o_ref[...] = _deepnorm(x_ref[...], f_ref[...], g_ref[...], b_ref[...])


def _add_ln(x2d, f2d, g, b):
    n, d = x2d.shape
    tm = PROJ_ROWS
    row = pl.BlockSpec((tm, d), lambda i: (i, 0))
    vec = pl.BlockSpec((1, d), lambda i: (0, 0))
    return pl.pallas_call(
        _add_ln_kernel,
        grid=(n // tm,),
        in_specs=[row, row, vec, vec],
        out_specs=row,
        out_shape=jax.ShapeDtypeStruct((n, d), F32),
        compiler_params=_params(("parallel",)),
    )(x2d, f2d, g.reshape(1, d), b.reshape(1, d))


def _router_kernel(xh_ref, rw_ref, gate_ref, lposc_ref, lpost_ref, cum_ref, carry_ref, *, tiles_per_group):
    tm = xh_ref.shape[0]
    i = pl.program_id(0)

    @pl.when(i % tiles_per_group == 0)
    def _():
        carry_ref[...] = jnp.zeros_like(carry_ref)

    lane = lax.broadcasted_iota(I32, (1, LANES), 1)
    logits = jnp.where(lane < N_EXPERTS, _dot(xh_ref[...], rw_ref[...]), NEG_INF)
    m1 = jnp.max(logits, axis=1, keepdims=True)
    i1 = jnp.min(jnp.where(logits == m1, lane, LANES), axis=1, keepdims=True)
    rest = jnp.where(lane == i1, NEG_INF, logits)
    m2 = jnp.max(rest, axis=1, keepdims=True)
    i2 = jnp.min(jnp.where(rest == m2, lane, LANES), axis=1, keepdims=True)
    e2 = jnp.exp(m2 - m1)
    g1 = 1.0 / (1.0 + e2)
    g2 = e2 / (1.0 + e2)
    sel1 = lane == i1
    sel2 = lane == i2
    routed = sel1 | sel2
    gate_ref[...] = jnp.where(sel1, g1, jnp.where(sel2, g2, 0.0))
    routedf = jnp.where(routed, 1.0, 0.0)
    earlier = (lax.broadcasted_iota(I32, (tm, tm), 1) < lax.broadcasted_iota(I32, (tm, tm), 0))
    earlier = jnp.where(earlier, 1.0, 0.0).astype(BF16)
    carry = carry_ref[...]
    lpos = jnp.where(routed, _dot(earlier, routedf.astype(BF16)) + carry, -1.0)
    lposc_ref[...] = lpos
    lpost_ref[...] = jnp.transpose(lpos)[0:N_EXPERTS, :]
    carry = carry + jnp.sum(routedf, axis=0, keepdims=True)
    carry_ref[...] = carry
    cum_ref[...] = carry.astype(I32).reshape(1, 1, LANES)


def _router(xh, rw):
    n, d = xh.shape
    tm = ROUTER_ROWS
    nt = n // tm
    row = pl.BlockSpec((tm, LANES), lambda i: (i, 0))
    return pl.pallas_call(
        functools.partial(_router_kernel, tiles_per_group=MOE_GROUP // tm),
        grid=(nt,),
        in_specs=[pl.BlockSpec((tm, d), lambda i: (i, 0)), pl.BlockSpec(rw.shape, lambda i: (0, 0))],
        out_specs=[row, row,
                   pl.BlockSpec((N_EXPERTS, tm), lambda i: (0, i)),
                   pl.BlockSpec((1, 1, LANES), lambda i: (i, 0, 0))],
        out_shape=[jax.ShapeDtypeStruct((n, LANES), F32), jax.ShapeDtypeStruct((n, LANES), F32),
                   jax.ShapeDtypeStruct((N_EXPERTS, n), F32), jax.ShapeDtypeStruct((nt, 1, LANES), I32)],
        scratch_shapes=[pltpu.VMEM((1, LANES), F32)],
        compiler_params=_params(("arbitrary",)),
    )(xh, rw)


def _moe_kernel(bnd_ref, xh_ref, lpost_ref, lposc_ref, gatec_ref, w1_ref, w3_ref, w2_ref,
                o_ref, xg_ref, yacc_ref):
    tg = xh_ref.shape[0]
    sub, ch = MOE_SUB, ROUTER_ROWS
    nch = tg // ch
    g, e, f = pl.program_id(0), pl.program_id(1), pl.program_id(2)
    nf = pl.num_programs(2)

    def bound(c):
        return bnd_ref[(g * (nch + 1) + c) * N_EXPERTS + e]

    n_active = (bound(nch) + sub - 1) // sub

    def overlaps(s, c):
        return (bound(c) < (s + 1) * sub) & (bound(c + 1) > s * sub)

    def row_block(s):
        return pl.ds(pl.multiple_of(s * sub, sub), sub)

    def tok_chunk(c):
        return pl.ds(pl.multiple_of(c * ch, ch), ch)

    @pl.when((e == 0) & (f == 0))
    def _():
        o_ref[...] = jnp.zeros_like(o_ref)

    @pl.when(f == 0)
    def _gather():
        def per_block(s, carry):
            rows = row_block(s)
            xg_ref[rows, :] = jnp.zeros((sub, xg_ref.shape[1]), xg_ref.dtype)
            yacc_ref[rows, :] = jnp.zeros((sub, yacc_ref.shape[1]), F32)
            want = (lax.broadcasted_iota(I32, (sub, 1), 0) + s * sub).astype(F32)

            def per_chunk(c, carry2):
                @pl.when(overlaps(s, c))
                def _():
                    lp = lpost_ref[pl.ds(e, 1), tok_chunk(c)]
                    pick = jnp.where(lp == want, 1.0, 0.0).astype(BF16)
                    xg_ref[rows, :] += _dot(pick, xh_ref[tok_chunk(c), :]).astype(xg_ref.dtype)
                return carry2

            return lax.fori_loop(0, nch, per_chunk, carry)

        lax.fori_loop(0, n_active, per_block, 0)

    def ffn_rows(rows):
        xs = xg_ref[rows, :]
        hid = (_silu(_dot(xs, w1_ref[...])) * _dot(xs, w3_ref[...])).astype(BF16)
        yacc_ref[rows, :] += _dot(hid, w2_ref[...])

    def ffn_pair(p, carry):
        ffn_rows(pl.ds(pl.multiple_of(p * 2 * sub, 2 * sub), 2 * sub))
        return carry

    lax.fori_loop(0, n_active // 2, ffn_pair, 0)

    @pl.when(n_active % 2 == 1)
    def _():
        ffn_rows(row_block(n_active - 1))

    @pl.when(f == nf - 1)
    def _scatter():
        lane = lax.broadcasted_iota(I32, (1, LANES), 1)

        def per_block(s, carry):
            y = yacc_ref[row_block(s), :]
            y_hi = y.astype(BF16)
            y_lo = (y - y_hi.astype(F32)).astype(BF16)
            want = (lax.broadcasted_iota(I32, (1, sub), 1) + s * sub).astype(F32)

            def per_chunk(c, carry2):
                @pl.when(overlaps(s, c))
                def _():
                    toks = tok_chunk(c)
                    lp = jnp.sum(jnp.where(lane == e, lposc_ref[toks, :], 0.0), axis=1, keepdims=True)
                    gt = jnp.sum(jnp.where(lane == e, gatec_ref[toks, :], 0.0), axis=1, keepdims=True)
                    place = jnp.where(lp == want, 1.0, 0.0).astype(BF16)
                    o_ref[toks, :] += gt * (_dot(place, y_hi) + _dot(place, y_lo))
                return carry2

            return lax.fori_loop(0, nch, per_chunk, carry)

        lax.fori_loop(0, n_active, per_block, 0)


def _chunk_bounds(cum):
    nch = MOE_GROUP // ROUTER_ROWS
    per_group = cum[:, 0, :N_EXPERTS].reshape(-1, nch, N_EXPERTS)
    return jnp.pad(per_group, ((0, 0), (1, 0), (0, 0))).reshape(-1)


def _moe(bounds, xh, lpost, lposc, gatec, w1, w3, w2, layer):
    n, d = xh.shape
    tg = MOE_GROUP
    dff = w1.shape[3]
    fc = MOE_FF_CHUNK
    cap = -(-tg // MOE_SUB) * MOE_SUB
    grid_spec = pltpu.PrefetchScalarGridSpec(
        num_scalar_prefetch=1,
        grid=(n // tg, N_EXPERTS, dff // fc),
        in_specs=[pl.BlockSpec((tg, d), lambda g, e, f, c: (g, 0)),
                  pl.BlockSpec((N_EXPERTS, tg), lambda g, e, f, c: (0, g)),
                  pl.BlockSpec((tg, LANES), lambda g, e, f, c: (g, 0)),
                  pl.BlockSpec((tg, LANES), lambda g, e, f, c: (g, 0)),
                  pl.BlockSpec((None, None, d, fc), lambda g, e, f, c: (layer, e, 0, f)),
                  pl.BlockSpec((None, None, d, fc), lambda g, e, f, c: (layer, e, 0, f)),
                  pl.BlockSpec((None, None, fc, d), lambda g, e, f, c: (layer, e, f, 0))],
        out_specs=pl.BlockSpec((tg, d), lambda g, e, f, c: (g, 0)),
        scratch_shapes=[pltpu.VMEM((cap, d), BF16), pltpu.VMEM((cap, d), F32)],
    )
    return pl.pallas_call(
        _moe_kernel,
        grid_spec=grid_spec,
        out_shape=jax.ShapeDtypeStruct((n, d), F32),
        compiler_params=_params(("parallel", "arbitrary", "arbitrary")),
    )(bounds, xh, lpost, lposc, gatec, w1, w3, w2)


def _even_weight_layout(w_in):
    hd = HEAD_DIM
    widths = (A_HEADS * hd, hd, hd, IDX_HEADS * IDX_DIM, IDX_DIM, IDX_HEADS, B_HEADS * hd, B_HEADS * hd, B_HEADS * hd)
    offs = [0]
    for wd in widths:
        offs.append(offs[-1] + wd)
    qa, ka, va, iq, ik, iw, qb, kb, vb = (w_in[:, offs[j]:offs[j + 1]] for j in range(9))
    iw_pad = jnp.pad(iw, ((0, 0), (0, LANES - IDX_HEADS)))
    w = jnp.concatenate([qa, qb, kb, vb, ka, ka, va, va, iq, ik, ik, ik, ik, iw_pad], axis=1)
    mixw = B_HEADS * hd
    sections = ((0, A_HEADS * hd, hd, QK_SCALE),
                (512, mixw, hd, QK_SCALE),
                (1024, mixw, hd, None),
                (1536, mixw, None, None),
                (2048, LANES, hd, None),
                (2176, LANES, None, None),
                (2304, IDX_HEADS * IDX_DIM, IDX_DIM, None),
                (2560, LANES, IDX_DIM, None),
                (2688, LANES, None, None))
    dtypes = (BF16,) * 8 + (F32,)
    return w.astype(BF16), sections, dtypes


def kernel(x, even_w_in, even_w_out, even_ln1_g, even_ln1_b, even_w1, even_w3, even_w2, even_ln2_g, even_ln2_b, odd_w_in, odd_w_out, odd_ln1_g, odd_ln1_b, odd_router, odd_w1, odd_w3, odd_w2, odd_ln2_g, odd_ln2_b):
    batch, seq, d = x.shape
    n = batch * seq
    tables = _rope_tables(seq)
    x2d = x.reshape(n, d)
    odd_w1h, odd_w3h, odd_w2h = odd_w1.astype(BF16), odd_w3.astype(BF16), odd_w2.astype(BF16)
    for layer in range(DEPTH):
        i = layer // 2
        if layer % 2 == 0:
            w, sections, dtypes = _even_weight_layout(even_w_in[i])
            qa, qb, kb, vb, kk, vv, iq, ik4, iw = _project(x2d, w, tables, sections, dtypes, seq)
            o_a = _dsa_attention(iq, iw, ik4, qa, kk, vv, batch, seq)
            o_b = _moba_attention(qb, kb, vb, batch, seq)
            x2d, xh = _outproj_ln([o_a, o_b], even_w_out[i].astype(BF16), x2d, even_ln1_g[i], even_ln1_b[i])
            x2d = _swiglu_ln(xh, x2d, even_w1[i].astype(BF16), even_w3[i].astype(BF16), even_w2[i].astype(BF16),
                             even_ln2_g[i], even_ln2_b[i])
        else:
            mix = C_HEADS * HEAD_DIM
            sections = ((0, mix, HEAD_DIM, QK_SCALE), (mix, mix, HEAD_DIM, None), (2 * mix, mix, None, None))
            q, k, v = _project(x2d, odd_w_in[i].astype(BF16), tables, sections, (F32, F32, F32), seq)
            o = _dilated_attention(q, k, v, batch, seq)
            x2d, xh = _outproj_ln([o], odd_w_out[i].astype(BF16), x2d, odd_ln1_g[i], odd_ln1_b[i])
            rw = jnp.pad(odd_router[i], ((0, 0), (0, LANES - N_EXPERTS))).astype(BF16)
            gatec, lposc, lpost, cum = _router(xh, rw)
            ffn = _moe(_chunk_bounds(cum), xh, lpost, lposc, gatec, odd_w1h, odd_w3h, odd_w2h, i)
            x2d = _add_ln(x2d, ffn, odd_ln2_g[i], odd_ln2_b[i])
    return x2d.reshape(batch, seq, d)
```

```python
import functools

import jax
import jax.numpy as jnp
from jax import lax
from jax.experimental import pallas as pl
from jax.experimental.pallas import tpu as pltpu

F32 = jnp.float32
BF16 = jnp.bfloat16
I32 = jnp.int32

DEPTH = 4
HEAD_DIM = 64
ROPE_THETA = 10000.0
LN_EPS = 1e-5
A_HEADS = 8
IDX_HEADS = 8
IDX_DIM = 32
DSA_TOPK = 256
DSA_Q_BLOCK = 128
B_HEADS = 8
MOBA_BLOCK = 256
MOBA_TOPK = 3
C_HEADS = 16
DILATED_CFG = ((128, 1), (512, 4), (2048, 16))
BAND_BLOCK = 128
N_EXPERTS = 8
DEEPNORM_ALPHA = (2 * DEPTH) ** 0.25
QK_SCALE = HEAD_DIM ** -0.5 * 1.4426950408889634

LANES = 128
VMEM_LIMIT_BYTES = 52 * 1024 * 1024

PROJ_ROWS = 512
FFN_ROWS = 512
ROUTER_ROWS = 512
MOE_GROUP = 2048
MOE_WIN = 256
MOE_FFN_ROWS = 272
MOE_ROW_ALIGN = 16
MOE_FF_CHUNK = 512
DSA_SEARCH_GROUPS = 4
DSA_SEARCH_UNROLL = 4
MOBA_GATE_ROWS = 16
DILATED_UNITS = 4
DILATED_SLABS = 4
ATTN_KEY_STEP = 512

NEG_INF = float("-inf")
NEG_INF_KEY = -2139095041


def _params(semantics):
    return pltpu.CompilerParams(dimension_semantics=semantics, vmem_limit_bytes=VMEM_LIMIT_BYTES)


def _dot(a, b):
    return jnp.dot(a, b, preferred_element_type=F32)


def _dot_nt(a, b):
    return lax.dot_general(a, b, (((1,), (1,)), ((), ())), preferred_element_type=F32)


def _lane_group(shape, group, idx):
    lane = lax.broadcasted_iota(I32, shape, len(shape) - 1)
    return (lane // group) == idx


def _head_pair_attention(units):
    first = _lane_group((1, LANES), HEAD_DIM, 0)
    halves = (first, jnp.logical_not(first))
    raw = [_dot_nt(jnp.where(mine, q2, jnp.zeros_like(q2)), k2) for q2, k2, _, _ in units for mine in halves]
    scores = []
    for u, (_, _, _, mask) in enumerate(units):
        if callable(mask):
            mask = mask()
        masks = mask if isinstance(mask, (tuple, list)) else (mask, mask)
        for h in range(2):
            if masks[h].dtype == jnp.bool_:
                scores.append(jnp.where(masks[h], raw[2 * u + h], NEG_INF))
            else:
                scores.append(raw[2 * u + h] + masks[h])
    maxes = [jnp.max(s, axis=1, keepdims=True) for s in scores]
    weights = [jnp.exp2(s - mx).astype(BF16) for s, mx in zip(scores, maxes)]
    results = []
    for u, (_, _, v2, _) in enumerate(units):
        ones = jnp.ones_like(v2)
        pv0 = _dot(weights[2 * u], jnp.where(first, v2, ones))
        pv1 = _dot(weights[2 * u + 1], jnp.where(first, ones, v2))
        den = jnp.where(first, pv0[:, LANES - 1:LANES], pv1[:, 0:1])
        results.append((jnp.where(first, maxes[2 * u], maxes[2 * u + 1]), den, jnp.where(first, pv0, pv1)))
    return results


def _rope_lanes(t, cosf, sinf, half):
    lane = lax.broadcasted_iota(I32, t.shape, 1)
    first = (lane % (2 * half)) < half
    swapped = jnp.where(first, pltpu.roll(t, LANES - half, 1), pltpu.roll(t, half, 1))
    return t * cosf + swapped * sinf


def _proj_kernel(x_ref, w_ref, c64_ref, s64_ref, c32_ref, s32_ref, *out_refs, sections):
    x = x_ref[...].astype(BF16)
    for o_ref, (start, width, rope, scale) in zip(out_refs, sections):
        h = _dot(x, w_ref[:, start:start + width])
        if rope is not None:
            cosf = (c64_ref if rope == HEAD_DIM else c32_ref)[...]
            sinf = (s64_ref if rope == HEAD_DIM else s32_ref)[...]
            parts = [_rope_lanes(h[:, g * LANES:(g + 1) * LANES], cosf, sinf, rope // 2)
                     for g in range(width // LANES)]
            h = parts[0] if len(parts) == 1 else jnp.concatenate(parts, axis=1)
        if scale is not None:
            h = h * scale
        o_ref[...] = h.astype(o_ref.dtype)


def _project(x2d, w, tables, sections, out_dtypes, seq):
    n, d = x2d.shape
    tm = PROJ_ROWS
    pos_blocks = seq // tm
    tab_spec = pl.BlockSpec((tm, LANES), lambda i: (i % pos_blocks, 0))
    return pl.pallas_call(
        functools.partial(_proj_kernel, sections=sections),
        grid=(n // tm,),
        in_specs=[pl.BlockSpec((tm, d), lambda i: (i, 0)),
                  pl.BlockSpec(w.shape, lambda i: (0, 0)),
                  tab_spec, tab_spec, tab_spec, tab_spec],
        out_specs=[pl.BlockSpec((tm, sec[1]), lambda i: (i, 0)) for sec in sections],
        out_shape=[jax.ShapeDtypeStruct((n, sec[1]), dt) for sec, dt in zip(sections, out_dtypes)],
        compiler_params=_params(("parallel",)),
    )(x2d, w, *tables)


def _rope_tables(seq):
    out = []
    for dim in (HEAD_DIM, IDX_DIM):
        inv = ROPE_THETA ** (-jnp.arange(0, dim, 2, dtype=F32) / dim)
        ang = jnp.arange(seq, dtype=F32)[:, None] * inv[None, :]
        cos, sin = jnp.cos(ang), jnp.sin(ang)
        reps = LANES // dim
        out.append(jnp.tile(jnp.concatenate([cos, cos], -1), (1, reps)))
        out.append(jnp.tile(jnp.concatenate([-sin, sin], -1), (1, reps)))
    return out


def _dsa_block(i, iq_ref, iw_ref, ik_ref, q_ref, kk_ref, vv_ref, o_ref, key_ref, sel_ref, *, kext, topk):
    qb = DSA_Q_BLOCK
    keys = slice(0, kext)
    qpos = i * qb + lax.broadcasted_iota(I32, (qb, 1), 0)
    kpos = lax.broadcasted_iota(I32, (1, kext), 1)
    causal = kpos <= qpos

    ik = ik_ref[keys, :]
    iw = iw_ref[...] * (IDX_DIM ** -0.5 * IDX_HEADS ** -0.5)
    heads_per_group = LANES // IDX_DIM
    score = jnp.zeros((qb, kext), F32)
    for h in range(IDX_HEADS):
        grp, j = divmod(h, heads_per_group)
        iqg = iq_ref[:, grp * LANES:(grp + 1) * LANES]
        iqm = jnp.where(_lane_group((1, LANES), IDX_DIM, j), iqg, jnp.zeros_like(iqg))
        rel = jnp.maximum(_dot_nt(iqm, ik), 0.0)
        score = score + rel * iw[:, h:h + 1]
    bits = pltpu.bitcast(score, I32)
    neg = bits >> 31
    int_min = jnp.int32(-2 ** 31)
    key_ref[:, keys] = jnp.where(causal, (bits ^ (neg & 0x7FFFFFFF)) - neg, int_min)

    grp_rows = qb // DSA_SEARCH_GROUPS
    groups = [slice(g * grp_rows, (g + 1) * grp_rows) for g in range(DSA_SEARCH_GROUPS)]

    def count_ge(rows, t):
        return jnp.sum(jnp.where(key_ref[rows, keys] >= t, 1.0, 0.0), axis=1, keepdims=True)

    def search(it, thrs):
        bit = jnp.left_shift(jnp.int32(1), 30 - it)
        return tuple(jnp.where(count_ge(rows, thr | bit) >= topk, thr | bit, thr)
                     for rows, thr in zip(groups, thrs))

    zero = jnp.zeros((grp_rows, 1), I32)
    thrs = tuple(jnp.where(count_ge(rows, zero) >= topk, zero, int_min) for rows in groups)
    thrs = lax.fori_loop(0, 31, search, thrs, unroll=DSA_SEARCH_UNROLL)
    thr = jnp.concatenate(thrs, axis=0)

    floor = jnp.maximum(thr, int_min + 1)
    key = key_ref[:, keys]
    sel_ref[:, keys] = jnp.where(key >= floor, 0.0, NEG_INF)
    n_ge = jnp.sum(jnp.where(key >= thr, 1.0, 0.0), axis=1, keepdims=True)
    tie_rows = (n_ge > topk) & (thr > int_min)

    @pl.when(jnp.max(tie_rows.astype(I32)) > 0)
    def _break_ties():
        upper = (lax.broadcasted_iota(I32, (LANES, LANES), 0)
                 < lax.broadcasted_iota(I32, (LANES, LANES), 1))
        upper = jnp.where(upper, 1.0, 0.0).astype(BF16)
        n_gt = jnp.sum(jnp.where(key_ref[:, keys] > thr, 1.0, 0.0), axis=1, keepdims=True)
        need = topk - n_gt
        before = jnp.zeros((qb, 1), F32)
        for c in range(kext // LANES):
            sl = slice(c * LANES, (c + 1) * LANES)
            keyc = key_ref[:, sl]
            eqc = keyc == thr
            eqf = jnp.where(eqc, 1.0, 0.0)
            rank = _dot(eqf.astype(BF16), upper) + before
            take = ((keyc > thr) | (eqc & (rank < need))) & (keyc > int_min)
            sel_ref[:, sl] = jnp.where(take, 0.0, NEG_INF)
            before = before + jnp.sum(eqf, axis=1, keepdims=True)

    selected = sel_ref[:, keys]
    kk = kk_ref[keys, :]
    vv = vv_ref[keys, :]
    pairs = [(q_ref[:, hp * LANES:(hp + 1) * LANES], kk, vv, selected) for hp in range(A_HEADS // 2)]
    for hp, (_, den, num) in enumerate(_head_pair_attention(pairs)):
        o_ref[:, hp * LANES:(hp + 1) * LANES] = (num / den).astype(o_ref.dtype)


def _dsa_kernel(*refs, seq, topk):
    i = pl.program_id(1)
    blocks_per_step = ATTN_KEY_STEP // DSA_Q_BLOCK
    for v in range(seq // ATTN_KEY_STEP):
        @pl.when(i // blocks_per_step == v)
        def _(v=v):
            _dsa_block(i, *refs, kext=(v + 1) * ATTN_KEY_STEP, topk=topk)


def _dsa_attention(iq, iw, ik4, qa, kk, vv, batch, seq):
    n = batch * seq
    qb = DSA_Q_BLOCK
    nq = seq // qb
    topk = min(DSA_TOPK, seq // 4)
    rows = lambda width: pl.BlockSpec((qb, width), lambda b, i: (b * nq + i, 0))
    whole = pl.BlockSpec((seq, LANES), lambda b, i: (b, 0))
    return pl.pallas_call(
        functools.partial(_dsa_kernel, seq=seq, topk=topk),
        grid=(batch, nq),
        in_specs=[rows(iq.shape[1]), rows(LANES), whole, rows(qa.shape[1]), whole, whole],
        out_specs=rows(qa.shape[1]),
        out_shape=jax.ShapeDtypeStruct((n, qa.shape[1]), BF16),
        scratch_shapes=[pltpu.VMEM((qb, seq), I32), pltpu.VMEM((qb, seq), F32)],
        compiler_params=_params(("parallel", "parallel")),
    )(iq, iw, ik4, qa, kk, vv)


def _moba_block(qi, q_ref, k_ref, v_ref, o_ref, *, kext, kt):
    bs = MOBA_BLOCK
    nb = kext // bs
    q2 = q_ref[...]
    k2 = k_ref[0:kext, :]
    v2 = v_ref[0:kext, :]
    nbp = max(nb, MOBA_GATE_ROWS)
    blk_row = lax.broadcasted_iota(I32, (nbp, kext), 0)
    blk_of_key = lax.broadcasted_iota(I32, (nbp, kext), 1) // bs
    member = jnp.where(blk_row == blk_of_key, 1.0, 0.0).astype(BF16)
    kmean = (_dot(member, k2) * (1.0 / bs)).astype(BF16)
    n_iota = lax.broadcasted_iota(I32, (nbp, 1), 0)
    past = n_iota < qi
    qrow = lax.broadcasted_iota(I32, (bs, 1), 0)
    kcol = lax.broadcasted_iota(I32, (1, kext), 1)
    own = (kcol >= qi * bs) & (kcol <= qi * bs + qrow)
    def block_masks():
        gates = [jnp.where(past, _dot_nt(kmean, jnp.where(_lane_group((1, LANES), HEAD_DIM, half), q2,
                                                          jnp.zeros_like(q2))), NEG_INF)
                 for half in range(2)]
        chosen = []
        for gate in gates:
            rank = jnp.zeros((nbp, bs), I32)
            for m in range(nb):
                gm = gate[m:m + 1, :]
                beats = (gm > gate) | ((gm == gate) & (m < n_iota))
                rank = rank + beats.astype(I32)
            chosen.append(jnp.transpose(jnp.where(past & (rank < kt), 1.0, 0.0)).astype(BF16))
        return [(_dot(c, member) > 0.5) | own for c in chosen]

    (_, den, num), = _head_pair_attention([(q2, k2, v2, block_masks)])
    o_ref[...] = (num / den).astype(o_ref.dtype)


def _moba_kernel(*refs, seq):
    qi = pl.program_id(2)
    kt = min(MOBA_TOPK, seq // MOBA_BLOCK - 1)
    blocks_per_step = ATTN_KEY_STEP // MOBA_BLOCK
    for v in range(seq // ATTN_KEY_STEP):
        @pl.when(qi // blocks_per_step == v)
        def _(v=v):
            _moba_block(qi, *refs, kext=(v + 1) * ATTN_KEY_STEP, kt=kt)


def _moba_attention(q, k, v, batch, seq):
    n, width = q.shape
    bs = MOBA_BLOCK
    nb = seq // bs
    rows = pl.BlockSpec((bs, LANES), lambda b, hp, qi: (b * nb + qi, hp))
    whole = pl.BlockSpec((seq, LANES), lambda b, hp, qi: (b, hp))
    return pl.pallas_call(
        functools.partial(_moba_kernel, seq=seq),
        grid=(batch, width // LANES, nb),
        in_specs=[rows, whole, whole],
        out_specs=rows,
        out_shape=jax.ShapeDtypeStruct((n, width), BF16),
        compiler_params=_params(("parallel", "parallel", "parallel")),
    )(q, k, v)


def _rows(start, size, stride):
    return pl.ds(start, size) if stride == 1 else pl.ds(start, size, stride=stride)


def _largest_divisor(n, limit):
    return max(d for d in range(1, limit + 1) if n % d == 0)


def _repeat(trips, body):
    if trips == 1:
        body(0, 0)
    else:
        lax.fori_loop(0, trips, body, 0)


def _dilated_kernel(q_ref, k_ref, v_ref, o_ref, qp_ref, kp_ref, vp_ref, nat_ref, slab_ref, *, seq):
    blk = BAND_BLOCK
    ns = DILATED_SLABS
    slab_len = seq // ns

    for s in range(ns):
        rows = slice(s * slab_len, (s + 1) * slab_len)
        qp_ref[rows, :] = q_ref[pl.ds(s, slab_len, stride=ns), :]
        kp_ref[rows, :] = k_ref[pl.ds(s, slab_len, stride=ns), :]
        vp_ref[rows, :] = v_ref[pl.ds(s, slab_len, stride=ns), :]

    def operands(srcs, qstart, kstart, nk, stride):
        ksl = _rows(kstart, nk, stride)
        i = lax.broadcasted_iota(I32, (blk, 1), 0)
        j = lax.broadcasted_iota(I32, (1, nk), 1)
        if nk == blk:
            mask = j <= i
        else:
            mask = (j >= i) & (j <= i + blk)
        return (srcs[0][_rows(qstart, blk, stride), :].astype(BF16), srcs[1][ksl, :].astype(BF16),
                srcs[2][ksl, :].astype(BF16), mask)

    def merged(old, new):
        (m_old, l_old, a_old), (m_new, l_new, a_new) = old, new
        m_tot = jnp.maximum(m_old, m_new)
        w_old = jnp.exp2(m_old - m_tot)
        w_new = jnp.exp2(m_new - m_tot)
        return m_tot, w_old * l_old + w_new * l_new, w_old * a_old + w_new * a_new

    def run_units(srcs, state_ref, units, nk, stride, merge):
        results = _head_pair_attention([operands(srcs, qstart, kstart, nk, stride) for qstart, kstart in units])
        for (qstart, _), new in zip(units, results):
            qsl = _rows(qstart, blk, stride)
            if merge:
                new = merged(tuple(state_ref[t, qsl, :] for t in range(3)), new)
            for t in range(3):
                state_ref[t, qsl, :] = new[t]

    written = set()
    for window, dil in DILATED_CFG:
        assert window // dil == blk
        if dil % ns == 0:
            srcs, state_ref, stride, group_len = (qp_ref, kp_ref, vp_ref), slab_ref, dil // ns, slab_len
        else:
            srcs, state_ref, stride, group_len = (q_ref, k_ref, v_ref), nat_ref, dil, seq
        merge = id(state_ref) in written
        written.add(id(state_ref))
        nblk = seq // dil // blk
        ub = _largest_divisor(dil, DILATED_UNITS)

        def class_start(c, stride=stride, group_len=group_len):
            return (c // stride) * group_len + c % stride if group_len != seq else c

        def first_blocks(it, carry, srcs=srcs, state_ref=state_ref, stride=stride, merge=merge, ub=ub,
                         class_start=class_start):
            starts = [class_start(it * ub + u) for u in range(ub)]
            run_units(srcs, state_ref, [(st, st) for st in starts], blk, stride, merge)
            return carry

        _repeat(dil // ub, first_blocks)
        if nblk > 1:
            ul = _largest_divisor(nblk - 1, DILATED_UNITS)
            per_class = (nblk - 1) // ul

            def later_blocks(it, carry, srcs=srcs, state_ref=state_ref, stride=stride, merge=merge, ul=ul,
                             per_class=per_class, class_start=class_start):
                base, grp = class_start(it // per_class), it % per_class
                qstarts = [base + stride * blk * (1 + grp * ul + u) for u in range(ul)]
                run_units(srcs, state_ref, [(qs, qs - stride * blk) for qs in qstarts], 2 * blk, stride, merge)
                return carry

            _repeat(dil * per_class, later_blocks)

    for s in range(ns):
        nat_rows = pl.ds(s, slab_len, stride=ns)
        slab_rows = slice(s * slab_len, (s + 1) * slab_len)
        _, den, num = merged(tuple(nat_ref[t, nat_rows, :] for t in range(3)),
                             tuple(slab_ref[t, slab_rows, :] for t in range(3)))
        nat_ref[2, nat_rows, :] = num / den
    o_ref[...] = nat_ref[2].astype(o_ref.dtype)


def _dilated_attention(q, k, v, batch, seq):
    n, width = q.shape
    for window, dil in DILATED_CFG:
        assert seq % (dil * BAND_BLOCK) == 0
    assert {dil % DILATED_SLABS == 0 for _, dil in DILATED_CFG} == {True, False}
    whole = pl.BlockSpec((seq, LANES), lambda b, hp: (b, hp))
    return pl.pallas_call(
        functools.partial(_dilated_kernel, seq=seq),
        grid=(batch, width // LANES),
        in_specs=[whole, whole, whole],
        out_specs=whole,
        out_shape=jax.ShapeDtypeStruct((n, width), BF16),
        scratch_shapes=[pltpu.VMEM((seq, LANES), F32)] * 3 + [pltpu.VMEM((3, seq, LANES), F32)] * 2,
        compiler_params=_params(("parallel", "parallel")),
    )(q, k, v)


def _deepnorm(x, f, g, b):
    y = DEEPNORM_ALPHA * x + f
    mu = jnp.mean(y, axis=1, keepdims=True)
    yc = y - mu
    var = jnp.mean(yc * yc, axis=1, keepdims=True)
    return yc * lax.rsqrt(var + LN_EPS) * g + b


def _outproj_ln_kernel(*refs, n_parts):
    part_refs = refs[:n_parts]
    w_ref, x_ref, g_ref, b_ref, o_ref, oh_ref = refs[n_parts:]
    mix = None
    start = 0
    for p_ref in part_refs:
        width = p_ref.shape[1]
        t = _dot(p_ref[...], w_ref[start:start + width, :])
        mix = t if mix is None else mix + t
        start += width
    y = _deepnorm(x_ref[...], mix, g_ref[...], b_ref[...])
    o_ref[...] = y
    oh_ref[...] = y.astype(BF16)


def _outproj_ln(parts, w, x2d, g, b):
    n, d = x2d.shape
    tm = PROJ_ROWS
    row = lambda width: pl.BlockSpec((tm, width), lambda i: (i, 0))
    vec = pl.BlockSpec((1, d), lambda i: (0, 0))
    return pl.pallas_call(
        functools.partial(_outproj_ln_kernel, n_parts=len(parts)),
        grid=(n // tm,),
        in_specs=[row(p.shape[1]) for p in parts] + [pl.BlockSpec(w.shape, lambda i: (0, 0)), row(d), vec, vec],
        out_specs=[row(d), row(d)],
        out_shape=[jax.ShapeDtypeStruct((n, d), F32), jax.ShapeDtypeStruct((n, d), BF16)],
        compiler_params=_params(("parallel",)),
    )(*parts, w, x2d, g.reshape(1, d), b.reshape(1, d))


def _silu(h):
    return h * (1.0 / (1.0 + jnp.exp(-h)))


def _swiglu_ln_kernel(xh_ref, x_ref, w1_ref, w3_ref, w2_ref, g_ref, b_ref, o_ref, acc_ref):
    f = pl.program_id(1)
    xh = xh_ref[...]
    hid = (_silu(_dot(xh, w1_ref[...])) * _dot(xh, w3_ref[...])).astype(BF16)
    y = _dot(hid, w2_ref[...])

    @pl.when(f == 0)
    def _():
        acc_ref[...] = y

    @pl.when(f > 0)
    def _():
        acc_ref[...] += y

    @pl.when(f == pl.num_programs(1) - 1)
    def _():
        o_ref[...] = _deepnorm(x_ref[...], acc_ref[...], g_ref[...], b_ref[...])


def _swiglu_ln(xh, x2d, w1, w3, w2, g, b):
    n, d = x2d.shape
    dff = w1.shape[1]
    tm = FFN_ROWS
    nf = 2 if dff % (2 * LANES) == 0 else 1
    fc = dff // nf
    row = lambda: pl.BlockSpec((tm, d), lambda i, f: (i, 0))
    vec = pl.BlockSpec((1, d), lambda i, f: (0, 0))
    return pl.pallas_call(
        _swiglu_ln_kernel,
        grid=(n // tm, nf),
        in_specs=[row(), row(),
                  pl.BlockSpec((d, fc), lambda i, f: (0, f)),
                  pl.BlockSpec((d, fc), lambda i, f: (0, f)),
                  pl.BlockSpec((fc, d), lambda i, f: (f, 0)),
                  vec, vec],
        out_specs=row(),
        out_shape=jax.ShapeDtypeStruct((n, d), F32),
        scratch_shapes=[pltpu.VMEM((tm, d), F32)],
        compiler_params=_params(("parallel", "arbitrary")),
    )(xh, x2d, w1, w3, w2, g.reshape(1, d), b.reshape(1, d))


def _add_ln_kernel(x_ref, f_ref, g_ref, b_ref, o_ref):
    o_ref[...] = _deepnorm(x_ref[...], f_ref[...], g_ref[...], b_ref[...])


def _add_ln(x2d, f2d, g, b):
    n, d = x2d.shape
    tm = PROJ_ROWS
    row = pl.BlockSpec((tm, d), lambda i: (i, 0))
    vec = pl.BlockSpec((1, d), lambda i: (0, 0))
    return pl.pallas_call(
        _add_ln_kernel,
        grid=(n // tm,),
        in_specs=[row, row, vec, vec],
        out_specs=row,
        out_shape=jax.ShapeDtypeStruct((n, d), F32),
        compiler_params=_params(("parallel",)),
    )(x2d, f2d, g.reshape(1, d), b.reshape(1, d))


def _router_kernel(xh_ref, rw_ref, gate_ref, lposc_ref, lpost_ref, cum_ref, carry_ref, *, tiles_per_group):
    tm = xh_ref.shape[0]
    i = pl.program_id(0)

    @pl.when(i % tiles_per_group == 0)
    def _():
        carry_ref[...] = jnp.zeros_like(carry_ref)

    lane = lax.broadcasted_iota(I32, (1, LANES), 1)
    logits = jnp.where(lane < N_EXPERTS, _dot(xh_ref[...], rw_ref[...]), NEG_INF)
    m1 = jnp.max(logits, axis=1, keepdims=True)
    i1 = jnp.min(jnp.where(logits == m1, lane, LANES), axis=1, keepdims=True)
    rest = jnp.where(lane == i1, NEG_INF, logits)
    m2 = jnp.max(rest, axis=1, keepdims=True)
    i2 = jnp.min(jnp.where(rest == m2, lane, LANES), axis=1, keepdims=True)
    e2 = jnp.exp(m2 - m1)
    g1 = 1.0 / (1.0 + e2)
    g2 = e2 / (1.0 + e2)
    sel1 = lane == i1
    sel2 = lane == i2
    routed = sel1 | sel2
    gate_ref[...] = jnp.where(sel1, g1, jnp.where(sel2, g2, 0.0))
    routedf = jnp.where(routed, 1.0, 0.0)
    earlier = (lax.broadcasted_iota(I32, (tm, tm), 1) < lax.broadcasted_iota(I32, (tm, tm), 0))
    earlier = jnp.where(earlier, 1.0, 0.0).astype(BF16)
    carry = carry_ref[...]
    lpos = jnp.where(routed, _dot(earlier, routedf.astype(BF16)) + carry, -1.0)
    lposc_ref[...] = lpos
    lpost_ref[...] = jnp.transpose(lpos)[0:N_EXPERTS, :]
    carry = carry + jnp.sum(routedf, axis=0, keepdims=True)
    carry_ref[...] = carry
    cum_ref[...] = carry.astype(I32).reshape(1, 1, LANES)


def _router(xh, rw):
    n, d = xh.shape
    tm = ROUTER_ROWS
    nt = n // tm
    row = pl.BlockSpec((tm, LANES), lambda i: (i, 0))
    return pl.pallas_call(
        functools.partial(_router_kernel, tiles_per_group=MOE_GROUP // tm),
        grid=(nt,),
        in_specs=[pl.BlockSpec((tm, d), lambda i: (i, 0)), pl.BlockSpec(rw.shape, lambda i: (0, 0))],
        out_specs=[row, row,
                   pl.BlockSpec((N_EXPERTS, tm), lambda i: (0, i)),
                   pl.BlockSpec((1, 1, LANES), lambda i: (i, 0, 0))],
        out_shape=[jax.ShapeDtypeStruct((n, LANES), F32), jax.ShapeDtypeStruct((n, LANES), F32),
                   jax.ShapeDtypeStruct((N_EXPERTS, n), F32), jax.ShapeDtypeStruct((nt, 1, LANES), I32)],
        scratch_shapes=[pltpu.VMEM((1, LANES), F32)],
        compiler_params=_params(("arbitrary",)),
    )(xh, rw)


def _moe_kernel(bnd_ref, xh_ref, lpost_ref, lposc_ref, gatec_ref, w1_ref, w3_ref, w2_ref,
                o_ref, xg_ref, yacc_ref):
    tg = xh_ref.shape[0]
    win, ch, fr, align = MOE_WIN, ROUTER_ROWS, MOE_FFN_ROWS, MOE_ROW_ALIGN
    nch = tg // ch
    g, e, f = pl.program_id(0), pl.program_id(1), pl.program_id(2)
    nf = pl.num_programs(2)

    def bound(c):
        return bnd_ref[(g * (nch + 1) + c) * N_EXPERTS + e]

    cnt = bound(nch)
    n_ffn = (cnt + fr - 1) // fr

    def tok_chunk(c):
        return pl.ds(pl.multiple_of(c * ch, ch), ch)

    def windows(c):
        r0, r1 = bound(c), bound(c + 1)
        first = (r0 // align) * align
        return first, jnp.where(r1 > r0, (r1 - first + win - 1) // win, 0)

    @pl.when((e == 0) & (f == 0))
    def _():
        o_ref[...] = jnp.zeros_like(o_ref)

    @pl.when(f == 0)
    def _compact():
        def clear(s, carry):
            rows = pl.ds(pl.multiple_of(s * win, win), win)
            xg_ref[rows, :] = jnp.zeros((win, xg_ref.shape[1]), xg_ref.dtype)
            yacc_ref[rows, :] = jnp.zeros((win, yacc_ref.shape[1]), F32)
            return carry

        lax.fori_loop(0, (cnt + max(fr, win) + win - 1) // win, clear, 0)

        def per_chunk(c, carry):
            first, n_win = windows(c)
            lp = lpost_ref[pl.ds(e, 1), tok_chunk(c)]

            def per_window(w, carry2):
                start = pl.multiple_of(first + w * win, align)
                want = (lax.broadcasted_iota(I32, (win, 1), 0) + start).astype(F32)
                pick = jnp.where(lp == want, 1.0, 0.0).astype(BF16)
                rows = pl.ds(start, win)
                xg_ref[rows, :] += _dot(pick, xh_ref[tok_chunk(c), :]).astype(xg_ref.dtype)
                return carry2

            return lax.fori_loop(0, n_win, per_window, carry)

        lax.fori_loop(0, nch, per_chunk, 0)

    def ffn_rows(rows):
        xs = xg_ref[rows, :]
        hid = (_silu(_dot(xs, w1_ref[...])) * _dot(xs, w3_ref[...])).astype(BF16)
        yacc_ref[rows, :] += _dot(hid, w2_ref[...])

    def ffn_pair(p, carry):
        ffn_rows(pl.ds(pl.multiple_of(p * 2 * fr, align), 2 * fr))
        return carry

    lax.fori_loop(0, n_ffn // 2, ffn_pair, 0)

    @pl.when(n_ffn % 2 == 1)
    def _():
        ffn_rows(pl.ds(pl.multiple_of((n_ffn - 1) * fr, align), fr))

    @pl.when(f == nf - 1)
    def _scatter():
        lane = lax.broadcasted_iota(I32, (1, LANES), 1)

        def per_chunk(c, carry):
            first, n_win = windows(c)
            toks = tok_chunk(c)
            lp = jnp.sum(jnp.where(lane == e, lposc_ref[toks, :], 0.0), axis=1, keepdims=True)
            gt = jnp.sum(jnp.where(lane == e, gatec_ref[toks, :], 0.0), axis=1, keepdims=True)

            def per_window(w, carry2):
                start = pl.multiple_of(first + w * win, align)
                y = yacc_ref[pl.ds(start, win), :]
                y_hi = y.astype(BF16)
                y_lo = (y - y_hi.astype(F32)).astype(BF16)
                want = (lax.broadcasted_iota(I32, (1, win), 1) + start).astype(F32)
                place = jnp.where(lp == want, 1.0, 0.0).astype(BF16)
                o_ref[toks, :] += gt * (_dot(place, y_hi) + _dot(place, y_lo))
                return carry2

            return lax.fori_loop(0, n_win, per_window, carry)

        lax.fori_loop(0, nch, per_chunk, 0)


def _chunk_bounds(cum):
    nch = MOE_GROUP // ROUTER_ROWS
    per_group = cum[:, 0, :N_EXPERTS].reshape(-1, nch, N_EXPERTS)
    return jnp.pad(per_group, ((0, 0), (1, 0), (0, 0))).reshape(-1)


def _moe(bounds, xh, lpost, lposc, gatec, w1, w3, w2, layer):
    n, d = xh.shape
    tg = MOE_GROUP
    dff = w1.shape[3]
    fc = MOE_FF_CHUNK
    cap = -(-(tg + max(MOE_FFN_ROWS, MOE_WIN)) // MOE_WIN) * MOE_WIN
    grid_spec = pltpu.PrefetchScalarGridSpec(
        num_scalar_prefetch=1,
        grid=(n // tg, N_EXPERTS, dff // fc),
        in_specs=[pl.BlockSpec((tg, d), lambda g, e, f, c: (g, 0)),
                  pl.BlockSpec((N_EXPERTS, tg), lambda g, e, f, c: (0, g)),
                  pl.BlockSpec((tg, LANES), lambda g, e, f, c: (g, 0)),
                  pl.BlockSpec((tg, LANES), lambda g, e, f, c: (g, 0)),
                  pl.BlockSpec((None, None, d, fc), lambda g, e, f, c: (layer, e, 0, f)),
                  pl.BlockSpec((None, None, d, fc), lambda g, e, f, c: (layer, e, 0, f)),
                  pl.BlockSpec((None, None, fc, d), lambda g, e, f, c: (layer, e, f, 0))],
        out_specs=pl.BlockSpec((tg, d), lambda g, e, f, c: (g, 0)),
        scratch_shapes=[pltpu.VMEM((cap, d), BF16), pltpu.VMEM((cap, d), F32)],
    )
    return pl.pallas_call(
        _moe_kernel,
        grid_spec=grid_spec,
        out_shape=jax.ShapeDtypeStruct((n, d), F32),
        compiler_params=_params(("parallel", "arbitrary", "arbitrary")),
    )(bounds, xh, lpost, lposc, gatec, w1, w3, w2)


def _even_weight_layout(w_in):
    hd = HEAD_DIM
    widths = (A_HEADS * hd, hd, hd, IDX_HEADS * IDX_DIM, IDX_DIM, IDX_HEADS, B_HEADS * hd, B_HEADS * hd, B_HEADS * hd)
    offs = [0]
    for wd in widths:
        offs.append(offs[-1] + wd)
    qa, ka, va, iq, ik, iw, qb, kb, vb = (w_in[:, offs[j]:offs[j + 1]] for j in range(9))
    iw_pad = jnp.pad(iw, ((0, 0), (0, LANES - IDX_HEADS)))
    w = jnp.concatenate([qa, qb, kb, vb, ka, ka, va, va, iq, ik, ik, ik, ik, iw_pad], axis=1)
    mixw = B_HEADS * hd
    sections = ((0, A_HEADS * hd, hd, QK_SCALE),
                (512, mixw, hd, QK_SCALE),
                (1024, mixw, hd, None),
                (1536, mixw, None, None),
                (2048, LANES, hd, None),
                (2176, LANES, None, None),
                (2304, IDX_HEADS * IDX_DIM, IDX_DIM, None),
                (2560, LANES, IDX_DIM, None),
                (2688, LANES, None, None))
    dtypes = (BF16,) * 8 + (F32,)
    return w.astype(BF16), sections, dtypes


def kernel(x, even_w_in, even_w_out, even_ln1_g, even_ln1_b, even_w1, even_w3, even_w2, even_ln2_g, even_ln2_b, odd_w_in, odd_w_out, odd_ln1_g, odd_ln1_b, odd_router, odd_w1, odd_w3, odd_w2, odd_ln2_g, odd_ln2_b):
    batch, seq, d = x.shape
    n = batch * seq
    tables = _rope_tables(seq)
    x2d = x.reshape(n, d)
    odd_w1h, odd_w3h, odd_w2h = odd_w1.astype(BF16), odd_w3.astype(BF16), odd_w2.astype(BF16)
    for layer in range(DEPTH):
        i = layer // 2
        if layer % 2 == 0:
            w, sections, dtypes = _even_weight_layout(even_w_in[i])
            qa, qb, kb, vb, kk, vv, iq, ik4, iw = _project(x2d, w, tables, sections, dtypes, seq)
            o_a = _dsa_attention(iq, iw, ik4, qa, kk, vv, batch, seq)
            o_b = _moba_attention(qb, kb, vb, batch, seq)
            x2d, xh = _outproj_ln([o_a, o_b], even_w_out[i].astype(BF16), x2d, even_ln1_g[i], even_ln1_b[i])
            x2d = _swiglu_ln(xh, x2d, even_w1[i].astype(BF16), even_w3[i].astype(BF16), even_w2[i].astype(BF16),
                             even_ln2_g[i], even_ln2_b[i])
        else:
            mix = C_HEADS * HEAD_DIM
            sections = ((0, mix, HEAD_DIM, QK_SCALE), (mix, mix, HEAD_DIM, None), (2 * mix, mix, None, None))
            q, k, v = _project(x2d, odd_w_in[i].astype(BF16), tables, sections, (F32, F32, F32), seq)
            o = _dilated_attention(q, k, v, batch, seq)
            x2d, xh = _outproj_ln([o], odd_w_out[i].astype(BF16), x2d, odd_ln1_g[i], odd_ln1_b[i])
            rw = jnp.pad(odd_router[i], ((0, 0), (0, LANES - N_EXPERTS))).astype(BF16)
            gatec, lposc, lpost, cum = _router(xh, rw)
            ffn = _moe(_chunk_bounds(cum), xh, lpost, lposc, gatec, odd_w1h, odd_w3h, odd_w2h, i)
            x2d = _add_ln(x2d, ffn, odd_ln2_g[i], odd_ln2_b[i])
    return x2d.reshape(batch, seq, d)
```

```python
import functools

import jax
import jax.numpy as jnp
from jax import lax
from jax.experimental import pallas as pl
from jax.experimental.pallas import tpu as pltpu

F32 = jnp.float32
BF16 = jnp.bfloat16
I32 = jnp.int32

DEPTH = 4
HEAD_DIM = 64
ROPE_THETA = 10000.0
LN_EPS = 1e-5
A_HEADS = 8
IDX_HEADS = 8
IDX_DIM = 32
DSA_TOPK = 256
DSA_Q_BLOCK = 128
B_HEADS = 8
MOBA_BLOCK = 256
MOBA_TOPK = 3
C_HEADS = 16
DILATED_CFG = ((128, 1), (512, 4), (2048, 16))
BAND_BLOCK = 128
N_EXPERTS = 8
DEEPNORM_ALPHA = (2 * DEPTH) ** 0.25
QK_SCALE = HEAD_DIM ** -0.5 * 1.4426950408889634

LANES = 128
VMEM_LIMIT_BYTES = 52 * 1024 * 1024

PROJ_ROWS = 512
FFN_ROWS = 512
ROUTER_ROWS = 512
MOE_GROUP = 2048
MOE_WIN = 256
MOE_FFN_ROWS = 272
MOE_ROW_ALIGN = 16
MOE_FF_CHUNK = 896
DSA_SEARCH_GROUPS = 4
DSA_SEARCH_UNROLL = 4
MOBA_GATE_ROWS = 16
DILATED_UNITS = 4
DILATED_SLABS = 4
ATTN_KEY_STEP = 512

NEG_INF = float("-inf")
NEG_INF_KEY = -2139095041


def _params(semantics):
    return pltpu.CompilerParams(dimension_semantics=semantics, vmem_limit_bytes=VMEM_LIMIT_BYTES)


def _dot(a, b):
    return jnp.dot(a, b, preferred_element_type=F32)


def _dot_nt(a, b):
    return lax.dot_general(a, b, (((1,), (1,)), ((), ())), preferred_element_type=F32)


def _lane_group(shape, group, idx):
    lane = lax.broadcasted_iota(I32, shape, len(shape) - 1)
    return (lane // group) == idx


def _head_pair_attention(units):
    first = _lane_group((1, LANES), HEAD_DIM, 0)
    raw, unit_masks = [], []
    for q2, k2, _, mask in units:
        raw.append(_dot_nt(jnp.where(first, q2, jnp.zeros_like(q2)), k2))
        unit_masks.append(mask() if callable(mask) else mask)
        raw.append(_dot_nt(jnp.where(first, jnp.zeros_like(q2), q2), k2))
    scores = []
    for u, mask in enumerate(unit_masks):
        masks = mask if isinstance(mask, (tuple, list)) else (mask, mask)
        for h in range(2):
            if masks[h].dtype == jnp.bool_:
                scores.append(jnp.where(masks[h], raw[2 * u + h], NEG_INF))
            else:
                scores.append(raw[2 * u + h] + masks[h])
    maxes = [jnp.max(s, axis=1, keepdims=True) for s in scores]
    weights = [jnp.exp2(s - mx).astype(BF16) for s, mx in zip(scores, maxes)]
    results = []
    for u, (_, _, v2, _) in enumerate(units):
        ones = jnp.ones_like(v2)
        pv0 = _dot(weights[2 * u], jnp.where(first, v2, ones))
        pv1 = _dot(weights[2 * u + 1], jnp.where(first, ones, v2))
        den = jnp.where(first, pv0[:, LANES - 1:LANES], pv1[:, 0:1])
        results.append((jnp.where(first, maxes[2 * u], maxes[2 * u + 1]), den, jnp.where(first, pv0, pv1)))
    return results


def _rope_lanes(t, cosf, sinf, half):
    lane = lax.broadcasted_iota(I32, t.shape, 1)
    first = (lane % (2 * half)) < half
    swapped = jnp.where(first, pltpu.roll(t, LANES - half, 1), pltpu.roll(t, half, 1))
    return t * cosf + swapped * sinf


def _proj_kernel(x_ref, w_ref, c64_ref, s64_ref, c32_ref, s32_ref, *out_refs, sections):
    x = x_ref[...].astype(BF16)
    for o_ref, (start, width, rope, scale) in zip(out_refs, sections):
        h = _dot(x, w_ref[:, start:start + width])
        if rope is not None:
            cosf = (c64_ref if rope == HEAD_DIM else c32_ref)[...]
            sinf = (s64_ref if rope == HEAD_DIM else s32_ref)[...]
            parts = [_rope_lanes(h[:, g * LANES:(g + 1) * LANES], cosf, sinf, rope // 2)
                     for g in range(width // LANES)]
            h = parts[0] if len(parts) == 1 else jnp.concatenate(parts, axis=1)
        if scale is not None:
            h = h * scale
        o_ref[...] = h.astype(o_ref.dtype)


def _project(x2d, w, tables, sections, out_dtypes, seq):
    n, d = x2d.shape
    tm = PROJ_ROWS
    pos_blocks = seq // tm
    tab_spec = pl.BlockSpec((tm, LANES), lambda i: (i % pos_blocks, 0))
    return pl.pallas_call(
        functools.partial(_proj_kernel, sections=sections),
        grid=(n // tm,),
        in_specs=[pl.BlockSpec((tm, d), lambda i: (i, 0)),
                  pl.BlockSpec(w.shape, lambda i: (0, 0)),
                  tab_spec, tab_spec, tab_spec, tab_spec],
        out_specs=[pl.BlockSpec((tm, sec[1]), lambda i: (i, 0)) for sec in sections],
        out_shape=[jax.ShapeDtypeStruct((n, sec[1]), dt) for sec, dt in zip(sections, out_dtypes)],
        compiler_params=_params(("parallel",)),
    )(x2d, w, *tables)


def _rope_tables(seq):
    out = []
    for dim in (HEAD_DIM, IDX_DIM):
        inv = ROPE_THETA ** (-jnp.arange(0, dim, 2, dtype=F32) / dim)
        ang = jnp.arange(seq, dtype=F32)[:, None] * inv[None, :]
        cos, sin = jnp.cos(ang), jnp.sin(ang)
        reps = LANES // dim
        out.append(jnp.tile(jnp.concatenate([cos, cos], -1), (1, reps)))
        out.append(jnp.tile(jnp.concatenate([-sin, sin], -1), (1, reps)))
    return out


def _dsa_block(i, iq_ref, iw_ref, ik_ref, q_ref, kk_ref, vv_ref, o_ref, key_ref, sel_ref, *, kext, topk):
    qb = DSA_Q_BLOCK
    keys = slice(0, kext)
    qpos = i * qb + lax.broadcasted_iota(I32, (qb, 1), 0)
    kpos = lax.broadcasted_iota(I32, (1, kext), 1)
    causal = kpos <= qpos

    ik = ik_ref[keys, :]
    iw = iw_ref[...] * (IDX_DIM ** -0.5 * IDX_HEADS ** -0.5)
    heads_per_group = LANES // IDX_DIM
    score = jnp.zeros((qb, kext), F32)
    for h in range(IDX_HEADS):
        grp, j = divmod(h, heads_per_group)
        iqg = iq_ref[:, grp * LANES:(grp + 1) * LANES]
        iqm = jnp.where(_lane_group((1, LANES), IDX_DIM, j), iqg, jnp.zeros_like(iqg))
        rel = jnp.maximum(_dot_nt(iqm, ik), 0.0)
        score = score + rel * iw[:, h:h + 1]
    bits = pltpu.bitcast(score, I32)
    neg = bits >> 31
    int_min = jnp.int32(-2 ** 31)
    key_ref[:, keys] = jnp.where(causal, (bits ^ (neg & 0x7FFFFFFF)) - neg, int_min)

    grp_rows = qb // DSA_SEARCH_GROUPS
    groups = [slice(g * grp_rows, (g + 1) * grp_rows) for g in range(DSA_SEARCH_GROUPS)]

    def count_ge(rows, t):
        return jnp.sum(jnp.where(key_ref[rows, keys] >= t, 1.0, 0.0), axis=1, keepdims=True)

    def search(it, thrs):
        bit = jnp.left_shift(jnp.int32(1), 30 - it)
        return tuple(jnp.where(count_ge(rows, thr | bit) >= topk, thr | bit, thr)
                     for rows, thr in zip(groups, thrs))

    zero = jnp.zeros((grp_rows, 1), I32)
    thrs = tuple(jnp.where(count_ge(rows, zero) >= topk, zero, int_min) for rows in groups)
    thrs = lax.fori_loop(0, 31, search, thrs, unroll=DSA_SEARCH_UNROLL)
    thr = jnp.concatenate(thrs, axis=0)

    floor = jnp.maximum(thr, int_min + 1)
    key = key_ref[:, keys]
    sel_ref[:, keys] = jnp.where(key >= floor, 0.0, NEG_INF)
    n_ge = jnp.sum(jnp.where(key >= thr, 1.0, 0.0), axis=1, keepdims=True)
    tie_rows = (n_ge > topk) & (thr > int_min)

    @pl.when(jnp.max(tie_rows.astype(I32)) > 0)
    def _break_ties():
        upper = (lax.broadcasted_iota(I32, (LANES, LANES), 0)
                 < lax.broadcasted_iota(I32, (LANES, LANES), 1))
        upper = jnp.where(upper, 1.0, 0.0).astype(BF16)
        n_gt = jnp.sum(jnp.where(key_ref[:, keys] > thr, 1.0, 0.0), axis=1, keepdims=True)
        need = topk - n_gt
        before = jnp.zeros((qb, 1), F32)
        for c in range(kext // LANES):
            sl = slice(c * LANES, (c + 1) * LANES)
            keyc = key_ref[:, sl]
            eqc = keyc == thr
            eqf = jnp.where(eqc, 1.0, 0.0)
            rank = _dot(eqf.astype(BF16), upper) + before
            take = ((keyc > thr) | (eqc & (rank < need))) & (keyc > int_min)
            sel_ref[:, sl] = jnp.where(take, 0.0, NEG_INF)
            before = before + jnp.sum(eqf, axis=1, keepdims=True)

    selected = sel_ref[:, keys]
    kk = kk_ref[keys, :]
    vv = vv_ref[keys, :]
    pairs = [(q_ref[:, hp * LANES:(hp + 1) * LANES], kk, vv, selected) for hp in range(A_HEADS // 2)]
    for hp, (_, den, num) in enumerate(_head_pair_attention(pairs)):
        o_ref[:, hp * LANES:(hp + 1) * LANES] = (num / den).astype(o_ref.dtype)


def _dsa_kernel(*refs, seq, topk):
    i = pl.program_id(1)
    blocks_per_step = ATTN_KEY_STEP // DSA_Q_BLOCK
    for v in range(seq // ATTN_KEY_STEP):
        @pl.when(i // blocks_per_step == v)
        def _(v=v):
            _dsa_block(i, *refs, kext=(v + 1) * ATTN_KEY_STEP, topk=topk)


def _dsa_attention(iq, iw, ik4, qa, kk, vv, batch, seq):
    n = batch * seq
    qb = DSA_Q_BLOCK
    nq = seq // qb
    topk = min(DSA_TOPK, seq // 4)
    rows = lambda width: pl.BlockSpec((qb, width), lambda b, i: (b * nq + i, 0))
    whole = pl.BlockSpec((seq, LANES), lambda b, i: (b, 0))
    return pl.pallas_call(
        functools.partial(_dsa_kernel, seq=seq, topk=topk),
        grid=(batch, nq),
        in_specs=[rows(iq.shape[1]), rows(LANES), whole, rows(qa.shape[1]), whole, whole],
        out_specs=rows(qa.shape[1]),
        out_shape=jax.ShapeDtypeStruct((n, qa.shape[1]), BF16),
        scratch_shapes=[pltpu.VMEM((qb, seq), I32), pltpu.VMEM((qb, seq), F32)],
        compiler_params=_params(("parallel", "parallel")),
    )(iq, iw, ik4, qa, kk, vv)


def _moba_block(qi, q_ref, k_ref, v_ref, o_ref, *, kext, kt):
    bs = MOBA_BLOCK
    nb = kext // bs
    q2 = q_ref[...]
    k2 = k_ref[0:kext, :]
    v2 = v_ref[0:kext, :]
    nbp = max(nb, MOBA_GATE_ROWS)
    blk_row = lax.broadcasted_iota(I32, (nbp, kext), 0)
    blk_of_key = lax.broadcasted_iota(I32, (nbp, kext), 1) // bs
    member = jnp.where(blk_row == blk_of_key, 1.0, 0.0).astype(BF16)
    kmean = (_dot(member, k2) * (1.0 / bs)).astype(BF16)
    n_iota = lax.broadcasted_iota(I32, (nbp, 1), 0)
    past = n_iota < qi
    qrow = lax.broadcasted_iota(I32, (bs, 1), 0)
    kcol = lax.broadcasted_iota(I32, (1, kext), 1)
    own = (kcol >= qi * bs) & (kcol <= qi * bs + qrow)
    gates = [jnp.where(past, _dot_nt(kmean, jnp.where(_lane_group((1, LANES), HEAD_DIM, half), q2,
                                                      jnp.zeros_like(q2))), NEG_INF)
             for half in range(2)]

    def block_masks():
        chosen = []
        for gate in gates:
            rank = jnp.zeros((nbp, bs), I32)
            for m in range(nb):
                gm = gate[m:m + 1, :]
                beats = (gm > gate) | ((gm == gate) & (m < n_iota))
                rank = rank + beats.astype(I32)
            chosen.append(jnp.transpose(jnp.where(past & (rank < kt), 1.0, 0.0)).astype(BF16))
        return [(_dot(c, member) > 0.5) | own for c in chosen]

    (_, den, num), = _head_pair_attention([(q2, k2, v2, block_masks)])
    o_ref[...] = (num / den).astype(o_ref.dtype)


def _moba_kernel(*refs, seq):
    qi = pl.program_id(2)
    kt = min(MOBA_TOPK, seq // MOBA_BLOCK - 1)
    blocks_per_step = ATTN_KEY_STEP // MOBA_BLOCK
    for v in range(seq // ATTN_KEY_STEP):
        @pl.when(qi // blocks_per_step == v)
        def _(v=v):
            _moba_block(qi, *refs, kext=(v + 1) * ATTN_KEY_STEP, kt=kt)


def _moba_attention(q, k, v, batch, seq):
    n, width = q.shape
    bs = MOBA_BLOCK
    nb = seq // bs
    rows = pl.BlockSpec((bs, LANES), lambda b, hp, qi: (b * nb + qi, hp))
    whole = pl.BlockSpec((seq, LANES), lambda b, hp, qi: (b, hp))
    return pl.pallas_call(
        functools.partial(_moba_kernel, seq=seq),
        grid=(batch, width // LANES, nb),
        in_specs=[rows, whole, whole],
        out_specs=rows,
        out_shape=jax.ShapeDtypeStruct((n, width), BF16),
        compiler_params=_params(("parallel", "parallel", "parallel")),
    )(q, k, v)


def _rows(start, size, stride):
    return pl.ds(start, size) if stride == 1 else pl.ds(start, size, stride=stride)


def _largest_divisor(n, limit):
    return max(d for d in range(1, limit + 1) if n % d == 0)


def _repeat(trips, body):
    if trips == 1:
        body(0, 0)
    else:
        lax.fori_loop(0, trips, body, 0)


def _dilated_kernel(q_ref, k_ref, v_ref, o_ref, qp_ref, kp_ref, vp_ref, nat_ref, slab_ref, *, seq):
    blk = BAND_BLOCK
    ns = DILATED_SLABS
    slab_len = seq // ns

    for s in range(ns):
        rows = slice(s * slab_len, (s + 1) * slab_len)
        qp_ref[rows, :] = q_ref[pl.ds(s, slab_len, stride=ns), :]
        kp_ref[rows, :] = k_ref[pl.ds(s, slab_len, stride=ns), :]
        vp_ref[rows, :] = v_ref[pl.ds(s, slab_len, stride=ns), :]

    def operands(srcs, qstart, kstart, nk, stride):
        ksl = _rows(kstart, nk, stride)
        i = lax.broadcasted_iota(I32, (blk, 1), 0)
        j = lax.broadcasted_iota(I32, (1, nk), 1)
        if nk == blk:
            mask = j <= i
        else:
            mask = (j >= i) & (j <= i + blk)
        return (srcs[0][_rows(qstart, blk, stride), :].astype(BF16), srcs[1][ksl, :].astype(BF16),
                srcs[2][ksl, :].astype(BF16), mask)

    def merged(old, new):
        (m_old, l_old, a_old), (m_new, l_new, a_new) = old, new
        m_tot = jnp.maximum(m_old, m_new)
        w_old = jnp.exp2(m_old - m_tot)
        w_new = jnp.exp2(m_new - m_tot)
        return m_tot, w_old * l_old + w_new * l_new, w_old * a_old + w_new * a_new

    def run_units(srcs, state_ref, units, nk, stride, merge):
        results = _head_pair_attention([operands(srcs, qstart, kstart, nk, stride) for qstart, kstart in units])
        for (qstart, _), new in zip(units, results):
            qsl = _rows(qstart, blk, stride)
            if merge:
                new = merged(tuple(state_ref[t, qsl, :] for t in range(3)), new)
            for t in range(3):
                state_ref[t, qsl, :] = new[t]

    written = set()
    for window, dil in DILATED_CFG:
        assert window // dil == blk
        if dil % ns == 0:
            srcs, state_ref, stride, group_len = (qp_ref, kp_ref, vp_ref), slab_ref, dil // ns, slab_len
        else:
            srcs, state_ref, stride, group_len = (q_ref, k_ref, v_ref), nat_ref, dil, seq
        merge = id(state_ref) in written
        written.add(id(state_ref))
        nblk = seq // dil // blk
        ub = _largest_divisor(dil, DILATED_UNITS)

        def class_start(c, stride=stride, group_len=group_len):
            return (c // stride) * group_len + c % stride if group_len != seq else c

        def first_blocks(it, carry, srcs=srcs, state_ref=state_ref, stride=stride, merge=merge, ub=ub,
                         class_start=class_start):
            starts = [class_start(it * ub + u) for u in range(ub)]
            run_units(srcs, state_ref, [(st, st) for st in starts], blk, stride, merge)
            return carry

        _repeat(dil // ub, first_blocks)
        if nblk > 1:
            ul = _largest_divisor(nblk - 1, DILATED_UNITS)
            per_class = (nblk - 1) // ul

            def later_blocks(it, carry, srcs=srcs, state_ref=state_ref, stride=stride, merge=merge, ul=ul,
                             per_class=per_class, class_start=class_start):
                base, grp = class_start(it // per_class), it % per_class
                qstarts = [base + stride * blk * (1 + grp * ul + u) for u in range(ul)]
                run_units(srcs, state_ref, [(qs, qs - stride * blk) for qs in qstarts], 2 * blk, stride, merge)
                return carry

            _repeat(dil * per_class, later_blocks)

    for s in range(ns):
        nat_rows = pl.ds(s, slab_len, stride=ns)
        slab_rows = slice(s * slab_len, (s + 1) * slab_len)
        _, den, num = merged(tuple(nat_ref[t, nat_rows, :] for t in range(3)),
                             tuple(slab_ref[t, slab_rows, :] for t in range(3)))
        nat_ref[2, nat_rows, :] = num / den
    o_ref[...] = nat_ref[2].astype(o_ref.dtype)


def _dilated_attention(q, k, v, batch, seq):
    n, width = q.shape
    for window, dil in DILATED_CFG:
        assert seq % (dil * BAND_BLOCK) == 0
    assert {dil % DILATED_SLABS == 0 for _, dil in DILATED_CFG} == {True, False}
    whole = pl.BlockSpec((seq, LANES), lambda b, hp: (b, hp))
    return pl.pallas_call(
        functools.partial(_dilated_kernel, seq=seq),
        grid=(batch, width // LANES),
        in_specs=[whole, whole, whole],
        out_specs=whole,
        out_shape=jax.ShapeDtypeStruct((n, width), BF16),
        scratch_shapes=[pltpu.VMEM((seq, LANES), F32)] * 3 + [pltpu.VMEM((3, seq, LANES), F32)] * 2,
        compiler_params=_params(("parallel", "parallel")),
    )(q, k, v)


def _deepnorm(x, f, g, b):
    y = DEEPNORM_ALPHA * x + f
    mu = jnp.mean(y, axis=1, keepdims=True)
    yc = y - mu
    var = jnp.mean(yc * yc, axis=1, keepdims=True)
    return yc * lax.rsqrt(var + LN_EPS) * g + b


def _outproj_ln_kernel(*refs, n_parts):
    part_refs = refs[:n_parts]
    w_ref, x_ref, g_ref, b_ref, o_ref, oh_ref = refs[n_parts:]
    mix = None
    start = 0
    for p_ref in part_refs:
        width = p_ref.shape[1]
        t = _dot(p_ref[...], w_ref[start:start + width, :])
        mix = t if mix is None else mix + t
        start += width
    y = _deepnorm(x_ref[...], mix, g_ref[...], b_ref[...])
    o_ref[...] = y
    oh_ref[...] = y.astype(BF16)


def _outproj_ln(parts, w, x2d, g, b):
    n, d = x2d.shape
    tm = PROJ_ROWS
    row = lambda width: pl.BlockSpec((tm, width), lambda i: (i, 0))
    vec = pl.BlockSpec((1, d), lambda i: (0, 0))
    return pl.pallas_call(
        functools.partial(_outproj_ln_kernel, n_parts=len(parts)),
        grid=(n // tm,),
        in_specs=[row(p.shape[1]) for p in parts] + [pl.BlockSpec(w.shape, lambda i: (0, 0)), row(d), vec, vec],
        out_specs=[row(d), row(d)],
        out_shape=[jax.ShapeDtypeStruct((n, d), F32), jax.ShapeDtypeStruct((n, d), BF16)],
        compiler_params=_params(("parallel",)),
    )(*parts, w, x2d, g.reshape(1, d), b.reshape(1, d))


def _silu(h):
    return h * (1.0 / (1.0 + jnp.exp(-h)))


def _swiglu_ln_kernel(xh_ref, x_ref, w1_ref, w3_ref, w2_ref, g_ref, b_ref, o_ref, acc_ref):
    f = pl.program_id(1)
    xh = xh_ref[...]
    hid = (_silu(_dot(xh, w1_ref[...])) * _dot(xh, w3_ref[...])).astype(BF16)
    y = _dot(hid, w2_ref[...])

    @pl.when(f == 0)
    def _():
        acc_ref[...] = y

    @pl.when(f > 0)
    def _():
        acc_ref[...] += y

    @pl.when(f == pl.num_programs(1) - 1)
    def _():
        o_ref[...] = _deepnorm(x_ref[...], acc_ref[...], g_ref[...], b_ref[...])


def _swiglu_ln(xh, x2d, w1, w3, w2, g, b):
    n, d = x2d.shape
    dff = w1.shape[1]
    tm = FFN_ROWS
    nf = 2 if dff % (2 * LANES) == 0 else 1
    fc = dff // nf
    row = lambda: pl.BlockSpec((tm, d), lambda i, f: (i, 0))
    vec = pl.BlockSpec((1, d), lambda i, f: (0, 0))
    return pl.pallas_call(
        _swiglu_ln_kernel,
        grid=(n // tm, nf),
        in_specs=[row(), row(),
                  pl.BlockSpec((d, fc), lambda i, f: (0, f)),
                  pl.BlockSpec((d, fc), lambda i, f: (0, f)),
                  pl.BlockSpec((fc, d), lambda i, f: (f, 0)),
                  vec, vec],
        out_specs=row(),
        out_shape=jax.ShapeDtypeStruct((n, d), F32),
        scratch_shapes=[pltpu.VMEM((tm, d), F32)],
        compiler_params=_params(("parallel", "arbitrary")),
    )(xh, x2d, w1, w3, w2, g.reshape(1, d), b.reshape(1, d))


def _add_ln_kernel(x_ref, f_ref, g_ref, b_ref, o_ref):
    o_ref[...] = _deepnorm(x_ref[...], f_ref[...], g_ref[...], b_ref[...])


def _add_ln(x2d, f2d, g, b):
    n, d = x2d.shape
    tm = PROJ_ROWS
    row = pl.BlockSpec((tm, d), lambda i: (i, 0))
    vec = pl.BlockSpec((1, d), lambda i: (0, 0))
    return pl.pallas_call(
        _add_ln_kernel,
        grid=(n // tm,),
        in_specs=[row, row, vec, vec],
        out_specs=row,
        out_shape=jax.ShapeDtypeStruct((n, d), F32),
        compiler_params=_params(("parallel",)),
    )(x2d, f2d, g.reshape(1, d), b.reshape(1, d))


def _router_kernel(xh_ref, rw_ref, gate_ref, lposc_ref, lpost_ref, cum_ref, carry_ref, *, tiles_per_group):
    tm = xh_ref.shape[0]
    i = pl.program_id(0)

    @pl.when(i % tiles_per_group == 0)
    def _():
        carry_ref[...] = jnp.zeros_like(carry_ref)

    lane = lax.broadcasted_iota(I32, (1, LANES), 1)
    logits = jnp.where(lane < N_EXPERTS, _dot(xh_ref[...], rw_ref[...]), NEG_INF)
    m1 = jnp.max(logits, axis=1, keepdims=True)
    i1 = jnp.min(jnp.where(logits == m1, lane, LANES), axis=1, keepdims=True)
    rest = jnp.where(lane == i1, NEG_INF, logits)
    m2 = jnp.max(rest, axis=1, keepdims=True)
    i2 = jnp.min(jnp.where(rest == m2, lane, LANES), axis=1, keepdims=True)
    e2 = jnp.exp(m2 - m1)
    g1 = 1.0 / (1.0 + e2)
    g2 = e2 / (1.0 + e2)
    sel1 = lane == i1
    sel2 = lane == i2
    routed = sel1 | sel2
    gate_ref[...] = jnp.where(sel1, g1, jnp.where(sel2, g2, 0.0))
    routedf = jnp.where(routed, 1.0, 0.0)
    earlier = (lax.broadcasted_iota(I32, (tm, tm), 1) < lax.broadcasted_iota(I32, (tm, tm), 0))
    earlier = jnp.where(earlier, 1.0, 0.0).astype(BF16)
    carry = carry_ref[...]
    lpos = jnp.where(routed, _dot(earlier, routedf.astype(BF16)) + carry, -1.0)
    lposc_ref[...] = lpos
    lpost_ref[...] = jnp.transpose(lpos)[0:N_EXPERTS, :]
    carry = carry + jnp.sum(routedf, axis=0, keepdims=True)
    carry_ref[...] = carry
    cum_ref[...] = carry.astype(I32).reshape(1, 1, LANES)


def _router(xh, rw):
    n, d = xh.shape
    tm = ROUTER_ROWS
    nt = n // tm
    row = pl.BlockSpec((tm, LANES), lambda i: (i, 0))
    return pl.pallas_call(
        functools.partial(_router_kernel, tiles_per_group=MOE_GROUP // tm),
        grid=(nt,),
        in_specs=[pl.BlockSpec((tm, d), lambda i: (i, 0)), pl.BlockSpec(rw.shape, lambda i: (0, 0))],
        out_specs=[row, row,
                   pl.BlockSpec((N_EXPERTS, tm), lambda i: (0, i)),
                   pl.BlockSpec((1, 1, LANES), lambda i: (i, 0, 0))],
        out_shape=[jax.ShapeDtypeStruct((n, LANES), F32), jax.ShapeDtypeStruct((n, LANES), F32),
                   jax.ShapeDtypeStruct((N_EXPERTS, n), F32), jax.ShapeDtypeStruct((nt, 1, LANES), I32)],
        scratch_shapes=[pltpu.VMEM((1, LANES), F32)],
        compiler_params=_params(("arbitrary",)),
    )(xh, rw)


def _moe_kernel(bnd_ref, xh_ref, lpost_ref, lposc_ref, gatec_ref, w1_ref, w3_ref, w2_ref,
                o_ref, xg_ref, yacc_ref):
    tg = xh_ref.shape[0]
    win, ch, fr, align = MOE_WIN, ROUTER_ROWS, MOE_FFN_ROWS, MOE_ROW_ALIGN
    nch = tg // ch
    g, e, f = pl.program_id(0), pl.program_id(1), pl.program_id(2)
    nf = pl.num_programs(2)

    def bound(c):
        return bnd_ref[(g * (nch + 1) + c) * N_EXPERTS + e]

    cnt = bound(nch)
    n_ffn = (cnt + fr - 1) // fr

    def tok_chunk(c):
        return pl.ds(pl.multiple_of(c * ch, ch), ch)

    def windows(c):
        r0, r1 = bound(c), bound(c + 1)
        first = (r0 // align) * align
        return first, jnp.where(r1 > r0, (r1 - first + win - 1) // win, 0)

    @pl.when((e == 0) & (f == 0))
    def _():
        o_ref[...] = jnp.zeros_like(o_ref)

    @pl.when(f == 0)
    def _compact():
        def clear(s, carry):
            rows = pl.ds(pl.multiple_of(s * win, win), win)
            xg_ref[rows, :] = jnp.zeros((win, xg_ref.shape[1]), xg_ref.dtype)
            yacc_ref[rows, :] = jnp.zeros((win, yacc_ref.shape[1]), F32)
            return carry

        lax.fori_loop(0, (cnt + max(fr, win) + win - 1) // win, clear, 0)

        def per_chunk(c, carry):
            first, n_win = windows(c)
            lp = lpost_ref[pl.ds(e, 1), tok_chunk(c)]

            def per_window(w, carry2):
                start = pl.multiple_of(first + w * win, align)
                want = (lax.broadcasted_iota(I32, (win, 1), 0) + start).astype(F32)
                pick = jnp.where(lp == want, 1.0, 0.0).astype(BF16)
                rows = pl.ds(start, win)
                xg_ref[rows, :] += _dot(pick, xh_ref[tok_chunk(c), :]).astype(xg_ref.dtype)
                return carry2

            return lax.fori_loop(0, n_win, per_window, carry)

        lax.fori_loop(0, nch, per_chunk, 0)

    def ffn_rows(rows):
        xs = xg_ref[rows, :]
        hid = (_silu(_dot(xs, w1_ref[...])) * _dot(xs, w3_ref[...])).astype(BF16)
        yacc_ref[rows, :] += _dot(hid, w2_ref[...])

    def ffn_pair(p, carry):
        ffn_rows(pl.ds(pl.multiple_of(p * 2 * fr, align), 2 * fr))
        return carry

    lax.fori_loop(0, n_ffn // 2, ffn_pair, 0)

    @pl.when(n_ffn % 2 == 1)
    def _():
        ffn_rows(pl.ds(pl.multiple_of((n_ffn - 1) * fr, align), fr))

    @pl.when(f == nf - 1)
    def _scatter():
        lane = lax.broadcasted_iota(I32, (1, LANES), 1)

        def per_chunk(c, carry):
            first, n_win = windows(c)
            toks = tok_chunk(c)
            lp = jnp.sum(jnp.where(lane == e, lposc_ref[toks, :], 0.0), axis=1, keepdims=True)
            gt = jnp.sum(jnp.where(lane == e, gatec_ref[toks, :], 0.0), axis=1, keepdims=True)

            def per_window(w, carry2):
                start = pl.multiple_of(first + w * win, align)
                y = yacc_ref[pl.ds(start, win), :]
                y_hi = y.astype(BF16)
                y_lo = (y - y_hi.astype(F32)).astype(BF16)
                want = (lax.broadcasted_iota(I32, (1, win), 1) + start).astype(F32)
                place = jnp.where(lp == want, 1.0, 0.0).astype(BF16)
                o_ref[toks, :] += gt * (_dot(place, y_hi) + _dot(place, y_lo))
                return carry2

            return lax.fori_loop(0, n_win, per_window, carry)

        lax.fori_loop(0, nch, per_chunk, 0)


def _chunk_bounds(cum):
    nch = MOE_GROUP // ROUTER_ROWS
    per_group = cum[:, 0, :N_EXPERTS].reshape(-1, nch, N_EXPERTS)
    return jnp.pad(per_group, ((0, 0), (1, 0), (0, 0))).reshape(-1)


def _moe(bounds, xh, lpost, lposc, gatec, w1, w3, w2, layer):
    n, d = xh.shape
    tg = MOE_GROUP
    dff = w1.shape[3]
    fc = MOE_FF_CHUNK
    cap = -(-(tg + max(MOE_FFN_ROWS, MOE_WIN)) // MOE_WIN) * MOE_WIN
    once = pl.Buffered(1)
    grid_spec = pltpu.PrefetchScalarGridSpec(
        num_scalar_prefetch=1,
        grid=(n // tg, N_EXPERTS, dff // fc),
        in_specs=[pl.BlockSpec((tg, d), lambda g, e, f, c: (g, 0), pipeline_mode=once),
                  pl.BlockSpec((N_EXPERTS, tg), lambda g, e, f, c: (0, g), pipeline_mode=once),
                  pl.BlockSpec((tg, LANES), lambda g, e, f, c: (g, 0), pipeline_mode=once),
                  pl.BlockSpec((tg, LANES), lambda g, e, f, c: (g, 0), pipeline_mode=once),
                  pl.BlockSpec((None, None, d, fc), lambda g, e, f, c: (layer, e, 0, f)),
                  pl.BlockSpec((None, None, d, fc), lambda g, e, f, c: (layer, e, 0, f)),
                  pl.BlockSpec((None, None, fc, d), lambda g, e, f, c: (layer, e, f, 0))],
        out_specs=pl.BlockSpec((tg, d), lambda g, e, f, c: (g, 0)),
        scratch_shapes=[pltpu.VMEM((cap, d), BF16), pltpu.VMEM((cap, d), F32)],
    )
    return pl.pallas_call(
        _moe_kernel,
        grid_spec=grid_spec,
        out_shape=jax.ShapeDtypeStruct((n, d), F32),
        compiler_params=_params(("parallel", "arbitrary", "arbitrary")),
    )(bounds, xh, lpost, lposc, gatec, w1, w3, w2)


def _even_weight_layout(w_in):
    hd = HEAD_DIM
    widths = (A_HEADS * hd, hd, hd, IDX_HEADS * IDX_DIM, IDX_DIM, IDX_HEADS, B_HEADS * hd, B_HEADS * hd, B_HEADS * hd)
    offs = [0]
    for wd in widths:
        offs.append(offs[-1] + wd)
    qa, ka, va, iq, ik, iw, qb, kb, vb = (w_in[:, offs[j]:offs[j + 1]] for j in range(9))
    iw_pad = jnp.pad(iw, ((0, 0), (0, LANES - IDX_HEADS)))
    w = jnp.concatenate([qa, qb, kb, vb, ka, ka, va, va, iq, ik, ik, ik, ik, iw_pad], axis=1)
    mixw = B_HEADS * hd
    sections = ((0, A_HEADS * hd, hd, QK_SCALE),
                (512, mixw, hd, QK_SCALE),
                (1024, mixw, hd, None),
                (1536, mixw, None, None),
                (2048, LANES, hd, None),
                (2176, LANES, None, None),
                (2304, IDX_HEADS * IDX_DIM, IDX_DIM, None),
                (2560, LANES, IDX_DIM, None),
                (2688, LANES, None, None))
    dtypes = (BF16,) * 8 + (F32,)
    return w.astype(BF16), sections, dtypes


def kernel(x, even_w_in, even_w_out, even_ln1_g, even_ln1_b, even_w1, even_w3, even_w2, even_ln2_g, even_ln2_b, odd_w_in, odd_w_out, odd_ln1_g, odd_ln1_b, odd_router, odd_w1, odd_w3, odd_w2, odd_ln2_g, odd_ln2_b):
    batch, seq, d = x.shape
    n = batch * seq
    tables = _rope_tables(seq)
    x2d = x.reshape(n, d)
    odd_w1h, odd_w3h, odd_w2h = odd_w1.astype(BF16), odd_w3.astype(BF16), odd_w2.astype(BF16)
    for layer in range(DEPTH):
        i = layer // 2
        if layer % 2 == 0:
            w, sections, dtypes = _even_weight_layout(even_w_in[i])
            qa, qb, kb, vb, kk, vv, iq, ik4, iw = _project(x2d, w, tables, sections, dtypes, seq)
            o_a = _dsa_attention(iq, iw, ik4, qa, kk, vv, batch, seq)
            o_b = _moba_attention(qb, kb, vb, batch, seq)
            x2d, xh = _outproj_ln([o_a, o_b], even_w_out[i].astype(BF16), x2d, even_ln1_g[i], even_ln1_b[i])
            x2d = _swiglu_ln(xh, x2d, even_w1[i].astype(BF16), even_w3[i].astype(BF16), even_w2[i].astype(BF16),
                             even_ln2_g[i], even_ln2_b[i])
        else:
            mix = C_HEADS * HEAD_DIM
            sections = ((0, mix, HEAD_DIM, QK_SCALE), (mix, mix, HEAD_DIM, None), (2 * mix, mix, None, None))
            q, k, v = _project(x2d, odd_w_in[i].astype(BF16), tables, sections, (F32, F32, F32), seq)
            o = _dilated_attention(q, k, v, batch, seq)
            x2d, xh = _outproj_ln([o], odd_w_out[i].astype(BF16), x2d, odd_ln1_g[i], odd_ln1_b[i])
            rw = jnp.pad(odd_router[i], ((0, 0), (0, LANES - N_EXPERTS))).astype(BF16)
            gatec, lposc, lpost, cum = _router(xh, rw)
            ffn = _moe(_chunk_bounds(cum), xh, lpost, lposc, gatec, odd_w1h, odd_w3h, odd_w2h, i)
            x2d = _add_ln(x2d, ffn, odd_ln2_g[i], odd_ln2_b[i])
    return x2d.reshape(batch, seq, d)
```

```python
import functools

import jax
import jax.numpy as jnp
from jax import lax
from jax.experimental import pallas as pl
from jax.experimental.pallas import tpu as pltpu

F32 = jnp.float32
BF16 = jnp.bfloat16
I32 = jnp.int32

DEPTH = 4
HEAD_DIM = 64
ROPE_THETA = 10000.0
LN_EPS = 1e-5
A_HEADS = 8
IDX_HEADS = 8
IDX_DIM = 32
DSA_TOPK = 256
DSA_Q_BLOCK = 128
B_HEADS = 8
MOBA_BLOCK = 256
MOBA_TOPK = 3
C_HEADS = 16
DILATED_CFG = ((128, 1), (512, 4), (2048, 16))
BAND_BLOCK = 128
N_EXPERTS = 8
DEEPNORM_ALPHA = (2 * DEPTH) ** 0.25
QK_SCALE = HEAD_DIM ** -0.5 * 1.4426950408889634

LANES = 128
VMEM_LIMIT_BYTES = 52 * 1024 * 1024

PROJ_ROWS = 512
FFN_ROWS = 512
ROUTER_ROWS = 512
MOE_GROUP = 2048
MOE_WIN = 256
MOE_FFN_ROWS = 272
MOE_ROW_ALIGN = 16
MOE_FF_CHUNK = 512
DSA_SEARCH_GROUPS = 4
DSA_SEARCH_UNROLL = 4
MOBA_GATE_ROWS = 16
DILATED_UNITS = 4
DILATED_SLABS = 4
ATTN_KEY_STEP = 512

NEG_INF = float("-inf")
NEG_INF_KEY = -2139095041


def _params(semantics):
    return pltpu.CompilerParams(dimension_semantics=semantics, vmem_limit_bytes=VMEM_LIMIT_BYTES)


def _dot(a, b):
    return jnp.dot(a, b, preferred_element_type=F32)


def _dot_nt(a, b):
    return lax.dot_general(a, b, (((1,), (1,)), ((), ())), preferred_element_type=F32)


def _lane_group(shape, group, idx):
    lane = lax.broadcasted_iota(I32, shape, len(shape) - 1)
    return (lane // group) == idx


def _head_pair_attention(units):
    first = _lane_group((1, LANES), HEAD_DIM, 0)
    raw, unit_masks = [], []
    for q2, k2, _, mask in units:
        raw.append(_dot_nt(jnp.where(first, q2, jnp.zeros_like(q2)), k2))
        unit_masks.append(mask() if callable(mask) else mask)
        raw.append(_dot_nt(jnp.where(first, jnp.zeros_like(q2), q2), k2))
    scores = []
    for u, mask in enumerate(unit_masks):
        masks = mask if isinstance(mask, (tuple, list)) else (mask, mask)
        for h in range(2):
            if masks[h].dtype == jnp.bool_:
                scores.append(jnp.where(masks[h], raw[2 * u + h], NEG_INF))
            else:
                scores.append(raw[2 * u + h] + masks[h])
    maxes = [jnp.max(s, axis=1, keepdims=True) for s in scores]
    weights = [jnp.exp2(s - mx).astype(BF16) for s, mx in zip(scores, maxes)]
    results = []
    for u, (_, _, v2, _) in enumerate(units):
        ones = jnp.ones_like(v2)
        pv0 = _dot(weights[2 * u], jnp.where(first, v2, ones))
        pv1 = _dot(weights[2 * u + 1], jnp.where(first, ones, v2))
        den = jnp.where(first, pv0[:, LANES - 1:LANES], pv1[:, 0:1])
        results.append((jnp.where(first, maxes[2 * u], maxes[2 * u + 1]), den, jnp.where(first, pv0, pv1)))
    return results


def _stacked_head_pair_attention(units):
    first = _lane_group((1, LANES), HEAD_DIM, 0)
    raw = [_dot_nt(jnp.concatenate([jnp.where(first, q2, jnp.zeros_like(q2)),
                                    jnp.where(first, jnp.zeros_like(q2), q2)], axis=0), k2)
           for q2, k2, _, _ in units]
    stats, weights = [], []
    for r, (q2, _, _, mask) in zip(raw, units):
        rows = q2.shape[0]
        halves = [jnp.where(mask, r[h * rows:(h + 1) * rows], NEG_INF) for h in range(2)]
        maxes = [jnp.max(s, axis=1, keepdims=True) for s in halves]
        exps = [jnp.exp2(s - mx) for s, mx in zip(halves, maxes)]
        dens = [jnp.sum(e, axis=1, keepdims=True) for e in exps]
        stats.append((jnp.where(first, maxes[0], maxes[1]), jnp.where(first, dens[0], dens[1])))
        weights.append(jnp.concatenate(exps, axis=0).astype(BF16))
    results = []
    for (mx, den), w, (q2, _, v2, _) in zip(stats, weights, units):
        rows = q2.shape[0]
        pv = _dot(w, v2)
        results.append((mx, den, jnp.where(first, pv[:rows], pv[rows:])))
    return results


def _rope_lanes(t, cosf, sinf, half):
    lane = lax.broadcasted_iota(I32, t.shape, 1)
    first = (lane % (2 * half)) < half
    swapped = jnp.where(first, pltpu.roll(t, LANES - half, 1), pltpu.roll(t, half, 1))
    return t * cosf + swapped * sinf


def _proj_kernel(x_ref, w_ref, c64_ref, s64_ref, c32_ref, s32_ref, *out_refs, sections, mean_section):
    x = x_ref[...].astype(BF16)
    for idx, (o_ref, (start, width, rope, scale)) in enumerate(zip(out_refs, sections)):
        h = _dot(x, w_ref[:, start:start + width])
        if rope is not None:
            cosf = (c64_ref if rope == HEAD_DIM else c32_ref)[...]
            sinf = (s64_ref if rope == HEAD_DIM else s32_ref)[...]
            parts = [_rope_lanes(h[:, g * LANES:(g + 1) * LANES], cosf, sinf, rope // 2)
                     for g in range(width // LANES)]
            h = parts[0] if len(parts) == 1 else jnp.concatenate(parts, axis=1)
        if scale is not None:
            h = h * scale
        o_ref[...] = h.astype(o_ref.dtype)
        if idx == mean_section:
            m_ref = out_refs[len(sections)]
            for blk in range(h.shape[0] // MOBA_BLOCK):
                rows = h[blk * MOBA_BLOCK:(blk + 1) * MOBA_BLOCK]
                m_ref[0, blk:blk + 1, :] = jnp.sum(rows, axis=0, keepdims=True) * (1.0 / MOBA_BLOCK)


def _project(x2d, w, tables, sections, out_dtypes, seq, mean_section=None):
    n, d = x2d.shape
    tm = PROJ_ROWS
    pos_blocks = seq // tm
    tab_spec = pl.BlockSpec((tm, LANES), lambda i: (i % pos_blocks, 0))
    out_specs = [pl.BlockSpec((tm, sec[1]), lambda i: (i, 0)) for sec in sections]
    out_shape = [jax.ShapeDtypeStruct((n, sec[1]), dt) for sec, dt in zip(sections, out_dtypes)]
    if mean_section is not None:
        per_tile, width = tm // MOBA_BLOCK, sections[mean_section][1]
        out_specs.append(pl.BlockSpec((1, per_tile, width), lambda i: (i, 0, 0)))
        out_shape.append(jax.ShapeDtypeStruct((n // tm, per_tile, width), F32))
    outs = pl.pallas_call(
        functools.partial(_proj_kernel, sections=sections, mean_section=mean_section),
        grid=(n // tm,),
        in_specs=[pl.BlockSpec((tm, d), lambda i: (i, 0)),
                  pl.BlockSpec(w.shape, lambda i: (0, 0)),
                  tab_spec, tab_spec, tab_spec, tab_spec],
        out_specs=out_specs,
        out_shape=out_shape,
        compiler_params=_params(("parallel",)),
    )(x2d, w, *tables)
    outs = list(outs)
    if mean_section is not None:
        outs[-1] = outs[-1].reshape(n // MOBA_BLOCK, -1)
    return outs


def _rope_tables(seq):
    out = []
    for dim in (HEAD_DIM, IDX_DIM):
        inv = ROPE_THETA ** (-jnp.arange(0, dim, 2, dtype=F32) / dim)
        ang = jnp.arange(seq, dtype=F32)[:, None] * inv[None, :]
        cos, sin = jnp.cos(ang), jnp.sin(ang)
        reps = LANES // dim
        out.append(jnp.tile(jnp.concatenate([cos, cos], -1), (1, reps)))
        out.append(jnp.tile(jnp.concatenate([-sin, sin], -1), (1, reps)))
    return out


def _dsa_block(i, iq_ref, iw_ref, ik_ref, q_ref, kk_ref, vv_ref, o_ref, key_ref, sel_ref, *, kext, topk):
    qb = DSA_Q_BLOCK
    keys = slice(0, kext)
    qpos = i * qb + lax.broadcasted_iota(I32, (qb, 1), 0)
    kpos = lax.broadcasted_iota(I32, (1, kext), 1)
    causal = kpos <= qpos

    ik = ik_ref[keys, :]
    iw = iw_ref[...] * (IDX_DIM ** -0.5 * IDX_HEADS ** -0.5)
    heads_per_group = LANES // IDX_DIM
    score = jnp.zeros((qb, kext), F32)
    for h in range(IDX_HEADS):
        grp, j = divmod(h, heads_per_group)
        iqg = iq_ref[:, grp * LANES:(grp + 1) * LANES]
        iqm = jnp.where(_lane_group((1, LANES), IDX_DIM, j), iqg, jnp.zeros_like(iqg))
        rel = jnp.maximum(_dot_nt(iqm, ik), 0.0)
        score = score + rel * iw[:, h:h + 1]
    bits = pltpu.bitcast(score, I32)
    neg = bits >> 31
    int_min = jnp.int32(-2 ** 31)
    key_ref[:, keys] = jnp.where(causal, (bits ^ (neg & 0x7FFFFFFF)) - neg, int_min)

    grp_rows = qb // DSA_SEARCH_GROUPS
    groups = [slice(g * grp_rows, (g + 1) * grp_rows) for g in range(DSA_SEARCH_GROUPS)]

    def count_ge(rows, t):
        return jnp.sum(jnp.where(key_ref[rows, keys] >= t, 1.0, 0.0), axis=1, keepdims=True)

    def search(it, thrs):
        bit = jnp.left_shift(jnp.int32(1), 30 - it)
        return tuple(jnp.where(count_ge(rows, thr | bit) >= topk, thr | bit, thr)
                     for rows, thr in zip(groups, thrs))

    zero = jnp.zeros((grp_rows, 1), I32)
    thrs = tuple(jnp.where(count_ge(rows, zero) >= topk, zero, int_min) for rows in groups)
    thrs = lax.fori_loop(0, 31, search, thrs, unroll=DSA_SEARCH_UNROLL)
    thr = jnp.concatenate(thrs, axis=0)

    floor = jnp.maximum(thr, int_min + 1)
    key = key_ref[:, keys]
    sel_ref[:, keys] = jnp.where(key >= floor, 0.0, NEG_INF)
    n_ge = jnp.sum(jnp.where(key >= thr, 1.0, 0.0), axis=1, keepdims=True)
    tie_rows = (n_ge > topk) & (thr > int_min)

    @pl.when(jnp.max(tie_rows.astype(I32)) > 0)
    def _break_ties():
        upper = (lax.broadcasted_iota(I32, (LANES, LANES), 0)
                 < lax.broadcasted_iota(I32, (LANES, LANES), 1))
        upper = jnp.where(upper, 1.0, 0.0).astype(BF16)
        n_gt = jnp.sum(jnp.where(key_ref[:, keys] > thr, 1.0, 0.0), axis=1, keepdims=True)
        need = topk - n_gt
        before = jnp.zeros((qb, 1), F32)
        for c in range(kext // LANES):
            sl = slice(c * LANES, (c + 1) * LANES)
            keyc = key_ref[:, sl]
            eqc = keyc == thr
            eqf = jnp.where(eqc, 1.0, 0.0)
            rank = _dot(eqf.astype(BF16), upper) + before
            take = ((keyc > thr) | (eqc & (rank < need))) & (keyc > int_min)
            sel_ref[:, sl] = jnp.where(take, 0.0, NEG_INF)
            before = before + jnp.sum(eqf, axis=1, keepdims=True)

    selected = sel_ref[:, keys]
    kk = kk_ref[keys, :]
    vv = vv_ref[keys, :]
    pairs = [(q_ref[:, hp * LANES:(hp + 1) * LANES], kk, vv, selected) for hp in range(A_HEADS // 2)]
    for hp, (_, den, num) in enumerate(_head_pair_attention(pairs)):
        o_ref[:, hp * LANES:(hp + 1) * LANES] = (num / den).astype(o_ref.dtype)


def _dsa_kernel(*refs, seq, topk):
    i = pl.program_id(1)
    blocks_per_step = ATTN_KEY_STEP // DSA_Q_BLOCK
    for v in range(seq // ATTN_KEY_STEP):
        @pl.when(i // blocks_per_step == v)
        def _(v=v):
            _dsa_block(i, *refs, kext=(v + 1) * ATTN_KEY_STEP, topk=topk)


def _dsa_attention(iq, iw, ik4, qa, kk, vv, batch, seq):
    n = batch * seq
    qb = DSA_Q_BLOCK
    nq = seq // qb
    topk = min(DSA_TOPK, seq // 4)
    rows = lambda width: pl.BlockSpec((qb, width), lambda b, i: (b * nq + i, 0))
    whole = pl.BlockSpec((seq, LANES), lambda b, i: (b, 0))
    return pl.pallas_call(
        functools.partial(_dsa_kernel, seq=seq, topk=topk),
        grid=(batch, nq),
        in_specs=[rows(iq.shape[1]), rows(LANES), whole, rows(qa.shape[1]), whole, whole],
        out_specs=rows(qa.shape[1]),
        out_shape=jax.ShapeDtypeStruct((n, qa.shape[1]), BF16),
        scratch_shapes=[pltpu.VMEM((qb, seq), I32), pltpu.VMEM((qb, seq), F32)],
        compiler_params=_params(("parallel", "parallel")),
    )(iq, iw, ik4, qa, kk, vv)


def _moba_block(qi, q_ref, k_ref, v_ref, kmean_ref, o_ref, *, kext, kt):
    bs = MOBA_BLOCK
    nb = kext // bs
    q2 = q_ref[...]
    k2 = k_ref[0:kext, :]
    v2 = v_ref[0:kext, :]
    nb_all = kmean_ref.shape[0]
    nbp = max(nb_all, MOBA_GATE_ROWS)
    blk_row = lax.broadcasted_iota(I32, (nbp, kext), 0)
    blk_of_key = lax.broadcasted_iota(I32, (nbp, kext), 1) // bs
    member = jnp.where(blk_row == blk_of_key, 1.0, 0.0).astype(BF16)
    kmean = jnp.concatenate([kmean_ref[...], jnp.zeros((nbp - nb_all, LANES), F32)], axis=0).astype(BF16)
    n_iota = lax.broadcasted_iota(I32, (nbp, 1), 0)
    past = n_iota < qi
    qrow = lax.broadcasted_iota(I32, (bs, 1), 0)
    kcol = lax.broadcasted_iota(I32, (1, kext), 1)
    own = (kcol >= qi * bs) & (kcol <= qi * bs + qrow)
    gates = [jnp.where(past, _dot_nt(kmean, jnp.where(_lane_group((1, LANES), HEAD_DIM, half), q2,
                                                      jnp.zeros_like(q2))), NEG_INF)
             for half in range(2)]

    def block_masks():
        chosen = []
        for gate in gates:
            rank = jnp.zeros((nbp, bs), I32)
            for m in range(nb):
                gm = gate[m:m + 1, :]
                beats = (gm > gate) | ((gm == gate) & (m < n_iota))
                rank = rank + beats.astype(I32)
            chosen.append(jnp.transpose(jnp.where(past & (rank < kt), 1.0, 0.0)).astype(BF16))
        return [(_dot(c, member) > 0.5) | own for c in chosen]

    (_, den, num), = _head_pair_attention([(q2, k2, v2, block_masks)])
    o_ref[...] = (num / den).astype(o_ref.dtype)


def _moba_kernel(*refs, seq):
    qi = pl.program_id(2)
    kt = min(MOBA_TOPK, seq // MOBA_BLOCK - 1)
    blocks_per_step = ATTN_KEY_STEP // MOBA_BLOCK
    for v in range(seq // ATTN_KEY_STEP):
        @pl.when(qi // blocks_per_step == v)
        def _(v=v):
            _moba_block(qi, *refs, kext=(v + 1) * ATTN_KEY_STEP, kt=kt)


def _moba_attention(q, k, v, kmean, batch, seq):
    n, width = q.shape
    bs = MOBA_BLOCK
    nb = seq // bs
    rows = pl.BlockSpec((bs, LANES), lambda b, hp, qi: (b * nb + qi, hp))
    whole = pl.BlockSpec((seq, LANES), lambda b, hp, qi: (b, hp))
    means = pl.BlockSpec((nb, LANES), lambda b, hp, qi: (b, hp))
    return pl.pallas_call(
        functools.partial(_moba_kernel, seq=seq),
        grid=(batch, width // LANES, nb),
        in_specs=[rows, whole, whole, means],
        out_specs=rows,
        out_shape=jax.ShapeDtypeStruct((n, width), BF16),
        compiler_params=_params(("parallel", "parallel", "parallel")),
    )(q, k, v, kmean)


def _rows(start, size, stride):
    return pl.ds(start, size) if stride == 1 else pl.ds(start, size, stride=stride)


def _largest_divisor(n, limit):
    return max(d for d in range(1, limit + 1) if n % d == 0)


def _repeat(trips, body):
    if trips == 1:
        body(0, 0)
    else:
        lax.fori_loop(0, trips, body, 0)


def _dilated_kernel(q_ref, k_ref, v_ref, o_ref, qp_ref, kp_ref, vp_ref, nat_ref, slab_ref, *, seq):
    blk = BAND_BLOCK
    ns = DILATED_SLABS
    slab_len = seq // ns

    for s in range(ns):
        rows = slice(s * slab_len, (s + 1) * slab_len)
        qp_ref[rows, :] = q_ref[pl.ds(s, slab_len, stride=ns), :]
        kp_ref[rows, :] = k_ref[pl.ds(s, slab_len, stride=ns), :]
        vp_ref[rows, :] = v_ref[pl.ds(s, slab_len, stride=ns), :]

    def operands(srcs, qstart, kstart, nk, stride):
        ksl = _rows(kstart, nk, stride)
        i = lax.broadcasted_iota(I32, (blk, 1), 0)
        j = lax.broadcasted_iota(I32, (1, nk), 1)
        if nk == blk:
            mask = j <= i
        else:
            mask = (j >= i) & (j <= i + blk)
        return (srcs[0][_rows(qstart, blk, stride), :].astype(BF16), srcs[1][ksl, :].astype(BF16),
                srcs[2][ksl, :].astype(BF16), mask)

    def merged(old, new):
        (m_old, l_old, a_old), (m_new, l_new, a_new) = old, new
        m_tot = jnp.maximum(m_old, m_new)
        w_old = jnp.exp2(m_old - m_tot)
        w_new = jnp.exp2(m_new - m_tot)
        return m_tot, w_old * l_old + w_new * l_new, w_old * a_old + w_new * a_new

    def run_units(srcs, state_ref, units, nk, stride, merge):
        results = _stacked_head_pair_attention([operands(srcs, qstart, kstart, nk, stride)
                                                for qstart, kstart in units])
        for (qstart, _), new in zip(units, results):
            qsl = _rows(qstart, blk, stride)
            if merge:
                new = merged(tuple(state_ref[t, qsl, :] for t in range(3)), new)
            for t in range(3):
                state_ref[t, qsl, :] = new[t]

    written = set()
    for window, dil in DILATED_CFG:
        assert window // dil == blk
        if dil % ns == 0:
            srcs, state_ref, stride, group_len = (qp_ref, kp_ref, vp_ref), slab_ref, dil // ns, slab_len
        else:
            srcs, state_ref, stride, group_len = (q_ref, k_ref, v_ref), nat_ref, dil, seq
        merge = id(state_ref) in written
        written.add(id(state_ref))
        nblk = seq // dil // blk
        ub = _largest_divisor(dil, DILATED_UNITS)

        def class_start(c, stride=stride, group_len=group_len):
            return (c // stride) * group_len + c % stride if group_len != seq else c

        def first_blocks(it, carry, srcs=srcs, state_ref=state_ref, stride=stride, merge=merge, ub=ub,
                         class_start=class_start):
            starts = [class_start(it * ub + u) for u in range(ub)]
            run_units(srcs, state_ref, [(st, st) for st in starts], blk, stride, merge)
            return carry

        _repeat(dil // ub, first_blocks)
        if nblk > 1:
            ul = _largest_divisor(nblk - 1, DILATED_UNITS)
            per_class = (nblk - 1) // ul

            def later_blocks(it, carry, srcs=srcs, state_ref=state_ref, stride=stride, merge=merge, ul=ul,
                             per_class=per_class, class_start=class_start):
                base, grp = class_start(it // per_class), it % per_class
                qstarts = [base + stride * blk * (1 + grp * ul + u) for u in range(ul)]
                run_units(srcs, state_ref, [(qs, qs - stride * blk) for qs in qstarts], 2 * blk, stride, merge)
                return carry

            _repeat(dil * per_class, later_blocks)

    for s in range(ns):
        nat_rows = pl.ds(s, slab_len, stride=ns)
        slab_rows = slice(s * slab_len, (s + 1) * slab_len)
        _, den, num = merged(tuple(nat_ref[t, nat_rows, :] for t in range(3)),
                             tuple(slab_ref[t, slab_rows, :] for t in range(3)))
        nat_ref[2, nat_rows, :] = num / den
    o_ref[...] = nat_ref[2].astype(o_ref.dtype)


def _dilated_attention(q, k, v, batch, seq):
    n, width = q.shape
    for window, dil in DILATED_CFG:
        assert seq % (dil * BAND_BLOCK) == 0
    assert {dil % DILATED_SLABS == 0 for _, dil in DILATED_CFG} == {True, False}
    whole = pl.BlockSpec((seq, LANES), lambda b, hp: (b, hp))
    return pl.pallas_call(
        functools.partial(_dilated_kernel, seq=seq),
        grid=(batch, width // LANES),
        in_specs=[whole, whole, whole],
        out_specs=whole,
        out_shape=jax.ShapeDtypeStruct((n, width), BF16),
        scratch_shapes=[pltpu.VMEM((seq, LANES), F32)] * 3 + [pltpu.VMEM((3, seq, LANES), F32)] * 2,
        compiler_params=_params(("parallel", "parallel")),
    )(q, k, v)


def _deepnorm(x, f, g, b):
    y = DEEPNORM_ALPHA * x + f
    mu = jnp.mean(y, axis=1, keepdims=True)
    yc = y - mu
    var = jnp.mean(yc * yc, axis=1, keepdims=True)
    return yc * lax.rsqrt(var + LN_EPS) * g + b


def _outproj_ln_kernel(*refs, n_parts):
    part_refs = refs[:n_parts]
    w_ref, x_ref, g_ref, b_ref, o_ref, oh_ref = refs[n_parts:]
    mix = None
    start = 0
    for p_ref in part_refs:
        width = p_ref.shape[1]
        t = _dot(p_ref[...], w_ref[start:start + width, :])
        mix = t if mix is None else mix + t
        start += width
    y = _deepnorm(x_ref[...], mix, g_ref[...], b_ref[...])
    o_ref[...] = y
    oh_ref[...] = y.astype(BF16)


def _outproj_ln(parts, w, x2d, g, b):
    n, d = x2d.shape
    tm = PROJ_ROWS
    row = lambda width: pl.BlockSpec((tm, width), lambda i: (i, 0))
    vec = pl.BlockSpec((1, d), lambda i: (0, 0))
    return pl.pallas_call(
        functools.partial(_outproj_ln_kernel, n_parts=len(parts)),
        grid=(n // tm,),
        in_specs=[row(p.shape[1]) for p in parts] + [pl.BlockSpec(w.shape, lambda i: (0, 0)), row(d), vec, vec],
        out_specs=[row(d), row(d)],
        out_shape=[jax.ShapeDtypeStruct((n, d), F32), jax.ShapeDtypeStruct((n, d), BF16)],
        compiler_params=_params(("parallel",)),
    )(*parts, w, x2d, g.reshape(1, d), b.reshape(1, d))


def _silu(h):
    return h * (1.0 / (1.0 + jnp.exp(-h)))


def _swiglu_ln_kernel(xh_ref, x_ref, w1_ref, w3_ref, w2_ref, g_ref, b_ref, o_ref, acc_ref):
    f = pl.program_id(1)
    xh = xh_ref[...]
    hid = (_silu(_dot(xh, w1_ref[...])) * _dot(xh, w3_ref[...])).astype(BF16)
    y = _dot(hid, w2_ref[...])

    @pl.when(f == 0)
    def _():
        acc_ref[...] = y

    @pl.when(f > 0)
    def _():
        acc_ref[...] += y

    @pl.when(f == pl.num_programs(1) - 1)
    def _():
        o_ref[...] = _deepnorm(x_ref[...], acc_ref[...], g_ref[...], b_ref[...])


def _swiglu_ln(xh, x2d, w1, w3, w2, g, b):
    n, d = x2d.shape
    dff = w1.shape[1]
    tm = FFN_ROWS
    nf = 2 if dff % (2 * LANES) == 0 else 1
    fc = dff // nf
    row = lambda: pl.BlockSpec((tm, d), lambda i, f: (i, 0))
    vec = pl.BlockSpec((1, d), lambda i, f: (0, 0))
    return pl.pallas_call(
        _swiglu_ln_kernel,
        grid=(n // tm, nf),
        in_specs=[row(), row(),
                  pl.BlockSpec((d, fc), lambda i, f: (0, f)),
                  pl.BlockSpec((d, fc), lambda i, f: (0, f)),
                  pl.BlockSpec((fc, d), lambda i, f: (f, 0)),
                  vec, vec],
        out_specs=row(),
        out_shape=jax.ShapeDtypeStruct((n, d), F32),
        scratch_shapes=[pltpu.VMEM((tm, d), F32)],
        compiler_params=_params(("parallel", "arbitrary")),
    )(xh, x2d, w1, w3, w2, g.reshape(1, d), b.reshape(1, d))


def _add_ln_kernel(x_ref, f_ref, g_ref, b_ref, o_ref):
    o_ref[...] = _deepnorm(x_ref[...], f_ref[...], g_ref[...], b_ref[...])


def _add_ln(x2d, f2d, g, b):
    n, d = x2d.shape
    tm = PROJ_ROWS
    row = pl.BlockSpec((tm, d), lambda i: (i, 0))
    vec = pl.BlockSpec((1, d), lambda i: (0, 0))
    return pl.pallas_call(
        _add_ln_kernel,
        grid=(n // tm,),
        in_specs=[row, row, vec, vec],
        out_specs=row,
        out_shape=jax.ShapeDtypeStruct((n, d), F32),
        compiler_params=_params(("parallel",)),
    )(x2d, f2d, g.reshape(1, d), b.reshape(1, d))


def _router_kernel(xh_ref, rw_ref, gate_ref, lposc_ref, lpost_ref, cum_ref, carry_ref, *, tiles_per_group):
    tm = xh_ref.shape[0]
    i = pl.program_id(0)

    @pl.when(i % tiles_per_group == 0)
    def _():
        carry_ref[...] = jnp.zeros_like(carry_ref)

    lane = lax.broadcasted_iota(I32, (1, LANES), 1)
    logits = jnp.where(lane < N_EXPERTS, _dot(xh_ref[...], rw_ref[...]), NEG_INF)
    m1 = jnp.max(logits, axis=1, keepdims=True)
    i1 = jnp.min(jnp.where(logits == m1, lane, LANES), axis=1, keepdims=True)
    rest = jnp.where(lane == i1, NEG_INF, logits)
    m2 = jnp.max(rest, axis=1, keepdims=True)
    i2 = jnp.min(jnp.where(rest == m2, lane, LANES), axis=1, keepdims=True)
    e2 = jnp.exp(m2 - m1)
    g1 = 1.0 / (1.0 + e2)
    g2 = e2 / (1.0 + e2)
    sel1 = lane == i1
    sel2 = lane == i2
    routed = sel1 | sel2
    gate_ref[...] = jnp.where(sel1, g1, jnp.where(sel2, g2, 0.0))
    routedf = jnp.where(routed, 1.0, 0.0)
    earlier = (lax.broadcasted_iota(I32, (tm, tm), 1) < lax.broadcasted_iota(I32, (tm, tm), 0))
    earlier = jnp.where(earlier, 1.0, 0.0).astype(BF16)
    carry = carry_ref[...]
    lpos = jnp.where(routed, _dot(earlier, routedf.astype(BF16)) + carry, -1.0)
    lposc_ref[...] = lpos
    lpost_ref[...] = jnp.transpose(lpos)[0:N_EXPERTS, :]
    carry = carry + jnp.sum(routedf, axis=0, keepdims=True)
    carry_ref[...] = carry
    cum_ref[...] = carry.astype(I32).reshape(1, 1, LANES)


def _router(xh, rw):
    n, d = xh.shape
    tm = ROUTER_ROWS
    nt = n // tm
    row = pl.BlockSpec((tm, LANES), lambda i: (i, 0))
    return pl.pallas_call(
        functools.partial(_router_kernel, tiles_per_group=MOE_GROUP // tm),
        grid=(nt,),
        in_specs=[pl.BlockSpec((tm, d), lambda i: (i, 0)), pl.BlockSpec(rw.shape, lambda i: (0, 0))],
        out_specs=[row, row,
                   pl.BlockSpec((N_EXPERTS, tm), lambda i: (0, i)),
                   pl.BlockSpec((1, 1, LANES), lambda i: (i, 0, 0))],
        out_shape=[jax.ShapeDtypeStruct((n, LANES), F32), jax.ShapeDtypeStruct((n, LANES), F32),
                   jax.ShapeDtypeStruct((N_EXPERTS, n), F32), jax.ShapeDtypeStruct((nt, 1, LANES), I32)],
        scratch_shapes=[pltpu.VMEM((1, LANES), F32)],
        compiler_params=_params(("arbitrary",)),
    )(xh, rw)


def _moe_kernel(bnd_ref, xh_ref, lpost_ref, lposc_ref, gatec_ref, w1_ref, w3_ref, w2_ref,
                o_ref, xg_ref, yacc_ref):
    tg = xh_ref.shape[0]
    win, ch, fr, align = MOE_WIN, ROUTER_ROWS, MOE_FFN_ROWS, MOE_ROW_ALIGN
    nch = tg // ch
    g, e, f = pl.program_id(0), pl.program_id(1), pl.program_id(2)
    nf = pl.num_programs(2)

    def bound(c):
        return bnd_ref[(g * (nch + 1) + c) * N_EXPERTS + e]

    cnt = bound(nch)
    n_ffn = (cnt + fr - 1) // fr

    def tok_chunk(c):
        return pl.ds(pl.multiple_of(c * ch, ch), ch)

    def windows(c):
        r0, r1 = bound(c), bound(c + 1)
        first = (r0 // align) * align
        return first, jnp.where(r1 > r0, (r1 - first + win - 1) // win, 0)

    @pl.when((e == 0) & (f == 0))
    def _():
        o_ref[...] = jnp.zeros_like(o_ref)

    @pl.when(f == 0)
    def _compact():
        def clear(s, carry):
            rows = pl.ds(pl.multiple_of(s * win, win), win)
            xg_ref[rows, :] = jnp.zeros((win, xg_ref.shape[1]), xg_ref.dtype)
            yacc_ref[rows, :] = jnp.zeros((win, yacc_ref.shape[1]), F32)
            return carry

        lax.fori_loop(0, (cnt + max(fr, win) + win - 1) // win, clear, 0)

        def per_chunk(c, carry):
            first, n_win = windows(c)
            lp = lpost_ref[pl.ds(e, 1), tok_chunk(c)]

            def per_window(w, carry2):
                start = pl.multiple_of(first + w * win, align)
                want = (lax.broadcasted_iota(I32, (win, 1), 0) + start).astype(F32)
                pick = jnp.where(lp == want, 1.0, 0.0).astype(BF16)
                rows = pl.ds(start, win)
                xg_ref[rows, :] += _dot(pick, xh_ref[tok_chunk(c), :]).astype(xg_ref.dtype)
                return carry2

            return lax.fori_loop(0, n_win, per_window, carry)

        lax.fori_loop(0, nch, per_chunk, 0)

    def ffn_rows(rows):
        xs = xg_ref[rows, :]
        hid = (_silu(_dot(xs, w1_ref[...])) * _dot(xs, w3_ref[...])).astype(BF16)
        yacc_ref[rows, :] += _dot(hid, w2_ref[...])

    def ffn_pair(p, carry):
        ffn_rows(pl.ds(pl.multiple_of(p * 2 * fr, align), 2 * fr))
        return carry

    lax.fori_loop(0, n_ffn // 2, ffn_pair, 0)

    @pl.when(n_ffn % 2 == 1)
    def _():
        ffn_rows(pl.ds(pl.multiple_of((n_ffn - 1) * fr, align), fr))

    @pl.when(f == nf - 1)
    def _scatter():
        lane = lax.broadcasted_iota(I32, (1, LANES), 1)

        def per_chunk(c, carry):
            first, n_win = windows(c)
            toks = tok_chunk(c)
            lp = jnp.sum(jnp.where(lane == e, lposc_ref[toks, :], 0.0), axis=1, keepdims=True)
            gt = jnp.sum(jnp.where(lane == e, gatec_ref[toks, :], 0.0), axis=1, keepdims=True)

            def per_window(w, carry2):
                start = pl.multiple_of(first + w * win, align)
                y = yacc_ref[pl.ds(start, win), :]
                y_hi = y.astype(BF16)
                y_lo = (y - y_hi.astype(F32)).astype(BF16)
                want = (lax.broadcasted_iota(I32, (1, win), 1) + start).astype(F32)
                place = jnp.where(lp == want, 1.0, 0.0).astype(BF16)
                o_ref[toks, :] += gt * (_dot(place, y_hi) + _dot(place, y_lo))
                return carry2

            return lax.fori_loop(0, n_win, per_window, carry)

        lax.fori_loop(0, nch, per_chunk, 0)


def _chunk_bounds(cum):
    nch = MOE_GROUP // ROUTER_ROWS
    per_group = cum[:, 0, :N_EXPERTS].reshape(-1, nch, N_EXPERTS)
    return jnp.pad(per_group, ((0, 0), (1, 0), (0, 0))).reshape(-1)


def _moe(bounds, xh, lpost, lposc, gatec, w1, w3, w2, layer):
    n, d = xh.shape
    tg = MOE_GROUP
    dff = w1.shape[3]
    fc = MOE_FF_CHUNK
    cap = -(-(tg + max(MOE_FFN_ROWS, MOE_WIN)) // MOE_WIN) * MOE_WIN
    grid_spec = pltpu.PrefetchScalarGridSpec(
        num_scalar_prefetch=1,
        grid=(n // tg, N_EXPERTS, dff // fc),
        in_specs=[pl.BlockSpec((tg, d), lambda g, e, f, c: (g, 0)),
                  pl.BlockSpec((N_EXPERTS, tg), lambda g, e, f, c: (0, g)),
                  pl.BlockSpec((tg, LANES), lambda g, e, f, c: (g, 0)),
                  pl.BlockSpec((tg, LANES), lambda g, e, f, c: (g, 0)),
                  pl.BlockSpec((None, None, d, fc), lambda g, e, f, c: (layer, e, 0, f)),
                  pl.BlockSpec((None, None, d, fc), lambda g, e, f, c: (layer, e, 0, f)),
                  pl.BlockSpec((None, None, fc, d), lambda g, e, f, c: (layer, e, f, 0))],
        out_specs=pl.BlockSpec((tg, d), lambda g, e, f, c: (g, 0)),
        scratch_shapes=[pltpu.VMEM((cap, d), BF16), pltpu.VMEM((cap, d), F32)],
    )
    return pl.pallas_call(
        _moe_kernel,
        grid_spec=grid_spec,
        out_shape=jax.ShapeDtypeStruct((n, d), F32),
        compiler_params=_params(("parallel", "arbitrary", "arbitrary")),
    )(bounds, xh, lpost, lposc, gatec, w1, w3, w2)


EVEN_KB_SECTION = 2


def _even_weight_layout(w_in):
    hd = HEAD_DIM
    widths = (A_HEADS * hd, hd, hd, IDX_HEADS * IDX_DIM, IDX_DIM, IDX_HEADS, B_HEADS * hd, B_HEADS * hd, B_HEADS * hd)
    offs = [0]
    for wd in widths:
        offs.append(offs[-1] + wd)
    qa, ka, va, iq, ik, iw, qb, kb, vb = (w_in[:, offs[j]:offs[j + 1]] for j in range(9))
    iw_pad = jnp.pad(iw, ((0, 0), (0, LANES - IDX_HEADS)))
    w = jnp.concatenate([qa, qb, kb, vb, ka, ka, va, va, iq, ik, ik, ik, ik, iw_pad], axis=1)
    mixw = B_HEADS * hd
    sections = ((0, A_HEADS * hd, hd, QK_SCALE),
                (512, mixw, hd, QK_SCALE),
                (1024, mixw, hd, None),
                (1536, mixw, None, None),
                (2048, LANES, hd, None),
                (2176, LANES, None, None),
                (2304, IDX_HEADS * IDX_DIM, IDX_DIM, None),
                (2560, LANES, IDX_DIM, None),
                (2688, LANES, None, None))
    dtypes = (BF16,) * 8 + (F32,)
    return w.astype(BF16), sections, dtypes


def kernel(x, even_w_in, even_w_out, even_ln1_g, even_ln1_b, even_w1, even_w3, even_w2, even_ln2_g, even_ln2_b, odd_w_in, odd_w_out, odd_ln1_g, odd_ln1_b, odd_router, odd_w1, odd_w3, odd_w2, odd_ln2_g, odd_ln2_b):
    batch, seq, d = x.shape
    n = batch * seq
    tables = _rope_tables(seq)
    x2d = x.reshape(n, d)
    odd_w1h, odd_w3h, odd_w2h = odd_w1.astype(BF16), odd_w3.astype(BF16), odd_w2.astype(BF16)
    for layer in range(DEPTH):
        i = layer // 2
        if layer % 2 == 0:
            w, sections, dtypes = _even_weight_layout(even_w_in[i])
            qa, qb, kb, vb, kk, vv, iq, ik4, iw, kb_mean = _project(x2d, w, tables, sections, dtypes, seq,
                                                                    mean_section=EVEN_KB_SECTION)
            o_a = _dsa_attention(iq, iw, ik4, qa, kk, vv, batch, seq)
            o_b = _moba_attention(qb, kb, vb, kb_mean, batch, seq)
            x2d, xh = _outproj_ln([o_a, o_b], even_w_out[i].astype(BF16), x2d, even_ln1_g[i], even_ln1_b[i])
            x2d = _swiglu_ln(xh, x2d, even_w1[i].astype(BF16), even_w3[i].astype(BF16), even_w2[i].astype(BF16),
                             even_ln2_g[i], even_ln2_b[i])
        else:
            mix = C_HEADS * HEAD_DIM
            sections = ((0, mix, HEAD_DIM, QK_SCALE), (mix, mix, HEAD_DIM, None), (2 * mix, mix, None, None))
            q, k, v = _project(x2d, odd_w_in[i].astype(BF16), tables, sections, (F32, F32, F32), seq)
            o = _dilated_attention(q, k, v, batch, seq)
            x2d, xh = _outproj_ln([o], odd_w_out[i].astype(BF16), x2d, odd_ln1_g[i], odd_ln1_b[i])
            rw = jnp.pad(odd_router[i], ((0, 0), (0, LANES - N_EXPERTS))).astype(BF16)
            gatec, lposc, lpost, cum = _router(xh, rw)
            ffn = _moe(_chunk_bounds(cum), xh, lpost, lposc, gatec, odd_w1h, odd_w3h, odd_w2h, i)
            x2d = _add_ln(x2d, ffn, odd_ln2_g[i], odd_ln2_b[i])
    return x2d.reshape(batch, seq, d)
```

```python
import functools

import jax
import jax.numpy as jnp
from jax import lax
from jax.experimental import pallas as pl
from jax.experimental.pallas import tpu as pltpu

F32 = jnp.float32
BF16 = jnp.bfloat16
I32 = jnp.int32

DEPTH = 4
HEAD_DIM = 64
ROPE_THETA = 10000.0
LN_EPS = 1e-5
A_HEADS = 8
IDX_HEADS = 8
IDX_DIM = 32
DSA_TOPK = 256
B_HEADS = 8
MOBA_BLOCK = 256
MOBA_TOPK = 3
C_HEADS = 16
DILATED_CFG = ((128, 1), (512, 4), (2048, 16))
BAND_BLOCK = 128
N_EXPERTS = 8
DEEPNORM_ALPHA = (2 * DEPTH) ** 0.25
QK_SCALE = HEAD_DIM ** -0.5 * 1.4426950408889634

LANES = 128
VMEM_LIMIT_BYTES = 52 * 1024 * 1024

PROJ_ROWS = 512
FFN_ROWS = 512
ROUTER_ROWS = 512
MOE_GROUP = 2048
MOE_WIN = 256
MOE_FFN_ROWS = 272
MOE_ROW_ALIGN = 16
MOE_FF_CHUNK = 512
DSA_ROWS = 256
DSA_SEARCH_GROUPS = 4
DSA_SEARCH_UNROLL = 4
MOBA_GATE_ROWS = 16
DILATED_UNITS = 4
DILATED_SLABS = 4
ATTN_KEY_STEP = 512

NEG_INF = float("-inf")
NEG_INF_KEY = -2139095041


def _params(semantics):
    return pltpu.CompilerParams(dimension_semantics=semantics, vmem_limit_bytes=VMEM_LIMIT_BYTES)


def _dot(a, b):
    return jnp.dot(a, b, preferred_element_type=F32)


def _dot_nt(a, b):
    return lax.dot_general(a, b, (((1,), (1,)), ((), ())), preferred_element_type=F32)


def _lane_group(shape, group, idx):
    lane = lax.broadcasted_iota(I32, shape, len(shape) - 1)
    return (lane // group) == idx


def _head_pair_attention(units):
    first = _lane_group((1, LANES), HEAD_DIM, 0)
    raw, unit_masks = [], []
    for q2, k2, _, mask in units:
        raw.append(_dot_nt(jnp.where(first, q2, jnp.zeros_like(q2)), k2))
        unit_masks.append(mask() if callable(mask) else mask)
        raw.append(_dot_nt(jnp.where(first, jnp.zeros_like(q2), q2), k2))
    scores = []
    for u, mask in enumerate(unit_masks):
        masks = mask if isinstance(mask, (tuple, list)) else (mask, mask)
        for h in range(2):
            if masks[h].dtype == jnp.bool_:
                scores.append(jnp.where(masks[h], raw[2 * u + h], NEG_INF))
            else:
                scores.append(raw[2 * u + h] + masks[h])
    maxes = [jnp.max(s, axis=1, keepdims=True) for s in scores]
    weights = [jnp.exp2(s - mx).astype(BF16) for s, mx in zip(scores, maxes)]
    results = []
    for u, (_, _, v2, _) in enumerate(units):
        ones = jnp.ones_like(v2)
        pv0 = _dot(weights[2 * u], jnp.where(first, v2, ones))
        pv1 = _dot(weights[2 * u + 1], jnp.where(first, ones, v2))
        den = jnp.where(first, pv0[:, LANES - 1:LANES], pv1[:, 0:1])
        results.append((jnp.where(first, maxes[2 * u], maxes[2 * u + 1]), den, jnp.where(first, pv0, pv1)))
    return results


def _stacked_head_pair_attention(units):
    first = _lane_group((1, LANES), HEAD_DIM, 0)
    raw = [_dot_nt(jnp.concatenate([jnp.where(first, q2, jnp.zeros_like(q2)),
                                    jnp.where(first, jnp.zeros_like(q2), q2)], axis=0), k2)
           for q2, k2, _, _ in units]
    stats, weights = [], []
    for r, (q2, _, _, mask) in zip(raw, units):
        rows = q2.shape[0]
        halves = [jnp.where(mask, r[h * rows:(h + 1) * rows], NEG_INF) for h in range(2)]
        maxes = [jnp.max(s, axis=1, keepdims=True) for s in halves]
        exps = [jnp.exp2(s - mx) for s, mx in zip(halves, maxes)]
        dens = [jnp.sum(e, axis=1, keepdims=True) for e in exps]
        stats.append((jnp.where(first, maxes[0], maxes[1]), jnp.where(first, dens[0], dens[1])))
        weights.append(jnp.concatenate(exps, axis=0).astype(BF16))
    results = []
    for (mx, den), w, (q2, _, v2, _) in zip(stats, weights, units):
        rows = q2.shape[0]
        pv = _dot(w, v2)
        results.append((mx, den, jnp.where(first, pv[:rows], pv[rows:])))
    return results


def _rope_lanes(t, cosf, sinf, half):
    lane = lax.broadcasted_iota(I32, t.shape, 1)
    first = (lane % (2 * half)) < half
    swapped = jnp.where(first, pltpu.roll(t, LANES - half, 1), pltpu.roll(t, half, 1))
    return t * cosf + swapped * sinf


def _proj_kernel(x_ref, w_ref, c64_ref, s64_ref, c32_ref, s32_ref, *out_refs, sections, mean_section):
    x = x_ref[...].astype(BF16)
    for idx, (o_ref, (start, width, rope, scale)) in enumerate(zip(out_refs, sections)):
        h = _dot(x, w_ref[:, start:start + width])
        if rope is not None:
            cosf = (c64_ref if rope == HEAD_DIM else c32_ref)[...]
            sinf = (s64_ref if rope == HEAD_DIM else s32_ref)[...]
            parts = [_rope_lanes(h[:, g * LANES:(g + 1) * LANES], cosf, sinf, rope // 2)
                     for g in range(width // LANES)]
            h = parts[0] if len(parts) == 1 else jnp.concatenate(parts, axis=1)
        if scale is not None:
            h = h * scale
        o_ref[...] = h.astype(o_ref.dtype)
        if idx == mean_section:
            m_ref = out_refs[len(sections)]
            for blk in range(h.shape[0] // MOBA_BLOCK):
                rows = h[blk * MOBA_BLOCK:(blk + 1) * MOBA_BLOCK]
                m_ref[0, blk:blk + 1, :] = jnp.sum(rows, axis=0, keepdims=True) * (1.0 / MOBA_BLOCK)


def _project(x2d, w, tables, sections, out_dtypes, seq, mean_section=None):
    n, d = x2d.shape
    tm = PROJ_ROWS
    pos_blocks = seq // tm
    tab_spec = pl.BlockSpec((tm, LANES), lambda i: (i % pos_blocks, 0))
    out_specs = [pl.BlockSpec((tm, sec[1]), lambda i: (i, 0)) for sec in sections]
    out_shape = [jax.ShapeDtypeStruct((n, sec[1]), dt) for sec, dt in zip(sections, out_dtypes)]
    if mean_section is not None:
        per_tile, width = tm // MOBA_BLOCK, sections[mean_section][1]
        out_specs.append(pl.BlockSpec((1, per_tile, width), lambda i: (i, 0, 0)))
        out_shape.append(jax.ShapeDtypeStruct((n // tm, per_tile, width), F32))
    outs = pl.pallas_call(
        functools.partial(_proj_kernel, sections=sections, mean_section=mean_section),
        grid=(n // tm,),
        in_specs=[pl.BlockSpec((tm, d), lambda i: (i, 0)),
                  pl.BlockSpec(w.shape, lambda i: (0, 0)),
                  tab_spec, tab_spec, tab_spec, tab_spec],
        out_specs=out_specs,
        out_shape=out_shape,
        compiler_params=_params(("parallel",)),
    )(x2d, w, *tables)
    outs = list(outs)
    if mean_section is not None:
        outs[-1] = outs[-1].reshape(n // MOBA_BLOCK, -1)
    return outs


def _rope_tables(seq):
    out = []
    for dim in (HEAD_DIM, IDX_DIM):
        inv = ROPE_THETA ** (-jnp.arange(0, dim, 2, dtype=F32) / dim)
        ang = jnp.arange(seq, dtype=F32)[:, None] * inv[None, :]
        cos, sin = jnp.cos(ang), jnp.sin(ang)
        reps = LANES // dim
        out.append(jnp.tile(jnp.concatenate([cos, cos], -1), (1, reps)))
        out.append(jnp.tile(jnp.concatenate([-sin, sin], -1), (1, reps)))
    return out


def _dsa_block(i, iq_ref, iw_ref, ik_ref, q_ref, kk_ref, vv_ref, o_ref, key_ref, sel_ref, *, kext, topk):
    qb = DSA_ROWS
    keys = slice(0, kext)
    qpos = i * qb + lax.broadcasted_iota(I32, (qb, 1), 0)
    kpos = lax.broadcasted_iota(I32, (1, kext), 1)
    causal = kpos <= qpos

    ik = ik_ref[keys, :]
    iw = iw_ref[...] * (IDX_DIM ** -0.5 * IDX_HEADS ** -0.5)
    heads_per_group = LANES // IDX_DIM
    score = jnp.zeros((qb, kext), F32)
    for h in range(IDX_HEADS):
        grp, j = divmod(h, heads_per_group)
        iqg = iq_ref[:, grp * LANES:(grp + 1) * LANES]
        iqm = jnp.where(_lane_group((1, LANES), IDX_DIM, j), iqg, jnp.zeros_like(iqg))
        rel = jnp.maximum(_dot_nt(iqm, ik), 0.0)
        score = score + rel * iw[:, h:h + 1]
    bits = pltpu.bitcast(score, I32)
    neg = bits >> 31
    int_min = jnp.int32(-2 ** 31)
    key_ref[:, keys] = jnp.where(causal, (bits ^ (neg & 0x7FFFFFFF)) - neg, int_min)

    grp_rows = qb // DSA_SEARCH_GROUPS
    groups = [slice(g * grp_rows, (g + 1) * grp_rows) for g in range(DSA_SEARCH_GROUPS)]

    def count_ge(rows, t):
        return jnp.sum(jnp.where(key_ref[rows, keys] >= t, 1.0, 0.0), axis=1, keepdims=True)

    def search(it, thrs):
        bit = jnp.left_shift(jnp.int32(1), 30 - it)
        return tuple(jnp.where(count_ge(rows, thr | bit) >= topk, thr | bit, thr)
                     for rows, thr in zip(groups, thrs))

    zero = jnp.zeros((grp_rows, 1), I32)
    thrs = tuple(jnp.where(count_ge(rows, zero) >= topk, zero, int_min) for rows in groups)
    thrs = lax.fori_loop(0, 31, search, thrs, unroll=DSA_SEARCH_UNROLL)
    thr = jnp.concatenate(thrs, axis=0)

    floor = jnp.maximum(thr, int_min + 1)
    key = key_ref[:, keys]
    sel_ref[:, keys] = jnp.where(key >= floor, 0.0, NEG_INF)
    n_ge = jnp.sum(jnp.where(key >= thr, 1.0, 0.0), axis=1, keepdims=True)
    tie_rows = (n_ge > topk) & (thr > int_min)

    @pl.when(jnp.max(tie_rows.astype(I32)) > 0)
    def _break_ties():
        upper = (lax.broadcasted_iota(I32, (LANES, LANES), 0)
                 < lax.broadcasted_iota(I32, (LANES, LANES), 1))
        upper = jnp.where(upper, 1.0, 0.0).astype(BF16)
        n_gt = jnp.sum(jnp.where(key_ref[:, keys] > thr, 1.0, 0.0), axis=1, keepdims=True)
        need = topk - n_gt
        before = jnp.zeros((qb, 1), F32)
        for c in range(kext // LANES):
            sl = slice(c * LANES, (c + 1) * LANES)
            keyc = key_ref[:, sl]
            eqc = keyc == thr
            eqf = jnp.where(eqc, 1.0, 0.0)
            rank = _dot(eqf.astype(BF16), upper) + before
            take = ((keyc > thr) | (eqc & (rank < need))) & (keyc > int_min)
            sel_ref[:, sl] = jnp.where(take, 0.0, NEG_INF)
            before = before + jnp.sum(eqf, axis=1, keepdims=True)

    selected = sel_ref[:, keys]
    kk = kk_ref[keys, :]
    vv = vv_ref[keys, :]
    pairs = [(q_ref[:, hp * LANES:(hp + 1) * LANES], kk, vv, selected) for hp in range(A_HEADS // 2)]
    for hp, (_, den, num) in enumerate(_head_pair_attention(pairs)):
        o_ref[:, hp * LANES:(hp + 1) * LANES] = (num / den).astype(o_ref.dtype)


def _dsa_kernel(*refs, seq, topk):
    i = pl.program_id(1)
    blocks_per_step = ATTN_KEY_STEP // DSA_ROWS
    for v in range(seq // ATTN_KEY_STEP):
        @pl.when(i // blocks_per_step == v)
        def _(v=v):
            _dsa_block(i, *refs, kext=(v + 1) * ATTN_KEY_STEP, topk=topk)


def _dsa_attention(iq, iw, ik4, qa, kk, vv, batch, seq):
    n = batch * seq
    qb = DSA_ROWS
    nq = seq // qb
    topk = min(DSA_TOPK, seq // 4)
    rows = lambda width: pl.BlockSpec((qb, width), lambda b, i: (b * nq + i, 0))
    whole = pl.BlockSpec((seq, LANES), lambda b, i: (b, 0))
    return pl.pallas_call(
        functools.partial(_dsa_kernel, seq=seq, topk=topk),
        grid=(batch, nq),
        in_specs=[rows(iq.shape[1]), rows(LANES), whole, rows(qa.shape[1]), whole, whole],
        out_specs=rows(qa.shape[1]),
        out_shape=jax.ShapeDtypeStruct((n, qa.shape[1]), BF16),
        scratch_shapes=[pltpu.VMEM((qb, seq), I32), pltpu.VMEM((qb, seq), F32)],
        compiler_params=_params(("parallel", "parallel")),
    )(iq, iw, ik4, qa, kk, vv)


def _moba_block(qi, q_ref, k_ref, v_ref, kmean_ref, o_ref, *, kext, kt):
    bs = MOBA_BLOCK
    nb = kext // bs
    q2 = q_ref[...]
    k2 = k_ref[0:kext, :]
    v2 = v_ref[0:kext, :]
    nb_all = kmean_ref.shape[0]
    nbp = max(nb_all, MOBA_GATE_ROWS)
    blk_row = lax.broadcasted_iota(I32, (nbp, kext), 0)
    blk_of_key = lax.broadcasted_iota(I32, (nbp, kext), 1) // bs
    member = jnp.where(blk_row == blk_of_key, 1.0, 0.0).astype(BF16)
    kmean = jnp.concatenate([kmean_ref[...], jnp.zeros((nbp - nb_all, LANES), F32)], axis=0).astype(BF16)
    n_iota = lax.broadcasted_iota(I32, (nbp, 1), 0)
    past = n_iota < qi
    qrow = lax.broadcasted_iota(I32, (bs, 1), 0)
    kcol = lax.broadcasted_iota(I32, (1, kext), 1)
    own = (kcol >= qi * bs) & (kcol <= qi * bs + qrow)
    gates = [jnp.where(past, _dot_nt(kmean, jnp.where(_lane_group((1, LANES), HEAD_DIM, half), q2,
                                                      jnp.zeros_like(q2))), NEG_INF)
             for half in range(2)]

    def block_masks():
        chosen = []
        for gate in gates:
            rank = jnp.zeros((nbp, bs), I32)
            for m in range(nb):
                gm = gate[m:m + 1, :]
                beats = (gm > gate) | ((gm == gate) & (m < n_iota))
                rank = rank + beats.astype(I32)
            chosen.append(jnp.transpose(jnp.where(past & (rank < kt), 1.0, 0.0)).astype(BF16))
        return [(_dot(c, member) > 0.5) | own for c in chosen]

    (_, den, num), = _head_pair_attention([(q2, k2, v2, block_masks)])
    o_ref[...] = (num / den).astype(o_ref.dtype)


def _moba_kernel(*refs, seq):
    qi = pl.program_id(2)
    kt = min(MOBA_TOPK, seq // MOBA_BLOCK - 1)
    blocks_per_step = ATTN_KEY_STEP // MOBA_BLOCK
    for v in range(seq // ATTN_KEY_STEP):
        @pl.when(qi // blocks_per_step == v)
        def _(v=v):
            _moba_block(qi, *refs, kext=(v + 1) * ATTN_KEY_STEP, kt=kt)


def _moba_attention(q, k, v, kmean, batch, seq):
    n, width = q.shape
    bs = MOBA_BLOCK
    nb = seq // bs
    rows = pl.BlockSpec((bs, LANES), lambda b, hp, qi: (b * nb + qi, hp))
    whole = pl.BlockSpec((seq, LANES), lambda b, hp, qi: (b, hp))
    means = pl.BlockSpec((nb, LANES), lambda b, hp, qi: (b, hp))
    return pl.pallas_call(
        functools.partial(_moba_kernel, seq=seq),
        grid=(batch, width // LANES, nb),
        in_specs=[rows, whole, whole, means],
        out_specs=rows,
        out_shape=jax.ShapeDtypeStruct((n, width), BF16),
        compiler_params=_params(("parallel", "parallel", "parallel")),
    )(q, k, v, kmean)


def _rows(start, size, stride):
    return pl.ds(start, size) if stride == 1 else pl.ds(start, size, stride=stride)


def _largest_divisor(n, limit):
    return max(d for d in range(1, limit + 1) if n % d == 0)


def _repeat(trips, body):
    if trips == 1:
        body(0, 0)
    else:
        lax.fori_loop(0, trips, body, 0)


def _dilated_kernel(q_ref, k_ref, v_ref, o_ref, qp_ref, kp_ref, vp_ref, nat_ref, slab_ref, *, seq):
    blk = BAND_BLOCK
    ns = DILATED_SLABS
    slab_len = seq // ns

    for s in range(ns):
        rows = slice(s * slab_len, (s + 1) * slab_len)
        qp_ref[rows, :] = q_ref[pl.ds(s, slab_len, stride=ns), :]
        kp_ref[rows, :] = k_ref[pl.ds(s, slab_len, stride=ns), :]
        vp_ref[rows, :] = v_ref[pl.ds(s, slab_len, stride=ns), :]

    def operands(srcs, qstart, kstart, nk, stride):
        ksl = _rows(kstart, nk, stride)
        i = lax.broadcasted_iota(I32, (blk, 1), 0)
        j = lax.broadcasted_iota(I32, (1, nk), 1)
        if nk == blk:
            mask = j <= i
        else:
            mask = (j >= i) & (j <= i + blk)
        return (srcs[0][_rows(qstart, blk, stride), :].astype(BF16), srcs[1][ksl, :].astype(BF16),
                srcs[2][ksl, :].astype(BF16), mask)

    def merged(old, new):
        (m_old, l_old, a_old), (m_new, l_new, a_new) = old, new
        m_tot = jnp.maximum(m_old, m_new)
        w_old = jnp.exp2(m_old - m_tot)
        w_new = jnp.exp2(m_new - m_tot)
        return m_tot, w_old * l_old + w_new * l_new, w_old * a_old + w_new * a_new

    def run_units(srcs, state_ref, units, nk, stride, merge):
        results = _stacked_head_pair_attention([operands(srcs, qstart, kstart, nk, stride)
                                                for qstart, kstart in units])
        for (qstart, _), new in zip(units, results):
            qsl = _rows(qstart, blk, stride)
            if merge:
                new = merged(tuple(state_ref[t, qsl, :] for t in range(3)), new)
            for t in range(3):
                state_ref[t, qsl, :] = new[t]

    written = set()
    for window, dil in DILATED_CFG:
        assert window // dil == blk
        if dil % ns == 0:
            srcs, state_ref, stride, group_len = (qp_ref, kp_ref, vp_ref), slab_ref, dil // ns, slab_len
        else:
            srcs, state_ref, stride, group_len = (q_ref, k_ref, v_ref), nat_ref, dil, seq
        merge = id(state_ref) in written
        written.add(id(state_ref))
        nblk = seq // dil // blk
        ub = _largest_divisor(dil, DILATED_UNITS)

        def class_start(c, stride=stride, group_len=group_len):
            return (c // stride) * group_len + c % stride if group_len != seq else c

        def first_blocks(it, carry, srcs=srcs, state_ref=state_ref, stride=stride, merge=merge, ub=ub,
                         class_start=class_start):
            starts = [class_start(it * ub + u) for u in range(ub)]
            run_units(srcs, state_ref, [(st, st) for st in starts], blk, stride, merge)
            return carry

        _repeat(dil // ub, first_blocks)
        if nblk > 1:
            ul = _largest_divisor(nblk - 1, DILATED_UNITS)
            per_class = (nblk - 1) // ul

            def later_blocks(it, carry, srcs=srcs, state_ref=state_ref, stride=stride, merge=merge, ul=ul,
                             per_class=per_class, class_start=class_start):
                base, grp = class_start(it // per_class), it % per_class
                qstarts = [base + stride * blk * (1 + grp * ul + u) for u in range(ul)]
                run_units(srcs, state_ref, [(qs, qs - stride * blk) for qs in qstarts], 2 * blk, stride, merge)
                return carry

            _repeat(dil * per_class, later_blocks)

    for s in range(ns):
        nat_rows = pl.ds(s, slab_len, stride=ns)
        slab_rows = slice(s * slab_len, (s + 1) * slab_len)
        _, den, num = merged(tuple(nat_ref[t, nat_rows, :] for t in range(3)),
                             tuple(slab_ref[t, slab_rows, :] for t in range(3)))
        nat_ref[2, nat_rows, :] = num / den
    o_ref[...] = nat_ref[2].astype(o_ref.dtype)


def _dilated_attention(q, k, v, batch, seq):
    n, width = q.shape
    for window, dil in DILATED_CFG:
        assert seq % (dil * BAND_BLOCK) == 0
    assert {dil % DILATED_SLABS == 0 for _, dil in DILATED_CFG} == {True, False}
    whole = pl.BlockSpec((seq, LANES), lambda b, hp: (b, hp))
    return pl.pallas_call(
        functools.partial(_dilated_kernel, seq=seq),
        grid=(batch, width // LANES),
        in_specs=[whole, whole, whole],
        out_specs=whole,
        out_shape=jax.ShapeDtypeStruct((n, width), BF16),
        scratch_shapes=[pltpu.VMEM((seq, LANES), F32)] * 3 + [pltpu.VMEM((3, seq, LANES), F32)] * 2,
        compiler_params=_params(("parallel", "parallel")),
    )(q, k, v)


def _deepnorm(x, f, g, b):
    y = DEEPNORM_ALPHA * x + f
    mu = jnp.mean(y, axis=1, keepdims=True)
    yc = y - mu
    var = jnp.mean(yc * yc, axis=1, keepdims=True)
    return yc * lax.rsqrt(var + LN_EPS) * g + b


def _outproj_ln_kernel(*refs, n_parts):
    part_refs = refs[:n_parts]
    w_ref, x_ref, g_ref, b_ref, o_ref, oh_ref = refs[n_parts:]
    mix = None
    start = 0
    for p_ref in part_refs:
        width = p_ref.shape[1]
        t = _dot(p_ref[...], w_ref[start:start + width, :])
        mix = t if mix is None else mix + t
        start += width
    y = _deepnorm(x_ref[...], mix, g_ref[...], b_ref[...])
    o_ref[...] = y
    oh_ref[...] = y.astype(BF16)


def _outproj_ln(parts, w, x2d, g, b):
    n, d = x2d.shape
    tm = PROJ_ROWS
    row = lambda width: pl.BlockSpec((tm, width), lambda i: (i, 0))
    vec = pl.BlockSpec((1, d), lambda i: (0, 0))
    return pl.pallas_call(
        functools.partial(_outproj_ln_kernel, n_parts=len(parts)),
        grid=(n // tm,),
        in_specs=[row(p.shape[1]) for p in parts] + [pl.BlockSpec(w.shape, lambda i: (0, 0)), row(d), vec, vec],
        out_specs=[row(d), row(d)],
        out_shape=[jax.ShapeDtypeStruct((n, d), F32), jax.ShapeDtypeStruct((n, d), BF16)],
        compiler_params=_params(("parallel",)),
    )(*parts, w, x2d, g.reshape(1, d), b.reshape(1, d))


def _silu(h):
    return h * (1.0 / (1.0 + jnp.exp(-h)))


def _swiglu_ln_kernel(xh_ref, x_ref, w1_ref, w3_ref, w2_ref, g_ref, b_ref, o_ref, acc_ref):
    f = pl.program_id(1)
    xh = xh_ref[...]
    hid = (_silu(_dot(xh, w1_ref[...])) * _dot(xh, w3_ref[...])).astype(BF16)
    y = _dot(hid, w2_ref[...])

    @pl.when(f == 0)
    def _():
        acc_ref[...] = y

    @pl.when(f > 0)
    def _():
        acc_ref[...] += y

    @pl.when(f == pl.num_programs(1) - 1)
    def _():
        o_ref[...] = _deepnorm(x_ref[...], acc_ref[...], g_ref[...], b_ref[...])


def _swiglu_ln(xh, x2d, w1, w3, w2, g, b):
    n, d = x2d.shape
    dff = w1.shape[1]
    tm = FFN_ROWS
    nf = 2 if dff % (2 * LANES) == 0 else 1
    fc = dff // nf
    row = lambda: pl.BlockSpec((tm, d), lambda i, f: (i, 0))
    vec = pl.BlockSpec((1, d), lambda i, f: (0, 0))
    return pl.pallas_call(
        _swiglu_ln_kernel,
        grid=(n // tm, nf),
        in_specs=[row(), row(),
                  pl.BlockSpec((d, fc), lambda i, f: (0, f)),
                  pl.BlockSpec((d, fc), lambda i, f: (0, f)),
                  pl.BlockSpec((fc, d), lambda i, f: (f, 0)),
                  vec, vec],
        out_specs=row(),
        out_shape=jax.ShapeDtypeStruct((n, d), F32),
        scratch_shapes=[pltpu.VMEM((tm, d), F32)],
        compiler_params=_params(("parallel", "arbitrary")),
    )(xh, x2d, w1, w3, w2, g.reshape(1, d), b.reshape(1, d))


def _add_ln_kernel(x_ref, f_ref, g_ref, b_ref, o_ref):
    o_ref[...] = _deepnorm(x_ref[...], f_ref[...], g_ref[...], b_ref[...])


def _add_ln(x2d, f2d, g, b):
    n, d = x2d.shape
    tm = PROJ_ROWS
    row = pl.BlockSpec((tm, d), lambda i: (i, 0))
    vec = pl.BlockSpec((1, d), lambda i: (0, 0))
    return pl.pallas_call(
        _add_ln_kernel,
        grid=(n // tm,),
        in_specs=[row, row, vec, vec],
        out_specs=row,
        out_shape=jax.ShapeDtypeStruct((n, d), F32),
        compiler_params=_params(("parallel",)),
    )(x2d, f2d, g.reshape(1, d), b.reshape(1, d))


def _router_kernel(xh_ref, rw_ref, gate_ref, lposc_ref, lpost_ref, cum_ref, carry_ref, *, tiles_per_group):
    tm = xh_ref.shape[0]
    i = pl.program_id(0)

    @pl.when(i % tiles_per_group == 0)
    def _():
        carry_ref[...] = jnp.zeros_like(carry_ref)

    lane = lax.broadcasted_iota(I32, (1, LANES), 1)
    logits = jnp.where(lane < N_EXPERTS, _dot(xh_ref[...], rw_ref[...]), NEG_INF)
    m1 = jnp.max(logits, axis=1, keepdims=True)
    i1 = jnp.min(jnp.where(logits == m1, lane, LANES), axis=1, keepdims=True)
    rest = jnp.where(lane == i1, NEG_INF, logits)
    m2 = jnp.max(rest, axis=1, keepdims=True)
    i2 = jnp.min(jnp.where(rest == m2, lane, LANES), axis=1, keepdims=True)
    e2 = jnp.exp(m2 - m1)
    g1 = 1.0 / (1.0 + e2)
    g2 = e2 / (1.0 + e2)
    sel1 = lane == i1
    sel2 = lane == i2
    routed = sel1 | sel2
    gate_ref[...] = jnp.where(sel1, g1, jnp.where(sel2, g2, 0.0))
    routedf = jnp.where(routed, 1.0, 0.0)
    earlier = (lax.broadcasted_iota(I32, (tm, tm), 1) < lax.broadcasted_iota(I32, (tm, tm), 0))
    earlier = jnp.where(earlier, 1.0, 0.0).astype(BF16)
    carry = carry_ref[...]
    lpos = jnp.where(routed, _dot(earlier, routedf.astype(BF16)) + carry, -1.0)
    lposc_ref[...] = lpos
    lpost_ref[...] = jnp.transpose(lpos)[0:N_EXPERTS, :]
    carry = carry + jnp.sum(routedf, axis=0, keepdims=True)
    carry_ref[...] = carry
    cum_ref[...] = carry.astype(I32).reshape(1, 1, LANES)


def _router(xh, rw):
    n, d = xh.shape
    tm = ROUTER_ROWS
    nt = n // tm
    row = pl.BlockSpec((tm, LANES), lambda i: (i, 0))
    return pl.pallas_call(
        functools.partial(_router_kernel, tiles_per_group=MOE_GROUP // tm),
        grid=(nt,),
        in_specs=[pl.BlockSpec((tm, d), lambda i: (i, 0)), pl.BlockSpec(rw.shape, lambda i: (0, 0))],
        out_specs=[row, row,
                   pl.BlockSpec((N_EXPERTS, tm), lambda i: (0, i)),
                   pl.BlockSpec((1, 1, LANES), lambda i: (i, 0, 0))],
        out_shape=[jax.ShapeDtypeStruct((n, LANES), F32), jax.ShapeDtypeStruct((n, LANES), F32),
                   jax.ShapeDtypeStruct((N_EXPERTS, n), F32), jax.ShapeDtypeStruct((nt, 1, LANES), I32)],
        scratch_shapes=[pltpu.VMEM((1, LANES), F32)],
        compiler_params=_params(("arbitrary",)),
    )(xh, rw)


def _moe_kernel(bnd_ref, xh_ref, lpost_ref, lposc_ref, gatec_ref, w1_ref, w3_ref, w2_ref,
                o_ref, xg_ref, yacc_ref):
    tg = xh_ref.shape[0]
    win, ch, fr, align = MOE_WIN, ROUTER_ROWS, MOE_FFN_ROWS, MOE_ROW_ALIGN
    nch = tg // ch
    g, e, f = pl.program_id(0), pl.program_id(1), pl.program_id(2)
    nf = pl.num_programs(2)

    def bound(c):
        return bnd_ref[(g * (nch + 1) + c) * N_EXPERTS + e]

    cnt = bound(nch)
    n_ffn = (cnt + fr - 1) // fr

    def tok_chunk(c):
        return pl.ds(pl.multiple_of(c * ch, ch), ch)

    def windows(c):
        r0, r1 = bound(c), bound(c + 1)
        first = (r0 // align) * align
        return first, jnp.where(r1 > r0, (r1 - first + win - 1) // win, 0)

    @pl.when((e == 0) & (f == 0))
    def _():
        o_ref[...] = jnp.zeros_like(o_ref)

    @pl.when(f == 0)
    def _compact():
        def clear(s, carry):
            rows = pl.ds(pl.multiple_of(s * win, win), win)
            xg_ref[rows, :] = jnp.zeros((win, xg_ref.shape[1]), xg_ref.dtype)
            yacc_ref[rows, :] = jnp.zeros((win, yacc_ref.shape[1]), F32)
            return carry

        lax.fori_loop(0, (cnt + max(fr, win) + win - 1) // win, clear, 0)

        def per_chunk(c, carry):
            first, n_win = windows(c)
            lp = lpost_ref[pl.ds(e, 1), tok_chunk(c)]

            def per_window(w, carry2):
                start = pl.multiple_of(first + w * win, align)
                want = (lax.broadcasted_iota(I32, (win, 1), 0) + start).astype(F32)
                pick = jnp.where(lp == want, 1.0, 0.0).astype(BF16)
                rows = pl.ds(start, win)
                xg_ref[rows, :] += _dot(pick, xh_ref[tok_chunk(c), :]).astype(xg_ref.dtype)
                return carry2

            return lax.fori_loop(0, n_win, per_window, carry)

        lax.fori_loop(0, nch, per_chunk, 0)

    def ffn_rows(rows):
        xs = xg_ref[rows, :]
        hid = (_silu(_dot(xs, w1_ref[...])) * _dot(xs, w3_ref[...])).astype(BF16)
        yacc_ref[rows, :] += _dot(hid, w2_ref[...])

    def ffn_pair(p, carry):
        ffn_rows(pl.ds(pl.multiple_of(p * 2 * fr, align), 2 * fr))
        return carry

    lax.fori_loop(0, n_ffn // 2, ffn_pair, 0)

    @pl.when(n_ffn % 2 == 1)
    def _():
        ffn_rows(pl.ds(pl.multiple_of((n_ffn - 1) * fr, align), fr))

    @pl.when(f == nf - 1)
    def _scatter():
        lane = lax.broadcasted_iota(I32, (1, LANES), 1)

        def per_chunk(c, carry):
            first, n_win = windows(c)
            toks = tok_chunk(c)
            lp = jnp.sum(jnp.where(lane == e, lposc_ref[toks, :], 0.0), axis=1, keepdims=True)
            gt = jnp.sum(jnp.where(lane == e, gatec_ref[toks, :], 0.0), axis=1, keepdims=True)

            def per_window(w, carry2):
                start = pl.multiple_of(first + w * win, align)
                y = yacc_ref[pl.ds(start, win), :]
                y_hi = y.astype(BF16)
                y_lo = (y - y_hi.astype(F32)).astype(BF16)
                want = (lax.broadcasted_iota(I32, (1, win), 1) + start).astype(F32)
                place = jnp.where(lp == want, 1.0, 0.0).astype(BF16)
                o_ref[toks, :] += gt * (_dot(place, y_hi) + _dot(place, y_lo))
                return carry2

            return lax.fori_loop(0, n_win, per_window, carry)

        lax.fori_loop(0, nch, per_chunk, 0)


def _chunk_bounds(cum):
    nch = MOE_GROUP // ROUTER_ROWS
    per_group = cum[:, 0, :N_EXPERTS].reshape(-1, nch, N_EXPERTS)
    return jnp.pad(per_group, ((0, 0), (1, 0), (0, 0))).reshape(-1)


def _moe(bounds, xh, lpost, lposc, gatec, w1, w3, w2, layer):
    n, d = xh.shape
    tg = MOE_GROUP
    dff = w1.shape[3]
    fc = MOE_FF_CHUNK
    cap = -(-(tg + max(MOE_FFN_ROWS, MOE_WIN)) // MOE_WIN) * MOE_WIN
    grid_spec = pltpu.PrefetchScalarGridSpec(
        num_scalar_prefetch=1,
        grid=(n // tg, N_EXPERTS, dff // fc),
        in_specs=[pl.BlockSpec((tg, d), lambda g, e, f, c: (g, 0)),
                  pl.BlockSpec((N_EXPERTS, tg), lambda g, e, f, c: (0, g)),
                  pl.BlockSpec((tg, LANES), lambda g, e, f, c: (g, 0)),
                  pl.BlockSpec((tg, LANES), lambda g, e, f, c: (g, 0)),
                  pl.BlockSpec((None, None, d, fc), lambda g, e, f, c: (layer, e, 0, f)),
                  pl.BlockSpec((None, None, d, fc), lambda g, e, f, c: (layer, e, 0, f)),
                  pl.BlockSpec((None, None, fc, d), lambda g, e, f, c: (layer, e, f, 0))],
        out_specs=pl.BlockSpec((tg, d), lambda g, e, f, c: (g, 0)),
        scratch_shapes=[pltpu.VMEM((cap, d), BF16), pltpu.VMEM((cap, d), F32)],
    )
    return pl.pallas_call(
        _moe_kernel,
        grid_spec=grid_spec,
        out_shape=jax.ShapeDtypeStruct((n, d), F32),
        compiler_params=_params(("parallel", "arbitrary", "arbitrary")),
    )(bounds, xh, lpost, lposc, gatec, w1, w3, w2)


EVEN_KB_SECTION = 2


def _even_weight_layout(w_in):
    hd = HEAD_DIM
    widths = (A_HEADS * hd, hd, hd, IDX_HEADS * IDX_DIM, IDX_DIM, IDX_HEADS, B_HEADS * hd, B_HEADS * hd, B_HEADS * hd)
    offs = [0]
    for wd in widths:
        offs.append(offs[-1] + wd)
    qa, ka, va, iq, ik, iw, qb, kb, vb = (w_in[:, offs[j]:offs[j + 1]] for j in range(9))
    iw_pad = jnp.pad(iw, ((0, 0), (0, LANES - IDX_HEADS)))
    w = jnp.concatenate([qa, qb, kb, vb, ka, ka, va, va, iq, ik, ik, ik, ik, iw_pad], axis=1)
    mixw = B_HEADS * hd
    sections = ((0, A_HEADS * hd, hd, QK_SCALE),
                (512, mixw, hd, QK_SCALE),
                (1024, mixw, hd, None),
                (1536, mixw, None, None),
                (2048, LANES, hd, None),
                (2176, LANES, None, None),
                (2304, IDX_HEADS * IDX_DIM, IDX_DIM, None),
                (2560, LANES, IDX_DIM, None),
                (2688, LANES, None, None))
    dtypes = (BF16,) * 8 + (F32,)
    return w.astype(BF16), sections, dtypes


def kernel(x, even_w_in, even_w_out, even_ln1_g, even_ln1_b, even_w1, even_w3, even_w2, even_ln2_g, even_ln2_b, odd_w_in, odd_w_out, odd_ln1_g, odd_ln1_b, odd_router, odd_w1, odd_w3, odd_w2, odd_ln2_g, odd_ln2_b):
    batch, seq, d = x.shape
    n = batch * seq
    tables = _rope_tables(seq)
    x2d = x.reshape(n, d)
    odd_w1h, odd_w3h, odd_w2h = odd_w1.astype(BF16), odd_w3.astype(BF16), odd_w2.astype(BF16)
    for layer in range(DEPTH):
        i = layer // 2
        if layer % 2 == 0:
            w, sections, dtypes = _even_weight_layout(even_w_in[i])
            qa, qb, kb, vb, kk, vv, iq, ik4, iw, kb_mean = _project(x2d, w, tables, sections, dtypes, seq,
                                                                    mean_section=EVEN_KB_SECTION)
            o_a = _dsa_attention(iq, iw, ik4, qa, kk, vv, batch, seq)
            o_b = _moba_attention(qb, kb, vb, kb_mean, batch, seq)
            x2d, xh = _outproj_ln([o_a, o_b], even_w_out[i].astype(BF16), x2d, even_ln1_g[i], even_ln1_b[i])
            x2d = _swiglu_ln(xh, x2d, even_w1[i].astype(BF16), even_w3[i].astype(BF16), even_w2[i].astype(BF16),
                             even_ln2_g[i], even_ln2_b[i])
        else:
            mix = C_HEADS * HEAD_DIM
            sections = ((0, mix, HEAD_DIM, QK_SCALE), (mix, mix, HEAD_DIM, None), (2 * mix, mix, None, None))
            q, k, v = _project(x2d, odd_w_in[i].astype(BF16), tables, sections, (F32, F32, F32), seq)
            o = _dilated_attention(q, k, v, batch, seq)
            x2d, xh = _outproj_ln([o], odd_w_out[i].astype(BF16), x2d, odd_ln1_g[i], odd_ln1_b[i])
            rw = jnp.pad(odd_router[i], ((0, 0), (0, LANES - N_EXPERTS))).astype(BF16)
            gatec, lposc, lpost, cum = _router(xh, rw)
            ffn = _moe(_chunk_bounds(cum), xh, lpost, lposc, gatec, odd_w1h, odd_w3h, odd_w2h, i)
            x2d = _add_ln(x2d, ffn, odd_ln2_g[i], odd_ln2_b[i])
    return x2d.reshape(batch, seq, d)
```

```python
import functools

import jax
import jax.numpy as jnp
from jax import lax
from jax.experimental import pallas as pl
from jax.experimental.pallas import tpu as pltpu

F32 = jnp.float32
BF16 = jnp.bfloat16
I32 = jnp.int32

DEPTH = 4
HEAD_DIM = 64
ROPE_THETA = 10000.0
LN_EPS = 1e-5
A_HEADS = 8
IDX_HEADS = 8
IDX_DIM = 32
DSA_TOPK = 256
B_HEADS = 8
MOBA_BLOCK = 256
MOBA_TOPK = 3
C_HEADS = 16
DILATED_CFG = ((128, 1), (512, 4), (2048, 16))
BAND_BLOCK = 128
N_EXPERTS = 8
DEEPNORM_ALPHA = (2 * DEPTH) ** 0.25
QK_SCALE = HEAD_DIM ** -0.5 * 1.4426950408889634

LANES = 128
VMEM_LIMIT_BYTES = 52 * 1024 * 1024

PROJ_ROWS = 512
FFN_ROWS = 512
FFN_CHUNKS = 1
ROUTER_ROWS = 512
MOE_GROUP = 2048
MOE_WIN = 256
MOE_FFN_ROWS = 272
MOE_ROW_ALIGN = 16
MOE_FF_CHUNK = 512
DSA_ROWS = 256
DSA_SEARCH_GROUPS = 4
DSA_SEARCH_UNROLL = 4
MOBA_GATE_ROWS = 16
DILATED_UNITS = 4
DILATED_SLABS = 4
ATTN_KEY_STEP = 512
MOBA_KEY_STEP = 256

NEG_INF = float("-inf")
NEG_INF_KEY = -2139095041


def _params(semantics):
    return pltpu.CompilerParams(dimension_semantics=semantics, vmem_limit_bytes=VMEM_LIMIT_BYTES)


def _dot(a, b):
    return jnp.dot(a, b, preferred_element_type=F32)


def _dot_nt(a, b):
    return lax.dot_general(a, b, (((1,), (1,)), ((), ())), preferred_element_type=F32)


def _lane_group(shape, group, idx):
    lane = lax.broadcasted_iota(I32, shape, len(shape) - 1)
    return (lane // group) == idx


def _head_pair_attention(units):
    first = _lane_group((1, LANES), HEAD_DIM, 0)
    raw, unit_masks = [], []
    for q2, k2, _, mask in units:
        raw.append(_dot_nt(jnp.where(first, q2, jnp.zeros_like(q2)), k2))
        unit_masks.append(mask() if callable(mask) else mask)
        raw.append(_dot_nt(jnp.where(first, jnp.zeros_like(q2), q2), k2))
    scores = []
    for u, mask in enumerate(unit_masks):
        masks = mask if isinstance(mask, (tuple, list)) else (mask, mask)
        for h in range(2):
            if masks[h].dtype == jnp.bool_:
                scores.append(jnp.where(masks[h], raw[2 * u + h], NEG_INF))
            else:
                scores.append(raw[2 * u + h] + masks[h])
    maxes = [jnp.max(s, axis=1, keepdims=True) for s in scores]
    weights = [jnp.exp2(s - mx).astype(BF16) for s, mx in zip(scores, maxes)]
    results = []
    for u, (_, _, v2, _) in enumerate(units):
        ones = jnp.ones_like(v2)
        pv0 = _dot(weights[2 * u], jnp.where(first, v2, ones))
        pv1 = _dot(weights[2 * u + 1], jnp.where(first, ones, v2))
        den = jnp.where(first, pv0[:, LANES - 1:LANES], pv1[:, 0:1])
        results.append((jnp.where(first, maxes[2 * u], maxes[2 * u + 1]), den, jnp.where(first, pv0, pv1)))
    return results


def _stacked_head_pair_attention(units):
    first = _lane_group((1, LANES), HEAD_DIM, 0)
    raw = [_dot_nt(jnp.concatenate([jnp.where(first, q2, jnp.zeros_like(q2)),
                                    jnp.where(first, jnp.zeros_like(q2), q2)], axis=0), k2)
           for q2, k2, _, _ in units]
    stats, weights = [], []
    for r, (q2, _, _, mask) in zip(raw, units):
        rows = q2.shape[0]
        halves = [jnp.where(mask, r[h * rows:(h + 1) * rows], NEG_INF) for h in range(2)]
        maxes = [jnp.max(s, axis=1, keepdims=True) for s in halves]
        exps = [jnp.exp2(s - mx) for s, mx in zip(halves, maxes)]
        dens = [jnp.sum(e, axis=1, keepdims=True) for e in exps]
        stats.append((jnp.where(first, maxes[0], maxes[1]), jnp.where(first, dens[0], dens[1])))
        weights.append(jnp.concatenate(exps, axis=0).astype(BF16))
    results = []
    for (mx, den), w, (q2, _, v2, _) in zip(stats, weights, units):
        rows = q2.shape[0]
        pv = _dot(w, v2)
        results.append((mx, den, jnp.where(first, pv[:rows], pv[rows:])))
    return results


def _rope_lanes(t, cosf, sinf, half):
    lane = lax.broadcasted_iota(I32, t.shape, 1)
    first = (lane % (2 * half)) < half
    swapped = jnp.where(first, pltpu.roll(t, LANES - half, 1), pltpu.roll(t, half, 1))
    return t * cosf + swapped * sinf


def _proj_kernel(x_ref, w_ref, c64_ref, s64_ref, c32_ref, s32_ref, *out_refs, sections, mean_section):
    x = x_ref[...].astype(BF16)
    for idx, (o_ref, (start, width, rope, scale)) in enumerate(zip(out_refs, sections)):
        h = _dot(x, w_ref[:, start:start + width])
        if rope is not None:
            cosf = (c64_ref if rope == HEAD_DIM else c32_ref)[...]
            sinf = (s64_ref if rope == HEAD_DIM else s32_ref)[...]
            parts = [_rope_lanes(h[:, g * LANES:(g + 1) * LANES], cosf, sinf, rope // 2)
                     for g in range(width // LANES)]
            h = parts[0] if len(parts) == 1 else jnp.concatenate(parts, axis=1)
        if scale is not None:
            h = h * scale
        o_ref[...] = h.astype(o_ref.dtype)
        if idx == mean_section:
            m_ref = out_refs[len(sections)]
            for blk in range(h.shape[0] // MOBA_BLOCK):
                rows = h[blk * MOBA_BLOCK:(blk + 1) * MOBA_BLOCK]
                m_ref[0, blk:blk + 1, :] = jnp.sum(rows, axis=0, keepdims=True) * (1.0 / MOBA_BLOCK)


def _project(x2d, w, tables, sections, out_dtypes, seq, mean_section=None):
    n, d = x2d.shape
    tm = PROJ_ROWS
    pos_blocks = seq // tm
    tab_spec = pl.BlockSpec((tm, LANES), lambda i: (i % pos_blocks, 0))
    out_specs = [pl.BlockSpec((tm, sec[1]), lambda i: (i, 0)) for sec in sections]
    out_shape = [jax.ShapeDtypeStruct((n, sec[1]), dt) for sec, dt in zip(sections, out_dtypes)]
    if mean_section is not None:
        per_tile, width = tm // MOBA_BLOCK, sections[mean_section][1]
        out_specs.append(pl.BlockSpec((1, per_tile, width), lambda i: (i, 0, 0)))
        out_shape.append(jax.ShapeDtypeStruct((n // tm, per_tile, width), F32))
    outs = pl.pallas_call(
        functools.partial(_proj_kernel, sections=sections, mean_section=mean_section),
        grid=(n // tm,),
        in_specs=[pl.BlockSpec((tm, d), lambda i: (i, 0)),
                  pl.BlockSpec(w.shape, lambda i: (0, 0)),
                  tab_spec, tab_spec, tab_spec, tab_spec],
        out_specs=out_specs,
        out_shape=out_shape,
        compiler_params=_params(("parallel",)),
    )(x2d, w, *tables)
    outs = list(outs)
    if mean_section is not None:
        outs[-1] = outs[-1].reshape(n // MOBA_BLOCK, -1)
    return outs


def _rope_tables(seq):
    out = []
    for dim in (HEAD_DIM, IDX_DIM):
        inv = ROPE_THETA ** (-jnp.arange(0, dim, 2, dtype=F32) / dim)
        ang = jnp.arange(seq, dtype=F32)[:, None] * inv[None, :]
        cos, sin = jnp.cos(ang), jnp.sin(ang)
        reps = LANES // dim
        out.append(jnp.tile(jnp.concatenate([cos, cos], -1), (1, reps)))
        out.append(jnp.tile(jnp.concatenate([-sin, sin], -1), (1, reps)))
    return out


def _dsa_block(i, iq_ref, iw_ref, ik_ref, q_ref, kk_ref, vv_ref, o_ref, key_ref, sel_ref, *, kext, topk):
    qb = DSA_ROWS
    keys = slice(0, kext)
    qpos = i * qb + lax.broadcasted_iota(I32, (qb, 1), 0)
    kpos = lax.broadcasted_iota(I32, (1, kext), 1)
    causal = kpos <= qpos

    ik = ik_ref[keys, :]
    iw = iw_ref[...] * (IDX_DIM ** -0.5 * IDX_HEADS ** -0.5)
    heads_per_group = LANES // IDX_DIM
    score = jnp.zeros((qb, kext), F32)
    for h in range(IDX_HEADS):
        grp, j = divmod(h, heads_per_group)
        iqg = iq_ref[:, grp * LANES:(grp + 1) * LANES]
        iqm = jnp.where(_lane_group((1, LANES), IDX_DIM, j), iqg, jnp.zeros_like(iqg))
        rel = jnp.maximum(_dot_nt(iqm, ik), 0.0)
        score = score + rel * iw[:, h:h + 1]
    bits = pltpu.bitcast(score, I32)
    neg = bits >> 31
    int_min = jnp.int32(-2 ** 31)
    key_ref[:, keys] = jnp.where(causal, (bits ^ (neg & 0x7FFFFFFF)) - neg, int_min)

    grp_rows = qb // DSA_SEARCH_GROUPS
    groups = [slice(g * grp_rows, (g + 1) * grp_rows) for g in range(DSA_SEARCH_GROUPS)]

    def count_ge(rows, t):
        return jnp.sum(jnp.where(key_ref[rows, keys] >= t, 1.0, 0.0), axis=1, keepdims=True)

    def search(it, thrs):
        bit = jnp.left_shift(jnp.int32(1), 30 - it)
        return tuple(jnp.where(count_ge(rows, thr | bit) >= topk, thr | bit, thr)
                     for rows, thr in zip(groups, thrs))

    zero = jnp.zeros((grp_rows, 1), I32)
    thrs = tuple(jnp.where(count_ge(rows, zero) >= topk, zero, int_min) for rows in groups)
    thrs = lax.fori_loop(0, 31, search, thrs, unroll=DSA_SEARCH_UNROLL)
    thr = jnp.concatenate(thrs, axis=0)

    floor = jnp.maximum(thr, int_min + 1)
    key = key_ref[:, keys]
    sel_ref[:, keys] = jnp.where(key >= floor, 0.0, NEG_INF)
    n_ge = jnp.sum(jnp.where(key >= thr, 1.0, 0.0), axis=1, keepdims=True)
    tie_rows = (n_ge > topk) & (thr > int_min)

    @pl.when(jnp.max(tie_rows.astype(I32)) > 0)
    def _break_ties():
        upper = (lax.broadcasted_iota(I32, (LANES, LANES), 0)
                 < lax.broadcasted_iota(I32, (LANES, LANES), 1))
        upper = jnp.where(upper, 1.0, 0.0).astype(BF16)
        n_gt = jnp.sum(jnp.where(key_ref[:, keys] > thr, 1.0, 0.0), axis=1, keepdims=True)
        need = topk - n_gt
        before = jnp.zeros((qb, 1), F32)
        for c in range(kext // LANES):
            sl = slice(c * LANES, (c + 1) * LANES)
            keyc = key_ref[:, sl]
            eqc = keyc == thr
            eqf = jnp.where(eqc, 1.0, 0.0)
            rank = _dot(eqf.astype(BF16), upper) + before
            take = ((keyc > thr) | (eqc & (rank < need))) & (keyc > int_min)
            sel_ref[:, sl] = jnp.where(take, 0.0, NEG_INF)
            before = before + jnp.sum(eqf, axis=1, keepdims=True)

    selected = sel_ref[:, keys]
    kk = kk_ref[keys, :]
    vv = vv_ref[keys, :]
    pairs = [(q_ref[:, hp * LANES:(hp + 1) * LANES], kk, vv, selected) for hp in range(A_HEADS // 2)]
    for hp, (_, den, num) in enumerate(_head_pair_attention(pairs)):
        o_ref[:, hp * LANES:(hp + 1) * LANES] = (num / den).astype(o_ref.dtype)


def _dsa_kernel(*refs, seq, topk):
    i = pl.program_id(1)
    blocks_per_step = ATTN_KEY_STEP // DSA_ROWS
    for v in range(seq // ATTN_KEY_STEP):
        @pl.when(i // blocks_per_step == v)
        def _(v=v):
            _dsa_block(i, *refs, kext=(v + 1) * ATTN_KEY_STEP, topk=topk)


def _dsa_attention(iq, iw, ik4, qa, kk, vv, batch, seq):
    n = batch * seq
    qb = DSA_ROWS
    nq = seq // qb
    topk = min(DSA_TOPK, seq // 4)
    rows = lambda width: pl.BlockSpec((qb, width), lambda b, i: (b * nq + i, 0))
    whole = pl.BlockSpec((seq, LANES), lambda b, i: (b, 0))
    return pl.pallas_call(
        functools.partial(_dsa_kernel, seq=seq, topk=topk),
        grid=(batch, nq),
        in_specs=[rows(iq.shape[1]), rows(LANES), whole, rows(qa.shape[1]), whole, whole],
        out_specs=rows(qa.shape[1]),
        out_shape=jax.ShapeDtypeStruct((n, qa.shape[1]), BF16),
        scratch_shapes=[pltpu.VMEM((qb, seq), I32), pltpu.VMEM((qb, seq), F32)],
        compiler_params=_params(("parallel", "parallel")),
    )(iq, iw, ik4, qa, kk, vv)


def _moba_block(qi, q_ref, k_ref, v_ref, kmean_ref, o_ref, *, kext, kt):
    bs = MOBA_BLOCK
    nb = kext // bs
    q2 = q_ref[...]
    k2 = k_ref[0:kext, :]
    v2 = v_ref[0:kext, :]
    nb_all = kmean_ref.shape[0]
    nbp = max(nb_all, MOBA_GATE_ROWS)
    blk_row = lax.broadcasted_iota(I32, (nbp, kext), 0)
    blk_of_key = lax.broadcasted_iota(I32, (nbp, kext), 1) // bs
    member = jnp.where(blk_row == blk_of_key, 1.0, 0.0).astype(BF16)
    kmean = jnp.concatenate([kmean_ref[...], jnp.zeros((nbp - nb_all, LANES), F32)], axis=0).astype(BF16)
    n_iota = lax.broadcasted_iota(I32, (nbp, 1), 0)
    past = n_iota < qi
    qrow = lax.broadcasted_iota(I32, (bs, 1), 0)
    kcol = lax.broadcasted_iota(I32, (1, kext), 1)
    own = (kcol >= qi * bs) & (kcol <= qi * bs + qrow)
    gates = [jnp.where(past, _dot_nt(kmean, jnp.where(_lane_group((1, LANES), HEAD_DIM, half), q2,
                                                      jnp.zeros_like(q2))), NEG_INF)
             for half in range(2)]

    def block_masks():
        chosen = []
        for gate in gates:
            rank = jnp.zeros((nbp, bs), I32)
            for m in range(nb):
                gm = gate[m:m + 1, :]
                beats = (gm > gate) | ((gm == gate) & (m < n_iota))
                rank = rank + beats.astype(I32)
            chosen.append(jnp.transpose(jnp.where(past & (rank < kt), 1.0, 0.0)).astype(BF16))
        return [(_dot(c, member) > 0.5) | own for c in chosen]

    (_, den, num), = _head_pair_attention([(q2, k2, v2, block_masks)])
    o_ref[...] = (num / den).astype(o_ref.dtype)


def _moba_kernel(*refs, seq):
    qi = pl.program_id(2)
    kt = min(MOBA_TOPK, seq // MOBA_BLOCK - 1)
    blocks_per_step = MOBA_KEY_STEP // MOBA_BLOCK
    for v in range(seq // MOBA_KEY_STEP):
        @pl.when(qi // blocks_per_step == v)
        def _(v=v):
            _moba_block(qi, *refs, kext=(v + 1) * MOBA_KEY_STEP, kt=kt)


def _moba_attention(q, k, v, kmean, batch, seq):
    n, width = q.shape
    bs = MOBA_BLOCK
    nb = seq // bs
    rows = pl.BlockSpec((bs, LANES), lambda b, hp, qi: (b * nb + qi, hp))
    whole = pl.BlockSpec((seq, LANES), lambda b, hp, qi: (b, hp))
    means = pl.BlockSpec((nb, LANES), lambda b, hp, qi: (b, hp))
    return pl.pallas_call(
        functools.partial(_moba_kernel, seq=seq),
        grid=(batch, width // LANES, nb),
        in_specs=[rows, whole, whole, means],
        out_specs=rows,
        out_shape=jax.ShapeDtypeStruct((n, width), BF16),
        compiler_params=_params(("parallel", "parallel", "parallel")),
    )(q, k, v, kmean)


def _rows(start, size, stride):
    return pl.ds(start, size) if stride == 1 else pl.ds(start, size, stride=stride)


def _largest_divisor(n, limit):
    return max(d for d in range(1, limit + 1) if n % d == 0)


def _repeat(trips, body):
    if trips == 1:
        body(0, 0)
    else:
        lax.fori_loop(0, trips, body, 0)


def _dilated_kernel(q_ref, k_ref, v_ref, o_ref, qp_ref, kp_ref, vp_ref, nat_ref, slab_ref, *, seq):
    blk = BAND_BLOCK
    ns = DILATED_SLABS
    slab_len = seq // ns

    for s in range(ns):
        rows = slice(s * slab_len, (s + 1) * slab_len)
        qp_ref[rows, :] = q_ref[pl.ds(s, slab_len, stride=ns), :]
        kp_ref[rows, :] = k_ref[pl.ds(s, slab_len, stride=ns), :]
        vp_ref[rows, :] = v_ref[pl.ds(s, slab_len, stride=ns), :]

    def operands(srcs, qstart, kstart, nk, stride):
        ksl = _rows(kstart, nk, stride)
        i = lax.broadcasted_iota(I32, (blk, 1), 0)
        j = lax.broadcasted_iota(I32, (1, nk), 1)
        if nk == blk:
            mask = j <= i
        else:
            mask = (j >= i) & (j <= i + blk)
        return (srcs[0][_rows(qstart, blk, stride), :].astype(BF16), srcs[1][ksl, :].astype(BF16),
                srcs[2][ksl, :].astype(BF16), mask)

    def merged(old, new):
        (m_old, l_old, a_old), (m_new, l_new, a_new) = old, new
        m_tot = jnp.maximum(m_old, m_new)
        w_old = jnp.exp2(m_old - m_tot)
        w_new = jnp.exp2(m_new - m_tot)
        return m_tot, w_old * l_old + w_new * l_new, w_old * a_old + w_new * a_new

    def run_units(srcs, state_ref, units, nk, stride, merge):
        results = _stacked_head_pair_attention([operands(srcs, qstart, kstart, nk, stride)
                                                for qstart, kstart in units])
        for (qstart, _), new in zip(units, results):
            qsl = _rows(qstart, blk, stride)
            if merge:
                new = merged(tuple(state_ref[t, qsl, :] for t in range(3)), new)
            for t in range(3):
                state_ref[t, qsl, :] = new[t]

    written = set()
    for window, dil in DILATED_CFG:
        assert window // dil == blk
        if dil % ns == 0:
            srcs, state_ref, stride, group_len = (qp_ref, kp_ref, vp_ref), slab_ref, dil // ns, slab_len
        else:
            srcs, state_ref, stride, group_len = (q_ref, k_ref, v_ref), nat_ref, dil, seq
        merge = id(state_ref) in written
        written.add(id(state_ref))
        nblk = seq // dil // blk
        ub = _largest_divisor(dil, DILATED_UNITS)

        def class_start(c, stride=stride, group_len=group_len):
            return (c // stride) * group_len + c % stride if group_len != seq else c

        def first_blocks(it, carry, srcs=srcs, state_ref=state_ref, stride=stride, merge=merge, ub=ub,
                         class_start=class_start):
            starts = [class_start(it * ub + u) for u in range(ub)]
            run_units(srcs, state_ref, [(st, st) for st in starts], blk, stride, merge)
            return carry

        _repeat(dil // ub, first_blocks)
        if nblk > 1:
            ul = _largest_divisor(nblk - 1, DILATED_UNITS)
            per_class = (nblk - 1) // ul

            def later_blocks(it, carry, srcs=srcs, state_ref=state_ref, stride=stride, merge=merge, ul=ul,
                             per_class=per_class, class_start=class_start):
                base, grp = class_start(it // per_class), it % per_class
                qstarts = [base + stride * blk * (1 + grp * ul + u) for u in range(ul)]
                run_units(srcs, state_ref, [(qs, qs - stride * blk) for qs in qstarts], 2 * blk, stride, merge)
                return carry

            _repeat(dil * per_class, later_blocks)

    for s in range(ns):
        nat_rows = pl.ds(s, slab_len, stride=ns)
        slab_rows = slice(s * slab_len, (s + 1) * slab_len)
        _, den, num = merged(tuple(nat_ref[t, nat_rows, :] for t in range(3)),
                             tuple(slab_ref[t, slab_rows, :] for t in range(3)))
        nat_ref[2, nat_rows, :] = num / den
    o_ref[...] = nat_ref[2].astype(o_ref.dtype)


def _dilated_attention(q, k, v, batch, seq):
    n, width = q.shape
    for window, dil in DILATED_CFG:
        assert seq % (dil * BAND_BLOCK) == 0
    assert {dil % DILATED_SLABS == 0 for _, dil in DILATED_CFG} == {True, False}
    whole = pl.BlockSpec((seq, LANES), lambda b, hp: (b, hp))
    return pl.pallas_call(
        functools.partial(_dilated_kernel, seq=seq),
        grid=(batch, width // LANES),
        in_specs=[whole, whole, whole],
        out_specs=whole,
        out_shape=jax.ShapeDtypeStruct((n, width), BF16),
        scratch_shapes=[pltpu.VMEM((seq, LANES), F32)] * 3 + [pltpu.VMEM((3, seq, LANES), F32)] * 2,
        compiler_params=_params(("parallel", "parallel")),
    )(q, k, v)


def _deepnorm(x, f, g, b):
    y = DEEPNORM_ALPHA * x + f
    mu = jnp.mean(y, axis=1, keepdims=True)
    yc = y - mu
    var = jnp.mean(yc * yc, axis=1, keepdims=True)
    return yc * lax.rsqrt(var + LN_EPS) * g + b


def _outproj_ln_kernel(*refs, n_parts):
    part_refs = refs[:n_parts]
    w_ref, x_ref, g_ref, b_ref, o_ref, oh_ref = refs[n_parts:]
    mix = None
    start = 0
    for p_ref in part_refs:
        width = p_ref.shape[1]
        t = _dot(p_ref[...], w_ref[start:start + width, :])
        mix = t if mix is None else mix + t
        start += width
    y = _deepnorm(x_ref[...], mix, g_ref[...], b_ref[...])
    o_ref[...] = y
    oh_ref[...] = y.astype(BF16)


def _outproj_ln(parts, w, x2d, g, b):
    n, d = x2d.shape
    tm = PROJ_ROWS
    row = lambda width: pl.BlockSpec((tm, width), lambda i: (i, 0))
    vec = pl.BlockSpec((1, d), lambda i: (0, 0))
    return pl.pallas_call(
        functools.partial(_outproj_ln_kernel, n_parts=len(parts)),
        grid=(n // tm,),
        in_specs=[row(p.shape[1]) for p in parts] + [pl.BlockSpec(w.shape, lambda i: (0, 0)), row(d), vec, vec],
        out_specs=[row(d), row(d)],
        out_shape=[jax.ShapeDtypeStruct((n, d), F32), jax.ShapeDtypeStruct((n, d), BF16)],
        compiler_params=_params(("parallel",)),
    )(*parts, w, x2d, g.reshape(1, d), b.reshape(1, d))


def _silu(h):
    return h * (1.0 / (1.0 + jnp.exp(-h)))


def _swiglu_ln_kernel(xh_ref, x_ref, w1_ref, w3_ref, w2_ref, g_ref, b_ref, o_ref, acc_ref):
    f = pl.program_id(1)
    xh = xh_ref[...]
    hid = (_silu(_dot(xh, w1_ref[...])) * _dot(xh, w3_ref[...])).astype(BF16)
    y = _dot(hid, w2_ref[...])

    @pl.when(f == 0)
    def _():
        acc_ref[...] = y

    @pl.when(f > 0)
    def _():
        acc_ref[...] += y

    @pl.when(f == pl.num_programs(1) - 1)
    def _():
        o_ref[...] = _deepnorm(x_ref[...], acc_ref[...], g_ref[...], b_ref[...])


def _swiglu_ln(xh, x2d, w1, w3, w2, g, b):
    n, d = x2d.shape
    dff = w1.shape[1]
    tm = FFN_ROWS
    nf = FFN_CHUNKS
    fc = dff // nf
    row = lambda: pl.BlockSpec((tm, d), lambda i, f: (i, 0))
    vec = pl.BlockSpec((1, d), lambda i, f: (0, 0))
    mode = pl.Buffered(1) if nf == 1 else None
    return pl.pallas_call(
        _swiglu_ln_kernel,
        grid=(n // tm, nf),
        in_specs=[row(), row(),
                  pl.BlockSpec((d, fc), lambda i, f: (0, f), pipeline_mode=mode),
                  pl.BlockSpec((d, fc), lambda i, f: (0, f), pipeline_mode=mode),
                  pl.BlockSpec((fc, d), lambda i, f: (f, 0), pipeline_mode=mode),
                  vec, vec],
        out_specs=row(),
        out_shape=jax.ShapeDtypeStruct((n, d), F32),
        scratch_shapes=[pltpu.VMEM((tm, d), F32)],
        compiler_params=_params(("parallel", "arbitrary")),
    )(xh, x2d, w1, w3, w2, g.reshape(1, d), b.reshape(1, d))


def _add_ln_kernel(x_ref, f_ref, g_ref, b_ref, o_ref):
    o_ref[...] = _deepnorm(x_ref[...], f_ref[...], g_ref[...], b_ref[...])


def _add_ln(x2d, f2d, g, b):
    n, d = x2d.shape
    tm = PROJ_ROWS
    row = pl.BlockSpec((tm, d), lambda i: (i, 0))
    vec = pl.BlockSpec((1, d), lambda i: (0, 0))
    return pl.pallas_call(
        _add_ln_kernel,
        grid=(n // tm,),
        in_specs=[row, row, vec, vec],
        out_specs=row,
        out_shape=jax.ShapeDtypeStruct((n, d), F32),
        compiler_params=_params(("parallel",)),
    )(x2d, f2d, g.reshape(1, d), b.reshape(1, d))


def _router_kernel(xh_ref, rw_ref, gate_ref, lposc_ref, lpost_ref, cum_ref, carry_ref, *, tiles_per_group):
    tm = xh_ref.shape[0]
    i = pl.program_id(0)

    @pl.when(i % tiles_per_group == 0)
    def _():
        carry_ref[...] = jnp.zeros_like(carry_ref)

    lane = lax.broadcasted_iota(I32, (1, LANES), 1)
    logits = jnp.where(lane < N_EXPERTS, _dot(xh_ref[...], rw_ref[...]), NEG_INF)
    m1 = jnp.max(logits, axis=1, keepdims=True)
    i1 = jnp.min(jnp.where(logits == m1, lane, LANES), axis=1, keepdims=True)
    rest = jnp.where(lane == i1, NEG_INF, logits)
    m2 = jnp.max(rest, axis=1, keepdims=True)
    i2 = jnp.min(jnp.where(rest == m2, lane, LANES), axis=1, keepdims=True)
    e2 = jnp.exp(m2 - m1)
    g1 = 1.0 / (1.0 + e2)
    g2 = e2 / (1.0 + e2)
    sel1 = lane == i1
    sel2 = lane == i2
    routed = sel1 | sel2
    gate_ref[...] = jnp.where(sel1, g1, jnp.where(sel2, g2, 0.0))
    routedf = jnp.where(routed, 1.0, 0.0)
    earlier = (lax.broadcasted_iota(I32, (tm, tm), 1) < lax.broadcasted_iota(I32, (tm, tm), 0))
    earlier = jnp.where(earlier, 1.0, 0.0).astype(BF16)
    carry = carry_ref[...]
    lpos = jnp.where(routed, _dot(earlier, routedf.astype(BF16)) + carry, -1.0)
    lposc_ref[...] = lpos
    lpost_ref[...] = jnp.transpose(lpos)[0:N_EXPERTS, :]
    carry = carry + jnp.sum(routedf, axis=0, keepdims=True)
    carry_ref[...] = carry
    cum_ref[...] = carry.astype(I32).reshape(1, 1, LANES)


def _router(xh, rw):
    n, d = xh.shape
    tm = ROUTER_ROWS
    nt = n // tm
    row = pl.BlockSpec((tm, LANES), lambda i: (i, 0))
    return pl.pallas_call(
        functools.partial(_router_kernel, tiles_per_group=MOE_GROUP // tm),
        grid=(nt,),
        in_specs=[pl.BlockSpec((tm, d), lambda i: (i, 0)), pl.BlockSpec(rw.shape, lambda i: (0, 0))],
        out_specs=[row, row,
                   pl.BlockSpec((N_EXPERTS, tm), lambda i: (0, i)),
                   pl.BlockSpec((1, 1, LANES), lambda i: (i, 0, 0))],
        out_shape=[jax.ShapeDtypeStruct((n, LANES), F32), jax.ShapeDtypeStruct((n, LANES), F32),
                   jax.ShapeDtypeStruct((N_EXPERTS, n), F32), jax.ShapeDtypeStruct((nt, 1, LANES), I32)],
        scratch_shapes=[pltpu.VMEM((1, LANES), F32)],
        compiler_params=_params(("arbitrary",)),
    )(xh, rw)


def _moe_kernel(bnd_ref, xh_ref, lpost_ref, lposc_ref, gatec_ref, w1_ref, w3_ref, w2_ref,
                o_ref, xg_ref, yacc_ref):
    tg = xh_ref.shape[0]
    win, ch, fr, align = MOE_WIN, ROUTER_ROWS, MOE_FFN_ROWS, MOE_ROW_ALIGN
    nch = tg // ch
    g, e, f = pl.program_id(0), pl.program_id(1), pl.program_id(2)
    nf = pl.num_programs(2)

    def bound(c):
        return bnd_ref[(g * (nch + 1) + c) * N_EXPERTS + e]

    cnt = bound(nch)
    n_ffn = (cnt + fr - 1) // fr

    def tok_chunk(c):
        return pl.ds(pl.multiple_of(c * ch, ch), ch)

    def windows(c):
        r0, r1 = bound(c), bound(c + 1)
        first = (r0 // align) * align
        return first, jnp.where(r1 > r0, (r1 - first + win - 1) // win, 0)

    @pl.when((e == 0) & (f == 0))
    def _():
        o_ref[...] = jnp.zeros_like(o_ref)

    @pl.when(f == 0)
    def _compact():
        def clear(s, carry):
            rows = pl.ds(pl.multiple_of(s * win, win), win)
            xg_ref[rows, :] = jnp.zeros((win, xg_ref.shape[1]), xg_ref.dtype)
            yacc_ref[rows, :] = jnp.zeros((win, yacc_ref.shape[1]), F32)
            return carry

        lax.fori_loop(0, (cnt + max(fr, win) + win - 1) // win, clear, 0)

        def per_chunk(c, carry):
            first, n_win = windows(c)
            lp = lpost_ref[pl.ds(e, 1), tok_chunk(c)]

            def per_window(w, carry2):
                start = pl.multiple_of(first + w * win, align)
                want = (lax.broadcasted_iota(I32, (win, 1), 0) + start).astype(F32)
                pick = jnp.where(lp == want, 1.0, 0.0).astype(BF16)
                rows = pl.ds(start, win)
                xg_ref[rows, :] += _dot(pick, xh_ref[tok_chunk(c), :]).astype(xg_ref.dtype)
                return carry2

            return lax.fori_loop(0, n_win, per_window, carry)

        lax.fori_loop(0, nch, per_chunk, 0)

    def ffn_rows(rows):
        xs = xg_ref[rows, :]
        hid = (_silu(_dot(xs, w1_ref[...])) * _dot(xs, w3_ref[...])).astype(BF16)
        yacc_ref[rows, :] += _dot(hid, w2_ref[...])

    def ffn_pair(p, carry):
        ffn_rows(pl.ds(pl.multiple_of(p * 2 * fr, align), 2 * fr))
        return carry

    lax.fori_loop(0, n_ffn // 2, ffn_pair, 0)

    @pl.when(n_ffn % 2 == 1)
    def _():
        ffn_rows(pl.ds(pl.multiple_of((n_ffn - 1) * fr, align), fr))

    @pl.when(f == nf - 1)
    def _scatter():
        lane = lax.broadcasted_iota(I32, (1, LANES), 1)

        def per_chunk(c, carry):
            first, n_win = windows(c)
            toks = tok_chunk(c)
            lp = jnp.sum(jnp.where(lane == e, lposc_ref[toks, :], 0.0), axis=1, keepdims=True)
            gt = jnp.sum(jnp.where(lane == e, gatec_ref[toks, :], 0.0), axis=1, keepdims=True)

            def per_window(w, carry2):
                start = pl.multiple_of(first + w * win, align)
                y = yacc_ref[pl.ds(start, win), :]
                y_hi = y.astype(BF16)
                y_lo = (y - y_hi.astype(F32)).astype(BF16)
                want = (lax.broadcasted_iota(I32, (1, win), 1) + start).astype(F32)
                place = jnp.where(lp == want, 1.0, 0.0).astype(BF16)
                o_ref[toks, :] += gt * (_dot(place, y_hi) + _dot(place, y_lo))
                return carry2

            return lax.fori_loop(0, n_win, per_window, carry)

        lax.fori_loop(0, nch, per_chunk, 0)


def _chunk_bounds(cum):
    nch = MOE_GROUP // ROUTER_ROWS
    per_group = cum[:, 0, :N_EXPERTS].reshape(-1, nch, N_EXPERTS)
    return jnp.pad(per_group, ((0, 0), (1, 0), (0, 0))).reshape(-1)


def _moe(bounds, xh, lpost, lposc, gatec, w1, w3, w2, layer):
    n, d = xh.shape
    tg = MOE_GROUP
    dff = w1.shape[3]
    fc = MOE_FF_CHUNK
    cap = -(-(tg + max(MOE_FFN_ROWS, MOE_WIN)) // MOE_WIN) * MOE_WIN
    grid_spec = pltpu.PrefetchScalarGridSpec(
        num_scalar_prefetch=1,
        grid=(n // tg, N_EXPERTS, dff // fc),
        in_specs=[pl.BlockSpec((tg, d), lambda g, e, f, c: (g, 0)),
                  pl.BlockSpec((N_EXPERTS, tg), lambda g, e, f, c: (0, g)),
                  pl.BlockSpec((tg, LANES), lambda g, e, f, c: (g, 0)),
                  pl.BlockSpec((tg, LANES), lambda g, e, f, c: (g, 0)),
                  pl.BlockSpec((None, None, d, fc), lambda g, e, f, c: (layer, e, 0, f)),
                  pl.BlockSpec((None, None, d, fc), lambda g, e, f, c: (layer, e, 0, f)),
                  pl.BlockSpec((None, None, fc, d), lambda g, e, f, c: (layer, e, f, 0))],
        out_specs=pl.BlockSpec((tg, d), lambda g, e, f, c: (g, 0)),
        scratch_shapes=[pltpu.VMEM((cap, d), BF16), pltpu.VMEM((cap, d), F32)],
    )
    return pl.pallas_call(
        _moe_kernel,
        grid_spec=grid_spec,
        out_shape=jax.ShapeDtypeStruct((n, d), F32),
        compiler_params=_params(("parallel", "arbitrary", "arbitrary")),
    )(bounds, xh, lpost, lposc, gatec, w1, w3, w2)


EVEN_KB_SECTION = 2


def _even_weight_layout(w_in):
    hd = HEAD_DIM
    widths = (A_HEADS * hd, hd, hd, IDX_HEADS * IDX_DIM, IDX_DIM, IDX_HEADS, B_HEADS * hd, B_HEADS * hd, B_HEADS * hd)
    offs = [0]
    for wd in widths:
        offs.append(offs[-1] + wd)
    qa, ka, va, iq, ik, iw, qb, kb, vb = (w_in[:, offs[j]:offs[j + 1]] for j in range(9))
    iw_pad = jnp.pad(iw, ((0, 0), (0, LANES - IDX_HEADS)))
    w = jnp.concatenate([qa, qb, kb, vb, ka, ka, va, va, iq, ik, ik, ik, ik, iw_pad], axis=1)
    mixw = B_HEADS * hd
    sections = ((0, A_HEADS * hd, hd, QK_SCALE),
                (512, mixw, hd, QK_SCALE),
                (1024, mixw, hd, None),
                (1536, mixw, None, None),
                (2048, LANES, hd, None),
                (2176, LANES, None, None),
                (2304, IDX_HEADS * IDX_DIM, IDX_DIM, None),
                (2560, LANES, IDX_DIM, None),
                (2688, LANES, None, None))
    dtypes = (BF16,) * 8 + (F32,)
    return w.astype(BF16), sections, dtypes


def kernel(x, even_w_in, even_w_out, even_ln1_g, even_ln1_b, even_w1, even_w3, even_w2, even_ln2_g, even_ln2_b, odd_w_in, odd_w_out, odd_ln1_g, odd_ln1_b, odd_router, odd_w1, odd_w3, odd_w2, odd_ln2_g, odd_ln2_b):
    batch, seq, d = x.shape
    n = batch * seq
    tables = _rope_tables(seq)
    x2d = x.reshape(n, d)
    odd_w1h, odd_w3h, odd_w2h = odd_w1.astype(BF16), odd_w3.astype(BF16), odd_w2.astype(BF16)
    for layer in range(DEPTH):
        i = layer // 2
        if layer % 2 == 0:
            w, sections, dtypes = _even_weight_layout(even_w_in[i])
            qa, qb, kb, vb, kk, vv, iq, ik4, iw, kb_mean = _project(x2d, w, tables, sections, dtypes, seq,
                                                                    mean_section=EVEN_KB_SECTION)
            o_a = _dsa_attention(iq, iw, ik4, qa, kk, vv, batch, seq)
            o_b = _moba_attention(qb, kb, vb, kb_mean, batch, seq)
            x2d, xh = _outproj_ln([o_a, o_b], even_w_out[i].astype(BF16), x2d, even_ln1_g[i], even_ln1_b[i])
            x2d = _swiglu_ln(xh, x2d, even_w1[i].astype(BF16), even_w3[i].astype(BF16), even_w2[i].astype(BF16),
                             even_ln2_g[i], even_ln2_b[i])
        else:
            mix = C_HEADS * HEAD_DIM
            sections = ((0, mix, HEAD_DIM, QK_SCALE), (mix, mix, HEAD_DIM, None), (2 * mix, mix, None, None))
            q, k, v = _project(x2d, odd_w_in[i].astype(BF16), tables, sections, (F32, F32, F32), seq)
            o = _dilated_attention(q, k, v, batch, seq)
            x2d, xh = _outproj_ln([o], odd_w_out[i].astype(BF16), x2d, odd_ln1_g[i], odd_ln1_b[i])
            rw = jnp.pad(odd_router[i], ((0, 0), (0, LANES - N_EXPERTS))).astype(BF16)
            gatec, lposc, lpost, cum = _router(xh, rw)
            ffn = _moe(_chunk_bounds(cum), xh, lpost, lposc, gatec, odd_w1h, odd_w3h, odd_w2h, i)
            x2d = _add_ln(x2d, ffn, odd_ln2_g[i], odd_ln2_b[i])
    return x2d.reshape(batch, seq, d)
```

```python
import functools

import jax
import jax.numpy as jnp
from jax import lax
from jax.experimental import pallas as pl
from jax.experimental.pallas import tpu as pltpu

F32 = jnp.float32
BF16 = jnp.bfloat16
I32 = jnp.int32

DEPTH = 4
HEAD_DIM = 64
ROPE_THETA = 10000.0
LN_EPS = 1e-5
A_HEADS = 8
IDX_HEADS = 8
IDX_DIM = 32
DSA_TOPK = 256
B_HEADS = 8
MOBA_BLOCK = 256
MOBA_TOPK = 3
C_HEADS = 16
DILATED_CFG = ((128, 1), (512, 4), (2048, 16))
BAND_BLOCK = 128
N_EXPERTS = 8
DEEPNORM_ALPHA = (2 * DEPTH) ** 0.25
QK_SCALE = HEAD_DIM ** -0.5 * 1.4426950408889634

LANES = 128
VMEM_LIMIT_BYTES = 52 * 1024 * 1024

PROJ_ROWS = 512
FFN_ROWS = 512
FFN_CHUNKS = 1
ROUTER_ROWS = 512
MOE_GROUP = 2048
MOE_WIN = 256
MOE_FFN_ROWS = 272
MOE_ROW_ALIGN = 16
MOE_FF_CHUNK = 512
DSA_ROWS = 256
DSA_SEARCH_GROUPS = 4
DSA_SEARCH_UNROLL = 4
MOBA_GATE_ROWS = 16
DILATED_UNITS = 8
DILATED_SLABS = 4
ATTN_KEY_STEP = 512
MOBA_KEY_STEP = 256

NEG_INF = float("-inf")
NEG_INF_KEY = -2139095041


def _params(semantics):
    return pltpu.CompilerParams(dimension_semantics=semantics, vmem_limit_bytes=VMEM_LIMIT_BYTES)


def _dot(a, b):
    return jnp.dot(a, b, preferred_element_type=F32)


def _dot_nt(a, b):
    return lax.dot_general(a, b, (((1,), (1,)), ((), ())), preferred_element_type=F32)


def _lane_group(shape, group, idx):
    lane = lax.broadcasted_iota(I32, shape, len(shape) - 1)
    return (lane // group) == idx


def _head_pair_attention(units):
    first = _lane_group((1, LANES), HEAD_DIM, 0)
    raw, unit_masks = [], []
    for q2, k2, _, mask in units:
        raw.append(_dot_nt(jnp.where(first, q2, jnp.zeros_like(q2)), k2))
        unit_masks.append(mask() if callable(mask) else mask)
        raw.append(_dot_nt(jnp.where(first, jnp.zeros_like(q2), q2), k2))
    scores = []
    for u, mask in enumerate(unit_masks):
        masks = mask if isinstance(mask, (tuple, list)) else (mask, mask)
        for h in range(2):
            if masks[h].dtype == jnp.bool_:
                scores.append(jnp.where(masks[h], raw[2 * u + h], NEG_INF))
            else:
                scores.append(raw[2 * u + h] + masks[h])
    maxes = [jnp.max(s, axis=1, keepdims=True) for s in scores]
    weights = [jnp.exp2(s - mx).astype(BF16) for s, mx in zip(scores, maxes)]
    results = []
    for u, (_, _, v2, _) in enumerate(units):
        ones = jnp.ones_like(v2)
        pv0 = _dot(weights[2 * u], jnp.where(first, v2, ones))
        pv1 = _dot(weights[2 * u + 1], jnp.where(first, ones, v2))
        den = jnp.where(first, pv0[:, LANES - 1:LANES], pv1[:, 0:1])
        results.append((jnp.where(first, maxes[2 * u], maxes[2 * u + 1]), den, jnp.where(first, pv0, pv1)))
    return results


def _stacked_head_pair_attention(units):
    first = _lane_group((1, LANES), HEAD_DIM, 0)
    raw = [_dot_nt(jnp.concatenate([jnp.where(first, q2, jnp.zeros_like(q2)),
                                    jnp.where(first, jnp.zeros_like(q2), q2)], axis=0), k2)
           for q2, k2, _, _ in units]
    stats, weights = [], []
    for r, (q2, _, _, mask) in zip(raw, units):
        rows = q2.shape[0]
        halves = [jnp.where(mask, r[h * rows:(h + 1) * rows], NEG_INF) for h in range(2)]
        maxes = [jnp.max(s, axis=1, keepdims=True) for s in halves]
        exps = [jnp.exp2(s - mx) for s, mx in zip(halves, maxes)]
        dens = [jnp.sum(e, axis=1, keepdims=True) for e in exps]
        stats.append((jnp.where(first, maxes[0], maxes[1]), jnp.where(first, dens[0], dens[1])))
        weights.append(jnp.concatenate(exps, axis=0).astype(BF16))
    results = []
    for (mx, den), w, (q2, _, v2, _) in zip(stats, weights, units):
        rows = q2.shape[0]
        pv = _dot(w, v2)
        results.append((mx, den, jnp.where(first, pv[:rows], pv[rows:])))
    return results


def _rope_lanes(t, cosf, sinf, half):
    lane = lax.broadcasted_iota(I32, t.shape, 1)
    first = (lane % (2 * half)) < half
    swapped = jnp.where(first, pltpu.roll(t, LANES - half, 1), pltpu.roll(t, half, 1))
    return t * cosf + swapped * sinf


def _proj_kernel(x_ref, w_ref, c64_ref, s64_ref, c32_ref, s32_ref, *out_refs, sections, mean_section):
    x = x_ref[...].astype(BF16)
    for idx, (o_ref, (start, width, rope, scale)) in enumerate(zip(out_refs, sections)):
        h = _dot(x, w_ref[:, start:start + width])
        if rope is not None:
            cosf = (c64_ref if rope == HEAD_DIM else c32_ref)[...]
            sinf = (s64_ref if rope == HEAD_DIM else s32_ref)[...]
            parts = [_rope_lanes(h[:, g * LANES:(g + 1) * LANES], cosf, sinf, rope // 2)
                     for g in range(width // LANES)]
            h = parts[0] if len(parts) == 1 else jnp.concatenate(parts, axis=1)
        if scale is not None:
            h = h * scale
        o_ref[...] = h.astype(o_ref.dtype)
        if idx == mean_section:
            m_ref = out_refs[len(sections)]
            for blk in range(h.shape[0] // MOBA_BLOCK):
                rows = h[blk * MOBA_BLOCK:(blk + 1) * MOBA_BLOCK]
                m_ref[0, blk:blk + 1, :] = jnp.sum(rows, axis=0, keepdims=True) * (1.0 / MOBA_BLOCK)


def _project(x2d, w, tables, sections, out_dtypes, seq, mean_section=None):
    n, d = x2d.shape
    tm = PROJ_ROWS
    pos_blocks = seq // tm
    tab_spec = pl.BlockSpec((tm, LANES), lambda i: (i % pos_blocks, 0))
    out_specs = [pl.BlockSpec((tm, sec[1]), lambda i: (i, 0)) for sec in sections]
    out_shape = [jax.ShapeDtypeStruct((n, sec[1]), dt) for sec, dt in zip(sections, out_dtypes)]
    if mean_section is not None:
        per_tile, width = tm // MOBA_BLOCK, sections[mean_section][1]
        out_specs.append(pl.BlockSpec((1, per_tile, width), lambda i: (i, 0, 0)))
        out_shape.append(jax.ShapeDtypeStruct((n // tm, per_tile, width), F32))
    outs = pl.pallas_call(
        functools.partial(_proj_kernel, sections=sections, mean_section=mean_section),
        grid=(n // tm,),
        in_specs=[pl.BlockSpec((tm, d), lambda i: (i, 0)),
                  pl.BlockSpec(w.shape, lambda i: (0, 0)),
                  tab_spec, tab_spec, tab_spec, tab_spec],
        out_specs=out_specs,
        out_shape=out_shape,
        compiler_params=_params(("parallel",)),
    )(x2d, w, *tables)
    outs = list(outs)
    if mean_section is not None:
        outs[-1] = outs[-1].reshape(n // MOBA_BLOCK, -1)
    return outs


def _rope_tables(seq):
    out = []
    for dim in (HEAD_DIM, IDX_DIM):
        inv = ROPE_THETA ** (-jnp.arange(0, dim, 2, dtype=F32) / dim)
        ang = jnp.arange(seq, dtype=F32)[:, None] * inv[None, :]
        cos, sin = jnp.cos(ang), jnp.sin(ang)
        reps = LANES // dim
        out.append(jnp.tile(jnp.concatenate([cos, cos], -1), (1, reps)))
        out.append(jnp.tile(jnp.concatenate([-sin, sin], -1), (1, reps)))
    return out


def _dsa_block(i, iq_ref, iw_ref, ik_ref, q_ref, kk_ref, vv_ref, o_ref, key_ref, sel_ref, *, kext, topk):
    qb = DSA_ROWS
    keys = slice(0, kext)
    qpos = i * qb + lax.broadcasted_iota(I32, (qb, 1), 0)
    kpos = lax.broadcasted_iota(I32, (1, kext), 1)
    causal = kpos <= qpos

    ik = ik_ref[keys, :]
    iw = iw_ref[...] * (IDX_DIM ** -0.5 * IDX_HEADS ** -0.5)
    heads_per_group = LANES // IDX_DIM
    score = jnp.zeros((qb, kext), F32)
    for h in range(IDX_HEADS):
        grp, j = divmod(h, heads_per_group)
        iqg = iq_ref[:, grp * LANES:(grp + 1) * LANES]
        iqm = jnp.where(_lane_group((1, LANES), IDX_DIM, j), iqg, jnp.zeros_like(iqg))
        rel = jnp.maximum(_dot_nt(iqm, ik), 0.0)
        score = score + rel * iw[:, h:h + 1]
    bits = pltpu.bitcast(score, I32)
    neg = bits >> 31
    int_min = jnp.int32(-2 ** 31)
    key_ref[:, keys] = jnp.where(causal, (bits ^ (neg & 0x7FFFFFFF)) - neg, int_min)

    grp_rows = qb // DSA_SEARCH_GROUPS
    groups = [slice(g * grp_rows, (g + 1) * grp_rows) for g in range(DSA_SEARCH_GROUPS)]

    def count_ge(rows, t):
        return jnp.sum(jnp.where(key_ref[rows, keys] >= t, 1.0, 0.0), axis=1, keepdims=True)

    def search(it, thrs):
        bit = jnp.left_shift(jnp.int32(1), 30 - it)
        return tuple(jnp.where(count_ge(rows, thr | bit) >= topk, thr | bit, thr)
                     for rows, thr in zip(groups, thrs))

    zero = jnp.zeros((grp_rows, 1), I32)
    thrs = tuple(jnp.where(count_ge(rows, zero) >= topk, zero, int_min) for rows in groups)
    thrs = lax.fori_loop(0, 31, search, thrs, unroll=DSA_SEARCH_UNROLL)
    thr = jnp.concatenate(thrs, axis=0)

    floor = jnp.maximum(thr, int_min + 1)
    key = key_ref[:, keys]
    sel_ref[:, keys] = jnp.where(key >= floor, 0.0, NEG_INF)
    n_ge = jnp.sum(jnp.where(key >= thr, 1.0, 0.0), axis=1, keepdims=True)
    tie_rows = (n_ge > topk) & (thr > int_min)

    @pl.when(jnp.max(tie_rows.astype(I32)) > 0)
    def _break_ties():
        upper = (lax.broadcasted_iota(I32, (LANES, LANES), 0)
                 < lax.broadcasted_iota(I32, (LANES, LANES), 1))
        upper = jnp.where(upper, 1.0, 0.0).astype(BF16)
        n_gt = jnp.sum(jnp.where(key_ref[:, keys] > thr, 1.0, 0.0), axis=1, keepdims=True)
        need = topk - n_gt
        before = jnp.zeros((qb, 1), F32)
        for c in range(kext // LANES):
            sl = slice(c * LANES, (c + 1) * LANES)
            keyc = key_ref[:, sl]
            eqc = keyc == thr
            eqf = jnp.where(eqc, 1.0, 0.0)
            rank = _dot(eqf.astype(BF16), upper) + before
            take = ((keyc > thr) | (eqc & (rank < need))) & (keyc > int_min)
            sel_ref[:, sl] = jnp.where(take, 0.0, NEG_INF)
            before = before + jnp.sum(eqf, axis=1, keepdims=True)

    selected = sel_ref[:, keys]
    kk = kk_ref[keys, :]
    vv = vv_ref[keys, :]
    pairs = [(q_ref[:, hp * LANES:(hp + 1) * LANES], kk, vv, selected) for hp in range(A_HEADS // 2)]
    for hp, (_, den, num) in enumerate(_head_pair_attention(pairs)):
        o_ref[:, hp * LANES:(hp + 1) * LANES] = (num / den).astype(o_ref.dtype)


def _dsa_kernel(*refs, seq, topk):
    i = pl.program_id(1)
    blocks_per_step = ATTN_KEY_STEP // DSA_ROWS
    for v in range(seq // ATTN_KEY_STEP):
        @pl.when(i // blocks_per_step == v)
        def _(v=v):
            _dsa_block(i, *refs, kext=(v + 1) * ATTN_KEY_STEP, topk=topk)


def _dsa_attention(iq, iw, ik4, qa, kk, vv, batch, seq):
    n = batch * seq
    qb = DSA_ROWS
    nq = seq // qb
    topk = min(DSA_TOPK, seq // 4)
    rows = lambda width: pl.BlockSpec((qb, width), lambda b, i: (b * nq + i, 0))
    whole = pl.BlockSpec((seq, LANES), lambda b, i: (b, 0))
    return pl.pallas_call(
        functools.partial(_dsa_kernel, seq=seq, topk=topk),
        grid=(batch, nq),
        in_specs=[rows(iq.shape[1]), rows(LANES), whole, rows(qa.shape[1]), whole, whole],
        out_specs=rows(qa.shape[1]),
        out_shape=jax.ShapeDtypeStruct((n, qa.shape[1]), BF16),
        scratch_shapes=[pltpu.VMEM((qb, seq), I32), pltpu.VMEM((qb, seq), F32)],
        compiler_params=_params(("parallel", "parallel")),
    )(iq, iw, ik4, qa, kk, vv)


def _moba_block(qi, q_ref, k_ref, v_ref, kmean_ref, o_ref, *, kext, kt):
    bs = MOBA_BLOCK
    nb = kext // bs
    q2 = q_ref[...]
    k2 = k_ref[0:kext, :]
    v2 = v_ref[0:kext, :]
    nb_all = kmean_ref.shape[0]
    nbp = max(nb_all, MOBA_GATE_ROWS)
    blk_row = lax.broadcasted_iota(I32, (nbp, kext), 0)
    blk_of_key = lax.broadcasted_iota(I32, (nbp, kext), 1) // bs
    member = jnp.where(blk_row == blk_of_key, 1.0, 0.0).astype(BF16)
    kmean = jnp.concatenate([kmean_ref[...], jnp.zeros((nbp - nb_all, LANES), F32)], axis=0).astype(BF16)
    n_iota = lax.broadcasted_iota(I32, (nbp, 1), 0)
    past = n_iota < qi
    qrow = lax.broadcasted_iota(I32, (bs, 1), 0)
    kcol = lax.broadcasted_iota(I32, (1, kext), 1)
    own = (kcol >= qi * bs) & (kcol <= qi * bs + qrow)
    gates = [jnp.where(past, _dot_nt(kmean, jnp.where(_lane_group((1, LANES), HEAD_DIM, half), q2,
                                                      jnp.zeros_like(q2))), NEG_INF)
             for half in range(2)]

    def block_masks():
        chosen = []
        for gate in gates:
            rank = jnp.zeros((nbp, bs), I32)
            for m in range(nb):
                gm = gate[m:m + 1, :]
                beats = (gm > gate) | ((gm == gate) & (m < n_iota))
                rank = rank + beats.astype(I32)
            chosen.append(jnp.transpose(jnp.where(past & (rank < kt), 1.0, 0.0)).astype(BF16))
        return [(_dot(c, member) > 0.5) | own for c in chosen]

    (_, den, num), = _head_pair_attention([(q2, k2, v2, block_masks)])
    o_ref[...] = (num / den).astype(o_ref.dtype)


def _moba_kernel(*refs, seq):
    qi = pl.program_id(2)
    kt = min(MOBA_TOPK, seq // MOBA_BLOCK - 1)
    blocks_per_step = MOBA_KEY_STEP // MOBA_BLOCK
    for v in range(seq // MOBA_KEY_STEP):
        @pl.when(qi // blocks_per_step == v)
        def _(v=v):
            _moba_block(qi, *refs, kext=(v + 1) * MOBA_KEY_STEP, kt=kt)


def _moba_attention(q, k, v, kmean, batch, seq):
    n, width = q.shape
    bs = MOBA_BLOCK
    nb = seq // bs
    rows = pl.BlockSpec((bs, LANES), lambda b, hp, qi: (b * nb + qi, hp))
    whole = pl.BlockSpec((seq, LANES), lambda b, hp, qi: (b, hp))
    means = pl.BlockSpec((nb, LANES), lambda b, hp, qi: (b, hp))
    return pl.pallas_call(
        functools.partial(_moba_kernel, seq=seq),
        grid=(batch, width // LANES, nb),
        in_specs=[rows, whole, whole, means],
        out_specs=rows,
        out_shape=jax.ShapeDtypeStruct((n, width), BF16),
        compiler_params=_params(("parallel", "parallel", "parallel")),
    )(q, k, v, kmean)


def _rows(start, size, stride):
    return pl.ds(start, size) if stride == 1 else pl.ds(start, size, stride=stride)


def _largest_divisor(n, limit):
    return max(d for d in range(1, limit + 1) if n % d == 0)


def _repeat(trips, body):
    if trips == 1:
        body(0, 0)
    else:
        lax.fori_loop(0, trips, body, 0)


def _dilated_kernel(q_ref, k_ref, v_ref, o_ref, qp_ref, kp_ref, vp_ref, nat_ref, slab_ref, *, seq):
    blk = BAND_BLOCK
    ns = DILATED_SLABS
    slab_len = seq // ns

    for s in range(ns):
        rows = slice(s * slab_len, (s + 1) * slab_len)
        qp_ref[rows, :] = q_ref[pl.ds(s, slab_len, stride=ns), :]
        kp_ref[rows, :] = k_ref[pl.ds(s, slab_len, stride=ns), :]
        vp_ref[rows, :] = v_ref[pl.ds(s, slab_len, stride=ns), :]

    def operands(srcs, qstart, kstart, nk, stride):
        ksl = _rows(kstart, nk, stride)
        i = lax.broadcasted_iota(I32, (blk, 1), 0)
        j = lax.broadcasted_iota(I32, (1, nk), 1)
        if nk == blk:
            mask = j <= i
        else:
            mask = (j >= i) & (j <= i + blk)
        return (srcs[0][_rows(qstart, blk, stride), :].astype(BF16), srcs[1][ksl, :].astype(BF16),
                srcs[2][ksl, :].astype(BF16), mask)

    def merged(old, new):
        (m_old, l_old, a_old), (m_new, l_new, a_new) = old, new
        m_tot = jnp.maximum(m_old, m_new)
        w_old = jnp.exp2(m_old - m_tot)
        w_new = jnp.exp2(m_new - m_tot)
        return m_tot, w_old * l_old + w_new * l_new, w_old * a_old + w_new * a_new

    def run_units(srcs, state_ref, units, nk, stride, merge):
        results = _stacked_head_pair_attention([operands(srcs, qstart, kstart, nk, stride)
                                                for qstart, kstart in units])
        for (qstart, _), new in zip(units, results):
            qsl = _rows(qstart, blk, stride)
            if merge:
                new = merged(tuple(state_ref[t, qsl, :] for t in range(3)), new)
            for t in range(3):
                state_ref[t, qsl, :] = new[t]

    written = set()
    for window, dil in DILATED_CFG:
        assert window // dil == blk
        if dil % ns == 0:
            srcs, state_ref, stride, group_len = (qp_ref, kp_ref, vp_ref), slab_ref, dil // ns, slab_len
        else:
            srcs, state_ref, stride, group_len = (q_ref, k_ref, v_ref), nat_ref, dil, seq
        merge = id(state_ref) in written
        written.add(id(state_ref))
        nblk = seq // dil // blk
        ub = _largest_divisor(dil, DILATED_UNITS)

        def class_start(c, stride=stride, group_len=group_len):
            return (c // stride) * group_len + c % stride if group_len != seq else c

        def first_blocks(it, carry, srcs=srcs, state_ref=state_ref, stride=stride, merge=merge, ub=ub,
                         class_start=class_start):
            starts = [class_start(it * ub + u) for u in range(ub)]
            run_units(srcs, state_ref, [(st, st) for st in starts], blk, stride, merge)
            return carry

        _repeat(dil // ub, first_blocks)
        if nblk > 1:
            later = nblk - 1
            ul = _largest_divisor(dil * later, DILATED_UNITS)

            def later_blocks(it, carry, srcs=srcs, state_ref=state_ref, stride=stride, merge=merge, ul=ul,
                             later=later, class_start=class_start):
                flat = [it * ul + u for u in range(ul)]
                qstarts = [class_start(t // later) + stride * blk * (1 + t % later) for t in flat]
                run_units(srcs, state_ref, [(qs, qs - stride * blk) for qs in qstarts], 2 * blk, stride, merge)
                return carry

            _repeat(dil * later // ul, later_blocks)

    for s in range(ns):
        nat_rows = pl.ds(s, slab_len, stride=ns)
        slab_rows = slice(s * slab_len, (s + 1) * slab_len)
        _, den, num = merged(tuple(nat_ref[t, nat_rows, :] for t in range(3)),
                             tuple(slab_ref[t, slab_rows, :] for t in range(3)))
        nat_ref[2, nat_rows, :] = num / den
    o_ref[...] = nat_ref[2].astype(o_ref.dtype)


def _dilated_attention(q, k, v, batch, seq):
    n, width = q.shape
    for window, dil in DILATED_CFG:
        assert seq % (dil * BAND_BLOCK) == 0
    assert {dil % DILATED_SLABS == 0 for _, dil in DILATED_CFG} == {True, False}
    whole = pl.BlockSpec((seq, LANES), lambda b, hp: (b, hp))
    return pl.pallas_call(
        functools.partial(_dilated_kernel, seq=seq),
        grid=(batch, width // LANES),
        in_specs=[whole, whole, whole],
        out_specs=whole,
        out_shape=jax.ShapeDtypeStruct((n, width), BF16),
        scratch_shapes=[pltpu.VMEM((seq, LANES), F32)] * 3 + [pltpu.VMEM((3, seq, LANES), F32)] * 2,
        compiler_params=_params(("parallel", "parallel")),
    )(q, k, v)


def _deepnorm(x, f, g, b):
    y = DEEPNORM_ALPHA * x + f
    mu = jnp.mean(y, axis=1, keepdims=True)
    yc = y - mu
    var = jnp.mean(yc * yc, axis=1, keepdims=True)
    return yc * lax.rsqrt(var + LN_EPS) * g + b


def _outproj_ln_kernel(*refs, n_parts):
    part_refs = refs[:n_parts]
    w_ref, x_ref, g_ref, b_ref, o_ref, oh_ref = refs[n_parts:]
    mix = None
    start = 0
    for p_ref in part_refs:
        width = p_ref.shape[1]
        t = _dot(p_ref[...], w_ref[start:start + width, :])
        mix = t if mix is None else mix + t
        start += width
    y = _deepnorm(x_ref[...], mix, g_ref[...], b_ref[...])
    o_ref[...] = y
    oh_ref[...] = y.astype(BF16)


def _outproj_ln(parts, w, x2d, g, b):
    n, d = x2d.shape
    tm = PROJ_ROWS
    row = lambda width: pl.BlockSpec((tm, width), lambda i: (i, 0))
    vec = pl.BlockSpec((1, d), lambda i: (0, 0))
    return pl.pallas_call(
        functools.partial(_outproj_ln_kernel, n_parts=len(parts)),
        grid=(n // tm,),
        in_specs=[row(p.shape[1]) for p in parts] + [pl.BlockSpec(w.shape, lambda i: (0, 0)), row(d), vec, vec],
        out_specs=[row(d), row(d)],
        out_shape=[jax.ShapeDtypeStruct((n, d), F32), jax.ShapeDtypeStruct((n, d), BF16)],
        compiler_params=_params(("parallel",)),
    )(*parts, w, x2d, g.reshape(1, d), b.reshape(1, d))


def _silu(h):
    return h * (1.0 / (1.0 + jnp.exp(-h)))


def _swiglu_ln_kernel(xh_ref, x_ref, w1_ref, w3_ref, w2_ref, g_ref, b_ref, o_ref, acc_ref):
    f = pl.program_id(1)
    xh = xh_ref[...]
    hid = (_silu(_dot(xh, w1_ref[...])) * _dot(xh, w3_ref[...])).astype(BF16)
    y = _dot(hid, w2_ref[...])

    @pl.when(f == 0)
    def _():
        acc_ref[...] = y

    @pl.when(f > 0)
    def _():
        acc_ref[...] += y

    @pl.when(f == pl.num_programs(1) - 1)
    def _():
        o_ref[...] = _deepnorm(x_ref[...], acc_ref[...], g_ref[...], b_ref[...])


def _swiglu_ln(xh, x2d, w1, w3, w2, g, b):
    n, d = x2d.shape
    dff = w1.shape[1]
    tm = FFN_ROWS
    nf = FFN_CHUNKS
    fc = dff // nf
    row = lambda: pl.BlockSpec((tm, d), lambda i, f: (i, 0))
    vec = pl.BlockSpec((1, d), lambda i, f: (0, 0))
    mode = pl.Buffered(1) if nf == 1 else None
    return pl.pallas_call(
        _swiglu_ln_kernel,
        grid=(n // tm, nf),
        in_specs=[row(), row(),
                  pl.BlockSpec((d, fc), lambda i, f: (0, f), pipeline_mode=mode),
                  pl.BlockSpec((d, fc), lambda i, f: (0, f), pipeline_mode=mode),
                  pl.BlockSpec((fc, d), lambda i, f: (f, 0), pipeline_mode=mode),
                  vec, vec],
        out_specs=row(),
        out_shape=jax.ShapeDtypeStruct((n, d), F32),
        scratch_shapes=[pltpu.VMEM((tm, d), F32)],
        compiler_params=_params(("parallel", "arbitrary")),
    )(xh, x2d, w1, w3, w2, g.reshape(1, d), b.reshape(1, d))


def _add_ln_kernel(x_ref, f_ref, g_ref, b_ref, o_ref):
    o_ref[...] = _deepnorm(x_ref[...], f_ref[...], g_ref[...], b_ref[...])


def _add_ln(x2d, f2d, g, b):
    n, d = x2d.shape
    tm = PROJ_ROWS
    row = pl.BlockSpec((tm, d), lambda i: (i, 0))
    vec = pl.BlockSpec((1, d), lambda i: (0, 0))
    return pl.pallas_call(
        _add_ln_kernel,
        grid=(n // tm,),
        in_specs=[row, row, vec, vec],
        out_specs=row,
        out_shape=jax.ShapeDtypeStruct((n, d), F32),
        compiler_params=_params(("parallel",)),
    )(x2d, f2d, g.reshape(1, d), b.reshape(1, d))


def _router_kernel(xh_ref, rw_ref, gate_ref, lposc_ref, lpost_ref, cum_ref, carry_ref, *, tiles_per_group):
    tm = xh_ref.shape[0]
    i = pl.program_id(0)

    @pl.when(i % tiles_per_group == 0)
    def _():
        carry_ref[...] = jnp.zeros_like(carry_ref)

    lane = lax.broadcasted_iota(I32, (1, LANES), 1)
    logits = jnp.where(lane < N_EXPERTS, _dot(xh_ref[...], rw_ref[...]), NEG_INF)
    m1 = jnp.max(logits, axis=1, keepdims=True)
    i1 = jnp.min(jnp.where(logits == m1, lane, LANES), axis=1, keepdims=True)
    rest = jnp.where(lane == i1, NEG_INF, logits)
    m2 = jnp.max(rest, axis=1, keepdims=True)
    i2 = jnp.min(jnp.where(rest == m2, lane, LANES), axis=1, keepdims=True)
    e2 = jnp.exp(m2 - m1)
    g1 = 1.0 / (1.0 + e2)
    g2 = e2 / (1.0 + e2)
    sel1 = lane == i1
    sel2 = lane == i2
    routed = sel1 | sel2
    gate_ref[...] = jnp.where(sel1, g1, jnp.where(sel2, g2, 0.0))
    routedf = jnp.where(routed, 1.0, 0.0)
    earlier = (lax.broadcasted_iota(I32, (tm, tm), 1) < lax.broadcasted_iota(I32, (tm, tm), 0))
    earlier = jnp.where(earlier, 1.0, 0.0).astype(BF16)
    carry = carry_ref[...]
    lpos = jnp.where(routed, _dot(earlier, routedf.astype(BF16)) + carry, -1.0)
    lposc_ref[...] = lpos
    lpost_ref[...] = jnp.transpose(lpos)[0:N_EXPERTS, :]
    carry = carry + jnp.sum(routedf, axis=0, keepdims=True)
    carry_ref[...] = carry
    cum_ref[...] = carry.astype(I32).reshape(1, 1, LANES)


def _router(xh, rw):
    n, d = xh.shape
    tm = ROUTER_ROWS
    nt = n // tm
    row = pl.BlockSpec((tm, LANES), lambda i: (i, 0))
    return pl.pallas_call(
        functools.partial(_router_kernel, tiles_per_group=MOE_GROUP // tm),
        grid=(nt,),
        in_specs=[pl.BlockSpec((tm, d), lambda i: (i, 0)), pl.BlockSpec(rw.shape, lambda i: (0, 0))],
        out_specs=[row, row,
                   pl.BlockSpec((N_EXPERTS, tm), lambda i: (0, i)),
                   pl.BlockSpec((1, 1, LANES), lambda i: (i, 0, 0))],
        out_shape=[jax.ShapeDtypeStruct((n, LANES), F32), jax.ShapeDtypeStruct((n, LANES), F32),
                   jax.ShapeDtypeStruct((N_EXPERTS, n), F32), jax.ShapeDtypeStruct((nt, 1, LANES), I32)],
        scratch_shapes=[pltpu.VMEM((1, LANES), F32)],
        compiler_params=_params(("arbitrary",)),
    )(xh, rw)


def _moe_kernel(bnd_ref, xh_ref, lpost_ref, lposc_ref, gatec_ref, w1_ref, w3_ref, w2_ref,
                o_ref, xg_ref, yacc_ref):
    tg = xh_ref.shape[0]
    win, ch, fr, align = MOE_WIN, ROUTER_ROWS, MOE_FFN_ROWS, MOE_ROW_ALIGN
    nch = tg // ch
    g, e, f = pl.program_id(0), pl.program_id(1), pl.program_id(2)
    nf = pl.num_programs(2)

    def bound(c):
        return bnd_ref[(g * (nch + 1) + c) * N_EXPERTS + e]

    cnt = bound(nch)
    n_ffn = (cnt + fr - 1) // fr

    def tok_chunk(c):
        return pl.ds(pl.multiple_of(c * ch, ch), ch)

    def windows(c):
        r0, r1 = bound(c), bound(c + 1)
        first = (r0 // align) * align
        return first, jnp.where(r1 > r0, (r1 - first + win - 1) // win, 0)

    @pl.when((e == 0) & (f == 0))
    def _():
        o_ref[...] = jnp.zeros_like(o_ref)

    @pl.when(f == 0)
    def _compact():
        def clear(s, carry):
            rows = pl.ds(pl.multiple_of(s * win, win), win)
            xg_ref[rows, :] = jnp.zeros((win, xg_ref.shape[1]), xg_ref.dtype)
            yacc_ref[rows, :] = jnp.zeros((win, yacc_ref.shape[1]), F32)
            return carry

        lax.fori_loop(0, (cnt + max(fr, win) + win - 1) // win, clear, 0)

        def per_chunk(c, carry):
            first, n_win = windows(c)
            lp = lpost_ref[pl.ds(e, 1), tok_chunk(c)]

            def per_window(w, carry2):
                start = pl.multiple_of(first + w * win, align)
                want = (lax.broadcasted_iota(I32, (win, 1), 0) + start).astype(F32)
                pick = jnp.where(lp == want, 1.0, 0.0).astype(BF16)
                rows = pl.ds(start, win)
                xg_ref[rows, :] += _dot(pick, xh_ref[tok_chunk(c), :]).astype(xg_ref.dtype)
                return carry2

            return lax.fori_loop(0, n_win, per_window, carry)

        lax.fori_loop(0, nch, per_chunk, 0)

    def ffn_rows(rows):
        xs = xg_ref[rows, :]
        hid = (_silu(_dot(xs, w1_ref[...])) * _dot(xs, w3_ref[...])).astype(BF16)
        yacc_ref[rows, :] += _dot(hid, w2_ref[...])

    def ffn_pair(p, carry):
        ffn_rows(pl.ds(pl.multiple_of(p * 2 * fr, align), 2 * fr))
        return carry

    lax.fori_loop(0, n_ffn // 2, ffn_pair, 0)

    @pl.when(n_ffn % 2 == 1)
    def _():
        ffn_rows(pl.ds(pl.multiple_of((n_ffn - 1) * fr, align), fr))

    @pl.when(f == nf - 1)
    def _scatter():
        lane = lax.broadcasted_iota(I32, (1, LANES), 1)

        def per_chunk(c, carry):
            first, n_win = windows(c)
            toks = tok_chunk(c)
            lp = jnp.sum(jnp.where(lane == e, lposc_ref[toks, :], 0.0), axis=1, keepdims=True)
            gt = jnp.sum(jnp.where(lane == e, gatec_ref[toks, :], 0.0), axis=1, keepdims=True)

            def per_window(w, carry2):
                start = pl.multiple_of(first + w * win, align)
                y = yacc_ref[pl.ds(start, win), :]
                y_hi = y.astype(BF16)
                y_lo = (y - y_hi.astype(F32)).astype(BF16)
                want = (lax.broadcasted_iota(I32, (1, win), 1) + start).astype(F32)
                place = jnp.where(lp == want, 1.0, 0.0).astype(BF16)
                o_ref[toks, :] += gt * (_dot(place, y_hi) + _dot(place, y_lo))
                return carry2

            return lax.fori_loop(0, n_win, per_window, carry)

        lax.fori_loop(0, nch, per_chunk, 0)


def _chunk_bounds(cum):
    nch = MOE_GROUP // ROUTER_ROWS
    per_group = cum[:, 0, :N_EXPERTS].reshape(-1, nch, N_EXPERTS)
    return jnp.pad(per_group, ((0, 0), (1, 0), (0, 0))).reshape(-1)


def _moe(bounds, xh, lpost, lposc, gatec, w1, w3, w2, layer):
    n, d = xh.shape
    tg = MOE_GROUP
    dff = w1.shape[3]
    fc = MOE_FF_CHUNK
    cap = -(-(tg + max(MOE_FFN_ROWS, MOE_WIN)) // MOE_WIN) * MOE_WIN
    grid_spec = pltpu.PrefetchScalarGridSpec(
        num_scalar_prefetch=1,
        grid=(n // tg, N_EXPERTS, dff // fc),
        in_specs=[pl.BlockSpec((tg, d), lambda g, e, f, c: (g, 0)),
                  pl.BlockSpec((N_EXPERTS, tg), lambda g, e, f, c: (0, g)),
                  pl.BlockSpec((tg, LANES), lambda g, e, f, c: (g, 0)),
                  pl.BlockSpec((tg, LANES), lambda g, e, f, c: (g, 0)),
                  pl.BlockSpec((None, None, d, fc), lambda g, e, f, c: (layer, e, 0, f)),
                  pl.BlockSpec((None, None, d, fc), lambda g, e, f, c: (layer, e, 0, f)),
                  pl.BlockSpec((None, None, fc, d), lambda g, e, f, c: (layer, e, f, 0))],
        out_specs=pl.BlockSpec((tg, d), lambda g, e, f, c: (g, 0)),
        scratch_shapes=[pltpu.VMEM((cap, d), BF16), pltpu.VMEM((cap, d), F32)],
    )
    return pl.pallas_call(
        _moe_kernel,
        grid_spec=grid_spec,
        out_shape=jax.ShapeDtypeStruct((n, d), F32),
        compiler_params=_params(("parallel", "arbitrary", "arbitrary")),
    )(bounds, xh, lpost, lposc, gatec, w1, w3, w2)


EVEN_KB_SECTION = 2


def _even_weight_layout(w_in):
    hd = HEAD_DIM
    widths = (A_HEADS * hd, hd, hd, IDX_HEADS * IDX_DIM, IDX_DIM, IDX_HEADS, B_HEADS * hd, B_HEADS * hd, B_HEADS * hd)
    offs = [0]
    for wd in widths:
        offs.append(offs[-1] + wd)
    qa, ka, va, iq, ik, iw, qb, kb, vb = (w_in[:, offs[j]:offs[j + 1]] for j in range(9))
    iw_pad = jnp.pad(iw, ((0, 0), (0, LANES - IDX_HEADS)))
    w = jnp.concatenate([qa, qb, kb, vb, ka, ka, va, va, iq, ik, ik, ik, ik, iw_pad], axis=1)
    mixw = B_HEADS * hd
    sections = ((0, A_HEADS * hd, hd, QK_SCALE),
                (512, mixw, hd, QK_SCALE),
                (1024, mixw, hd, None),
                (1536, mixw, None, None),
                (2048, LANES, hd, None),
                (2176, LANES, None, None),
                (2304, IDX_HEADS * IDX_DIM, IDX_DIM, None),
                (2560, LANES, IDX_DIM, None),
                (2688, LANES, None, None))
    dtypes = (BF16,) * 8 + (F32,)
    return w.astype(BF16), sections, dtypes


def kernel(x, even_w_in, even_w_out, even_ln1_g, even_ln1_b, even_w1, even_w3, even_w2, even_ln2_g, even_ln2_b, odd_w_in, odd_w_out, odd_ln1_g, odd_ln1_b, odd_router, odd_w1, odd_w3, odd_w2, odd_ln2_g, odd_ln2_b):
    batch, seq, d = x.shape
    n = batch * seq
    tables = _rope_tables(seq)
    x2d = x.reshape(n, d)
    odd_w1h, odd_w3h, odd_w2h = odd_w1.astype(BF16), odd_w3.astype(BF16), odd_w2.astype(BF16)
    for layer in range(DEPTH):
        i = layer // 2
        if layer % 2 == 0:
            w, sections, dtypes = _even_weight_layout(even_w_in[i])
            qa, qb, kb, vb, kk, vv, iq, ik4, iw, kb_mean = _project(x2d, w, tables, sections, dtypes, seq,
                                                                    mean_section=EVEN_KB_SECTION)
            o_a = _dsa_attention(iq, iw, ik4, qa, kk, vv, batch, seq)
            o_b = _moba_attention(qb, kb, vb, kb_mean, batch, seq)
            x2d, xh = _outproj_ln([o_a, o_b], even_w_out[i].astype(BF16), x2d, even_ln1_g[i], even_ln1_b[i])
            x2d = _swiglu_ln(xh, x2d, even_w1[i].astype(BF16), even_w3[i].astype(BF16), even_w2[i].astype(BF16),
                             even_ln2_g[i], even_ln2_b[i])
        else:
            mix = C_HEADS * HEAD_DIM
            sections = ((0, mix, HEAD_DIM, QK_SCALE), (mix, mix, HEAD_DIM, None), (2 * mix, mix, None, None))
            q, k, v = _project(x2d, odd_w_in[i].astype(BF16), tables, sections, (F32, F32, F32), seq)
            o = _dilated_attention(q, k, v, batch, seq)
            x2d, xh = _outproj_ln([o], odd_w_out[i].astype(BF16), x2d, odd_ln1_g[i], odd_ln1_b[i])
            rw = jnp.pad(odd_router[i], ((0, 0), (0, LANES - N_EXPERTS))).astype(BF16)
            gatec, lposc, lpost, cum = _router(xh, rw)
            ffn = _moe(_chunk_bounds(cum), xh, lpost, lposc, gatec, odd_w1h, odd_w3h, odd_w2h, i)
            x2d = _add_ln(x2d, ffn, odd_ln2_g[i], odd_ln2_b[i])
    return x2d.reshape(batch, seq, d)
```

```python
import functools

import jax
import jax.numpy as jnp
from jax import lax
from jax.experimental import pallas as pl
from jax.experimental.pallas import tpu as pltpu

F32 = jnp.float32
BF16 = jnp.bfloat16
I32 = jnp.int32

DEPTH = 4
HEAD_DIM = 64
ROPE_THETA = 10000.0
LN_EPS = 1e-5
A_HEADS = 8
IDX_HEADS = 8
IDX_DIM = 32
DSA_TOPK = 256
B_HEADS = 8
MOBA_BLOCK = 256
MOBA_TOPK = 3
C_HEADS = 16
DILATED_CFG = ((128, 1), (512, 4), (2048, 16))
BAND_BLOCK = 128
N_EXPERTS = 8
DEEPNORM_ALPHA = (2 * DEPTH) ** 0.25
QK_SCALE = HEAD_DIM ** -0.5 * 1.4426950408889634

LANES = 128
VMEM_LIMIT_BYTES = 52 * 1024 * 1024

PROJ_ROWS = 512
FFN_ROWS = 512
FFN_CHUNKS = 1
ROUTER_ROWS = 512
MOE_GROUP = 2048
MOE_WIN = 256
MOE_FFN_ROWS = 272
MOE_ROW_ALIGN = 16
MOE_FF_CHUNK = 512
DSA_ROWS = 256
DSA_SEARCH_GROUPS = 4
DSA_SEARCH_UNROLL = 4
MOBA_GATE_ROWS = 16
DILATED_UNITS = 8
DILATED_SLABS = 4
ATTN_KEY_STEP = 512
MOBA_STEP_BLOCKS = 8

NEG_INF = float("-inf")
NEG_INF_KEY = -2139095041


def _params(semantics):
    return pltpu.CompilerParams(dimension_semantics=semantics, vmem_limit_bytes=VMEM_LIMIT_BYTES)


def _dot(a, b):
    return jnp.dot(a, b, preferred_element_type=F32)


def _dot_nt(a, b):
    return lax.dot_general(a, b, (((1,), (1,)), ((), ())), preferred_element_type=F32)


def _lane_group(shape, group, idx):
    lane = lax.broadcasted_iota(I32, shape, len(shape) - 1)
    return (lane // group) == idx


def _head_pair_attention(units):
    first = _lane_group((1, LANES), HEAD_DIM, 0)
    raw, unit_masks = [], []
    for q2, k2, _, mask in units:
        raw.append(_dot_nt(jnp.where(first, q2, jnp.zeros_like(q2)), k2))
        unit_masks.append(mask() if callable(mask) else mask)
        raw.append(_dot_nt(jnp.where(first, jnp.zeros_like(q2), q2), k2))
    scores = []
    for u, mask in enumerate(unit_masks):
        masks = mask if isinstance(mask, (tuple, list)) else (mask, mask)
        for h in range(2):
            if masks[h].dtype == jnp.bool_:
                scores.append(jnp.where(masks[h], raw[2 * u + h], NEG_INF))
            else:
                scores.append(raw[2 * u + h] + masks[h])
    maxes = [jnp.max(s, axis=1, keepdims=True) for s in scores]
    weights = [jnp.exp2(s - mx).astype(BF16) for s, mx in zip(scores, maxes)]
    results = []
    for u, (_, _, v2, _) in enumerate(units):
        ones = jnp.ones_like(v2)
        pv0 = _dot(weights[2 * u], jnp.where(first, v2, ones))
        pv1 = _dot(weights[2 * u + 1], jnp.where(first, ones, v2))
        den = jnp.where(first, pv0[:, LANES - 1:LANES], pv1[:, 0:1])
        results.append((jnp.where(first, maxes[2 * u], maxes[2 * u + 1]), den, jnp.where(first, pv0, pv1)))
    return results


def _stacked_head_pair_attention(units):
    first = _lane_group((1, LANES), HEAD_DIM, 0)
    raw = [_dot_nt(jnp.concatenate([jnp.where(first, q2, jnp.zeros_like(q2)),
                                    jnp.where(first, jnp.zeros_like(q2), q2)], axis=0), k2)
           for q2, k2, _, _ in units]
    stats, weights = [], []
    for r, (q2, _, _, mask) in zip(raw, units):
        rows = q2.shape[0]
        halves = [jnp.where(mask, r[h * rows:(h + 1) * rows], NEG_INF) for h in range(2)]
        maxes = [jnp.max(s, axis=1, keepdims=True) for s in halves]
        exps = [jnp.exp2(s - mx) for s, mx in zip(halves, maxes)]
        dens = [jnp.sum(e, axis=1, keepdims=True) for e in exps]
        stats.append((jnp.where(first, maxes[0], maxes[1]), jnp.where(first, dens[0], dens[1])))
        weights.append(jnp.concatenate(exps, axis=0).astype(BF16))
    results = []
    for (mx, den), w, (q2, _, v2, _) in zip(stats, weights, units):
        rows = q2.shape[0]
        pv = _dot(w, v2)
        results.append((mx, den, jnp.where(first, pv[:rows], pv[rows:])))
    return results


def _rope_lanes(t, cosf, sinf, half):
    lane = lax.broadcasted_iota(I32, t.shape, 1)
    first = (lane % (2 * half)) < half
    swapped = jnp.where(first, pltpu.roll(t, LANES - half, 1), pltpu.roll(t, half, 1))
    return t * cosf + swapped * sinf


def _proj_kernel(x_ref, w_ref, c64_ref, s64_ref, c32_ref, s32_ref, *out_refs, sections, mean_section):
    x = x_ref[...].astype(BF16)
    for idx, (o_ref, (start, width, rope, scale)) in enumerate(zip(out_refs, sections)):
        h = _dot(x, w_ref[:, start:start + width])
        if rope is not None:
            cosf = (c64_ref if rope == HEAD_DIM else c32_ref)[...]
            sinf = (s64_ref if rope == HEAD_DIM else s32_ref)[...]
            parts = [_rope_lanes(h[:, g * LANES:(g + 1) * LANES], cosf, sinf, rope // 2)
                     for g in range(width // LANES)]
            h = parts[0] if len(parts) == 1 else jnp.concatenate(parts, axis=1)
        if scale is not None:
            h = h * scale
        o_ref[...] = h.astype(o_ref.dtype)
        if idx == mean_section:
            m_ref = out_refs[len(sections)]
            for blk in range(h.shape[0] // MOBA_BLOCK):
                rows = h[blk * MOBA_BLOCK:(blk + 1) * MOBA_BLOCK]
                m_ref[0, blk:blk + 1, :] = jnp.sum(rows, axis=0, keepdims=True) * (1.0 / MOBA_BLOCK)


def _project(x2d, w, tables, sections, out_dtypes, seq, mean_section=None):
    n, d = x2d.shape
    tm = PROJ_ROWS
    pos_blocks = seq // tm
    tab_spec = pl.BlockSpec((tm, LANES), lambda i: (i % pos_blocks, 0))
    out_specs = [pl.BlockSpec((tm, sec[1]), lambda i: (i, 0)) for sec in sections]
    out_shape = [jax.ShapeDtypeStruct((n, sec[1]), dt) for sec, dt in zip(sections, out_dtypes)]
    if mean_section is not None:
        per_tile, width = tm // MOBA_BLOCK, sections[mean_section][1]
        out_specs.append(pl.BlockSpec((1, per_tile, width), lambda i: (i, 0, 0)))
        out_shape.append(jax.ShapeDtypeStruct((n // tm, per_tile, width), F32))
    outs = pl.pallas_call(
        functools.partial(_proj_kernel, sections=sections, mean_section=mean_section),
        grid=(n // tm,),
        in_specs=[pl.BlockSpec((tm, d), lambda i: (i, 0)),
                  pl.BlockSpec(w.shape, lambda i: (0, 0)),
                  tab_spec, tab_spec, tab_spec, tab_spec],
        out_specs=out_specs,
        out_shape=out_shape,
        compiler_params=_params(("parallel",)),
    )(x2d, w, *tables)
    outs = list(outs)
    if mean_section is not None:
        outs[-1] = outs[-1].reshape(n // MOBA_BLOCK, -1)
    return outs


def _rope_tables(seq):
    out = []
    for dim in (HEAD_DIM, IDX_DIM):
        inv = ROPE_THETA ** (-jnp.arange(0, dim, 2, dtype=F32) / dim)
        ang = jnp.arange(seq, dtype=F32)[:, None] * inv[None, :]
        cos, sin = jnp.cos(ang), jnp.sin(ang)
        reps = LANES // dim
        out.append(jnp.tile(jnp.concatenate([cos, cos], -1), (1, reps)))
        out.append(jnp.tile(jnp.concatenate([-sin, sin], -1), (1, reps)))
    return out


def _dsa_block(i, iq_ref, iw_ref, ik_ref, q_ref, kk_ref, vv_ref, o_ref, key_ref, sel_ref, *, kext, topk):
    qb = DSA_ROWS
    keys = slice(0, kext)
    qpos = i * qb + lax.broadcasted_iota(I32, (qb, 1), 0)
    kpos = lax.broadcasted_iota(I32, (1, kext), 1)
    causal = kpos <= qpos

    ik = ik_ref[keys, :]
    iw = iw_ref[...] * (IDX_DIM ** -0.5 * IDX_HEADS ** -0.5)
    heads_per_group = LANES // IDX_DIM
    score = jnp.zeros((qb, kext), F32)
    for h in range(IDX_HEADS):
        grp, j = divmod(h, heads_per_group)
        iqg = iq_ref[:, grp * LANES:(grp + 1) * LANES]
        iqm = jnp.where(_lane_group((1, LANES), IDX_DIM, j), iqg, jnp.zeros_like(iqg))
        rel = jnp.maximum(_dot_nt(iqm, ik), 0.0)
        score = score + rel * iw[:, h:h + 1]
    bits = pltpu.bitcast(score, I32)
    neg = bits >> 31
    int_min = jnp.int32(-2 ** 31)
    key_ref[:, keys] = jnp.where(causal, (bits ^ (neg & 0x7FFFFFFF)) - neg, int_min)

    grp_rows = qb // DSA_SEARCH_GROUPS
    groups = [slice(g * grp_rows, (g + 1) * grp_rows) for g in range(DSA_SEARCH_GROUPS)]

    def count_ge(rows, t):
        return jnp.sum(jnp.where(key_ref[rows, keys] >= t, 1.0, 0.0), axis=1, keepdims=True)

    def search(it, thrs):
        bit = jnp.left_shift(jnp.int32(1), 30 - it)
        return tuple(jnp.where(count_ge(rows, thr | bit) >= topk, thr | bit, thr)
                     for rows, thr in zip(groups, thrs))

    zero = jnp.zeros((grp_rows, 1), I32)
    thrs = tuple(jnp.where(count_ge(rows, zero) >= topk, zero, int_min) for rows in groups)
    thrs = lax.fori_loop(0, 31, search, thrs, unroll=DSA_SEARCH_UNROLL)
    thr = jnp.concatenate(thrs, axis=0)

    floor = jnp.maximum(thr, int_min + 1)
    key = key_ref[:, keys]
    sel_ref[:, keys] = jnp.where(key >= floor, 0.0, NEG_INF)
    n_ge = jnp.sum(jnp.where(key >= thr, 1.0, 0.0), axis=1, keepdims=True)
    tie_rows = (n_ge > topk) & (thr > int_min)

    @pl.when(jnp.max(tie_rows.astype(I32)) > 0)
    def _break_ties():
        upper = (lax.broadcasted_iota(I32, (LANES, LANES), 0)
                 < lax.broadcasted_iota(I32, (LANES, LANES), 1))
        upper = jnp.where(upper, 1.0, 0.0).astype(BF16)
        n_gt = jnp.sum(jnp.where(key_ref[:, keys] > thr, 1.0, 0.0), axis=1, keepdims=True)
        need = topk - n_gt
        before = jnp.zeros((qb, 1), F32)
        for c in range(kext // LANES):
            sl = slice(c * LANES, (c + 1) * LANES)
            keyc = key_ref[:, sl]
            eqc = keyc == thr
            eqf = jnp.where(eqc, 1.0, 0.0)
            rank = _dot(eqf.astype(BF16), upper) + before
            take = ((keyc > thr) | (eqc & (rank < need))) & (keyc > int_min)
            sel_ref[:, sl] = jnp.where(take, 0.0, NEG_INF)
            before = before + jnp.sum(eqf, axis=1, keepdims=True)

    selected = sel_ref[:, keys]
    kk = kk_ref[keys, :]
    vv = vv_ref[keys, :]
    pairs = [(q_ref[:, hp * LANES:(hp + 1) * LANES], kk, vv, selected) for hp in range(A_HEADS // 2)]
    for hp, (_, den, num) in enumerate(_head_pair_attention(pairs)):
        o_ref[:, hp * LANES:(hp + 1) * LANES] = (num / den).astype(o_ref.dtype)


def _dsa_kernel(*refs, seq, topk):
    i = pl.program_id(1)
    blocks_per_step = ATTN_KEY_STEP // DSA_ROWS
    for v in range(seq // ATTN_KEY_STEP):
        @pl.when(i // blocks_per_step == v)
        def _(v=v):
            _dsa_block(i, *refs, kext=(v + 1) * ATTN_KEY_STEP, topk=topk)


def _dsa_attention(iq, iw, ik4, qa, kk, vv, batch, seq):
    n = batch * seq
    qb = DSA_ROWS
    nq = seq // qb
    topk = min(DSA_TOPK, seq // 4)
    rows = lambda width: pl.BlockSpec((qb, width), lambda b, i: (b * nq + i, 0))
    whole = pl.BlockSpec((seq, LANES), lambda b, i: (b, 0))
    return pl.pallas_call(
        functools.partial(_dsa_kernel, seq=seq, topk=topk),
        grid=(batch, nq),
        in_specs=[rows(iq.shape[1]), rows(LANES), whole, rows(qa.shape[1]), whole, whole],
        out_specs=rows(qa.shape[1]),
        out_shape=jax.ShapeDtypeStruct((n, qa.shape[1]), BF16),
        scratch_shapes=[pltpu.VMEM((qb, seq), I32), pltpu.VMEM((qb, seq), F32)],
        compiler_params=_params(("parallel", "parallel")),
    )(iq, iw, ik4, qa, kk, vv)


def _moba_step(step, q_ref, k_ref, v_ref, kmean_ref, o_ref, *, kext, kt):
    bs, per = MOBA_BLOCK, MOBA_STEP_BLOCKS
    nb_all = kmean_ref.shape[0]
    nbp = max(nb_all, MOBA_GATE_ROWS)
    kmean = jnp.concatenate([kmean_ref[...], jnp.zeros((nbp - nb_all, LANES), F32)], axis=0).astype(BF16)
    n_iota = lax.broadcasted_iota(I32, (nbp, 1), 0)
    qrow = lax.broadcasted_iota(I32, (bs, 1), 0)
    units = []
    for j in range(per):
        qi = step * per + j
        ext = kext - (per - 1 - j) * bs
        nb = ext // bs
        q2 = q_ref[j * bs:(j + 1) * bs, :]
        blk_row = lax.broadcasted_iota(I32, (nbp, ext), 0)
        blk_of_key = lax.broadcasted_iota(I32, (nbp, ext), 1) // bs
        member = jnp.where(blk_row == blk_of_key, 1.0, 0.0).astype(BF16)
        past = n_iota < qi
        kcol = lax.broadcasted_iota(I32, (1, ext), 1)
        own = (kcol >= qi * bs) & (kcol <= qi * bs + qrow)
        gates = [jnp.where(past, _dot_nt(kmean, jnp.where(_lane_group((1, LANES), HEAD_DIM, half), q2,
                                                          jnp.zeros_like(q2))), NEG_INF)
                 for half in range(2)]

        def block_masks(gates=gates, past=past, own=own, member=member, nb=nb):
            chosen = []
            for gate in gates:
                rank = jnp.zeros((nbp, bs), I32)
                for m in range(nb):
                    gm = gate[m:m + 1, :]
                    beats = (gm > gate) | ((gm == gate) & (m < n_iota))
                    rank = rank + beats.astype(I32)
                chosen.append(jnp.transpose(jnp.where(past & (rank < kt), 1.0, 0.0)).astype(BF16))
            return [(_dot(c, member) > 0.5) | own for c in chosen]

        units.append((q2, k_ref[0:ext, :], v_ref[0:ext, :], block_masks))
    for j, (_, den, num) in enumerate(_head_pair_attention(units)):
        o_ref[j * bs:(j + 1) * bs, :] = (num / den).astype(o_ref.dtype)


def _moba_kernel(*refs, seq):
    step = pl.program_id(2)
    kt = min(MOBA_TOPK, seq // MOBA_BLOCK - 1)
    rows = MOBA_BLOCK * MOBA_STEP_BLOCKS
    for v in range(seq // rows):
        @pl.when(step == v)
        def _(v=v):
            _moba_step(step, *refs, kext=(v + 1) * rows, kt=kt)


def _moba_attention(q, k, v, kmean, batch, seq):
    n, width = q.shape
    tq = MOBA_BLOCK * MOBA_STEP_BLOCKS
    steps = seq // tq
    rows = pl.BlockSpec((tq, LANES), lambda b, hp, st: (b * steps + st, hp))
    whole = pl.BlockSpec((seq, LANES), lambda b, hp, st: (b, hp))
    means = pl.BlockSpec((seq // MOBA_BLOCK, LANES), lambda b, hp, st: (b, hp))
    return pl.pallas_call(
        functools.partial(_moba_kernel, seq=seq),
        grid=(batch, width // LANES, steps),
        in_specs=[rows, whole, whole, means],
        out_specs=rows,
        out_shape=jax.ShapeDtypeStruct((n, width), BF16),
        compiler_params=_params(("parallel", "parallel", "parallel")),
    )(q, k, v, kmean)


def _rows(start, size, stride):
    return pl.ds(start, size) if stride == 1 else pl.ds(start, size, stride=stride)


def _largest_divisor(n, limit):
    return max(d for d in range(1, limit + 1) if n % d == 0)


def _repeat(trips, body):
    if trips == 1:
        body(0, 0)
    else:
        lax.fori_loop(0, trips, body, 0)


def _dilated_kernel(q_ref, k_ref, v_ref, o_ref, qp_ref, kp_ref, vp_ref, nat_ref, slab_ref, *, seq):
    blk = BAND_BLOCK
    ns = DILATED_SLABS
    slab_len = seq // ns

    for s in range(ns):
        rows = slice(s * slab_len, (s + 1) * slab_len)
        qp_ref[rows, :] = q_ref[pl.ds(s, slab_len, stride=ns), :]
        kp_ref[rows, :] = k_ref[pl.ds(s, slab_len, stride=ns), :]
        vp_ref[rows, :] = v_ref[pl.ds(s, slab_len, stride=ns), :]

    def operands(srcs, qstart, kstart, nk, stride):
        ksl = _rows(kstart, nk, stride)
        i = lax.broadcasted_iota(I32, (blk, 1), 0)
        j = lax.broadcasted_iota(I32, (1, nk), 1)
        if nk == blk:
            mask = j <= i
        else:
            mask = (j >= i) & (j <= i + blk)
        return (srcs[0][_rows(qstart, blk, stride), :].astype(BF16), srcs[1][ksl, :].astype(BF16),
                srcs[2][ksl, :].astype(BF16), mask)

    def merged(old, new):
        (m_old, l_old, a_old), (m_new, l_new, a_new) = old, new
        m_tot = jnp.maximum(m_old, m_new)
        w_old = jnp.exp2(m_old - m_tot)
        w_new = jnp.exp2(m_new - m_tot)
        return m_tot, w_old * l_old + w_new * l_new, w_old * a_old + w_new * a_new

    def run_units(srcs, state_ref, units, nk, stride, merge):
        results = _stacked_head_pair_attention([operands(srcs, qstart, kstart, nk, stride)
                                                for qstart, kstart in units])
        for (qstart, _), new in zip(units, results):
            qsl = _rows(qstart, blk, stride)
            if merge:
                new = merged(tuple(state_ref[t, qsl, :] for t in range(3)), new)
            for t in range(3):
                state_ref[t, qsl, :] = new[t]

    written = set()
    for window, dil in DILATED_CFG:
        assert window // dil == blk
        if dil % ns == 0:
            srcs, state_ref, stride, group_len = (qp_ref, kp_ref, vp_ref), slab_ref, dil // ns, slab_len
        else:
            srcs, state_ref, stride, group_len = (q_ref, k_ref, v_ref), nat_ref, dil, seq
        merge = id(state_ref) in written
        written.add(id(state_ref))
        nblk = seq // dil // blk
        ub = _largest_divisor(dil, DILATED_UNITS)

        def class_start(c, stride=stride, group_len=group_len):
            return (c // stride) * group_len + c % stride if group_len != seq else c

        def first_blocks(it, carry, srcs=srcs, state_ref=state_ref, stride=stride, merge=merge, ub=ub,
                         class_start=class_start):
            starts = [class_start(it * ub + u) for u in range(ub)]
            run_units(srcs, state_ref, [(st, st) for st in starts], blk, stride, merge)
            return carry

        _repeat(dil // ub, first_blocks)
        if nblk > 1:
            later = nblk - 1
            ul = _largest_divisor(dil * later, DILATED_UNITS)

            def later_blocks(it, carry, srcs=srcs, state_ref=state_ref, stride=stride, merge=merge, ul=ul,
                             later=later, class_start=class_start):
                flat = [it * ul + u for u in range(ul)]
                qstarts = [class_start(t // later) + stride * blk * (1 + t % later) for t in flat]
                run_units(srcs, state_ref, [(qs, qs - stride * blk) for qs in qstarts], 2 * blk, stride, merge)
                return carry

            _repeat(dil * later // ul, later_blocks)

    for s in range(ns):
        nat_rows = pl.ds(s, slab_len, stride=ns)
        slab_rows = slice(s * slab_len, (s + 1) * slab_len)
        _, den, num = merged(tuple(nat_ref[t, nat_rows, :] for t in range(3)),
                             tuple(slab_ref[t, slab_rows, :] for t in range(3)))
        nat_ref[2, nat_rows, :] = num / den
    o_ref[...] = nat_ref[2].astype(o_ref.dtype)


def _dilated_attention(q, k, v, batch, seq):
    n, width = q.shape
    for window, dil in DILATED_CFG:
        assert seq % (dil * BAND_BLOCK) == 0
    assert {dil % DILATED_SLABS == 0 for _, dil in DILATED_CFG} == {True, False}
    whole = pl.BlockSpec((seq, LANES), lambda b, hp: (b, hp))
    return pl.pallas_call(
        functools.partial(_dilated_kernel, seq=seq),
        grid=(batch, width // LANES),
        in_specs=[whole, whole, whole],
        out_specs=whole,
        out_shape=jax.ShapeDtypeStruct((n, width), BF16),
        scratch_shapes=[pltpu.VMEM((seq, LANES), F32)] * 3 + [pltpu.VMEM((3, seq, LANES), F32)] * 2,
        compiler_params=_params(("parallel", "parallel")),
    )(q, k, v)


def _deepnorm(x, f, g, b):
    y = DEEPNORM_ALPHA * x + f
    mu = jnp.mean(y, axis=1, keepdims=True)
    yc = y - mu
    var = jnp.mean(yc * yc, axis=1, keepdims=True)
    return yc * lax.rsqrt(var + LN_EPS) * g + b


def _outproj_ln_kernel(*refs, n_parts):
    part_refs = refs[:n_parts]
    w_ref, x_ref, g_ref, b_ref, o_ref, oh_ref = refs[n_parts:]
    mix = None
    start = 0
    for p_ref in part_refs:
        width = p_ref.shape[1]
        t = _dot(p_ref[...], w_ref[start:start + width, :])
        mix = t if mix is None else mix + t
        start += width
    y = _deepnorm(x_ref[...], mix, g_ref[...], b_ref[...])
    o_ref[...] = y
    oh_ref[...] = y.astype(BF16)


def _outproj_ln(parts, w, x2d, g, b):
    n, d = x2d.shape
    tm = PROJ_ROWS
    row = lambda width: pl.BlockSpec((tm, width), lambda i: (i, 0))
    vec = pl.BlockSpec((1, d), lambda i: (0, 0))
    return pl.pallas_call(
        functools.partial(_outproj_ln_kernel, n_parts=len(parts)),
        grid=(n // tm,),
        in_specs=[row(p.shape[1]) for p in parts] + [pl.BlockSpec(w.shape, lambda i: (0, 0)), row(d), vec, vec],
        out_specs=[row(d), row(d)],
        out_shape=[jax.ShapeDtypeStruct((n, d), F32), jax.ShapeDtypeStruct((n, d), BF16)],
        compiler_params=_params(("parallel",)),
    )(*parts, w, x2d, g.reshape(1, d), b.reshape(1, d))


def _silu(h):
    return h * (1.0 / (1.0 + jnp.exp(-h)))


def _swiglu_ln_kernel(xh_ref, x_ref, w1_ref, w3_ref, w2_ref, g_ref, b_ref, o_ref, acc_ref):
    f = pl.program_id(1)
    xh = xh_ref[...]
    hid = (_silu(_dot(xh, w1_ref[...])) * _dot(xh, w3_ref[...])).astype(BF16)
    y = _dot(hid, w2_ref[...])

    @pl.when(f == 0)
    def _():
        acc_ref[...] = y

    @pl.when(f > 0)
    def _():
        acc_ref[...] += y

    @pl.when(f == pl.num_programs(1) - 1)
    def _():
        o_ref[...] = _deepnorm(x_ref[...], acc_ref[...], g_ref[...], b_ref[...])


def _swiglu_ln(xh, x2d, w1, w3, w2, g, b):
    n, d = x2d.shape
    dff = w1.shape[1]
    tm = FFN_ROWS
    nf = FFN_CHUNKS
    fc = dff // nf
    row = lambda: pl.BlockSpec((tm, d), lambda i, f: (i, 0))
    vec = pl.BlockSpec((1, d), lambda i, f: (0, 0))
    mode = pl.Buffered(1) if nf == 1 else None
    return pl.pallas_call(
        _swiglu_ln_kernel,
        grid=(n // tm, nf),
        in_specs=[row(), row(),
                  pl.BlockSpec((d, fc), lambda i, f: (0, f), pipeline_mode=mode),
                  pl.BlockSpec((d, fc), lambda i, f: (0, f), pipeline_mode=mode),
                  pl.BlockSpec((fc, d), lambda i, f: (f, 0), pipeline_mode=mode),
                  vec, vec],
        out_specs=row(),
        out_shape=jax.ShapeDtypeStruct((n, d), F32),
        scratch_shapes=[pltpu.VMEM((tm, d), F32)],
        compiler_params=_params(("parallel", "arbitrary")),
    )(xh, x2d, w1, w3, w2, g.reshape(1, d), b.reshape(1, d))


def _add_ln_kernel(x_ref, f_ref, g_ref, b_ref, o_ref):
    o_ref[...] = _deepnorm(x_ref[...], f_ref[...], g_ref[...], b_ref[...])


def _add_ln(x2d, f2d, g, b):
    n, d = x2d.shape
    tm = PROJ_ROWS
    row = pl.BlockSpec((tm, d), lambda i: (i, 0))
    vec = pl.BlockSpec((1, d), lambda i: (0, 0))
    return pl.pallas_call(
        _add_ln_kernel,
        grid=(n // tm,),
        in_specs=[row, row, vec, vec],
        out_specs=row,
        out_shape=jax.ShapeDtypeStruct((n, d), F32),
        compiler_params=_params(("parallel",)),
    )(x2d, f2d, g.reshape(1, d), b.reshape(1, d))


def _router_kernel(xh_ref, rw_ref, gate_ref, lposc_ref, lpost_ref, cum_ref, carry_ref, *, tiles_per_group):
    tm = xh_ref.shape[0]
    i = pl.program_id(0)

    @pl.when(i % tiles_per_group == 0)
    def _():
        carry_ref[...] = jnp.zeros_like(carry_ref)

    lane = lax.broadcasted_iota(I32, (1, LANES), 1)
    logits = jnp.where(lane < N_EXPERTS, _dot(xh_ref[...], rw_ref[...]), NEG_INF)
    m1 = jnp.max(logits, axis=1, keepdims=True)
    i1 = jnp.min(jnp.where(logits == m1, lane, LANES), axis=1, keepdims=True)
    rest = jnp.where(lane == i1, NEG_INF, logits)
    m2 = jnp.max(rest, axis=1, keepdims=True)
    i2 = jnp.min(jnp.where(rest == m2, lane, LANES), axis=1, keepdims=True)
    e2 = jnp.exp(m2 - m1)
    g1 = 1.0 / (1.0 + e2)
    g2 = e2 / (1.0 + e2)
    sel1 = lane == i1
    sel2 = lane == i2
    routed = sel1 | sel2
    gate_ref[...] = jnp.where(sel1, g1, jnp.where(sel2, g2, 0.0))
    routedf = jnp.where(routed, 1.0, 0.0)
    earlier = (lax.broadcasted_iota(I32, (tm, tm), 1) < lax.broadcasted_iota(I32, (tm, tm), 0))
    earlier = jnp.where(earlier, 1.0, 0.0).astype(BF16)
    carry = carry_ref[...]
    lpos = jnp.where(routed, _dot(earlier, routedf.astype(BF16)) + carry, -1.0)
    lposc_ref[...] = lpos
    lpost_ref[...] = jnp.transpose(lpos)[0:N_EXPERTS, :]
    carry = carry + jnp.sum(routedf, axis=0, keepdims=True)
    carry_ref[...] = carry
    cum_ref[...] = carry.astype(I32).reshape(1, 1, LANES)


def _router(xh, rw):
    n, d = xh.shape
    tm = ROUTER_ROWS
    nt = n // tm
    row = pl.BlockSpec((tm, LANES), lambda i: (i, 0))
    return pl.pallas_call(
        functools.partial(_router_kernel, tiles_per_group=MOE_GROUP // tm),
        grid=(nt,),
        in_specs=[pl.BlockSpec((tm, d), lambda i: (i, 0)), pl.BlockSpec(rw.shape, lambda i: (0, 0))],
        out_specs=[row, row,
                   pl.BlockSpec((N_EXPERTS, tm), lambda i: (0, i)),
                   pl.BlockSpec((1, 1, LANES), lambda i: (i, 0, 0))],
        out_shape=[jax.ShapeDtypeStruct((n, LANES), F32), jax.ShapeDtypeStruct((n, LANES), F32),
                   jax.ShapeDtypeStruct((N_EXPERTS, n), F32), jax.ShapeDtypeStruct((nt, 1, LANES), I32)],
        scratch_shapes=[pltpu.VMEM((1, LANES), F32)],
        compiler_params=_params(("arbitrary",)),
    )(xh, rw)


def _moe_kernel(bnd_ref, xh_ref, lpost_ref, lposc_ref, gatec_ref, w1_ref, w3_ref, w2_ref,
                o_ref, xg_ref, yacc_ref):
    tg = xh_ref.shape[0]
    win, ch, fr, align = MOE_WIN, ROUTER_ROWS, MOE_FFN_ROWS, MOE_ROW_ALIGN
    nch = tg // ch
    g, e, f = pl.program_id(0), pl.program_id(1), pl.program_id(2)
    nf = pl.num_programs(2)

    def bound(c):
        return bnd_ref[(g * (nch + 1) + c) * N_EXPERTS + e]

    cnt = bound(nch)
    n_ffn = (cnt + fr - 1) // fr

    def tok_chunk(c):
        return pl.ds(pl.multiple_of(c * ch, ch), ch)

    def windows(c):
        r0, r1 = bound(c), bound(c + 1)
        first = (r0 // align) * align
        return first, jnp.where(r1 > r0, (r1 - first + win - 1) // win, 0)

    @pl.when((e == 0) & (f == 0))
    def _():
        o_ref[...] = jnp.zeros_like(o_ref)

    @pl.when(f == 0)
    def _compact():
        def clear(s, carry):
            rows = pl.ds(pl.multiple_of(s * win, win), win)
            xg_ref[rows, :] = jnp.zeros((win, xg_ref.shape[1]), xg_ref.dtype)
            yacc_ref[rows, :] = jnp.zeros((win, yacc_ref.shape[1]), F32)
            return carry

        lax.fori_loop(0, (cnt + max(fr, win) + win - 1) // win, clear, 0)

        def per_chunk(c, carry):
            first, n_win = windows(c)
            lp = lpost_ref[pl.ds(e, 1), tok_chunk(c)]

            def per_window(w, carry2):
                start = pl.multiple_of(first + w * win, align)
                want = (lax.broadcasted_iota(I32, (win, 1), 0) + start).astype(F32)
                pick = jnp.where(lp == want, 1.0, 0.0).astype(BF16)
                rows = pl.ds(start, win)
                xg_ref[rows, :] += _dot(pick, xh_ref[tok_chunk(c), :]).astype(xg_ref.dtype)
                return carry2

            return lax.fori_loop(0, n_win, per_window, carry)

        lax.fori_loop(0, nch, per_chunk, 0)

    def ffn_rows(rows):
        xs = xg_ref[rows, :]
        hid = (_silu(_dot(xs, w1_ref[...])) * _dot(xs, w3_ref[...])).astype(BF16)
        yacc_ref[rows, :] += _dot(hid, w2_ref[...])

    def ffn_pair(p, carry):
        ffn_rows(pl.ds(pl.multiple_of(p * 2 * fr, align), 2 * fr))
        return carry

    lax.fori_loop(0, n_ffn // 2, ffn_pair, 0)

    @pl.when(n_ffn % 2 == 1)
    def _():
        ffn_rows(pl.ds(pl.multiple_of((n_ffn - 1) * fr, align), fr))

    @pl.when(f == nf - 1)
    def _scatter():
        lane = lax.broadcasted_iota(I32, (1, LANES), 1)

        def per_chunk(c, carry):
            first, n_win = windows(c)
            toks = tok_chunk(c)
            lp = jnp.sum(jnp.where(lane == e, lposc_ref[toks, :], 0.0), axis=1, keepdims=True)
            gt = jnp.sum(jnp.where(lane == e, gatec_ref[toks, :], 0.0), axis=1, keepdims=True)

            def per_window(w, carry2):
                start = pl.multiple_of(first + w * win, align)
                y = yacc_ref[pl.ds(start, win), :]
                y_hi = y.astype(BF16)
                y_lo = (y - y_hi.astype(F32)).astype(BF16)
                want = (lax.broadcasted_iota(I32, (1, win), 1) + start).astype(F32)
                place = jnp.where(lp == want, 1.0, 0.0).astype(BF16)
                o_ref[toks, :] += gt * (_dot(place, y_hi) + _dot(place, y_lo))
                return carry2

            return lax.fori_loop(0, n_win, per_window, carry)

        lax.fori_loop(0, nch, per_chunk, 0)


def _chunk_bounds(cum):
    nch = MOE_GROUP // ROUTER_ROWS
    per_group = cum[:, 0, :N_EXPERTS].reshape(-1, nch, N_EXPERTS)
    return jnp.pad(per_group, ((0, 0), (1, 0), (0, 0))).reshape(-1)


def _moe(bounds, xh, lpost, lposc, gatec, w1, w3, w2, layer):
    n, d = xh.shape
    tg = MOE_GROUP
    dff = w1.shape[3]
    fc = MOE_FF_CHUNK
    cap = -(-(tg + max(MOE_FFN_ROWS, MOE_WIN)) // MOE_WIN) * MOE_WIN
    grid_spec = pltpu.PrefetchScalarGridSpec(
        num_scalar_prefetch=1,
        grid=(n // tg, N_EXPERTS, dff // fc),
        in_specs=[pl.BlockSpec((tg, d), lambda g, e, f, c: (g, 0)),
                  pl.BlockSpec((N_EXPERTS, tg), lambda g, e, f, c: (0, g)),
                  pl.BlockSpec((tg, LANES), lambda g, e, f, c: (g, 0)),
                  pl.BlockSpec((tg, LANES), lambda g, e, f, c: (g, 0)),
                  pl.BlockSpec((None, None, d, fc), lambda g, e, f, c: (layer, e, 0, f)),
                  pl.BlockSpec((None, None, d, fc), lambda g, e, f, c: (layer, e, 0, f)),
                  pl.BlockSpec((None, None, fc, d), lambda g, e, f, c: (layer, e, f, 0))],
        out_specs=pl.BlockSpec((tg, d), lambda g, e, f, c: (g, 0)),
        scratch_shapes=[pltpu.VMEM((cap, d), BF16), pltpu.VMEM((cap, d), F32)],
    )
    return pl.pallas_call(
        _moe_kernel,
        grid_spec=grid_spec,
        out_shape=jax.ShapeDtypeStruct((n, d), F32),
        compiler_params=_params(("parallel", "arbitrary", "arbitrary")),
    )(bounds, xh, lpost, lposc, gatec, w1, w3, w2)


EVEN_KB_SECTION = 2


def _even_weight_layout(w_in):
    hd = HEAD_DIM
    widths = (A_HEADS * hd, hd, hd, IDX_HEADS * IDX_DIM, IDX_DIM, IDX_HEADS, B_HEADS * hd, B_HEADS * hd, B_HEADS * hd)
    offs = [0]
    for wd in widths:
        offs.append(offs[-1] + wd)
    qa, ka, va, iq, ik, iw, qb, kb, vb = (w_in[:, offs[j]:offs[j + 1]] for j in range(9))
    iw_pad = jnp.pad(iw, ((0, 0), (0, LANES - IDX_HEADS)))
    w = jnp.concatenate([qa, qb, kb, vb, ka, ka, va, va, iq, ik, ik, ik, ik, iw_pad], axis=1)
    mixw = B_HEADS * hd
    sections = ((0, A_HEADS * hd, hd, QK_SCALE),
                (512, mixw, hd, QK_SCALE),
                (1024, mixw, hd, None),
                (1536, mixw, None, None),
                (2048, LANES, hd, None),
                (2176, LANES, None, None),
                (2304, IDX_HEADS * IDX_DIM, IDX_DIM, None),
                (2560, LANES, IDX_DIM, None),
                (2688, LANES, None, None))
    dtypes = (BF16,) * 8 + (F32,)
    return w.astype(BF16), sections, dtypes


def kernel(x, even_w_in, even_w_out, even_ln1_g, even_ln1_b, even_w1, even_w3, even_w2, even_ln2_g, even_ln2_b, odd_w_in, odd_w_out, odd_ln1_g, odd_ln1_b, odd_router, odd_w1, odd_w3, odd_w2, odd_ln2_g, odd_ln2_b):
    batch, seq, d = x.shape
    n = batch * seq
    tables = _rope_tables(seq)
    x2d = x.reshape(n, d)
    odd_w1h, odd_w3h, odd_w2h = odd_w1.astype(BF16), odd_w3.astype(BF16), odd_w2.astype(BF16)
    for layer in range(DEPTH):
        i = layer // 2
        if layer % 2 == 0:
            w, sections, dtypes = _even_weight_layout(even_w_in[i])
            qa, qb, kb, vb, kk, vv, iq, ik4, iw, kb_mean = _project(x2d, w, tables, sections, dtypes, seq,
                                                                    mean_section=EVEN_KB_SECTION)
            o_a = _dsa_attention(iq, iw, ik4, qa, kk, vv, batch, seq)
            o_b = _moba_attention(qb, kb, vb, kb_mean, batch, seq)
            x2d, xh = _outproj_ln([o_a, o_b], even_w_out[i].astype(BF16), x2d, even_ln1_g[i], even_ln1_b[i])
            x2d = _swiglu_ln(xh, x2d, even_w1[i].astype(BF16), even_w3[i].astype(BF16), even_w2[i].astype(BF16),
                             even_ln2_g[i], even_ln2_b[i])
        else:
            mix = C_HEADS * HEAD_DIM
            sections = ((0, mix, HEAD_DIM, QK_SCALE), (mix, mix, HEAD_DIM, None), (2 * mix, mix, None, None))
            q, k, v = _project(x2d, odd_w_in[i].astype(BF16), tables, sections, (F32, F32, F32), seq)
            o = _dilated_attention(q, k, v, batch, seq)
            x2d, xh = _outproj_ln([o], odd_w_out[i].astype(BF16), x2d, odd_ln1_g[i], odd_ln1_b[i])
            rw = jnp.pad(odd_router[i], ((0, 0), (0, LANES - N_EXPERTS))).astype(BF16)
            gatec, lposc, lpost, cum = _router(xh, rw)
            ffn = _moe(_chunk_bounds(cum), xh, lpost, lposc, gatec, odd_w1h, odd_w3h, odd_w2h, i)
            x2d = _add_ln(x2d, ffn, odd_ln2_g[i], odd_ln2_b[i])
    return x2d.reshape(batch, seq, d)
```

```python
import functools

import jax
import jax.numpy as jnp
from jax import lax
from jax.experimental import pallas as pl
from jax.experimental.pallas import tpu as pltpu

F32 = jnp.float32
BF16 = jnp.bfloat16
I32 = jnp.int32

DEPTH = 4
HEAD_DIM = 64
ROPE_THETA = 10000.0
LN_EPS = 1e-5
A_HEADS = 8
IDX_HEADS = 8
IDX_DIM = 32
DSA_TOPK = 256
B_HEADS = 8
MOBA_BLOCK = 256
MOBA_TOPK = 3
C_HEADS = 16
DILATED_CFG = ((128, 1), (512, 4), (2048, 16))
BAND_BLOCK = 128
N_EXPERTS = 8
DEEPNORM_ALPHA = (2 * DEPTH) ** 0.25
QK_SCALE = HEAD_DIM ** -0.5 * 1.4426950408889634

LANES = 128
VMEM_LIMIT_BYTES = 52 * 1024 * 1024

PROJ_ROWS = 512
FFN_ROWS = 512
FFN_CHUNKS = 1
ROUTER_ROWS = 512
MOE_GROUP = 2048
MOE_WIN = 256
MOE_FFN_ROWS = 272
MOE_ROW_ALIGN = 16
MOE_FF_CHUNK = 512
DSA_ROWS = 256
DSA_SEARCH_GROUPS = 4
DSA_SEARCH_UNROLL = 4
MOBA_GATE_ROWS = 16
DILATED_UNITS = 8
DILATED_SLABS = 4
ATTN_KEY_STEP = 512
MOBA_STEP_BLOCKS = 8

NEG_INF = float("-inf")
NEG_INF_KEY = -2139095041


def _params(semantics):
    return pltpu.CompilerParams(dimension_semantics=semantics, vmem_limit_bytes=VMEM_LIMIT_BYTES)


def _dot(a, b):
    return jnp.dot(a, b, preferred_element_type=F32)


def _dot_nt(a, b):
    return lax.dot_general(a, b, (((1,), (1,)), ((), ())), preferred_element_type=F32)


def _lane_group(shape, group, idx):
    lane = lax.broadcasted_iota(I32, shape, len(shape) - 1)
    return (lane // group) == idx


def _head_pair_attention(units):
    first = _lane_group((1, LANES), HEAD_DIM, 0)
    raw, unit_masks = [], []
    for q2, k2, _, mask in units:
        raw.append(_dot_nt(jnp.where(first, q2, jnp.zeros_like(q2)), k2))
        unit_masks.append(mask() if callable(mask) else mask)
        raw.append(_dot_nt(jnp.where(first, jnp.zeros_like(q2), q2), k2))
    scores = []
    for u, mask in enumerate(unit_masks):
        masks = mask if isinstance(mask, (tuple, list)) else (mask, mask)
        for h in range(2):
            if masks[h].dtype == jnp.bool_:
                scores.append(jnp.where(masks[h], raw[2 * u + h], NEG_INF))
            else:
                scores.append(raw[2 * u + h] + masks[h])
    maxes = [jnp.max(s, axis=1, keepdims=True) for s in scores]
    weights = [jnp.exp2(s - mx).astype(BF16) for s, mx in zip(scores, maxes)]
    results = []
    for u, (_, _, v2, _) in enumerate(units):
        ones = jnp.ones_like(v2)
        pv0 = _dot(weights[2 * u], jnp.where(first, v2, ones))
        pv1 = _dot(weights[2 * u + 1], jnp.where(first, ones, v2))
        den = jnp.where(first, pv0[:, LANES - 1:LANES], pv1[:, 0:1])
        results.append((jnp.where(first, maxes[2 * u], maxes[2 * u + 1]), den, jnp.where(first, pv0, pv1)))
    return results


def _stacked_head_pair_attention(units):
    first = _lane_group((1, LANES), HEAD_DIM, 0)
    raw = [_dot_nt(jnp.concatenate([jnp.where(first, q2, jnp.zeros_like(q2)),
                                    jnp.where(first, jnp.zeros_like(q2), q2)], axis=0), k2)
           for q2, k2, _, _ in units]
    stats, weights = [], []
    for r, (q2, _, _, mask) in zip(raw, units):
        rows = q2.shape[0]
        halves = [jnp.where(mask, r[h * rows:(h + 1) * rows], NEG_INF) for h in range(2)]
        maxes = [jnp.max(s, axis=1, keepdims=True) for s in halves]
        exps = [jnp.exp2(s - mx) for s, mx in zip(halves, maxes)]
        dens = [jnp.sum(e, axis=1, keepdims=True) for e in exps]
        stats.append((jnp.where(first, maxes[0], maxes[1]), jnp.where(first, dens[0], dens[1])))
        weights.append(jnp.concatenate(exps, axis=0).astype(BF16))
    results = []
    for (mx, den), w, (q2, _, v2, _) in zip(stats, weights, units):
        rows = q2.shape[0]
        pv = _dot(w, v2)
        results.append((mx, den, jnp.where(first, pv[:rows], pv[rows:])))
    return results


def _rope_lanes(t, cosf, sinf, half):
    lane = lax.broadcasted_iota(I32, t.shape, 1)
    first = (lane % (2 * half)) < half
    swapped = jnp.where(first, pltpu.roll(t, LANES - half, 1), pltpu.roll(t, half, 1))
    return t * cosf + swapped * sinf


def _proj_kernel(x_ref, w_ref, c64_ref, s64_ref, c32_ref, s32_ref, *out_refs, sections, mean_section):
    x = x_ref[...].astype(BF16)
    for idx, (o_ref, (start, width, rope, scale)) in enumerate(zip(out_refs, sections)):
        h = _dot(x, w_ref[:, start:start + width])
        if rope is not None:
            cosf = (c64_ref if rope == HEAD_DIM else c32_ref)[...]
            sinf = (s64_ref if rope == HEAD_DIM else s32_ref)[...]
            parts = [_rope_lanes(h[:, g * LANES:(g + 1) * LANES], cosf, sinf, rope // 2)
                     for g in range(width // LANES)]
            h = parts[0] if len(parts) == 1 else jnp.concatenate(parts, axis=1)
        if scale is not None:
            h = h * scale
        o_ref[...] = h.astype(o_ref.dtype)
        if idx == mean_section:
            m_ref = out_refs[len(sections)]
            for blk in range(h.shape[0] // MOBA_BLOCK):
                rows = h[blk * MOBA_BLOCK:(blk + 1) * MOBA_BLOCK]
                m_ref[0, blk:blk + 1, :] = jnp.sum(rows, axis=0, keepdims=True) * (1.0 / MOBA_BLOCK)


def _project(x2d, w, tables, sections, out_dtypes, seq, mean_section=None):
    n, d = x2d.shape
    tm = PROJ_ROWS
    pos_blocks = seq // tm
    tab_spec = pl.BlockSpec((tm, LANES), lambda i: (i % pos_blocks, 0))
    out_specs = [pl.BlockSpec((tm, sec[1]), lambda i: (i, 0)) for sec in sections]
    out_shape = [jax.ShapeDtypeStruct((n, sec[1]), dt) for sec, dt in zip(sections, out_dtypes)]
    if mean_section is not None:
        per_tile, width = tm // MOBA_BLOCK, sections[mean_section][1]
        out_specs.append(pl.BlockSpec((1, per_tile, width), lambda i: (i, 0, 0)))
        out_shape.append(jax.ShapeDtypeStruct((n // tm, per_tile, width), F32))
    outs = pl.pallas_call(
        functools.partial(_proj_kernel, sections=sections, mean_section=mean_section),
        grid=(n // tm,),
        in_specs=[pl.BlockSpec((tm, d), lambda i: (i, 0)),
                  pl.BlockSpec(w.shape, lambda i: (0, 0)),
                  tab_spec, tab_spec, tab_spec, tab_spec],
        out_specs=out_specs,
        out_shape=out_shape,
        compiler_params=_params(("parallel",)),
    )(x2d, w, *tables)
    outs = list(outs)
    if mean_section is not None:
        outs[-1] = outs[-1].reshape(n // MOBA_BLOCK, -1)
    return outs


def _rope_tables(seq):
    out = []
    for dim in (HEAD_DIM, IDX_DIM):
        inv = ROPE_THETA ** (-jnp.arange(0, dim, 2, dtype=F32) / dim)
        ang = jnp.arange(seq, dtype=F32)[:, None] * inv[None, :]
        cos, sin = jnp.cos(ang), jnp.sin(ang)
        reps = LANES // dim
        out.append(jnp.tile(jnp.concatenate([cos, cos], -1), (1, reps)))
        out.append(jnp.tile(jnp.concatenate([-sin, sin], -1), (1, reps)))
    return out


def _dsa_block(i, iq_ref, iw_ref, ik_ref, q_ref, kk_ref, vv_ref, o_ref, key_ref, sel_ref, *, kext, topk):
    qb = DSA_ROWS
    keys = slice(0, kext)
    qpos = i * qb + lax.broadcasted_iota(I32, (qb, 1), 0)
    kpos = lax.broadcasted_iota(I32, (1, kext), 1)
    causal = kpos <= qpos

    ik = ik_ref[keys, :]
    iw = iw_ref[...] * (IDX_DIM ** -0.5 * IDX_HEADS ** -0.5)
    heads_per_group = LANES // IDX_DIM
    score = jnp.zeros((qb, kext), F32)
    for h in range(IDX_HEADS):
        grp, j = divmod(h, heads_per_group)
        iqg = iq_ref[:, grp * LANES:(grp + 1) * LANES]
        iqm = jnp.where(_lane_group((1, LANES), IDX_DIM, j), iqg, jnp.zeros_like(iqg))
        rel = jnp.maximum(_dot_nt(iqm, ik), 0.0)
        score = score + rel * iw[:, h:h + 1]
    bits = pltpu.bitcast(score, I32)
    neg = bits >> 31
    int_min = jnp.int32(-2 ** 31)
    key_ref[:, keys] = jnp.where(causal, (bits ^ (neg & 0x7FFFFFFF)) - neg, int_min)

    grp_rows = qb // DSA_SEARCH_GROUPS
    groups = [slice(g * grp_rows, (g + 1) * grp_rows) for g in range(DSA_SEARCH_GROUPS)]

    def count_ge(rows, t):
        return jnp.sum(jnp.where(key_ref[rows, keys] >= t, 1.0, 0.0), axis=1, keepdims=True)

    def search(it, thrs):
        bit = jnp.left_shift(jnp.int32(1), 30 - it)
        return tuple(jnp.where(count_ge(rows, thr | bit) >= topk, thr | bit, thr)
                     for rows, thr in zip(groups, thrs))

    zero = jnp.zeros((grp_rows, 1), I32)
    thrs = tuple(jnp.where(count_ge(rows, zero) >= topk, zero, int_min) for rows in groups)
    thrs = lax.fori_loop(0, 31, search, thrs, unroll=DSA_SEARCH_UNROLL)
    thr = jnp.concatenate(thrs, axis=0)

    floor = jnp.maximum(thr, int_min + 1)
    key = key_ref[:, keys]
    sel_ref[:, keys] = jnp.where(key >= floor, 0.0, NEG_INF)
    n_ge = jnp.sum(jnp.where(key >= thr, 1.0, 0.0), axis=1, keepdims=True)
    tie_rows = (n_ge > topk) & (thr > int_min)

    @pl.when(jnp.max(tie_rows.astype(I32)) > 0)
    def _break_ties():
        upper = (lax.broadcasted_iota(I32, (LANES, LANES), 0)
                 < lax.broadcasted_iota(I32, (LANES, LANES), 1))
        upper = jnp.where(upper, 1.0, 0.0).astype(BF16)
        n_gt = jnp.sum(jnp.where(key_ref[:, keys] > thr, 1.0, 0.0), axis=1, keepdims=True)
        need = topk - n_gt
        before = jnp.zeros((qb, 1), F32)
        for c in range(kext // LANES):
            sl = slice(c * LANES, (c + 1) * LANES)
            keyc = key_ref[:, sl]
            eqc = keyc == thr
            eqf = jnp.where(eqc, 1.0, 0.0)
            rank = _dot(eqf.astype(BF16), upper) + before
            take = ((keyc > thr) | (eqc & (rank < need))) & (keyc > int_min)
            sel_ref[:, sl] = jnp.where(take, 0.0, NEG_INF)
            before = before + jnp.sum(eqf, axis=1, keepdims=True)

    selected = sel_ref[:, keys]
    kk = kk_ref[keys, :]
    vv = vv_ref[keys, :]
    pairs = [(q_ref[:, hp * LANES:(hp + 1) * LANES], kk, vv, selected) for hp in range(A_HEADS // 2)]
    for hp, (_, den, num) in enumerate(_head_pair_attention(pairs)):
        o_ref[:, hp * LANES:(hp + 1) * LANES] = (num / den).astype(o_ref.dtype)


def _dsa_kernel(*refs, seq, topk):
    i = pl.program_id(1)
    blocks_per_step = ATTN_KEY_STEP // DSA_ROWS
    for v in range(seq // ATTN_KEY_STEP):
        @pl.when(i // blocks_per_step == v)
        def _(v=v):
            _dsa_block(i, *refs, kext=(v + 1) * ATTN_KEY_STEP, topk=topk)


def _dsa_attention(iq, iw, ik4, qa, kk, vv, batch, seq):
    n = batch * seq
    qb = DSA_ROWS
    nq = seq // qb
    topk = min(DSA_TOPK, seq // 4)
    rows = lambda width: pl.BlockSpec((qb, width), lambda b, i: (b * nq + i, 0))
    whole = pl.BlockSpec((seq, LANES), lambda b, i: (b, 0))
    return pl.pallas_call(
        functools.partial(_dsa_kernel, seq=seq, topk=topk),
        grid=(batch, nq),
        in_specs=[rows(iq.shape[1]), rows(LANES), whole, rows(qa.shape[1]), whole, whole],
        out_specs=rows(qa.shape[1]),
        out_shape=jax.ShapeDtypeStruct((n, qa.shape[1]), BF16),
        scratch_shapes=[pltpu.VMEM((qb, seq), I32), pltpu.VMEM((qb, seq), F32)],
        compiler_params=_params(("parallel", "parallel")),
    )(iq, iw, ik4, qa, kk, vv)


def _moba_step(step, q_ref, k_ref, v_ref, kmean_ref, o_ref, *, kext, kt):
    bs, per = MOBA_BLOCK, MOBA_STEP_BLOCKS
    nb_all = kmean_ref.shape[0]
    nbp = max(nb_all, MOBA_GATE_ROWS)
    kmean = jnp.concatenate([kmean_ref[...], jnp.zeros((nbp - nb_all, LANES), F32)], axis=0).astype(BF16)
    n_iota = lax.broadcasted_iota(I32, (nbp, 1), 0)
    qrow = lax.broadcasted_iota(I32, (bs, 1), 0)
    units = []
    for j in range(per):
        qi = step * per + j
        ext = kext - (per - 1 - j) * bs
        nb = ext // bs
        q2 = q_ref[j * bs:(j + 1) * bs, :]
        blk_row = lax.broadcasted_iota(I32, (nbp, ext), 0)
        blk_of_key = lax.broadcasted_iota(I32, (nbp, ext), 1) // bs
        member = jnp.where(blk_row == blk_of_key, 1.0, 0.0).astype(BF16)
        past = n_iota < qi
        kcol = lax.broadcasted_iota(I32, (1, ext), 1)
        own = (kcol >= qi * bs) & (kcol <= qi * bs + qrow)
        gates = [jnp.where(past, _dot_nt(kmean, jnp.where(_lane_group((1, LANES), HEAD_DIM, half), q2,
                                                          jnp.zeros_like(q2))), NEG_INF)
                 for half in range(2)]

        def block_masks(gates=gates, past=past, own=own, member=member, nb=nb):
            chosen = []
            for gate in gates:
                rank = jnp.zeros((nbp, bs), I32)
                for m in range(nb):
                    gm = gate[m:m + 1, :]
                    beats = (gm > gate) | ((gm == gate) & (m < n_iota))
                    rank = rank + beats.astype(I32)
                chosen.append(jnp.transpose(jnp.where(past & (rank < kt), 1.0, 0.0)).astype(BF16))
            return [(_dot(c, member) > 0.5) | own for c in chosen]

        units.append((q2, k_ref[0:ext, :], v_ref[0:ext, :], block_masks))
    for j, (_, den, num) in enumerate(_head_pair_attention(units)):
        o_ref[j * bs:(j + 1) * bs, :] = (num / den).astype(o_ref.dtype)


def _moba_kernel(*refs, seq):
    step = pl.program_id(2)
    kt = min(MOBA_TOPK, seq // MOBA_BLOCK - 1)
    rows = MOBA_BLOCK * MOBA_STEP_BLOCKS
    for v in range(seq // rows):
        @pl.when(step == v)
        def _(v=v):
            _moba_step(step, *refs, kext=(v + 1) * rows, kt=kt)


def _moba_attention(q, k, v, kmean, batch, seq):
    n, width = q.shape
    tq = MOBA_BLOCK * MOBA_STEP_BLOCKS
    steps = seq // tq
    rows = pl.BlockSpec((tq, LANES), lambda b, hp, st: (b * steps + st, hp))
    whole = pl.BlockSpec((seq, LANES), lambda b, hp, st: (b, hp))
    means = pl.BlockSpec((seq // MOBA_BLOCK, LANES), lambda b, hp, st: (b, hp))
    return pl.pallas_call(
        functools.partial(_moba_kernel, seq=seq),
        grid=(batch, width // LANES, steps),
        in_specs=[rows, whole, whole, means],
        out_specs=rows,
        out_shape=jax.ShapeDtypeStruct((n, width), BF16),
        compiler_params=_params(("parallel", "parallel", "parallel")),
    )(q, k, v, kmean)


def _rows(start, size, stride):
    return pl.ds(start, size) if stride == 1 else pl.ds(start, size, stride=stride)


def _largest_divisor(n, limit):
    return max(d for d in range(1, limit + 1) if n % d == 0)


def _repeat(trips, body):
    if trips == 1:
        body(0, 0)
    else:
        lax.fori_loop(0, trips, body, 0)


def _dilated_kernel(q_ref, k_ref, v_ref, o_ref, qp_ref, kp_ref, vp_ref, nat_ref, slab_ref, *, seq):
    blk = BAND_BLOCK
    ns = DILATED_SLABS
    slab_len = seq // ns

    for s in range(ns):
        rows = slice(s * slab_len, (s + 1) * slab_len)
        qp_ref[rows, :] = q_ref[pl.ds(s, slab_len, stride=ns), :]
        kp_ref[rows, :] = k_ref[pl.ds(s, slab_len, stride=ns), :]
        vp_ref[rows, :] = v_ref[pl.ds(s, slab_len, stride=ns), :]

    def operands(srcs, qstart, kstart, nk, stride):
        ksl = _rows(kstart, nk, stride)
        i = lax.broadcasted_iota(I32, (blk, 1), 0)
        j = lax.broadcasted_iota(I32, (1, nk), 1)
        if nk == blk:
            mask = j <= i
        else:
            mask = (j >= i) & (j <= i + blk)
        return (srcs[0][_rows(qstart, blk, stride), :].astype(BF16), srcs[1][ksl, :].astype(BF16),
                srcs[2][ksl, :].astype(BF16), mask)

    def merged(old, new):
        (m_old, l_old, a_old), (m_new, l_new, a_new) = old, new
        m_tot = jnp.maximum(m_old, m_new)
        w_old = jnp.exp2(m_old - m_tot)
        w_new = jnp.exp2(m_new - m_tot)
        return m_tot, w_old * l_old + w_new * l_new, w_old * a_old + w_new * a_new

    def run_units(srcs, state_ref, units, nk, stride, merge):
        results = _stacked_head_pair_attention([operands(srcs, qstart, kstart, nk, stride)
                                                for qstart, kstart in units])
        for (qstart, _), new in zip(units, results):
            qsl = _rows(qstart, blk, stride)
            if merge:
                new = merged(tuple(state_ref[t, qsl, :] for t in range(3)), new)
            for t in range(3):
                state_ref[t, qsl, :] = new[t]

    written = set()
    for window, dil in DILATED_CFG:
        assert window // dil == blk
        if dil % ns == 0:
            srcs, state_ref, stride, group_len = (qp_ref, kp_ref, vp_ref), slab_ref, dil // ns, slab_len
        else:
            srcs, state_ref, stride, group_len = (q_ref, k_ref, v_ref), nat_ref, dil, seq
        merge = id(state_ref) in written
        written.add(id(state_ref))
        nblk = seq // dil // blk
        ub = _largest_divisor(dil, DILATED_UNITS)

        def class_start(c, stride=stride, group_len=group_len):
            return (c // stride) * group_len + c % stride if group_len != seq else c

        def first_blocks(it, carry, srcs=srcs, state_ref=state_ref, stride=stride, merge=merge, ub=ub,
                         class_start=class_start):
            starts = [class_start(it * ub + u) for u in range(ub)]
            run_units(srcs, state_ref, [(st, st) for st in starts], blk, stride, merge)
            return carry

        _repeat(dil // ub, first_blocks)
        if nblk > 1:
            later = nblk - 1
            ul = _largest_divisor(dil * later, DILATED_UNITS)

            def later_blocks(it, carry, srcs=srcs, state_ref=state_ref, stride=stride, merge=merge, ul=ul,
                             later=later, class_start=class_start):
                flat = [it * ul + u for u in range(ul)]
                qstarts = [class_start(t // later) + stride * blk * (1 + t % later) for t in flat]
                run_units(srcs, state_ref, [(qs, qs - stride * blk) for qs in qstarts], 2 * blk, stride, merge)
                return carry

            _repeat(dil * later // ul, later_blocks)

    for s in range(ns):
        nat_rows = pl.ds(s, slab_len, stride=ns)
        slab_rows = slice(s * slab_len, (s + 1) * slab_len)
        _, den, num = merged(tuple(nat_ref[t, nat_rows, :] for t in range(3)),
                             tuple(slab_ref[t, slab_rows, :] for t in range(3)))
        nat_ref[2, nat_rows, :] = num / den
    o_ref[...] = nat_ref[2].astype(o_ref.dtype)


def _dilated_attention(q, k, v, batch, seq):
    n, width = q.shape
    for window, dil in DILATED_CFG:
        assert seq % (dil * BAND_BLOCK) == 0
    assert {dil % DILATED_SLABS == 0 for _, dil in DILATED_CFG} == {True, False}
    whole = pl.BlockSpec((seq, LANES), lambda b, hp: (b, hp))
    return pl.pallas_call(
        functools.partial(_dilated_kernel, seq=seq),
        grid=(batch, width // LANES),
        in_specs=[whole, whole, whole],
        out_specs=whole,
        out_shape=jax.ShapeDtypeStruct((n, width), BF16),
        scratch_shapes=[pltpu.VMEM((seq, LANES), F32)] * 3 + [pltpu.VMEM((3, seq, LANES), F32)] * 2,
        compiler_params=_params(("parallel", "parallel")),
    )(q, k, v)


def _deepnorm(x, f, g, b):
    y = DEEPNORM_ALPHA * x + f
    mu = jnp.mean(y, axis=1, keepdims=True)
    yc = y - mu
    var = jnp.mean(yc * yc, axis=1, keepdims=True)
    return yc * lax.rsqrt(var + LN_EPS) * g + b


def _outproj_ln_kernel(*refs, n_parts):
    part_refs = refs[:n_parts]
    w_ref, x_ref, g_ref, b_ref, o_ref, oh_ref = refs[n_parts:]
    mix = None
    start = 0
    for p_ref in part_refs:
        width = p_ref.shape[1]
        t = _dot(p_ref[...], w_ref[start:start + width, :])
        mix = t if mix is None else mix + t
        start += width
    y = _deepnorm(x_ref[...], mix, g_ref[...], b_ref[...])
    o_ref[...] = y
    oh_ref[...] = y.astype(BF16)


def _outproj_ln(parts, w, x2d, g, b):
    n, d = x2d.shape
    tm = PROJ_ROWS
    row = lambda width: pl.BlockSpec((tm, width), lambda i: (i, 0))
    vec = pl.BlockSpec((1, d), lambda i: (0, 0))
    return pl.pallas_call(
        functools.partial(_outproj_ln_kernel, n_parts=len(parts)),
        grid=(n // tm,),
        in_specs=[row(p.shape[1]) for p in parts] + [pl.BlockSpec(w.shape, lambda i: (0, 0)), row(d), vec, vec],
        out_specs=[row(d), row(d)],
        out_shape=[jax.ShapeDtypeStruct((n, d), F32), jax.ShapeDtypeStruct((n, d), BF16)],
        compiler_params=_params(("parallel",)),
    )(*parts, w, x2d, g.reshape(1, d), b.reshape(1, d))


def _silu(h):
    return h * (1.0 / (1.0 + jnp.exp(-h)))


def _swiglu_ln_kernel(xh_ref, x_ref, w1_ref, w3_ref, w2_ref, g_ref, b_ref, o_ref, acc_ref):
    f = pl.program_id(1)
    xh = xh_ref[...]
    hid = (_silu(_dot(xh, w1_ref[...])) * _dot(xh, w3_ref[...])).astype(BF16)
    y = _dot(hid, w2_ref[...])

    @pl.when(f == 0)
    def _():
        acc_ref[...] = y

    @pl.when(f > 0)
    def _():
        acc_ref[...] += y

    @pl.when(f == pl.num_programs(1) - 1)
    def _():
        o_ref[...] = _deepnorm(x_ref[...], acc_ref[...], g_ref[...], b_ref[...])


def _swiglu_ln(xh, x2d, w1, w3, w2, g, b):
    n, d = x2d.shape
    dff = w1.shape[1]
    tm = FFN_ROWS
    nf = FFN_CHUNKS
    fc = dff // nf
    row = lambda: pl.BlockSpec((tm, d), lambda i, f: (i, 0))
    vec = pl.BlockSpec((1, d), lambda i, f: (0, 0))
    mode = pl.Buffered(1) if nf == 1 else None
    return pl.pallas_call(
        _swiglu_ln_kernel,
        grid=(n // tm, nf),
        in_specs=[row(), row(),
                  pl.BlockSpec((d, fc), lambda i, f: (0, f), pipeline_mode=mode),
                  pl.BlockSpec((d, fc), lambda i, f: (0, f), pipeline_mode=mode),
                  pl.BlockSpec((fc, d), lambda i, f: (f, 0), pipeline_mode=mode),
                  vec, vec],
        out_specs=row(),
        out_shape=jax.ShapeDtypeStruct((n, d), F32),
        scratch_shapes=[pltpu.VMEM((tm, d), F32)],
        compiler_params=_params(("parallel", "arbitrary")),
    )(xh, x2d, w1, w3, w2, g.reshape(1, d), b.reshape(1, d))


def _add_ln_kernel(x_ref, f_ref, g_ref, b_ref, o_ref):
    o_ref[...] = _deepnorm(x_ref[...], f_ref[...], g_ref[...], b_ref[...])


def _add_ln(x2d, f2d, g, b):
    n, d = x2d.shape
    tm = PROJ_ROWS
    row = pl.BlockSpec((tm, d), lambda i: (i, 0))
    vec = pl.BlockSpec((1, d), lambda i: (0, 0))
    return pl.pallas_call(
        _add_ln_kernel,
        grid=(n // tm,),
        in_specs=[row, row, vec, vec],
        out_specs=row,
        out_shape=jax.ShapeDtypeStruct((n, d), F32),
        compiler_params=_params(("parallel",)),
    )(x2d, f2d, g.reshape(1, d), b.reshape(1, d))


def _router_kernel(xh_ref, rw_ref, gate_ref, lposc_ref, lpost_ref, cum_ref, carry_ref, *, tiles_per_group):
    tm = xh_ref.shape[0]
    i = pl.program_id(0)

    @pl.when(i % tiles_per_group == 0)
    def _():
        carry_ref[...] = jnp.zeros_like(carry_ref)

    lane = lax.broadcasted_iota(I32, (1, LANES), 1)
    logits = jnp.where(lane < N_EXPERTS, _dot(xh_ref[...], rw_ref[...]), NEG_INF)
    m1 = jnp.max(logits, axis=1, keepdims=True)
    i1 = jnp.min(jnp.where(logits == m1, lane, LANES), axis=1, keepdims=True)
    rest = jnp.where(lane == i1, NEG_INF, logits)
    m2 = jnp.max(rest, axis=1, keepdims=True)
    i2 = jnp.min(jnp.where(rest == m2, lane, LANES), axis=1, keepdims=True)
    e2 = jnp.exp(m2 - m1)
    g1 = 1.0 / (1.0 + e2)
    g2 = e2 / (1.0 + e2)
    sel1 = lane == i1
    sel2 = lane == i2
    routed = sel1 | sel2
    gate_ref[...] = jnp.where(sel1, g1, jnp.where(sel2, g2, 0.0))
    routedf = jnp.where(routed, 1.0, 0.0)
    earlier = (lax.broadcasted_iota(I32, (tm, tm), 1) < lax.broadcasted_iota(I32, (tm, tm), 0))
    earlier = jnp.where(earlier, 1.0, 0.0).astype(BF16)
    carry = carry_ref[...]
    lpos = jnp.where(routed, _dot(earlier, routedf.astype(BF16)) + carry, -1.0)
    lposc_ref[...] = lpos
    lpost_ref[...] = jnp.transpose(lpos)[0:N_EXPERTS, :]
    carry = carry + jnp.sum(routedf, axis=0, keepdims=True)
    carry_ref[...] = carry
    cum_ref[...] = carry.astype(I32).reshape(1, 1, LANES)


def _router(xh, rw):
    n, d = xh.shape
    tm = ROUTER_ROWS
    nt = n // tm
    row = pl.BlockSpec((tm, LANES), lambda i: (i, 0))
    return pl.pallas_call(
        functools.partial(_router_kernel, tiles_per_group=MOE_GROUP // tm),
        grid=(nt,),
        in_specs=[pl.BlockSpec((tm, d), lambda i: (i, 0)), pl.BlockSpec(rw.shape, lambda i: (0, 0))],
        out_specs=[row, row,
                   pl.BlockSpec((N_EXPERTS, tm), lambda i: (0, i)),
                   pl.BlockSpec((1, 1, LANES), lambda i: (i, 0, 0))],
        out_shape=[jax.ShapeDtypeStruct((n, LANES), F32), jax.ShapeDtypeStruct((n, LANES), F32),
                   jax.ShapeDtypeStruct((N_EXPERTS, n), F32), jax.ShapeDtypeStruct((nt, 1, LANES), I32)],
        scratch_shapes=[pltpu.VMEM((1, LANES), F32)],
        compiler_params=_params(("arbitrary",)),
    )(xh, rw)


def _moe_kernel(bnd_ref, xh_ref, lpost_ref, lposc_ref, gatec_ref, w1_ref, w3_ref, w2_ref,
                o_ref, xg_ref, yacc_ref):
    tg = xh_ref.shape[0]
    win, ch, fr, align = MOE_WIN, ROUTER_ROWS, MOE_FFN_ROWS, MOE_ROW_ALIGN
    nch = tg // ch
    g, e, f = pl.program_id(0), pl.program_id(1), pl.program_id(2)
    nf = pl.num_programs(2)

    def bound(c):
        return bnd_ref[(g * (nch + 1) + c) * N_EXPERTS + e]

    cnt = bound(nch)
    n_ffn = (cnt + fr - 1) // fr

    def tok_chunk(c):
        return pl.ds(pl.multiple_of(c * ch, ch), ch)

    def windows(c):
        r0, r1 = bound(c), bound(c + 1)
        first = (r0 // align) * align
        return first, jnp.where(r1 > r0, (r1 - first + win - 1) // win, 0)

    @pl.when((e == 0) & (f == 0))
    def _():
        o_ref[...] = jnp.zeros_like(o_ref)

    @pl.when(f == 0)
    def _compact():
        def clear(s, carry):
            rows = pl.ds(pl.multiple_of(s * win, win), win)
            xg_ref[rows, :] = jnp.zeros((win, xg_ref.shape[1]), xg_ref.dtype)
            yacc_ref[rows, :] = jnp.zeros((win, yacc_ref.shape[1]), F32)
            return carry

        lax.fori_loop(0, (cnt + max(fr, win) + win - 1) // win, clear, 0)

        def per_chunk(c, carry):
            first, n_win = windows(c)
            lp = lpost_ref[pl.ds(e, 1), tok_chunk(c)]

            def per_window(w, carry2):
                start = pl.multiple_of(first + w * win, align)
                want = (lax.broadcasted_iota(I32, (win, 1), 0) + start).astype(F32)
                pick = jnp.where(lp == want, 1.0, 0.0).astype(BF16)
                rows = pl.ds(start, win)
                xg_ref[rows, :] += _dot(pick, xh_ref[tok_chunk(c), :]).astype(xg_ref.dtype)
                return carry2

            return lax.fori_loop(0, n_win, per_window, carry)

        lax.fori_loop(0, nch, per_chunk, 0)

    def ffn_rows(rows):
        xs = xg_ref[rows, :]
        hid = (_silu(_dot(xs, w1_ref[...].astype(BF16))) * _dot(xs, w3_ref[...].astype(BF16))).astype(BF16)
        yacc_ref[rows, :] += _dot(hid, w2_ref[...].astype(BF16))

    def ffn_pair(p, carry):
        ffn_rows(pl.ds(pl.multiple_of(p * 2 * fr, align), 2 * fr))
        return carry

    lax.fori_loop(0, n_ffn // 2, ffn_pair, 0)

    @pl.when(n_ffn % 2 == 1)
    def _():
        ffn_rows(pl.ds(pl.multiple_of((n_ffn - 1) * fr, align), fr))

    @pl.when(f == nf - 1)
    def _scatter():
        lane = lax.broadcasted_iota(I32, (1, LANES), 1)

        def per_chunk(c, carry):
            first, n_win = windows(c)
            toks = tok_chunk(c)
            lp = jnp.sum(jnp.where(lane == e, lposc_ref[toks, :], 0.0), axis=1, keepdims=True)
            gt = jnp.sum(jnp.where(lane == e, gatec_ref[toks, :], 0.0), axis=1, keepdims=True)

            def per_window(w, carry2):
                start = pl.multiple_of(first + w * win, align)
                y = yacc_ref[pl.ds(start, win), :]
                y_hi = y.astype(BF16)
                y_lo = (y - y_hi.astype(F32)).astype(BF16)
                want = (lax.broadcasted_iota(I32, (1, win), 1) + start).astype(F32)
                place = jnp.where(lp == want, 1.0, 0.0).astype(BF16)
                o_ref[toks, :] += gt * (_dot(place, y_hi) + _dot(place, y_lo))
                return carry2

            return lax.fori_loop(0, n_win, per_window, carry)

        lax.fori_loop(0, nch, per_chunk, 0)


def _chunk_bounds(cum):
    nch = MOE_GROUP // ROUTER_ROWS
    per_group = cum[:, 0, :N_EXPERTS].reshape(-1, nch, N_EXPERTS)
    return jnp.pad(per_group, ((0, 0), (1, 0), (0, 0))).reshape(-1)


def _moe(bounds, xh, lpost, lposc, gatec, w1, w3, w2, layer):
    n, d = xh.shape
    tg = MOE_GROUP
    dff = w1.shape[3]
    fc = MOE_FF_CHUNK
    cap = -(-(tg + max(MOE_FFN_ROWS, MOE_WIN)) // MOE_WIN) * MOE_WIN
    grid_spec = pltpu.PrefetchScalarGridSpec(
        num_scalar_prefetch=1,
        grid=(n // tg, N_EXPERTS, dff // fc),
        in_specs=[pl.BlockSpec((tg, d), lambda g, e, f, c: (g, 0), pipeline_mode=pl.Buffered(1)),
                  pl.BlockSpec((N_EXPERTS, tg), lambda g, e, f, c: (0, g), pipeline_mode=pl.Buffered(1)),
                  pl.BlockSpec((tg, LANES), lambda g, e, f, c: (g, 0), pipeline_mode=pl.Buffered(1)),
                  pl.BlockSpec((tg, LANES), lambda g, e, f, c: (g, 0), pipeline_mode=pl.Buffered(1)),
                  pl.BlockSpec((None, None, d, fc), lambda g, e, f, c: (layer, e, 0, f)),
                  pl.BlockSpec((None, None, d, fc), lambda g, e, f, c: (layer, e, 0, f)),
                  pl.BlockSpec((None, None, fc, d), lambda g, e, f, c: (layer, e, f, 0))],
        out_specs=pl.BlockSpec((tg, d), lambda g, e, f, c: (g, 0)),
        scratch_shapes=[pltpu.VMEM((cap, d), BF16), pltpu.VMEM((cap, d), F32)],
    )
    return pl.pallas_call(
        _moe_kernel,
        grid_spec=grid_spec,
        out_shape=jax.ShapeDtypeStruct((n, d), F32),
        compiler_params=_params(("parallel", "arbitrary", "arbitrary")),
    )(bounds, xh, lpost, lposc, gatec, w1, w3, w2)


EVEN_KB_SECTION = 2


def _even_weight_layout(w_in):
    hd = HEAD_DIM
    widths = (A_HEADS * hd, hd, hd, IDX_HEADS * IDX_DIM, IDX_DIM, IDX_HEADS, B_HEADS * hd, B_HEADS * hd, B_HEADS * hd)
    offs = [0]
    for wd in widths:
        offs.append(offs[-1] + wd)
    qa, ka, va, iq, ik, iw, qb, kb, vb = (w_in[:, offs[j]:offs[j + 1]] for j in range(9))
    iw_pad = jnp.pad(iw, ((0, 0), (0, LANES - IDX_HEADS)))
    w = jnp.concatenate([qa, qb, kb, vb, ka, ka, va, va, iq, ik, ik, ik, ik, iw_pad], axis=1)
    mixw = B_HEADS * hd
    sections = ((0, A_HEADS * hd, hd, QK_SCALE),
                (512, mixw, hd, QK_SCALE),
                (1024, mixw, hd, None),
                (1536, mixw, None, None),
                (2048, LANES, hd, None),
                (2176, LANES, None, None),
                (2304, IDX_HEADS * IDX_DIM, IDX_DIM, None),
                (2560, LANES, IDX_DIM, None),
                (2688, LANES, None, None))
    dtypes = (BF16,) * 8 + (F32,)
    return w.astype(BF16), sections, dtypes


def kernel(x, even_w_in, even_w_out, even_ln1_g, even_ln1_b, even_w1, even_w3, even_w2, even_ln2_g, even_ln2_b, odd_w_in, odd_w_out, odd_ln1_g, odd_ln1_b, odd_router, odd_w1, odd_w3, odd_w2, odd_ln2_g, odd_ln2_b):
    batch, seq, d = x.shape
    n = batch * seq
    tables = _rope_tables(seq)
    x2d = x.reshape(n, d)
    for layer in range(DEPTH):
        i = layer // 2
        if layer % 2 == 0:
            w, sections, dtypes = _even_weight_layout(even_w_in[i])
            qa, qb, kb, vb, kk, vv, iq, ik4, iw, kb_mean = _project(x2d, w, tables, sections, dtypes, seq,
                                                                    mean_section=EVEN_KB_SECTION)
            o_a = _dsa_attention(iq, iw, ik4, qa, kk, vv, batch, seq)
            o_b = _moba_attention(qb, kb, vb, kb_mean, batch, seq)
            x2d, xh = _outproj_ln([o_a, o_b], even_w_out[i].astype(BF16), x2d, even_ln1_g[i], even_ln1_b[i])
            x2d = _swiglu_ln(xh, x2d, even_w1[i].astype(BF16), even_w3[i].astype(BF16), even_w2[i].astype(BF16),
                             even_ln2_g[i], even_ln2_b[i])
        else:
            mix = C_HEADS * HEAD_DIM
            sections = ((0, mix, HEAD_DIM, QK_SCALE), (mix, mix, HEAD_DIM, None), (2 * mix, mix, None, None))
            q, k, v = _project(x2d, odd_w_in[i].astype(BF16), tables, sections, (F32, F32, F32), seq)
            o = _dilated_attention(q, k, v, batch, seq)
            x2d, xh = _outproj_ln([o], odd_w_out[i].astype(BF16), x2d, odd_ln1_g[i], odd_ln1_b[i])
            rw = jnp.pad(odd_router[i], ((0, 0), (0, LANES - N_EXPERTS))).astype(BF16)
            gatec, lposc, lpost, cum = _router(xh, rw)
            ffn = _moe(_chunk_bounds(cum), xh, lpost, lposc, gatec, odd_w1, odd_w3, odd_w2, i)
            x2d = _add_ln(x2d, ffn, odd_ln2_g[i], odd_ln2_b[i])
    return x2d.reshape(batch, seq, d)
```

```python
import functools

import jax
import jax.numpy as jnp
from jax import lax
from jax.experimental import pallas as pl
from jax.experimental.pallas import tpu as pltpu

F32 = jnp.float32
BF16 = jnp.bfloat16
I32 = jnp.int32

DEPTH = 4
HEAD_DIM = 64
ROPE_THETA = 10000.0
LN_EPS = 1e-5
A_HEADS = 8
IDX_HEADS = 8
IDX_DIM = 32
DSA_TOPK = 256
B_HEADS = 8
MOBA_BLOCK = 256
MOBA_TOPK = 3
C_HEADS = 16
DILATED_CFG = ((128, 1), (512, 4), (2048, 16))
BAND_BLOCK = 128
N_EXPERTS = 8
DEEPNORM_ALPHA = (2 * DEPTH) ** 0.25
QK_SCALE = HEAD_DIM ** -0.5 * 1.4426950408889634

LANES = 128
VMEM_LIMIT_BYTES = 52 * 1024 * 1024

PROJ_ROWS = 512
FFN_ROWS = 512
FFN_CHUNKS = 1
ROUTER_ROWS = 512
MOE_GROUP = 2048
MOE_WIN = 256
MOE_FFN_ROWS = 272
MOE_ROW_ALIGN = 16
MOE_FF_CHUNK = 512
DSA_ROWS = 256
DSA_SEARCH_GROUPS = 4
DSA_SEARCH_UNROLL = 4
MOBA_GATE_ROWS = 16
DILATED_UNITS = 8
DILATED_SLABS = 4
ATTN_KEY_STEP = 512
MOBA_STEP_BLOCKS = 8

NEG_INF = float("-inf")
NEG_INF_KEY = -2139095041


def _params(semantics):
    return pltpu.CompilerParams(dimension_semantics=semantics, vmem_limit_bytes=VMEM_LIMIT_BYTES)


def _dot(a, b):
    return jnp.dot(a, b, preferred_element_type=F32)


def _dot_nt(a, b):
    return lax.dot_general(a, b, (((1,), (1,)), ((), ())), preferred_element_type=F32)


def _lane_group(shape, group, idx):
    lane = lax.broadcasted_iota(I32, shape, len(shape) - 1)
    return (lane // group) == idx


def _head_pair_attention(units):
    first = _lane_group((1, LANES), HEAD_DIM, 0)
    raw, unit_masks = [], []
    for q2, k2, _, mask in units:
        raw.append(_dot_nt(jnp.where(first, q2, jnp.zeros_like(q2)), k2))
        unit_masks.append(mask() if callable(mask) else mask)
        raw.append(_dot_nt(jnp.where(first, jnp.zeros_like(q2), q2), k2))
    scores = []
    for u, mask in enumerate(unit_masks):
        masks = mask if isinstance(mask, (tuple, list)) else (mask, mask)
        for h in range(2):
            if masks[h].dtype == jnp.bool_:
                scores.append(jnp.where(masks[h], raw[2 * u + h], NEG_INF))
            else:
                scores.append(raw[2 * u + h] + masks[h])
    maxes = [jnp.max(s, axis=1, keepdims=True) for s in scores]
    weights = [jnp.exp2(s - mx).astype(BF16) for s, mx in zip(scores, maxes)]
    results = []
    for u, (_, _, v2, _) in enumerate(units):
        ones = jnp.ones_like(v2)
        pv0 = _dot(weights[2 * u], jnp.where(first, v2, ones))
        pv1 = _dot(weights[2 * u + 1], jnp.where(first, ones, v2))
        den = jnp.where(first, pv0[:, LANES - 1:LANES], pv1[:, 0:1])
        results.append((jnp.where(first, maxes[2 * u], maxes[2 * u + 1]), den, jnp.where(first, pv0, pv1)))
    return results


def _stacked_head_pair_attention(units):
    first = _lane_group((1, LANES), HEAD_DIM, 0)
    raw = [_dot_nt(jnp.concatenate([jnp.where(first, q2, jnp.zeros_like(q2)),
                                    jnp.where(first, jnp.zeros_like(q2), q2)], axis=0), k2)
           for q2, k2, _, _ in units]
    stats, weights = [], []
    for r, (q2, _, _, mask) in zip(raw, units):
        rows = q2.shape[0]
        halves = [jnp.where(mask, r[h * rows:(h + 1) * rows], NEG_INF) for h in range(2)]
        maxes = [jnp.max(s, axis=1, keepdims=True) for s in halves]
        exps = [jnp.exp2(s - mx) for s, mx in zip(halves, maxes)]
        dens = [jnp.sum(e, axis=1, keepdims=True) for e in exps]
        stats.append((jnp.where(first, maxes[0], maxes[1]), jnp.where(first, dens[0], dens[1])))
        weights.append(jnp.concatenate(exps, axis=0).astype(BF16))
    results = []
    for (mx, den), w, (q2, _, v2, _) in zip(stats, weights, units):
        rows = q2.shape[0]
        pv = _dot(w, v2)
        results.append((mx, den, jnp.where(first, pv[:rows], pv[rows:])))
    return results


def _rope_lanes(t, cosf, sinf, half):
    lane = lax.broadcasted_iota(I32, t.shape, 1)
    first = (lane % (2 * half)) < half
    swapped = jnp.where(first, pltpu.roll(t, LANES - half, 1), pltpu.roll(t, half, 1))
    return t * cosf + swapped * sinf


def _proj_kernel(x_ref, w_ref, c64_ref, s64_ref, c32_ref, s32_ref, *out_refs, sections, mean_section):
    x = x_ref[...].astype(BF16)
    for idx, (o_ref, (start, width, rope, scale)) in enumerate(zip(out_refs, sections)):
        h = _dot(x, w_ref[:, start:start + width])
        if rope is not None:
            cosf = (c64_ref if rope == HEAD_DIM else c32_ref)[...]
            sinf = (s64_ref if rope == HEAD_DIM else s32_ref)[...]
            parts = [_rope_lanes(h[:, g * LANES:(g + 1) * LANES], cosf, sinf, rope // 2)
                     for g in range(width // LANES)]
            h = parts[0] if len(parts) == 1 else jnp.concatenate(parts, axis=1)
        if scale is not None:
            h = h * scale
        o_ref[...] = h.astype(o_ref.dtype)
        if idx == mean_section:
            m_ref = out_refs[len(sections)]
            for blk in range(h.shape[0] // MOBA_BLOCK):
                rows = h[blk * MOBA_BLOCK:(blk + 1) * MOBA_BLOCK]
                m_ref[0, blk:blk + 1, :] = jnp.sum(rows, axis=0, keepdims=True) * (1.0 / MOBA_BLOCK)


def _project(x2d, w, tables, sections, out_dtypes, seq, mean_section=None):
    n, d = x2d.shape
    tm = PROJ_ROWS
    pos_blocks = seq // tm
    tab_spec = pl.BlockSpec((tm, LANES), lambda i: (i % pos_blocks, 0))
    out_specs = [pl.BlockSpec((tm, sec[1]), lambda i: (i, 0)) for sec in sections]
    out_shape = [jax.ShapeDtypeStruct((n, sec[1]), dt) for sec, dt in zip(sections, out_dtypes)]
    if mean_section is not None:
        per_tile, width = tm // MOBA_BLOCK, sections[mean_section][1]
        out_specs.append(pl.BlockSpec((1, per_tile, width), lambda i: (i, 0, 0)))
        out_shape.append(jax.ShapeDtypeStruct((n // tm, per_tile, width), F32))
    outs = pl.pallas_call(
        functools.partial(_proj_kernel, sections=sections, mean_section=mean_section),
        grid=(n // tm,),
        in_specs=[pl.BlockSpec((tm, d), lambda i: (i, 0)),
                  pl.BlockSpec(w.shape, lambda i: (0, 0)),
                  tab_spec, tab_spec, tab_spec, tab_spec],
        out_specs=out_specs,
        out_shape=out_shape,
        compiler_params=_params(("parallel",)),
    )(x2d, w, *tables)
    outs = list(outs)
    if mean_section is not None:
        outs[-1] = outs[-1].reshape(n // MOBA_BLOCK, -1)
    return outs


def _rope_tables(seq):
    out = []
    for dim in (HEAD_DIM, IDX_DIM):
        inv = ROPE_THETA ** (-jnp.arange(0, dim, 2, dtype=F32) / dim)
        ang = jnp.arange(seq, dtype=F32)[:, None] * inv[None, :]
        cos, sin = jnp.cos(ang), jnp.sin(ang)
        reps = LANES // dim
        out.append(jnp.tile(jnp.concatenate([cos, cos], -1), (1, reps)))
        out.append(jnp.tile(jnp.concatenate([-sin, sin], -1), (1, reps)))
    return out


def _dsa_block(i, iq_ref, iw_ref, ik_ref, q_ref, kk_ref, vv_ref, o_ref, key_ref, sel_ref, *, kext, topk):
    qb = DSA_ROWS
    keys = slice(0, kext)
    qpos = i * qb + lax.broadcasted_iota(I32, (qb, 1), 0)
    kpos = lax.broadcasted_iota(I32, (1, kext), 1)
    causal = kpos <= qpos

    ik = ik_ref[keys, :]
    iw = iw_ref[...] * (IDX_DIM ** -0.5 * IDX_HEADS ** -0.5)
    heads_per_group = LANES // IDX_DIM
    score = jnp.zeros((qb, kext), F32)
    for h in range(IDX_HEADS):
        grp, j = divmod(h, heads_per_group)
        iqg = iq_ref[:, grp * LANES:(grp + 1) * LANES]
        iqm = jnp.where(_lane_group((1, LANES), IDX_DIM, j), iqg, jnp.zeros_like(iqg))
        rel = jnp.maximum(_dot_nt(iqm, ik), 0.0)
        score = score + rel * iw[:, h:h + 1]
    bits = pltpu.bitcast(score, I32)
    neg = bits >> 31
    int_min = jnp.int32(-2 ** 31)
    key_ref[:, keys] = jnp.where(causal, (bits ^ (neg & 0x7FFFFFFF)) - neg, int_min)

    grp_rows = qb // DSA_SEARCH_GROUPS
    groups = [slice(g * grp_rows, (g + 1) * grp_rows) for g in range(DSA_SEARCH_GROUPS)]

    def count_ge(rows, t):
        return jnp.sum(jnp.where(key_ref[rows, keys] >= t, 1.0, 0.0), axis=1, keepdims=True)

    def search(it, thrs):
        bit = jnp.left_shift(jnp.int32(1), 30 - it)
        return tuple(jnp.where(count_ge(rows, thr | bit) >= topk, thr | bit, thr)
                     for rows, thr in zip(groups, thrs))

    zero = jnp.zeros((grp_rows, 1), I32)
    thrs = tuple(jnp.where(count_ge(rows, zero) >= topk, zero, int_min) for rows in groups)
    thrs = lax.fori_loop(0, 31, search, thrs, unroll=DSA_SEARCH_UNROLL)
    thr = jnp.concatenate(thrs, axis=0)

    floor = jnp.maximum(thr, int_min + 1)
    key = key_ref[:, keys]
    sel_ref[:, keys] = jnp.where(key >= floor, 0.0, NEG_INF)
    n_ge = jnp.sum(jnp.where(key >= thr, 1.0, 0.0), axis=1, keepdims=True)
    tie_rows = (n_ge > topk) & (thr > int_min)

    @pl.when(jnp.max(tie_rows.astype(I32)) > 0)
    def _break_ties():
        upper = (lax.broadcasted_iota(I32, (LANES, LANES), 0)
                 < lax.broadcasted_iota(I32, (LANES, LANES), 1))
        upper = jnp.where(upper, 1.0, 0.0).astype(BF16)
        n_gt = jnp.sum(jnp.where(key_ref[:, keys] > thr, 1.0, 0.0), axis=1, keepdims=True)
        need = topk - n_gt
        before = jnp.zeros((qb, 1), F32)
        for c in range(kext // LANES):
            sl = slice(c * LANES, (c + 1) * LANES)
            keyc = key_ref[:, sl]
            eqc = keyc == thr
            eqf = jnp.where(eqc, 1.0, 0.0)
            rank = _dot(eqf.astype(BF16), upper) + before
            take = ((keyc > thr) | (eqc & (rank < need))) & (keyc > int_min)
            sel_ref[:, sl] = jnp.where(take, 0.0, NEG_INF)
            before = before + jnp.sum(eqf, axis=1, keepdims=True)

    selected = sel_ref[:, keys]
    kk = kk_ref[keys, :]
    vv = vv_ref[keys, :]
    pairs = [(q_ref[:, hp * LANES:(hp + 1) * LANES], kk, vv, selected) for hp in range(A_HEADS // 2)]
    for hp, (_, den, num) in enumerate(_head_pair_attention(pairs)):
        o_ref[:, hp * LANES:(hp + 1) * LANES] = (num / den).astype(o_ref.dtype)


def _dsa_kernel(*refs, seq, topk):
    i = pl.program_id(1)
    blocks_per_step = ATTN_KEY_STEP // DSA_ROWS
    for v in range(seq // ATTN_KEY_STEP):
        @pl.when(i // blocks_per_step == v)
        def _(v=v):
            _dsa_block(i, *refs, kext=(v + 1) * ATTN_KEY_STEP, topk=topk)


def _dsa_attention(iq, iw, ik4, qa, kk, vv, batch, seq):
    n = batch * seq
    qb = DSA_ROWS
    nq = seq // qb
    topk = min(DSA_TOPK, seq // 4)
    rows = lambda width: pl.BlockSpec((qb, width), lambda b, i: (b * nq + i, 0))
    whole = pl.BlockSpec((seq, LANES), lambda b, i: (b, 0))
    return pl.pallas_call(
        functools.partial(_dsa_kernel, seq=seq, topk=topk),
        grid=(batch, nq),
        in_specs=[rows(iq.shape[1]), rows(LANES), whole, rows(qa.shape[1]), whole, whole],
        out_specs=rows(qa.shape[1]),
        out_shape=jax.ShapeDtypeStruct((n, qa.shape[1]), BF16),
        scratch_shapes=[pltpu.VMEM((qb, seq), I32), pltpu.VMEM((qb, seq), F32)],
        compiler_params=_params(("parallel", "parallel")),
    )(iq, iw, ik4, qa, kk, vv)


def _moba_step(step, q_ref, k_ref, v_ref, kmean_ref, o_ref, *, kext, kt):
    bs, per = MOBA_BLOCK, MOBA_STEP_BLOCKS
    nb_all = kmean_ref.shape[0]
    nbp = max(nb_all, MOBA_GATE_ROWS)
    kmean = jnp.concatenate([kmean_ref[...], jnp.zeros((nbp - nb_all, LANES), F32)], axis=0).astype(BF16)
    n_iota = lax.broadcasted_iota(I32, (nbp, 1), 0)
    qrow = lax.broadcasted_iota(I32, (bs, 1), 0)
    units = []
    for j in range(per):
        qi = step * per + j
        ext = kext - (per - 1 - j) * bs
        nb = ext // bs
        q2 = q_ref[j * bs:(j + 1) * bs, :]
        blk_row = lax.broadcasted_iota(I32, (nbp, ext), 0)
        blk_of_key = lax.broadcasted_iota(I32, (nbp, ext), 1) // bs
        member = jnp.where(blk_row == blk_of_key, 1.0, 0.0).astype(BF16)
        past = n_iota < qi
        kcol = lax.broadcasted_iota(I32, (1, ext), 1)
        own = (kcol >= qi * bs) & (kcol <= qi * bs + qrow)
        gates = [jnp.where(past, _dot_nt(kmean, jnp.where(_lane_group((1, LANES), HEAD_DIM, half), q2,
                                                          jnp.zeros_like(q2))), NEG_INF)
                 for half in range(2)]

        def block_masks(gates=gates, past=past, own=own, member=member, nb=nb):
            chosen = []
            for gate in gates:
                rank = jnp.zeros((nbp, bs), I32)
                for m in range(nb):
                    gm = gate[m:m + 1, :]
                    beats = (gm > gate) | ((gm == gate) & (m < n_iota))
                    rank = rank + beats.astype(I32)
                chosen.append(jnp.transpose(jnp.where(past & (rank < kt), 1.0, 0.0)).astype(BF16))
            return [(_dot(c, member) > 0.5) | own for c in chosen]

        units.append((q2, k_ref[0:ext, :], v_ref[0:ext, :], block_masks))
    for j, (_, den, num) in enumerate(_head_pair_attention(units)):
        o_ref[j * bs:(j + 1) * bs, :] = (num / den).astype(o_ref.dtype)


def _moba_kernel(*refs, seq):
    step = pl.program_id(2)
    kt = min(MOBA_TOPK, seq // MOBA_BLOCK - 1)
    rows = MOBA_BLOCK * MOBA_STEP_BLOCKS
    for v in range(seq // rows):
        @pl.when(step == v)
        def _(v=v):
            _moba_step(step, *refs, kext=(v + 1) * rows, kt=kt)


def _moba_attention(q, k, v, kmean, batch, seq):
    n, width = q.shape
    tq = MOBA_BLOCK * MOBA_STEP_BLOCKS
    steps = seq // tq
    rows = pl.BlockSpec((tq, LANES), lambda b, hp, st: (b * steps + st, hp))
    whole = pl.BlockSpec((seq, LANES), lambda b, hp, st: (b, hp))
    means = pl.BlockSpec((seq // MOBA_BLOCK, LANES), lambda b, hp, st: (b, hp))
    return pl.pallas_call(
        functools.partial(_moba_kernel, seq=seq),
        grid=(batch, width // LANES, steps),
        in_specs=[rows, whole, whole, means],
        out_specs=rows,
        out_shape=jax.ShapeDtypeStruct((n, width), BF16),
        compiler_params=_params(("parallel", "parallel", "parallel")),
    )(q, k, v, kmean)


def _rows(start, size, stride):
    return pl.ds(start, size) if stride == 1 else pl.ds(start, size, stride=stride)


def _largest_divisor(n, limit):
    return max(d for d in range(1, limit + 1) if n % d == 0)


def _repeat(trips, body):
    if trips == 1:
        body(0, 0)
    else:
        lax.fori_loop(0, trips, body, 0)


def _dilated_kernel(q_ref, k_ref, v_ref, o_ref, qp_ref, kp_ref, vp_ref, nat_ref, slab_ref, *, seq):
    blk = BAND_BLOCK
    ns = DILATED_SLABS
    slab_len = seq // ns

    for s in range(ns):
        rows = slice(s * slab_len, (s + 1) * slab_len)
        qp_ref[rows, :] = q_ref[pl.ds(s, slab_len, stride=ns), :]
        kp_ref[rows, :] = k_ref[pl.ds(s, slab_len, stride=ns), :]
        vp_ref[rows, :] = v_ref[pl.ds(s, slab_len, stride=ns), :]

    def operands(srcs, qstart, kstart, nk, stride):
        ksl = _rows(kstart, nk, stride)
        i = lax.broadcasted_iota(I32, (blk, 1), 0)
        j = lax.broadcasted_iota(I32, (1, nk), 1)
        if nk == blk:
            mask = j <= i
        else:
            mask = (j >= i) & (j <= i + blk)
        return (srcs[0][_rows(qstart, blk, stride), :].astype(BF16), srcs[1][ksl, :].astype(BF16),
                srcs[2][ksl, :].astype(BF16), mask)

    def merged(old, new):
        (m_old, l_old, a_old), (m_new, l_new, a_new) = old, new
        m_tot = jnp.maximum(m_old, m_new)
        w_old = jnp.exp2(m_old - m_tot)
        w_new = jnp.exp2(m_new - m_tot)
        return m_tot, w_old * l_old + w_new * l_new, w_old * a_old + w_new * a_new

    def run_units(srcs, state_ref, units, nk, stride, merge):
        results = _stacked_head_pair_attention([operands(srcs, qstart, kstart, nk, stride)
                                                for qstart, kstart in units])
        for (qstart, _), new in zip(units, results):
            qsl = _rows(qstart, blk, stride)
            if merge:
                new = merged(tuple(state_ref[t, qsl, :] for t in range(3)), new)
            for t in range(3):
                state_ref[t, qsl, :] = new[t]

    written = set()
    for window, dil in DILATED_CFG:
        assert window // dil == blk
        if dil % ns == 0:
            srcs, state_ref, stride, group_len = (qp_ref, kp_ref, vp_ref), slab_ref, dil // ns, slab_len
        else:
            srcs, state_ref, stride, group_len = (q_ref, k_ref, v_ref), nat_ref, dil, seq
        merge = id(state_ref) in written
        written.add(id(state_ref))
        nblk = seq // dil // blk
        ub = _largest_divisor(dil, DILATED_UNITS)

        def class_start(c, stride=stride, group_len=group_len):
            return (c // stride) * group_len + c % stride if group_len != seq else c

        def first_blocks(it, carry, srcs=srcs, state_ref=state_ref, stride=stride, merge=merge, ub=ub,
                         class_start=class_start):
            starts = [class_start(it * ub + u) for u in range(ub)]
            run_units(srcs, state_ref, [(st, st) for st in starts], blk, stride, merge)
            return carry

        _repeat(dil // ub, first_blocks)
        if nblk > 1:
            later = nblk - 1
            ul = _largest_divisor(dil * later, DILATED_UNITS)

            def later_blocks(it, carry, srcs=srcs, state_ref=state_ref, stride=stride, merge=merge, ul=ul,
                             later=later, class_start=class_start):
                flat = [it * ul + u for u in range(ul)]
                qstarts = [class_start(t // later) + stride * blk * (1 + t % later) for t in flat]
                run_units(srcs, state_ref, [(qs, qs - stride * blk) for qs in qstarts], 2 * blk, stride, merge)
                return carry

            _repeat(dil * later // ul, later_blocks)

    for s in range(ns):
        nat_rows = pl.ds(s, slab_len, stride=ns)
        slab_rows = slice(s * slab_len, (s + 1) * slab_len)
        _, den, num = merged(tuple(nat_ref[t, nat_rows, :] for t in range(3)),
                             tuple(slab_ref[t, slab_rows, :] for t in range(3)))
        nat_ref[2, nat_rows, :] = num / den
    o_ref[...] = nat_ref[2].astype(o_ref.dtype)


def _dilated_attention(q, k, v, batch, seq):
    n, width = q.shape
    for window, dil in DILATED_CFG:
        assert seq % (dil * BAND_BLOCK) == 0
    assert {dil % DILATED_SLABS == 0 for _, dil in DILATED_CFG} == {True, False}
    whole = pl.BlockSpec((seq, LANES), lambda b, hp: (b, hp))
    return pl.pallas_call(
        functools.partial(_dilated_kernel, seq=seq),
        grid=(batch, width // LANES),
        in_specs=[whole, whole, whole],
        out_specs=whole,
        out_shape=jax.ShapeDtypeStruct((n, width), BF16),
        scratch_shapes=[pltpu.VMEM((seq, LANES), F32)] * 3 + [pltpu.VMEM((3, seq, LANES), F32)] * 2,
        compiler_params=_params(("parallel", "parallel")),
    )(q, k, v)


def _deepnorm(x, f, g, b):
    y = DEEPNORM_ALPHA * x + f
    mu = jnp.mean(y, axis=1, keepdims=True)
    yc = y - mu
    var = jnp.mean(yc * yc, axis=1, keepdims=True)
    return yc * lax.rsqrt(var + LN_EPS) * g + b


def _outproj_ln_kernel(*refs, n_parts):
    part_refs = refs[:n_parts]
    w_ref, x_ref, g_ref, b_ref, o_ref, oh_ref = refs[n_parts:]
    mix = None
    start = 0
    for p_ref in part_refs:
        width = p_ref.shape[1]
        t = _dot(p_ref[...], w_ref[start:start + width, :])
        mix = t if mix is None else mix + t
        start += width
    y = _deepnorm(x_ref[...], mix, g_ref[...], b_ref[...])
    o_ref[...] = y
    oh_ref[...] = y.astype(BF16)


def _outproj_ln(parts, w, x2d, g, b):
    n, d = x2d.shape
    tm = PROJ_ROWS
    row = lambda width: pl.BlockSpec((tm, width), lambda i: (i, 0))
    vec = pl.BlockSpec((1, d), lambda i: (0, 0))
    return pl.pallas_call(
        functools.partial(_outproj_ln_kernel, n_parts=len(parts)),
        grid=(n // tm,),
        in_specs=[row(p.shape[1]) for p in parts] + [pl.BlockSpec(w.shape, lambda i: (0, 0)), row(d), vec, vec],
        out_specs=[row(d), row(d)],
        out_shape=[jax.ShapeDtypeStruct((n, d), F32), jax.ShapeDtypeStruct((n, d), BF16)],
        compiler_params=_params(("parallel",)),
    )(*parts, w, x2d, g.reshape(1, d), b.reshape(1, d))


def _silu(h):
    return h * (1.0 / (1.0 + jnp.exp(-h)))


def _swiglu_ln_kernel(xh_ref, x_ref, w1_ref, w3_ref, w2_ref, g_ref, b_ref, o_ref, acc_ref):
    f = pl.program_id(1)
    xh = xh_ref[...]
    hid = (_silu(_dot(xh, w1_ref[...])) * _dot(xh, w3_ref[...])).astype(BF16)
    y = _dot(hid, w2_ref[...])

    @pl.when(f == 0)
    def _():
        acc_ref[...] = y

    @pl.when(f > 0)
    def _():
        acc_ref[...] += y

    @pl.when(f == pl.num_programs(1) - 1)
    def _():
        o_ref[...] = _deepnorm(x_ref[...], acc_ref[...], g_ref[...], b_ref[...])


def _swiglu_ln(xh, x2d, w1, w3, w2, g, b):
    n, d = x2d.shape
    dff = w1.shape[1]
    tm = FFN_ROWS
    nf = FFN_CHUNKS
    fc = dff // nf
    row = lambda: pl.BlockSpec((tm, d), lambda i, f: (i, 0))
    vec = pl.BlockSpec((1, d), lambda i, f: (0, 0))
    mode = pl.Buffered(1) if nf == 1 else None
    return pl.pallas_call(
        _swiglu_ln_kernel,
        grid=(n // tm, nf),
        in_specs=[row(), row(),
                  pl.BlockSpec((d, fc), lambda i, f: (0, f), pipeline_mode=mode),
                  pl.BlockSpec((d, fc), lambda i, f: (0, f), pipeline_mode=mode),
                  pl.BlockSpec((fc, d), lambda i, f: (f, 0), pipeline_mode=mode),
                  vec, vec],
        out_specs=row(),
        out_shape=jax.ShapeDtypeStruct((n, d), F32),
        scratch_shapes=[pltpu.VMEM((tm, d), F32)],
        compiler_params=_params(("parallel", "arbitrary")),
    )(xh, x2d, w1, w3, w2, g.reshape(1, d), b.reshape(1, d))


def _add_ln_kernel(x_ref, f_ref, g_ref, b_ref, o_ref):
    o_ref[...] = _deepnorm(x_ref[...], f_ref[...], g_ref[...], b_ref[...])


def _add_ln(x2d, f2d, g, b):
    n, d = x2d.shape
    tm = PROJ_ROWS
    row = pl.BlockSpec((tm, d), lambda i: (i, 0))
    vec = pl.BlockSpec((1, d), lambda i: (0, 0))
    return pl.pallas_call(
        _add_ln_kernel,
        grid=(n // tm,),
        in_specs=[row, row, vec, vec],
        out_specs=row,
        out_shape=jax.ShapeDtypeStruct((n, d), F32),
        compiler_params=_params(("parallel",)),
    )(x2d, f2d, g.reshape(1, d), b.reshape(1, d))


def _router_kernel(xh_ref, rw_ref, gate_ref, lposc_ref, lpost_ref, cum_ref, carry_ref, *, tiles_per_group):
    tm = xh_ref.shape[0]
    i = pl.program_id(0)

    @pl.when(i % tiles_per_group == 0)
    def _():
        carry_ref[...] = jnp.zeros_like(carry_ref)

    lane = lax.broadcasted_iota(I32, (1, LANES), 1)
    logits = jnp.where(lane < N_EXPERTS, _dot(xh_ref[...], rw_ref[...]), NEG_INF)
    m1 = jnp.max(logits, axis=1, keepdims=True)
    i1 = jnp.min(jnp.where(logits == m1, lane, LANES), axis=1, keepdims=True)
    rest = jnp.where(lane == i1, NEG_INF, logits)
    m2 = jnp.max(rest, axis=1, keepdims=True)
    i2 = jnp.min(jnp.where(rest == m2, lane, LANES), axis=1, keepdims=True)
    e2 = jnp.exp(m2 - m1)
    g1 = 1.0 / (1.0 + e2)
    g2 = e2 / (1.0 + e2)
    sel1 = lane == i1
    sel2 = lane == i2
    routed = sel1 | sel2
    gate_ref[...] = jnp.where(sel1, g1, jnp.where(sel2, g2, 0.0))
    routedf = jnp.where(routed, 1.0, 0.0)
    earlier = (lax.broadcasted_iota(I32, (tm, tm), 1) < lax.broadcasted_iota(I32, (tm, tm), 0))
    earlier = jnp.where(earlier, 1.0, 0.0).astype(BF16)
    carry = carry_ref[...]
    lpos = jnp.where(routed, _dot(earlier, routedf.astype(BF16)) + carry, -1.0)
    lposc_ref[...] = lpos
    lpost_ref[...] = jnp.transpose(lpos)[0:N_EXPERTS, :]
    carry = carry + jnp.sum(routedf, axis=0, keepdims=True)
    carry_ref[...] = carry
    cum_ref[...] = carry.astype(I32).reshape(1, 1, LANES)


def _router(xh, rw):
    n, d = xh.shape
    tm = ROUTER_ROWS
    nt = n // tm
    row = pl.BlockSpec((tm, LANES), lambda i: (i, 0))
    return pl.pallas_call(
        functools.partial(_router_kernel, tiles_per_group=MOE_GROUP // tm),
        grid=(nt,),
        in_specs=[pl.BlockSpec((tm, d), lambda i: (i, 0)), pl.BlockSpec(rw.shape, lambda i: (0, 0))],
        out_specs=[row, row,
                   pl.BlockSpec((N_EXPERTS, tm), lambda i: (0, i)),
                   pl.BlockSpec((1, 1, LANES), lambda i: (i, 0, 0))],
        out_shape=[jax.ShapeDtypeStruct((n, LANES), F32), jax.ShapeDtypeStruct((n, LANES), F32),
                   jax.ShapeDtypeStruct((N_EXPERTS, n), F32), jax.ShapeDtypeStruct((nt, 1, LANES), I32)],
        scratch_shapes=[pltpu.VMEM((1, LANES), F32)],
        compiler_params=_params(("arbitrary",)),
    )(xh, rw)


def _moe_kernel(bnd_ref, xh_ref, lpost_ref, lposc_ref, gatec_ref, w1_ref, w3_ref, w2_ref,
                o_ref, xg_ref, yacc_ref):
    tg = xh_ref.shape[0]
    win, ch, fr, align = MOE_WIN, ROUTER_ROWS, MOE_FFN_ROWS, MOE_ROW_ALIGN
    nch = tg // ch
    g, e, f = pl.program_id(0), pl.program_id(1), pl.program_id(2)
    nf = pl.num_programs(2)

    def bound(c):
        return bnd_ref[(g * (nch + 1) + c) * N_EXPERTS + e]

    cnt = bound(nch)
    n_ffn = (cnt + fr - 1) // fr

    def tok_chunk(c):
        return pl.ds(pl.multiple_of(c * ch, ch), ch)

    def windows(c):
        r0, r1 = bound(c), bound(c + 1)
        first = (r0 // align) * align
        return first, jnp.where(r1 > r0, (r1 - first + win - 1) // win, 0)

    one_window = functools.reduce(jnp.logical_and, [windows(c)[1] <= 1 for c in range(nch)])

    @pl.when((e == 0) & (f == 0))
    def _():
        o_ref[...] = jnp.zeros_like(o_ref)

    @pl.when(f == 0)
    def _compact():
        def clear(s, carry):
            rows = pl.ds(pl.multiple_of(s * win, win), win)
            xg_ref[rows, :] = jnp.zeros((win, xg_ref.shape[1]), xg_ref.dtype)
            yacc_ref[rows, :] = jnp.zeros((win, yacc_ref.shape[1]), F32)
            return carry

        lax.fori_loop(0, (cnt + max(fr, win) + win - 1) // win, clear, 0)

        def per_chunk(c, carry):
            first, n_win = windows(c)
            lp = lpost_ref[pl.ds(e, 1), tok_chunk(c)]

            def per_window(w, carry2):
                start = pl.multiple_of(first + w * win, align)
                want = (lax.broadcasted_iota(I32, (win, 1), 0) + start).astype(F32)
                pick = jnp.where(lp == want, 1.0, 0.0).astype(BF16)
                rows = pl.ds(start, win)
                xg_ref[rows, :] += _dot(pick, xh_ref[tok_chunk(c), :]).astype(xg_ref.dtype)
                return carry2

            return lax.fori_loop(0, n_win, per_window, carry)

        @pl.when(one_window)
        def _():
            for c in range(nch):
                toks = slice(c * ch, (c + 1) * ch)
                start = pl.multiple_of(windows(c)[0], align)
                want = (lax.broadcasted_iota(I32, (win, 1), 0) + start).astype(F32)
                pick = jnp.where(lpost_ref[pl.ds(e, 1), toks] == want, 1.0, 0.0).astype(BF16)
                xg_ref[pl.ds(start, win), :] += _dot(pick, xh_ref[toks, :]).astype(xg_ref.dtype)

        @pl.when(jnp.logical_not(one_window))
        def _():
            lax.fori_loop(0, nch, per_chunk, 0)

    def ffn_rows(rows):
        xs = xg_ref[rows, :]
        hid = (_silu(_dot(xs, w1_ref[...])) * _dot(xs, w3_ref[...])).astype(BF16)
        yacc_ref[rows, :] += _dot(hid, w2_ref[...])

    def ffn_pair(p, carry):
        ffn_rows(pl.ds(pl.multiple_of(p * 2 * fr, align), 2 * fr))
        return carry

    lax.fori_loop(0, n_ffn // 2, ffn_pair, 0)

    @pl.when(n_ffn % 2 == 1)
    def _():
        ffn_rows(pl.ds(pl.multiple_of((n_ffn - 1) * fr, align), fr))

    @pl.when(f == nf - 1)
    def _scatter():
        lane = lax.broadcasted_iota(I32, (1, LANES), 1)

        def per_chunk(c, carry):
            first, n_win = windows(c)
            toks = tok_chunk(c)
            lp = jnp.sum(jnp.where(lane == e, lposc_ref[toks, :], 0.0), axis=1, keepdims=True)
            gt = jnp.sum(jnp.where(lane == e, gatec_ref[toks, :], 0.0), axis=1, keepdims=True)

            def per_window(w, carry2):
                start = pl.multiple_of(first + w * win, align)
                y = yacc_ref[pl.ds(start, win), :]
                y_hi = y.astype(BF16)
                y_lo = (y - y_hi.astype(F32)).astype(BF16)
                want = (lax.broadcasted_iota(I32, (1, win), 1) + start).astype(F32)
                place = jnp.where(lp == want, 1.0, 0.0).astype(BF16)
                o_ref[toks, :] += gt * (_dot(place, y_hi) + _dot(place, y_lo))
                return carry2

            return lax.fori_loop(0, n_win, per_window, carry)

        @pl.when(one_window)
        def _():
            for c in range(nch):
                toks = slice(c * ch, (c + 1) * ch)
                start = pl.multiple_of(windows(c)[0], align)
                lp = jnp.sum(jnp.where(lane == e, lposc_ref[toks, :], 0.0), axis=1, keepdims=True)
                gt = jnp.sum(jnp.where(lane == e, gatec_ref[toks, :], 0.0), axis=1, keepdims=True)
                y = yacc_ref[pl.ds(start, win), :]
                y_hi = y.astype(BF16)
                y_lo = (y - y_hi.astype(F32)).astype(BF16)
                want = (lax.broadcasted_iota(I32, (1, win), 1) + start).astype(F32)
                place = jnp.where(lp == want, 1.0, 0.0).astype(BF16)
                o_ref[toks, :] += gt * (_dot(place, y_hi) + _dot(place, y_lo))

        @pl.when(jnp.logical_not(one_window))
        def _():
            lax.fori_loop(0, nch, per_chunk, 0)


def _chunk_bounds(cum):
    nch = MOE_GROUP // ROUTER_ROWS
    per_group = cum[:, 0, :N_EXPERTS].reshape(-1, nch, N_EXPERTS)
    return jnp.pad(per_group, ((0, 0), (1, 0), (0, 0))).reshape(-1)


def _moe(bounds, xh, lpost, lposc, gatec, w1, w3, w2, layer):
    n, d = xh.shape
    tg = MOE_GROUP
    dff = w1.shape[3]
    fc = MOE_FF_CHUNK
    cap = -(-(tg + max(MOE_FFN_ROWS, MOE_WIN)) // MOE_WIN) * MOE_WIN
    grid_spec = pltpu.PrefetchScalarGridSpec(
        num_scalar_prefetch=1,
        grid=(n // tg, N_EXPERTS, dff // fc),
        in_specs=[pl.BlockSpec((tg, d), lambda g, e, f, c: (g, 0)),
                  pl.BlockSpec((N_EXPERTS, tg), lambda g, e, f, c: (0, g)),
                  pl.BlockSpec((tg, LANES), lambda g, e, f, c: (g, 0)),
                  pl.BlockSpec((tg, LANES), lambda g, e, f, c: (g, 0)),
                  pl.BlockSpec((None, None, d, fc), lambda g, e, f, c: (layer, e, 0, f)),
                  pl.BlockSpec((None, None, d, fc), lambda g, e, f, c: (layer, e, 0, f)),
                  pl.BlockSpec((None, None, fc, d), lambda g, e, f, c: (layer, e, f, 0))],
        out_specs=pl.BlockSpec((tg, d), lambda g, e, f, c: (g, 0)),
        scratch_shapes=[pltpu.VMEM((cap, d), BF16), pltpu.VMEM((cap, d), F32)],
    )
    return pl.pallas_call(
        _moe_kernel,
        grid_spec=grid_spec,
        out_shape=jax.ShapeDtypeStruct((n, d), F32),
        compiler_params=_params(("parallel", "arbitrary", "arbitrary")),
    )(bounds, xh, lpost, lposc, gatec, w1, w3, w2)


EVEN_KB_SECTION = 2


def _even_weight_layout(w_in):
    hd = HEAD_DIM
    widths = (A_HEADS * hd, hd, hd, IDX_HEADS * IDX_DIM, IDX_DIM, IDX_HEADS, B_HEADS * hd, B_HEADS * hd, B_HEADS * hd)
    offs = [0]
    for wd in widths:
        offs.append(offs[-1] + wd)
    qa, ka, va, iq, ik, iw, qb, kb, vb = (w_in[:, offs[j]:offs[j + 1]] for j in range(9))
    iw_pad = jnp.pad(iw, ((0, 0), (0, LANES - IDX_HEADS)))
    w = jnp.concatenate([qa, qb, kb, vb, ka, ka, va, va, iq, ik, ik, ik, ik, iw_pad], axis=1)
    mixw = B_HEADS * hd
    sections = ((0, A_HEADS * hd, hd, QK_SCALE),
                (512, mixw, hd, QK_SCALE),
                (1024, mixw, hd, None),
                (1536, mixw, None, None),
                (2048, LANES, hd, None),
                (2176, LANES, None, None),
                (2304, IDX_HEADS * IDX_DIM, IDX_DIM, None),
                (2560, LANES, IDX_DIM, None),
                (2688, LANES, None, None))
    dtypes = (BF16,) * 8 + (F32,)
    return w.astype(BF16), sections, dtypes


def kernel(x, even_w_in, even_w_out, even_ln1_g, even_ln1_b, even_w1, even_w3, even_w2, even_ln2_g, even_ln2_b, odd_w_in, odd_w_out, odd_ln1_g, odd_ln1_b, odd_router, odd_w1, odd_w3, odd_w2, odd_ln2_g, odd_ln2_b):
    batch, seq, d = x.shape
    n = batch * seq
    tables = _rope_tables(seq)
    x2d = x.reshape(n, d)
    odd_w1h, odd_w3h, odd_w2h = odd_w1.astype(BF16), odd_w3.astype(BF16), odd_w2.astype(BF16)
    for layer in range(DEPTH):
        i = layer // 2
        if layer % 2 == 0:
            w, sections, dtypes = _even_weight_layout(even_w_in[i])
            qa, qb, kb, vb, kk, vv, iq, ik4, iw, kb_mean = _project(x2d, w, tables, sections, dtypes, seq,
                                                                    mean_section=EVEN_KB_SECTION)
            o_a = _dsa_attention(iq, iw, ik4, qa, kk, vv, batch, seq)
            o_b = _moba_attention(qb, kb, vb, kb_mean, batch, seq)
            x2d, xh = _outproj_ln([o_a, o_b], even_w_out[i].astype(BF16), x2d, even_ln1_g[i], even_ln1_b[i])
            x2d = _swiglu_ln(xh, x2d, even_w1[i].astype(BF16), even_w3[i].astype(BF16), even_w2[i].astype(BF16),
                             even_ln2_g[i], even_ln2_b[i])
        else:
            mix = C_HEADS * HEAD_DIM
            sections = ((0, mix, HEAD_DIM, QK_SCALE), (mix, mix, HEAD_DIM, None), (2 * mix, mix, None, None))
            q, k, v = _project(x2d, odd_w_in[i].astype(BF16), tables, sections, (F32, F32, F32), seq)
            o = _dilated_attention(q, k, v, batch, seq)
            x2d, xh = _outproj_ln([o], odd_w_out[i].astype(BF16), x2d, odd_ln1_g[i], odd_ln1_b[i])
            rw = jnp.pad(odd_router[i], ((0, 0), (0, LANES - N_EXPERTS))).astype(BF16)
            gatec, lposc, lpost, cum = _router(xh, rw)
            ffn = _moe(_chunk_bounds(cum), xh, lpost, lposc, gatec, odd_w1h, odd_w3h, odd_w2h, i)
            x2d = _add_ln(x2d, ffn, odd_ln2_g[i], odd_ln2_b[i])
    return x2d.reshape(batch, seq, d)
```

```python
import functools

import jax
import jax.numpy as jnp
from jax import lax
from jax.experimental import pallas as pl
from jax.experimental.pallas import tpu as pltpu

F32 = jnp.float32
BF16 = jnp.bfloat16
I32 = jnp.int32

DEPTH = 4
HEAD_DIM = 64
ROPE_THETA = 10000.0
LN_EPS = 1e-5
A_HEADS = 8
IDX_HEADS = 8
IDX_DIM = 32
DSA_TOPK = 256
B_HEADS = 8
MOBA_BLOCK = 256
MOBA_TOPK = 3
C_HEADS = 16
DILATED_CFG = ((128, 1), (512, 4), (2048, 16))
BAND_BLOCK = 128
N_EXPERTS = 8
DEEPNORM_ALPHA = (2 * DEPTH) ** 0.25
QK_SCALE = HEAD_DIM ** -0.5 * 1.4426950408889634

LANES = 128
VMEM_LIMIT_BYTES = 52 * 1024 * 1024

PROJ_ROWS = 1024
FFN_ROWS = 512
FFN_CHUNKS = 1
ROUTER_ROWS = 512
MOE_GROUP = 2048
MOE_WIN = 256
MOE_FFN_ROWS = 272
MOE_ROW_ALIGN = 16
MOE_FF_CHUNK = 512
DSA_ROWS = 256
DSA_SEARCH_GROUPS = 4
DSA_SEARCH_UNROLL = 4
MOBA_GATE_ROWS = 16
DILATED_UNITS = 8
DILATED_SLABS = 4
ATTN_KEY_STEP = 512
MOBA_STEP_BLOCKS = 8

NEG_INF = float("-inf")


def _params(semantics):
    return pltpu.CompilerParams(dimension_semantics=semantics, vmem_limit_bytes=VMEM_LIMIT_BYTES)


def _dot(a, b):
    return jnp.dot(a, b, preferred_element_type=F32)


def _dot_nt(a, b):
    return lax.dot_general(a, b, (((1,), (1,)), ((), ())), preferred_element_type=F32)


def _lane_group(shape, group, idx):
    lane = lax.broadcasted_iota(I32, shape, len(shape) - 1)
    return (lane // group) == idx


def _head_pair_attention(units):
    first = _lane_group((1, LANES), HEAD_DIM, 0)
    raw, unit_masks = [], []
    for q2, k2, _, mask in units:
        raw.append(_dot_nt(jnp.where(first, q2, jnp.zeros_like(q2)), k2))
        unit_masks.append(mask() if callable(mask) else mask)
        raw.append(_dot_nt(jnp.where(first, jnp.zeros_like(q2), q2), k2))
    scores = []
    for u, mask in enumerate(unit_masks):
        masks = mask if isinstance(mask, (tuple, list)) else (mask, mask)
        for h in range(2):
            if masks[h].dtype == jnp.bool_:
                scores.append(jnp.where(masks[h], raw[2 * u + h], NEG_INF))
            else:
                scores.append(raw[2 * u + h] + masks[h])
    maxes = [jnp.max(s, axis=1, keepdims=True) for s in scores]
    weights = [jnp.exp2(s - mx).astype(BF16) for s, mx in zip(scores, maxes)]
    results = []
    for u, (_, _, v2, _) in enumerate(units):
        ones = jnp.ones_like(v2)
        pv0 = _dot(weights[2 * u], jnp.where(first, v2, ones))
        pv1 = _dot(weights[2 * u + 1], jnp.where(first, ones, v2))
        den = jnp.where(first, pv0[:, LANES - 1:LANES], pv1[:, 0:1])
        results.append((jnp.where(first, maxes[2 * u], maxes[2 * u + 1]), den, jnp.where(first, pv0, pv1)))
    return results


def _stacked_head_pair_attention(units):
    first = _lane_group((1, LANES), HEAD_DIM, 0)
    raw = [_dot_nt(jnp.concatenate([jnp.where(first, q2, jnp.zeros_like(q2)),
                                    jnp.where(first, jnp.zeros_like(q2), q2)], axis=0), k2)
           for q2, k2, _, _ in units]
    stats, weights = [], []
    for r, (q2, _, _, mask) in zip(raw, units):
        rows = q2.shape[0]
        halves = [jnp.where(mask, r[h * rows:(h + 1) * rows], NEG_INF) for h in range(2)]
        maxes = [jnp.max(s, axis=1, keepdims=True) for s in halves]
        exps = [jnp.exp2(s - mx) for s, mx in zip(halves, maxes)]
        dens = [jnp.sum(e, axis=1, keepdims=True) for e in exps]
        stats.append((jnp.where(first, maxes[0], maxes[1]), jnp.where(first, dens[0], dens[1])))
        weights.append(jnp.concatenate(exps, axis=0).astype(BF16))
    results = []
    for (mx, den), w, (q2, _, v2, _) in zip(stats, weights, units):
        rows = q2.shape[0]
        pv = _dot(w, v2)
        results.append((mx, den, jnp.where(first, pv[:rows], pv[rows:])))
    return results


def _rope_lanes(t, cosf, sinf, half):
    lane = lax.broadcasted_iota(I32, t.shape, 1)
    first = (lane % (2 * half)) < half
    swapped = jnp.where(first, pltpu.roll(t, LANES - half, 1), pltpu.roll(t, half, 1))
    return t * cosf + swapped * sinf


def _proj_kernel(x_ref, w_ref, c64_ref, s64_ref, c32_ref, s32_ref, *out_refs, sections, mean_section):
    x = x_ref[...].astype(BF16)
    for idx, (o_ref, (start, width, rope, scale)) in enumerate(zip(out_refs, sections)):
        h = _dot(x, w_ref[:, start:start + width])
        if rope is not None:
            cosf = (c64_ref if rope == HEAD_DIM else c32_ref)[...]
            sinf = (s64_ref if rope == HEAD_DIM else s32_ref)[...]
            parts = [_rope_lanes(h[:, g * LANES:(g + 1) * LANES], cosf, sinf, rope // 2)
                     for g in range(width // LANES)]
            h = parts[0] if len(parts) == 1 else jnp.concatenate(parts, axis=1)
        if scale is not None:
            h = h * scale
        o_ref[...] = h.astype(o_ref.dtype)
        if idx == mean_section:
            m_ref = out_refs[len(sections)]
            for blk in range(h.shape[0] // MOBA_BLOCK):
                rows = h[blk * MOBA_BLOCK:(blk + 1) * MOBA_BLOCK]
                m_ref[0, blk:blk + 1, :] = jnp.sum(rows, axis=0, keepdims=True) * (1.0 / MOBA_BLOCK)


def _project(x2d, w, tables, sections, out_dtypes, seq, mean_section=None):
    n, d = x2d.shape
    tm = PROJ_ROWS
    pos_blocks = seq // tm
    tab_spec = pl.BlockSpec((tm, LANES), lambda i: (i % pos_blocks, 0))
    out_specs = [pl.BlockSpec((tm, sec[1]), lambda i: (i, 0)) for sec in sections]
    out_shape = [jax.ShapeDtypeStruct((n, sec[1]), dt) for sec, dt in zip(sections, out_dtypes)]
    if mean_section is not None:
        per_tile, width = tm // MOBA_BLOCK, sections[mean_section][1]
        out_specs.append(pl.BlockSpec((1, per_tile, width), lambda i: (i, 0, 0)))
        out_shape.append(jax.ShapeDtypeStruct((n // tm, per_tile, width), F32))
    outs = pl.pallas_call(
        functools.partial(_proj_kernel, sections=sections, mean_section=mean_section),
        grid=(n // tm,),
        in_specs=[pl.BlockSpec((tm, d), lambda i: (i, 0)),
                  pl.BlockSpec(w.shape, lambda i: (0, 0)),
                  tab_spec, tab_spec, tab_spec, tab_spec],
        out_specs=out_specs,
        out_shape=out_shape,
        compiler_params=_params(("parallel",)),
    )(x2d, w, *tables)
    outs = list(outs)
    if mean_section is not None:
        outs[-1] = outs[-1].reshape(n // MOBA_BLOCK, -1)
    return outs


def _rope_tables(seq):
    out = []
    for dim in (HEAD_DIM, IDX_DIM):
        inv = ROPE_THETA ** (-jnp.arange(0, dim, 2, dtype=F32) / dim)
        ang = jnp.arange(seq, dtype=F32)[:, None] * inv[None, :]
        cos, sin = jnp.cos(ang), jnp.sin(ang)
        reps = LANES // dim
        out.append(jnp.tile(jnp.concatenate([cos, cos], -1), (1, reps)))
        out.append(jnp.tile(jnp.concatenate([-sin, sin], -1), (1, reps)))
    return out


def _dsa_block(i, iq_ref, iw_ref, ik_ref, q_ref, kk_ref, vv_ref, o_ref, key_ref, sel_ref, *, kext, topk):
    qb = DSA_ROWS
    keys = slice(0, kext)
    qpos = i * qb + lax.broadcasted_iota(I32, (qb, 1), 0)
    kpos = lax.broadcasted_iota(I32, (1, kext), 1)
    causal = kpos <= qpos

    ik = ik_ref[keys, :]
    iw = iw_ref[...] * (IDX_DIM ** -0.5 * IDX_HEADS ** -0.5)
    heads_per_group = LANES // IDX_DIM
    score = jnp.zeros((qb, kext), F32)
    for h in range(IDX_HEADS):
        grp, j = divmod(h, heads_per_group)
        iqg = iq_ref[:, grp * LANES:(grp + 1) * LANES]
        iqm = jnp.where(_lane_group((1, LANES), IDX_DIM, j), iqg, jnp.zeros_like(iqg))
        rel = jnp.maximum(_dot_nt(iqm, ik), 0.0)
        score = score + rel * iw[:, h:h + 1]
    bits = pltpu.bitcast(score, I32)
    neg = bits >> 31
    int_min = jnp.int32(-2 ** 31)
    key_ref[:, keys] = jnp.where(causal, (bits ^ (neg & 0x7FFFFFFF)) - neg, int_min)

    grp_rows = qb // DSA_SEARCH_GROUPS
    groups = [slice(g * grp_rows, (g + 1) * grp_rows) for g in range(DSA_SEARCH_GROUPS)]

    def count_ge(rows, t):
        return jnp.sum(jnp.where(key_ref[rows, keys] >= t, 1.0, 0.0), axis=1, keepdims=True)

    def search(it, thrs):
        bit = jnp.left_shift(jnp.int32(1), 30 - it)
        return tuple(jnp.where(count_ge(rows, thr | bit) >= topk, thr | bit, thr)
                     for rows, thr in zip(groups, thrs))

    zero = jnp.zeros((grp_rows, 1), I32)
    thrs = tuple(jnp.where(count_ge(rows, zero) >= topk, zero, int_min) for rows in groups)
    thrs = lax.fori_loop(0, 31, search, thrs, unroll=DSA_SEARCH_UNROLL)
    thr = jnp.concatenate(thrs, axis=0)

    floor = jnp.maximum(thr, int_min + 1)
    key = key_ref[:, keys]
    sel_ref[:, keys] = jnp.where(key >= floor, 0.0, NEG_INF)
    n_ge = jnp.sum(jnp.where(key >= thr, 1.0, 0.0), axis=1, keepdims=True)
    tie_rows = (n_ge > topk) & (thr > int_min)

    @pl.when(jnp.max(tie_rows.astype(I32)) > 0)
    def _break_ties():
        upper = (lax.broadcasted_iota(I32, (LANES, LANES), 0)
                 < lax.broadcasted_iota(I32, (LANES, LANES), 1))
        upper = jnp.where(upper, 1.0, 0.0).astype(BF16)
        n_gt = jnp.sum(jnp.where(key_ref[:, keys] > thr, 1.0, 0.0), axis=1, keepdims=True)
        need = topk - n_gt
        before = jnp.zeros((qb, 1), F32)
        for c in range(kext // LANES):
            sl = slice(c * LANES, (c + 1) * LANES)
            keyc = key_ref[:, sl]
            eqc = keyc == thr
            eqf = jnp.where(eqc, 1.0, 0.0)
            rank = _dot(eqf.astype(BF16), upper) + before
            take = ((keyc > thr) | (eqc & (rank < need))) & (keyc > int_min)
            sel_ref[:, sl] = jnp.where(take, 0.0, NEG_INF)
            before = before + jnp.sum(eqf, axis=1, keepdims=True)

    selected = sel_ref[:, keys]
    kk = kk_ref[keys, :]
    vv = vv_ref[keys, :]
    pairs = [(q_ref[:, hp * LANES:(hp + 1) * LANES], kk, vv, selected) for hp in range(A_HEADS // 2)]
    for hp, (_, den, num) in enumerate(_head_pair_attention(pairs)):
        o_ref[:, hp * LANES:(hp + 1) * LANES] = (num / den).astype(o_ref.dtype)


def _dsa_kernel(*refs, seq, topk):
    i = pl.program_id(1)
    blocks_per_step = ATTN_KEY_STEP // DSA_ROWS
    for v in range(seq // ATTN_KEY_STEP):
        @pl.when(i // blocks_per_step == v)
        def _(v=v):
            _dsa_block(i, *refs, kext=(v + 1) * ATTN_KEY_STEP, topk=topk)


def _dsa_attention(iq, iw, ik4, qa, kk, vv, batch, seq):
    n = batch * seq
    qb = DSA_ROWS
    nq = seq // qb
    topk = min(DSA_TOPK, seq // 4)
    rows = lambda width: pl.BlockSpec((qb, width), lambda b, i: (b * nq + i, 0))
    whole = pl.BlockSpec((seq, LANES), lambda b, i: (b, 0))
    return pl.pallas_call(
        functools.partial(_dsa_kernel, seq=seq, topk=topk),
        grid=(batch, nq),
        in_specs=[rows(iq.shape[1]), rows(LANES), whole, rows(qa.shape[1]), whole, whole],
        out_specs=rows(qa.shape[1]),
        out_shape=jax.ShapeDtypeStruct((n, qa.shape[1]), BF16),
        scratch_shapes=[pltpu.VMEM((qb, seq), I32), pltpu.VMEM((qb, seq), F32)],
        compiler_params=_params(("parallel", "parallel")),
    )(iq, iw, ik4, qa, kk, vv)


def _moba_step(step, q_ref, k_ref, v_ref, kmean_ref, o_ref, *, kext, kt):
    bs, per = MOBA_BLOCK, MOBA_STEP_BLOCKS
    nb_all = kmean_ref.shape[0]
    nbp = max(nb_all, MOBA_GATE_ROWS)
    kmean = jnp.concatenate([kmean_ref[...], jnp.zeros((nbp - nb_all, LANES), F32)], axis=0).astype(BF16)
    n_iota = lax.broadcasted_iota(I32, (nbp, 1), 0)
    qrow = lax.broadcasted_iota(I32, (bs, 1), 0)
    units = []
    for j in range(per):
        qi = step * per + j
        ext = kext - (per - 1 - j) * bs
        nb = ext // bs
        q2 = q_ref[j * bs:(j + 1) * bs, :]
        blk_row = lax.broadcasted_iota(I32, (nbp, ext), 0)
        blk_of_key = lax.broadcasted_iota(I32, (nbp, ext), 1) // bs
        member = jnp.where(blk_row == blk_of_key, 1.0, 0.0).astype(BF16)
        past = n_iota < qi
        kcol = lax.broadcasted_iota(I32, (1, ext), 1)
        own = (kcol >= qi * bs) & (kcol <= qi * bs + qrow)
        gates = [jnp.where(past, _dot_nt(kmean, jnp.where(_lane_group((1, LANES), HEAD_DIM, half), q2,
                                                          jnp.zeros_like(q2))), NEG_INF)
                 for half in range(2)]

        def block_masks(gates=gates, past=past, own=own, member=member, nb=nb):
            chosen = []
            for gate in gates:
                rank = jnp.zeros((nbp, bs), I32)
                for m in range(nb):
                    gm = gate[m:m + 1, :]
                    beats = (gm > gate) | ((gm == gate) & (m < n_iota))
                    rank = rank + beats.astype(I32)
                chosen.append(jnp.transpose(jnp.where(past & (rank < kt), 1.0, 0.0)).astype(BF16))
            return [(_dot(c, member) > 0.5) | own for c in chosen]

        units.append((q2, k_ref[0:ext, :], v_ref[0:ext, :], block_masks))
    for j, (_, den, num) in enumerate(_head_pair_attention(units)):
        o_ref[j * bs:(j + 1) * bs, :] = (num / den).astype(o_ref.dtype)


def _moba_kernel(*refs, seq):
    step = pl.program_id(2)
    kt = min(MOBA_TOPK, seq // MOBA_BLOCK - 1)
    rows = MOBA_BLOCK * MOBA_STEP_BLOCKS
    for v in range(seq // rows):
        @pl.when(step == v)
        def _(v=v):
            _moba_step(step, *refs, kext=(v + 1) * rows, kt=kt)


def _moba_attention(q, k, v, kmean, batch, seq):
    n, width = q.shape
    tq = MOBA_BLOCK * MOBA_STEP_BLOCKS
    steps = seq // tq
    rows = pl.BlockSpec((tq, LANES), lambda b, hp, st: (b * steps + st, hp))
    whole = pl.BlockSpec((seq, LANES), lambda b, hp, st: (b, hp))
    means = pl.BlockSpec((seq // MOBA_BLOCK, LANES), lambda b, hp, st: (b, hp))
    return pl.pallas_call(
        functools.partial(_moba_kernel, seq=seq),
        grid=(batch, width // LANES, steps),
        in_specs=[rows, whole, whole, means],
        out_specs=rows,
        out_shape=jax.ShapeDtypeStruct((n, width), BF16),
        compiler_params=_params(("parallel", "parallel", "parallel")),
    )(q, k, v, kmean)


def _rows(start, size, stride):
    return pl.ds(start, size) if stride == 1 else pl.ds(start, size, stride=stride)


def _largest_divisor(n, limit):
    return max(d for d in range(1, limit + 1) if n % d == 0)


def _repeat(trips, body):
    if trips == 1:
        body(0, 0)
    else:
        lax.fori_loop(0, trips, body, 0)


def _dilated_kernel(q_ref, k_ref, v_ref, o_ref, qp_ref, kp_ref, vp_ref, nat_ref, slab_ref, *, seq):
    blk = BAND_BLOCK
    ns = DILATED_SLABS
    slab_len = seq // ns

    for s in range(ns):
        rows = slice(s * slab_len, (s + 1) * slab_len)
        qp_ref[rows, :] = q_ref[pl.ds(s, slab_len, stride=ns), :]
        kp_ref[rows, :] = k_ref[pl.ds(s, slab_len, stride=ns), :]
        vp_ref[rows, :] = v_ref[pl.ds(s, slab_len, stride=ns), :]

    def operands(srcs, qstart, kstart, nk, stride):
        ksl = _rows(kstart, nk, stride)
        i = lax.broadcasted_iota(I32, (blk, 1), 0)
        j = lax.broadcasted_iota(I32, (1, nk), 1)
        if nk == blk:
            mask = j <= i
        else:
            mask = (j >= i) & (j <= i + blk)
        return (srcs[0][_rows(qstart, blk, stride), :].astype(BF16), srcs[1][ksl, :].astype(BF16),
                srcs[2][ksl, :].astype(BF16), mask)

    def merged(old, new):
        (m_old, l_old, a_old), (m_new, l_new, a_new) = old, new
        m_tot = jnp.maximum(m_old, m_new)
        w_old = jnp.exp2(m_old - m_tot)
        w_new = jnp.exp2(m_new - m_tot)
        return m_tot, w_old * l_old + w_new * l_new, w_old * a_old + w_new * a_new

    def run_units(srcs, state_ref, units, nk, stride, merge):
        results = _stacked_head_pair_attention([operands(srcs, qstart, kstart, nk, stride)
                                                for qstart, kstart in units])
        for (qstart, _), new in zip(units, results):
            qsl = _rows(qstart, blk, stride)
            if merge:
                new = merged(tuple(state_ref[t, qsl, :] for t in range(3)), new)
            for t in range(3):
                state_ref[t, qsl, :] = new[t]

    written = set()
    for window, dil in DILATED_CFG:
        assert window // dil == blk
        if dil % ns == 0:
            srcs, state_ref, stride, group_len = (qp_ref, kp_ref, vp_ref), slab_ref, dil // ns, slab_len
        else:
            srcs, state_ref, stride, group_len = (q_ref, k_ref, v_ref), nat_ref, dil, seq
        merge = id(state_ref) in written
        written.add(id(state_ref))
        nblk = seq // dil // blk
        ub = _largest_divisor(dil, DILATED_UNITS)

        def class_start(c, stride=stride, group_len=group_len):
            return (c // stride) * group_len + c % stride if group_len != seq else c

        def first_blocks(it, carry, srcs=srcs, state_ref=state_ref, stride=stride, merge=merge, ub=ub,
                         class_start=class_start):
            starts = [class_start(it * ub + u) for u in range(ub)]
            run_units(srcs, state_ref, [(st, st) for st in starts], blk, stride, merge)
            return carry

        _repeat(dil // ub, first_blocks)
        if nblk > 1:
            later = nblk - 1
            ul = _largest_divisor(dil * later, DILATED_UNITS)

            def later_blocks(it, carry, srcs=srcs, state_ref=state_ref, stride=stride, merge=merge, ul=ul,
                             later=later, class_start=class_start):
                flat = [it * ul + u for u in range(ul)]
                qstarts = [class_start(t // later) + stride * blk * (1 + t % later) for t in flat]
                run_units(srcs, state_ref, [(qs, qs - stride * blk) for qs in qstarts], 2 * blk, stride, merge)
                return carry

            _repeat(dil * later // ul, later_blocks)

    for s in range(ns):
        nat_rows = pl.ds(s, slab_len, stride=ns)
        slab_rows = slice(s * slab_len, (s + 1) * slab_len)
        _, den, num = merged(tuple(nat_ref[t, nat_rows, :] for t in range(3)),
                             tuple(slab_ref[t, slab_rows, :] for t in range(3)))
        nat_ref[2, nat_rows, :] = num / den
    o_ref[...] = nat_ref[2].astype(o_ref.dtype)


def _dilated_attention(q, k, v, batch, seq):
    n, width = q.shape
    for window, dil in DILATED_CFG:
        assert seq % (dil * BAND_BLOCK) == 0
    assert {dil % DILATED_SLABS == 0 for _, dil in DILATED_CFG} == {True, False}
    whole = pl.BlockSpec((seq, LANES), lambda b, hp: (b, hp))
    return pl.pallas_call(
        functools.partial(_dilated_kernel, seq=seq),
        grid=(batch, width // LANES),
        in_specs=[whole, whole, whole],
        out_specs=whole,
        out_shape=jax.ShapeDtypeStruct((n, width), BF16),
        scratch_shapes=[pltpu.VMEM((seq, LANES), F32)] * 3 + [pltpu.VMEM((3, seq, LANES), F32)] * 2,
        compiler_params=_params(("parallel", "parallel")),
    )(q, k, v)


def _deepnorm(x, f, g, b):
    y = DEEPNORM_ALPHA * x + f
    mu = jnp.mean(y, axis=1, keepdims=True)
    yc = y - mu
    var = jnp.mean(yc * yc, axis=1, keepdims=True)
    return yc * lax.rsqrt(var + LN_EPS) * g + b


def _outproj_ln_kernel(*refs, n_parts):
    part_refs = refs[:n_parts]
    w_ref, x_ref, g_ref, b_ref, o_ref, oh_ref = refs[n_parts:]
    mix = None
    start = 0
    for p_ref in part_refs:
        width = p_ref.shape[1]
        t = _dot(p_ref[...], w_ref[start:start + width, :])
        mix = t if mix is None else mix + t
        start += width
    y = _deepnorm(x_ref[...], mix, g_ref[...], b_ref[...])
    o_ref[...] = y
    oh_ref[...] = y.astype(BF16)


def _outproj_ln(parts, w, x2d, g, b):
    n, d = x2d.shape
    tm = PROJ_ROWS
    row = lambda width: pl.BlockSpec((tm, width), lambda i: (i, 0))
    vec = pl.BlockSpec((1, d), lambda i: (0, 0))
    return pl.pallas_call(
        functools.partial(_outproj_ln_kernel, n_parts=len(parts)),
        grid=(n // tm,),
        in_specs=[row(p.shape[1]) for p in parts] + [pl.BlockSpec(w.shape, lambda i: (0, 0)), row(d), vec, vec],
        out_specs=[row(d), row(d)],
        out_shape=[jax.ShapeDtypeStruct((n, d), F32), jax.ShapeDtypeStruct((n, d), BF16)],
        compiler_params=_params(("parallel",)),
    )(*parts, w, x2d, g.reshape(1, d), b.reshape(1, d))


def _silu(h):
    return h * (1.0 / (1.0 + jnp.exp(-h)))


def _swiglu_ln_kernel(xh_ref, x_ref, w1_ref, w3_ref, w2_ref, g_ref, b_ref, o_ref, acc_ref):
    f = pl.program_id(1)
    xh = xh_ref[...]
    hid = (_silu(_dot(xh, w1_ref[...])) * _dot(xh, w3_ref[...])).astype(BF16)
    y = _dot(hid, w2_ref[...])

    @pl.when(f == 0)
    def _():
        acc_ref[...] = y

    @pl.when(f > 0)
    def _():
        acc_ref[...] += y

    @pl.when(f == pl.num_programs(1) - 1)
    def _():
        o_ref[...] = _deepnorm(x_ref[...], acc_ref[...], g_ref[...], b_ref[...])


def _swiglu_ln(xh, x2d, w1, w3, w2, g, b):
    n, d = x2d.shape
    dff = w1.shape[1]
    tm = FFN_ROWS
    nf = FFN_CHUNKS
    fc = dff // nf
    row = lambda: pl.BlockSpec((tm, d), lambda i, f: (i, 0))
    vec = pl.BlockSpec((1, d), lambda i, f: (0, 0))
    mode = pl.Buffered(1) if nf == 1 else None
    return pl.pallas_call(
        _swiglu_ln_kernel,
        grid=(n // tm, nf),
        in_specs=[row(), row(),
                  pl.BlockSpec((d, fc), lambda i, f: (0, f), pipeline_mode=mode),
                  pl.BlockSpec((d, fc), lambda i, f: (0, f), pipeline_mode=mode),
                  pl.BlockSpec((fc, d), lambda i, f: (f, 0), pipeline_mode=mode),
                  vec, vec],
        out_specs=row(),
        out_shape=jax.ShapeDtypeStruct((n, d), F32),
        scratch_shapes=[pltpu.VMEM((tm, d), F32)],
        compiler_params=_params(("parallel", "arbitrary")),
    )(xh, x2d, w1, w3, w2, g.reshape(1, d), b.reshape(1, d))


def _add_ln_kernel(x_ref, f_ref, g_ref, b_ref, o_ref):
    o_ref[...] = _deepnorm(x_ref[...], f_ref[...], g_ref[...], b_ref[...])


def _add_ln(x2d, f2d, g, b):
    n, d = x2d.shape
    tm = PROJ_ROWS
    row = pl.BlockSpec((tm, d), lambda i: (i, 0))
    vec = pl.BlockSpec((1, d), lambda i: (0, 0))
    return pl.pallas_call(
        _add_ln_kernel,
        grid=(n // tm,),
        in_specs=[row, row, vec, vec],
        out_specs=row,
        out_shape=jax.ShapeDtypeStruct((n, d), F32),
        compiler_params=_params(("parallel",)),
    )(x2d, f2d, g.reshape(1, d), b.reshape(1, d))


def _router_kernel(xh_ref, rw_ref, gate_ref, lposc_ref, lpost_ref, cum_ref, carry_ref, *, tiles_per_group):
    tm = xh_ref.shape[0]
    i = pl.program_id(0)

    @pl.when(i % tiles_per_group == 0)
    def _():
        carry_ref[...] = jnp.zeros_like(carry_ref)

    lane = lax.broadcasted_iota(I32, (1, LANES), 1)
    logits = jnp.where(lane < N_EXPERTS, _dot(xh_ref[...], rw_ref[...]), NEG_INF)
    m1 = jnp.max(logits, axis=1, keepdims=True)
    i1 = jnp.min(jnp.where(logits == m1, lane, LANES), axis=1, keepdims=True)
    rest = jnp.where(lane == i1, NEG_INF, logits)
    m2 = jnp.max(rest, axis=1, keepdims=True)
    i2 = jnp.min(jnp.where(rest == m2, lane, LANES), axis=1, keepdims=True)
    e2 = jnp.exp(m2 - m1)
    g1 = 1.0 / (1.0 + e2)
    g2 = e2 / (1.0 + e2)
    sel1 = lane == i1
    sel2 = lane == i2
    routed = sel1 | sel2
    gate_ref[...] = jnp.where(sel1, g1, jnp.where(sel2, g2, 0.0))
    routedf = jnp.where(routed, 1.0, 0.0)
    earlier = (lax.broadcasted_iota(I32, (tm, tm), 1) < lax.broadcasted_iota(I32, (tm, tm), 0))
    earlier = jnp.where(earlier, 1.0, 0.0).astype(BF16)
    carry = carry_ref[...]
    lpos = jnp.where(routed, _dot(earlier, routedf.astype(BF16)) + carry, -1.0)
    lposc_ref[...] = lpos
    lpost_ref[...] = jnp.transpose(lpos)[0:N_EXPERTS, :]
    carry = carry + jnp.sum(routedf, axis=0, keepdims=True)
    carry_ref[...] = carry
    cum_ref[...] = carry.astype(I32).reshape(1, 1, LANES)


def _router(xh, rw):
    n, d = xh.shape
    tm = ROUTER_ROWS
    nt = n // tm
    row = pl.BlockSpec((tm, LANES), lambda i: (i, 0))
    return pl.pallas_call(
        functools.partial(_router_kernel, tiles_per_group=MOE_GROUP // tm),
        grid=(nt,),
        in_specs=[pl.BlockSpec((tm, d), lambda i: (i, 0)), pl.BlockSpec(rw.shape, lambda i: (0, 0))],
        out_specs=[row, row,
                   pl.BlockSpec((N_EXPERTS, tm), lambda i: (0, i)),
                   pl.BlockSpec((1, 1, LANES), lambda i: (i, 0, 0))],
        out_shape=[jax.ShapeDtypeStruct((n, LANES), F32), jax.ShapeDtypeStruct((n, LANES), F32),
                   jax.ShapeDtypeStruct((N_EXPERTS, n), F32), jax.ShapeDtypeStruct((nt, 1, LANES), I32)],
        scratch_shapes=[pltpu.VMEM((1, LANES), F32)],
        compiler_params=_params(("arbitrary",)),
    )(xh, rw)


def _moe_kernel(bnd_ref, xh_ref, lpost_ref, lposc_ref, gatec_ref, w1_ref, w3_ref, w2_ref,
                o_ref, xg_ref, yacc_ref):
    tg = xh_ref.shape[0]
    win, ch, fr, align = MOE_WIN, ROUTER_ROWS, MOE_FFN_ROWS, MOE_ROW_ALIGN
    nch = tg // ch
    g, e, f = pl.program_id(0), pl.program_id(1), pl.program_id(2)
    nf = pl.num_programs(2)

    def bound(c):
        return bnd_ref[(g * (nch + 1) + c) * N_EXPERTS + e]

    cnt = bound(nch)
    n_ffn = (cnt + fr - 1) // fr

    def tok_chunk(c):
        return pl.ds(pl.multiple_of(c * ch, ch), ch)

    def windows(c):
        r0, r1 = bound(c), bound(c + 1)
        first = (r0 // align) * align
        return first, jnp.where(r1 > r0, (r1 - first + win - 1) // win, 0)

    one_window = functools.reduce(jnp.logical_and, [windows(c)[1] <= 1 for c in range(nch)])

    @pl.when((e == 0) & (f == 0))
    def _():
        o_ref[...] = jnp.zeros_like(o_ref)

    @pl.when(f == 0)
    def _compact():
        def clear(s, carry):
            rows = pl.ds(pl.multiple_of(s * win, win), win)
            xg_ref[rows, :] = jnp.zeros((win, xg_ref.shape[1]), xg_ref.dtype)
            yacc_ref[rows, :] = jnp.zeros((win, yacc_ref.shape[1]), F32)
            return carry

        lax.fori_loop(0, (cnt + max(fr, win) + win - 1) // win, clear, 0)

        def per_chunk(c, carry):
            first, n_win = windows(c)
            lp = lpost_ref[pl.ds(e, 1), tok_chunk(c)]

            def per_window(w, carry2):
                start = pl.multiple_of(first + w * win, align)
                want = (lax.broadcasted_iota(I32, (win, 1), 0) + start).astype(F32)
                pick = jnp.where(lp == want, 1.0, 0.0).astype(BF16)
                rows = pl.ds(start, win)
                xg_ref[rows, :] += _dot(pick, xh_ref[tok_chunk(c), :]).astype(xg_ref.dtype)
                return carry2

            return lax.fori_loop(0, n_win, per_window, carry)

        @pl.when(one_window)
        def _():
            for c in range(nch):
                toks = slice(c * ch, (c + 1) * ch)
                start = pl.multiple_of(windows(c)[0], align)
                want = (lax.broadcasted_iota(I32, (win, 1), 0) + start).astype(F32)
                pick = jnp.where(lpost_ref[pl.ds(e, 1), toks] == want, 1.0, 0.0).astype(BF16)
                xg_ref[pl.ds(start, win), :] += _dot(pick, xh_ref[toks, :]).astype(xg_ref.dtype)

        @pl.when(jnp.logical_not(one_window))
        def _():
            lax.fori_loop(0, nch, per_chunk, 0)

    def ffn_rows(rows):
        xs = xg_ref[rows, :]
        hid = (_silu(_dot(xs, w1_ref[...])) * _dot(xs, w3_ref[...])).astype(BF16)
        yacc_ref[rows, :] += _dot(hid, w2_ref[...])

    def ffn_pair(p, carry):
        ffn_rows(pl.ds(pl.multiple_of(p * 2 * fr, align), 2 * fr))
        return carry

    lax.fori_loop(0, n_ffn // 2, ffn_pair, 0)

    @pl.when(n_ffn % 2 == 1)
    def _():
        ffn_rows(pl.ds(pl.multiple_of((n_ffn - 1) * fr, align), fr))

    @pl.when(f == nf - 1)
    def _scatter():
        lane = lax.broadcasted_iota(I32, (1, LANES), 1)

        def per_chunk(c, carry):
            first, n_win = windows(c)
            toks = tok_chunk(c)
            lp = jnp.sum(jnp.where(lane == e, lposc_ref[toks, :], 0.0), axis=1, keepdims=True)
            gt = jnp.sum(jnp.where(lane == e, gatec_ref[toks, :], 0.0), axis=1, keepdims=True)

            def per_window(w, carry2):
                start = pl.multiple_of(first + w * win, align)
                y = yacc_ref[pl.ds(start, win), :]
                y_hi = y.astype(BF16)
                y_lo = (y - y_hi.astype(F32)).astype(BF16)
                want = (lax.broadcasted_iota(I32, (1, win), 1) + start).astype(F32)
                place = jnp.where(lp == want, 1.0, 0.0).astype(BF16)
                o_ref[toks, :] += gt * (_dot(place, y_hi) + _dot(place, y_lo))
                return carry2

            return lax.fori_loop(0, n_win, per_window, carry)

        @pl.when(one_window)
        def _():
            for c in range(nch):
                toks = slice(c * ch, (c + 1) * ch)
                start = pl.multiple_of(windows(c)[0], align)
                lp = jnp.sum(jnp.where(lane == e, lposc_ref[toks, :], 0.0), axis=1, keepdims=True)
                gt = jnp.sum(jnp.where(lane == e, gatec_ref[toks, :], 0.0), axis=1, keepdims=True)
                y = yacc_ref[pl.ds(start, win), :]
                y_hi = y.astype(BF16)
                y_lo = (y - y_hi.astype(F32)).astype(BF16)
                want = (lax.broadcasted_iota(I32, (1, win), 1) + start).astype(F32)
                place = jnp.where(lp == want, 1.0, 0.0).astype(BF16)
                o_ref[toks, :] += gt * (_dot(place, y_hi) + _dot(place, y_lo))

        @pl.when(jnp.logical_not(one_window))
        def _():
            lax.fori_loop(0, nch, per_chunk, 0)


def _chunk_bounds(cum):
    nch = MOE_GROUP // ROUTER_ROWS
    per_group = cum[:, 0, :N_EXPERTS].reshape(-1, nch, N_EXPERTS)
    return jnp.pad(per_group, ((0, 0), (1, 0), (0, 0))).reshape(-1)


def _moe(bounds, xh, lpost, lposc, gatec, w1, w3, w2, layer):
    n, d = xh.shape
    tg = MOE_GROUP
    dff = w1.shape[3]
    fc = MOE_FF_CHUNK
    cap = -(-(tg + max(MOE_FFN_ROWS, MOE_WIN)) // MOE_WIN) * MOE_WIN
    grid_spec = pltpu.PrefetchScalarGridSpec(
        num_scalar_prefetch=1,
        grid=(n // tg, N_EXPERTS, dff // fc),
        in_specs=[pl.BlockSpec((tg, d), lambda g, e, f, c: (g, 0)),
                  pl.BlockSpec((N_EXPERTS, tg), lambda g, e, f, c: (0, g)),
                  pl.BlockSpec((tg, LANES), lambda g, e, f, c: (g, 0)),
                  pl.BlockSpec((tg, LANES), lambda g, e, f, c: (g, 0)),
                  pl.BlockSpec((None, None, d, fc), lambda g, e, f, c: (layer, e, 0, f)),
                  pl.BlockSpec((None, None, d, fc), lambda g, e, f, c: (layer, e, 0, f)),
                  pl.BlockSpec((None, None, fc, d), lambda g, e, f, c: (layer, e, f, 0))],
        out_specs=pl.BlockSpec((tg, d), lambda g, e, f, c: (g, 0)),
        scratch_shapes=[pltpu.VMEM((cap, d), BF16), pltpu.VMEM((cap, d), F32)],
    )
    return pl.pallas_call(
        _moe_kernel,
        grid_spec=grid_spec,
        out_shape=jax.ShapeDtypeStruct((n, d), F32),
        compiler_params=_params(("parallel", "arbitrary", "arbitrary")),
    )(bounds, xh, lpost, lposc, gatec, w1, w3, w2)


EVEN_KB_SECTION = 2


def _even_weight_layout(w_in):
    hd = HEAD_DIM
    widths = (A_HEADS * hd, hd, hd, IDX_HEADS * IDX_DIM, IDX_DIM, IDX_HEADS, B_HEADS * hd, B_HEADS * hd, B_HEADS * hd)
    offs = [0]
    for wd in widths:
        offs.append(offs[-1] + wd)
    qa, ka, va, iq, ik, iw, qb, kb, vb = (w_in[:, offs[j]:offs[j + 1]] for j in range(9))
    iw_pad = jnp.pad(iw, ((0, 0), (0, LANES - IDX_HEADS)))
    w = jnp.concatenate([qa, qb, kb, vb, ka, ka, va, va, iq, ik, ik, ik, ik, iw_pad], axis=1)
    mixw = B_HEADS * hd
    sections = ((0, A_HEADS * hd, hd, QK_SCALE),
                (512, mixw, hd, QK_SCALE),
                (1024, mixw, hd, None),
                (1536, mixw, None, None),
                (2048, LANES, hd, None),
                (2176, LANES, None, None),
                (2304, IDX_HEADS * IDX_DIM, IDX_DIM, None),
                (2560, LANES, IDX_DIM, None),
                (2688, LANES, None, None))
    dtypes = (BF16,) * 8 + (F32,)
    return w.astype(BF16), sections, dtypes


def kernel(x, even_w_in, even_w_out, even_ln1_g, even_ln1_b, even_w1, even_w3, even_w2, even_ln2_g, even_ln2_b, odd_w_in, odd_w_out, odd_ln1_g, odd_ln1_b, odd_router, odd_w1, odd_w3, odd_w2, odd_ln2_g, odd_ln2_b):
    batch, seq, d = x.shape
    n = batch * seq
    tables = _rope_tables(seq)
    x2d = x.reshape(n, d)
    odd_w1h, odd_w3h, odd_w2h = odd_w1.astype(BF16), odd_w3.astype(BF16), odd_w2.astype(BF16)
    for layer in range(DEPTH):
        i = layer // 2
        if layer % 2 == 0:
            w, sections, dtypes = _even_weight_layout(even_w_in[i])
            qa, qb, kb, vb, kk, vv, iq, ik4, iw, kb_mean = _project(x2d, w, tables, sections, dtypes, seq,
                                                                    mean_section=EVEN_KB_SECTION)
            o_a = _dsa_attention(iq, iw, ik4, qa, kk, vv, batch, seq)
            o_b = _moba_attention(qb, kb, vb, kb_mean, batch, seq)
            x2d, xh = _outproj_ln([o_a, o_b], even_w_out[i].astype(BF16), x2d, even_ln1_g[i], even_ln1_b[i])
            x2d = _swiglu_ln(xh, x2d, even_w1[i].astype(BF16), even_w3[i].astype(BF16), even_w2[i].astype(BF16),
                             even_ln2_g[i], even_ln2_b[i])
        else:
            mix = C_HEADS * HEAD_DIM
            sections = ((0, mix, HEAD_DIM, QK_SCALE), (mix, mix, HEAD_DIM, None), (2 * mix, mix, None, None))
            q, k, v = _project(x2d, odd_w_in[i].astype(BF16), tables, sections, (F32, F32, F32), seq)
            o = _dilated_attention(q, k, v, batch, seq)
            x2d, xh = _outproj_ln([o], odd_w_out[i].astype(BF16), x2d, odd_ln1_g[i], odd_ln1_b[i])
            rw = jnp.pad(odd_router[i], ((0, 0), (0, LANES - N_EXPERTS))).astype(BF16)
            gatec, lposc, lpost, cum = _router(xh, rw)
            ffn = _moe(_chunk_bounds(cum), xh, lpost, lposc, gatec, odd_w1h, odd_w3h, odd_w2h, i)
            x2d = _add_ln(x2d, ffn, odd_ln2_g[i], odd_ln2_b[i])
    return x2d.reshape(batch, seq, d)
```

```python
import functools

import jax
import jax.numpy as jnp
from jax import lax
from jax.experimental import pallas as pl
from jax.experimental.pallas import tpu as pltpu

F32 = jnp.float32
BF16 = jnp.bfloat16
I32 = jnp.int32

DEPTH = 4
HEAD_DIM = 64
ROPE_THETA = 10000.0
LN_EPS = 1e-5
A_HEADS = 8
IDX_HEADS = 8
IDX_DIM = 32
DSA_TOPK = 256
B_HEADS = 8
MOBA_BLOCK = 256
MOBA_TOPK = 3
C_HEADS = 16
DILATED_CFG = ((128, 1), (512, 4), (2048, 16))
BAND_BLOCK = 128
N_EXPERTS = 8
DEEPNORM_ALPHA = (2 * DEPTH) ** 0.25
QK_SCALE = HEAD_DIM ** -0.5 * 1.4426950408889634

LANES = 128
VMEM_LIMIT_BYTES = 52 * 1024 * 1024

PROJ_ROWS = 1024
FFN_ROWS = 512
FFN_CHUNKS = 1
ROUTER_ROWS = 512
MOE_GROUP = 2048
MOE_WIN = 256
MOE_FFN_ROWS = 272
MOE_ROW_ALIGN = 16
MOE_FF_CHUNK = 512
DSA_ROWS = 256
DSA_SEARCH_GROUPS = 4
DSA_SEARCH_UNROLL = 4
MOBA_GATE_ROWS = 16
DILATED_UNITS = 16
DILATED_SLABS = 4
ATTN_KEY_STEP = 512
MOBA_STEP_BLOCKS = 8

NEG_INF = float("-inf")


def _params(semantics):
    return pltpu.CompilerParams(dimension_semantics=semantics, vmem_limit_bytes=VMEM_LIMIT_BYTES)


def _dot(a, b):
    return jnp.dot(a, b, preferred_element_type=F32)


def _dot_nt(a, b):
    return lax.dot_general(a, b, (((1,), (1,)), ((), ())), preferred_element_type=F32)


def _lane_group(shape, group, idx):
    lane = lax.broadcasted_iota(I32, shape, len(shape) - 1)
    return (lane // group) == idx


def _head_pair_attention(units):
    first = _lane_group((1, LANES), HEAD_DIM, 0)
    raw, unit_masks = [], []
    for q2, k2, _, mask in units:
        raw.append(_dot_nt(jnp.where(first, q2, jnp.zeros_like(q2)), k2))
        unit_masks.append(mask() if callable(mask) else mask)
        raw.append(_dot_nt(jnp.where(first, jnp.zeros_like(q2), q2), k2))
    scores = []
    for u, mask in enumerate(unit_masks):
        masks = mask if isinstance(mask, (tuple, list)) else (mask, mask)
        for h in range(2):
            if masks[h].dtype == jnp.bool_:
                scores.append(jnp.where(masks[h], raw[2 * u + h], NEG_INF))
            else:
                scores.append(raw[2 * u + h] + masks[h])
    maxes = [jnp.max(s, axis=1, keepdims=True) for s in scores]
    weights = [jnp.exp2(s - mx).astype(BF16) for s, mx in zip(scores, maxes)]
    results = []
    for u, (_, _, v2, _) in enumerate(units):
        ones = jnp.ones_like(v2)
        pv0 = _dot(weights[2 * u], jnp.where(first, v2, ones))
        pv1 = _dot(weights[2 * u + 1], jnp.where(first, ones, v2))
        den = jnp.where(first, pv0[:, LANES - 1:LANES], pv1[:, 0:1])
        results.append((jnp.where(first, maxes[2 * u], maxes[2 * u + 1]), den, jnp.where(first, pv0, pv1)))
    return results


def _stacked_head_pair_attention(units):
    first = _lane_group((1, LANES), HEAD_DIM, 0)
    raw = [_dot_nt(jnp.concatenate([jnp.where(first, q2, jnp.zeros_like(q2)),
                                    jnp.where(first, jnp.zeros_like(q2), q2)], axis=0), k2)
           for q2, k2, _, _ in units]
    stats, weights = [], []
    for r, (q2, _, _, mask) in zip(raw, units):
        rows = q2.shape[0]
        halves = [jnp.where(mask, r[h * rows:(h + 1) * rows], NEG_INF) for h in range(2)]
        maxes = [jnp.max(s, axis=1, keepdims=True) for s in halves]
        exps = [jnp.exp2(s - mx) for s, mx in zip(halves, maxes)]
        dens = [jnp.sum(e, axis=1, keepdims=True) for e in exps]
        stats.append((jnp.where(first, maxes[0], maxes[1]), jnp.where(first, dens[0], dens[1])))
        weights.append(jnp.concatenate(exps, axis=0).astype(BF16))
    results = []
    for (mx, den), w, (q2, _, v2, _) in zip(stats, weights, units):
        rows = q2.shape[0]
        pv = _dot(w, v2)
        results.append((mx, den, jnp.where(first, pv[:rows], pv[rows:])))
    return results


def _rope_lanes(t, cosf, sinf, half):
    lane = lax.broadcasted_iota(I32, t.shape, 1)
    first = (lane % (2 * half)) < half
    swapped = jnp.where(first, pltpu.roll(t, LANES - half, 1), pltpu.roll(t, half, 1))
    return t * cosf + swapped * sinf


def _proj_kernel(x_ref, w_ref, c64_ref, s64_ref, c32_ref, s32_ref, *out_refs, sections, mean_section):
    x = x_ref[...].astype(BF16)
    for idx, (o_ref, (start, width, rope, scale)) in enumerate(zip(out_refs, sections)):
        h = _dot(x, w_ref[:, start:start + width])
        if rope is not None:
            cosf = (c64_ref if rope == HEAD_DIM else c32_ref)[...]
            sinf = (s64_ref if rope == HEAD_DIM else s32_ref)[...]
            parts = [_rope_lanes(h[:, g * LANES:(g + 1) * LANES], cosf, sinf, rope // 2)
                     for g in range(width // LANES)]
            h = parts[0] if len(parts) == 1 else jnp.concatenate(parts, axis=1)
        if scale is not None:
            h = h * scale
        o_ref[...] = h.astype(o_ref.dtype)
        if idx == mean_section:
            m_ref = out_refs[len(sections)]
            for blk in range(h.shape[0] // MOBA_BLOCK):
                rows = h[blk * MOBA_BLOCK:(blk + 1) * MOBA_BLOCK]
                m_ref[0, blk:blk + 1, :] = jnp.sum(rows, axis=0, keepdims=True) * (1.0 / MOBA_BLOCK)


def _project(x2d, w, tables, sections, out_dtypes, seq, mean_section=None):
    n, d = x2d.shape
    tm = PROJ_ROWS
    pos_blocks = seq // tm
    tab_spec = pl.BlockSpec((tm, LANES), lambda i: (i % pos_blocks, 0))
    out_specs = [pl.BlockSpec((tm, sec[1]), lambda i: (i, 0)) for sec in sections]
    out_shape = [jax.ShapeDtypeStruct((n, sec[1]), dt) for sec, dt in zip(sections, out_dtypes)]
    if mean_section is not None:
        per_tile, width = tm // MOBA_BLOCK, sections[mean_section][1]
        out_specs.append(pl.BlockSpec((1, per_tile, width), lambda i: (i, 0, 0)))
        out_shape.append(jax.ShapeDtypeStruct((n // tm, per_tile, width), F32))
    outs = pl.pallas_call(
        functools.partial(_proj_kernel, sections=sections, mean_section=mean_section),
        grid=(n // tm,),
        in_specs=[pl.BlockSpec((tm, d), lambda i: (i, 0)),
                  pl.BlockSpec(w.shape, lambda i: (0, 0)),
                  tab_spec, tab_spec, tab_spec, tab_spec],
        out_specs=out_specs,
        out_shape=out_shape,
        compiler_params=_params(("parallel",)),
    )(x2d, w, *tables)
    outs = list(outs)
    if mean_section is not None:
        outs[-1] = outs[-1].reshape(n // MOBA_BLOCK, -1)
    return outs


def _rope_tables(seq):
    out = []
    for dim in (HEAD_DIM, IDX_DIM):
        inv = ROPE_THETA ** (-jnp.arange(0, dim, 2, dtype=F32) / dim)
        ang = jnp.arange(seq, dtype=F32)[:, None] * inv[None, :]
        cos, sin = jnp.cos(ang), jnp.sin(ang)
        reps = LANES // dim
        out.append(jnp.tile(jnp.concatenate([cos, cos], -1), (1, reps)))
        out.append(jnp.tile(jnp.concatenate([-sin, sin], -1), (1, reps)))
    return out


def _dsa_block(i, iq_ref, iw_ref, ik_ref, q_ref, kk_ref, vv_ref, o_ref, key_ref, sel_ref, *, kext, topk):
    qb = DSA_ROWS
    keys = slice(0, kext)
    qpos = i * qb + lax.broadcasted_iota(I32, (qb, 1), 0)
    kpos = lax.broadcasted_iota(I32, (1, kext), 1)
    causal = kpos <= qpos

    ik = ik_ref[keys, :]
    iw = iw_ref[...] * (IDX_DIM ** -0.5 * IDX_HEADS ** -0.5)
    heads_per_group = LANES // IDX_DIM
    score = jnp.zeros((qb, kext), F32)
    for h in range(IDX_HEADS):
        grp, j = divmod(h, heads_per_group)
        iqg = iq_ref[:, grp * LANES:(grp + 1) * LANES]
        iqm = jnp.where(_lane_group((1, LANES), IDX_DIM, j), iqg, jnp.zeros_like(iqg))
        rel = jnp.maximum(_dot_nt(iqm, ik), 0.0)
        score = score + rel * iw[:, h:h + 1]
    bits = pltpu.bitcast(score, I32)
    neg = bits >> 31
    int_min = jnp.int32(-2 ** 31)
    key_ref[:, keys] = jnp.where(causal, (bits ^ (neg & 0x7FFFFFFF)) - neg, int_min)

    grp_rows = qb // DSA_SEARCH_GROUPS
    groups = [slice(g * grp_rows, (g + 1) * grp_rows) for g in range(DSA_SEARCH_GROUPS)]

    def count_ge(rows, t):
        return jnp.sum(jnp.where(key_ref[rows, keys] >= t, 1.0, 0.0), axis=1, keepdims=True)

    def search(it, thrs):
        bit = jnp.left_shift(jnp.int32(1), 30 - it)
        return tuple(jnp.where(count_ge(rows, thr | bit) >= topk, thr | bit, thr)
                     for rows, thr in zip(groups, thrs))

    zero = jnp.zeros((grp_rows, 1), I32)
    thrs = tuple(jnp.where(count_ge(rows, zero) >= topk, zero, int_min) for rows in groups)
    thrs = lax.fori_loop(0, 31, search, thrs, unroll=DSA_SEARCH_UNROLL)
    thr = jnp.concatenate(thrs, axis=0)

    floor = jnp.maximum(thr, int_min + 1)
    key = key_ref[:, keys]
    sel_ref[:, keys] = jnp.where(key >= floor, 0.0, NEG_INF)
    n_ge = jnp.sum(jnp.where(key >= thr, 1.0, 0.0), axis=1, keepdims=True)
    tie_rows = (n_ge > topk) & (thr > int_min)

    @pl.when(jnp.max(tie_rows.astype(I32)) > 0)
    def _break_ties():
        upper = (lax.broadcasted_iota(I32, (LANES, LANES), 0)
                 < lax.broadcasted_iota(I32, (LANES, LANES), 1))
        upper = jnp.where(upper, 1.0, 0.0).astype(BF16)
        n_gt = jnp.sum(jnp.where(key_ref[:, keys] > thr, 1.0, 0.0), axis=1, keepdims=True)
        need = topk - n_gt
        before = jnp.zeros((qb, 1), F32)
        for c in range(kext // LANES):
            sl = slice(c * LANES, (c + 1) * LANES)
            keyc = key_ref[:, sl]
            eqc = keyc == thr
            eqf = jnp.where(eqc, 1.0, 0.0)
            rank = _dot(eqf.astype(BF16), upper) + before
            take = ((keyc > thr) | (eqc & (rank < need))) & (keyc > int_min)
            sel_ref[:, sl] = jnp.where(take, 0.0, NEG_INF)
            before = before + jnp.sum(eqf, axis=1, keepdims=True)

    selected = sel_ref[:, keys]
    kk = kk_ref[keys, :]
    vv = vv_ref[keys, :]
    pairs = [(q_ref[:, hp * LANES:(hp + 1) * LANES], kk, vv, selected) for hp in range(A_HEADS // 2)]
    for hp, (_, den, num) in enumerate(_head_pair_attention(pairs)):
        o_ref[:, hp * LANES:(hp + 1) * LANES] = (num / den).astype(o_ref.dtype)


def _dsa_kernel(*refs, seq, topk):
    i = pl.program_id(1)
    blocks_per_step = ATTN_KEY_STEP // DSA_ROWS
    for v in range(seq // ATTN_KEY_STEP):
        @pl.when(i // blocks_per_step == v)
        def _(v=v):
            _dsa_block(i, *refs, kext=(v + 1) * ATTN_KEY_STEP, topk=topk)


def _dsa_attention(iq, iw, ik4, qa, kk, vv, batch, seq):
    n = batch * seq
    qb = DSA_ROWS
    nq = seq // qb
    topk = min(DSA_TOPK, seq // 4)
    rows = lambda width: pl.BlockSpec((qb, width), lambda b, i: (b * nq + i, 0))
    whole = pl.BlockSpec((seq, LANES), lambda b, i: (b, 0))
    return pl.pallas_call(
        functools.partial(_dsa_kernel, seq=seq, topk=topk),
        grid=(batch, nq),
        in_specs=[rows(iq.shape[1]), rows(LANES), whole, rows(qa.shape[1]), whole, whole],
        out_specs=rows(qa.shape[1]),
        out_shape=jax.ShapeDtypeStruct((n, qa.shape[1]), BF16),
        scratch_shapes=[pltpu.VMEM((qb, seq), I32), pltpu.VMEM((qb, seq), F32)],
        compiler_params=_params(("parallel", "parallel")),
    )(iq, iw, ik4, qa, kk, vv)


def _moba_step(step, q_ref, k_ref, v_ref, kmean_ref, o_ref, *, kext, kt):
    bs, per = MOBA_BLOCK, MOBA_STEP_BLOCKS
    nb_all = kmean_ref.shape[0]
    nbp = max(nb_all, MOBA_GATE_ROWS)
    kmean = jnp.concatenate([kmean_ref[...], jnp.zeros((nbp - nb_all, LANES), F32)], axis=0).astype(BF16)
    n_iota = lax.broadcasted_iota(I32, (nbp, 1), 0)
    qrow = lax.broadcasted_iota(I32, (bs, 1), 0)
    units = []
    for j in range(per):
        qi = step * per + j
        ext = kext - (per - 1 - j) * bs
        nb = ext // bs
        q2 = q_ref[j * bs:(j + 1) * bs, :]
        blk_row = lax.broadcasted_iota(I32, (nbp, ext), 0)
        blk_of_key = lax.broadcasted_iota(I32, (nbp, ext), 1) // bs
        member = jnp.where(blk_row == blk_of_key, 1.0, 0.0).astype(BF16)
        past = n_iota < qi
        kcol = lax.broadcasted_iota(I32, (1, ext), 1)
        own = (kcol >= qi * bs) & (kcol <= qi * bs + qrow)
        gates = [jnp.where(past, _dot_nt(kmean, jnp.where(_lane_group((1, LANES), HEAD_DIM, half), q2,
                                                          jnp.zeros_like(q2))), NEG_INF)
                 for half in range(2)]

        def block_masks(gates=gates, past=past, own=own, member=member, nb=nb):
            chosen = []
            for gate in gates:
                rank = jnp.zeros((nbp, bs), I32)
                for m in range(nb):
                    gm = gate[m:m + 1, :]
                    beats = (gm > gate) | ((gm == gate) & (m < n_iota))
                    rank = rank + beats.astype(I32)
                chosen.append(jnp.transpose(jnp.where(past & (rank < kt), 1.0, 0.0)).astype(BF16))
            return [(_dot(c, member) > 0.5) | own for c in chosen]

        units.append((q2, k_ref[0:ext, :], v_ref[0:ext, :], block_masks))
    for j, (_, den, num) in enumerate(_head_pair_attention(units)):
        o_ref[j * bs:(j + 1) * bs, :] = (num / den).astype(o_ref.dtype)


def _moba_kernel(*refs, seq):
    step = pl.program_id(2)
    kt = min(MOBA_TOPK, seq // MOBA_BLOCK - 1)
    rows = MOBA_BLOCK * MOBA_STEP_BLOCKS
    for v in range(seq // rows):
        @pl.when(step == v)
        def _(v=v):
            _moba_step(step, *refs, kext=(v + 1) * rows, kt=kt)


def _moba_attention(q, k, v, kmean, batch, seq):
    n, width = q.shape
    tq = MOBA_BLOCK * MOBA_STEP_BLOCKS
    steps = seq // tq
    rows = pl.BlockSpec((tq, LANES), lambda b, hp, st: (b * steps + st, hp))
    whole = pl.BlockSpec((seq, LANES), lambda b, hp, st: (b, hp))
    means = pl.BlockSpec((seq // MOBA_BLOCK, LANES), lambda b, hp, st: (b, hp))
    return pl.pallas_call(
        functools.partial(_moba_kernel, seq=seq),
        grid=(batch, width // LANES, steps),
        in_specs=[rows, whole, whole, means],
        out_specs=rows,
        out_shape=jax.ShapeDtypeStruct((n, width), BF16),
        compiler_params=_params(("parallel", "parallel", "parallel")),
    )(q, k, v, kmean)


def _rows(start, size, stride):
    return pl.ds(start, size) if stride == 1 else pl.ds(start, size, stride=stride)


def _largest_divisor(n, limit):
    return max(d for d in range(1, limit + 1) if n % d == 0)


def _repeat(trips, body):
    if trips == 1:
        body(0, 0)
    else:
        lax.fori_loop(0, trips, body, 0)


def _dilated_kernel(q_ref, k_ref, v_ref, o_ref, qp_ref, kp_ref, vp_ref, nat_ref, slab_ref, *, seq):
    blk = BAND_BLOCK
    ns = DILATED_SLABS
    slab_len = seq // ns

    for s in range(ns):
        rows = slice(s * slab_len, (s + 1) * slab_len)
        qp_ref[rows, :] = q_ref[pl.ds(s, slab_len, stride=ns), :]
        kp_ref[rows, :] = k_ref[pl.ds(s, slab_len, stride=ns), :]
        vp_ref[rows, :] = v_ref[pl.ds(s, slab_len, stride=ns), :]

    def operands(srcs, qstart, kstart, nk, stride):
        ksl = _rows(kstart, nk, stride)
        i = lax.broadcasted_iota(I32, (blk, 1), 0)
        j = lax.broadcasted_iota(I32, (1, nk), 1)
        if nk == blk:
            mask = j <= i
        else:
            mask = (j >= i) & (j <= i + blk)
        return (srcs[0][_rows(qstart, blk, stride), :].astype(BF16), srcs[1][ksl, :].astype(BF16),
                srcs[2][ksl, :].astype(BF16), mask)

    def merged(old, new):
        (m_old, l_old, a_old), (m_new, l_new, a_new) = old, new
        m_tot = jnp.maximum(m_old, m_new)
        w_old = jnp.exp2(m_old - m_tot)
        w_new = jnp.exp2(m_new - m_tot)
        return m_tot, w_old * l_old + w_new * l_new, w_old * a_old + w_new * a_new

    def run_units(srcs, state_ref, units, nk, stride, merge):
        results = _stacked_head_pair_attention([operands(srcs, qstart, kstart, nk, stride)
                                                for qstart, kstart in units])
        for (qstart, _), new in zip(units, results):
            qsl = _rows(qstart, blk, stride)
            if merge:
                new = merged(tuple(state_ref[t, qsl, :] for t in range(3)), new)
            for t in range(3):
                state_ref[t, qsl, :] = new[t]

    written = set()
    for window, dil in DILATED_CFG:
        assert window // dil == blk
        if dil % ns == 0:
            srcs, state_ref, stride, group_len = (qp_ref, kp_ref, vp_ref), slab_ref, dil // ns, slab_len
        else:
            srcs, state_ref, stride, group_len = (q_ref, k_ref, v_ref), nat_ref, dil, seq
        merge = id(state_ref) in written
        written.add(id(state_ref))
        nblk = seq // dil // blk
        ub = _largest_divisor(dil, DILATED_UNITS)

        def class_start(c, stride=stride, group_len=group_len):
            return (c // stride) * group_len + c % stride if group_len != seq else c

        def first_blocks(it, carry, srcs=srcs, state_ref=state_ref, stride=stride, merge=merge, ub=ub,
                         class_start=class_start):
            starts = [class_start(it * ub + u) for u in range(ub)]
            run_units(srcs, state_ref, [(st, st) for st in starts], blk, stride, merge)
            return carry

        _repeat(dil // ub, first_blocks)
        if nblk > 1:
            later = nblk - 1
            ul = _largest_divisor(dil * later, DILATED_UNITS)

            def later_blocks(it, carry, srcs=srcs, state_ref=state_ref, stride=stride, merge=merge, ul=ul,
                             later=later, class_start=class_start):
                flat = [it * ul + u for u in range(ul)]
                qstarts = [class_start(t // later) + stride * blk * (1 + t % later) for t in flat]
                run_units(srcs, state_ref, [(qs, qs - stride * blk) for qs in qstarts], 2 * blk, stride, merge)
                return carry

            _repeat(dil * later // ul, later_blocks)

    for s in range(ns):
        nat_rows = pl.ds(s, slab_len, stride=ns)
        slab_rows = slice(s * slab_len, (s + 1) * slab_len)
        _, den, num = merged(tuple(nat_ref[t, nat_rows, :] for t in range(3)),
                             tuple(slab_ref[t, slab_rows, :] for t in range(3)))
        nat_ref[2, nat_rows, :] = num / den
    o_ref[...] = nat_ref[2].astype(o_ref.dtype)


def _dilated_attention(q, k, v, batch, seq):
    n, width = q.shape
    for window, dil in DILATED_CFG:
        assert seq % (dil * BAND_BLOCK) == 0
    assert {dil % DILATED_SLABS == 0 for _, dil in DILATED_CFG} == {True, False}
    whole = pl.BlockSpec((seq, LANES), lambda b, hp: (b, hp))
    return pl.pallas_call(
        functools.partial(_dilated_kernel, seq=seq),
        grid=(batch, width // LANES),
        in_specs=[whole, whole, whole],
        out_specs=whole,
        out_shape=jax.ShapeDtypeStruct((n, width), BF16),
        scratch_shapes=[pltpu.VMEM((seq, LANES), F32)] * 3 + [pltpu.VMEM((3, seq, LANES), F32)] * 2,
        compiler_params=_params(("parallel", "parallel")),
    )(q, k, v)


def _deepnorm(x, f, g, b):
    y = DEEPNORM_ALPHA * x + f
    mu = jnp.mean(y, axis=1, keepdims=True)
    yc = y - mu
    var = jnp.mean(yc * yc, axis=1, keepdims=True)
    return yc * lax.rsqrt(var + LN_EPS) * g + b


def _outproj_ln_kernel(*refs, n_parts):
    part_refs = refs[:n_parts]
    w_ref, x_ref, g_ref, b_ref, o_ref, oh_ref = refs[n_parts:]
    mix = None
    start = 0
    for p_ref in part_refs:
        width = p_ref.shape[1]
        t = _dot(p_ref[...], w_ref[start:start + width, :])
        mix = t if mix is None else mix + t
        start += width
    y = _deepnorm(x_ref[...], mix, g_ref[...], b_ref[...])
    o_ref[...] = y
    oh_ref[...] = y.astype(BF16)


def _outproj_ln(parts, w, x2d, g, b):
    n, d = x2d.shape
    tm = PROJ_ROWS
    row = lambda width: pl.BlockSpec((tm, width), lambda i: (i, 0))
    vec = pl.BlockSpec((1, d), lambda i: (0, 0))
    return pl.pallas_call(
        functools.partial(_outproj_ln_kernel, n_parts=len(parts)),
        grid=(n // tm,),
        in_specs=[row(p.shape[1]) for p in parts] + [pl.BlockSpec(w.shape, lambda i: (0, 0)), row(d), vec, vec],
        out_specs=[row(d), row(d)],
        out_shape=[jax.ShapeDtypeStruct((n, d), F32), jax.ShapeDtypeStruct((n, d), BF16)],
        compiler_params=_params(("parallel",)),
    )(*parts, w, x2d, g.reshape(1, d), b.reshape(1, d))


def _silu(h):
    return h * (1.0 / (1.0 + jnp.exp(-h)))


def _swiglu_ln_kernel(xh_ref, x_ref, w1_ref, w3_ref, w2_ref, g_ref, b_ref, o_ref, acc_ref):
    f = pl.program_id(1)
    xh = xh_ref[...]
    hid = (_silu(_dot(xh, w1_ref[...])) * _dot(xh, w3_ref[...])).astype(BF16)
    y = _dot(hid, w2_ref[...])

    @pl.when(f == 0)
    def _():
        acc_ref[...] = y

    @pl.when(f > 0)
    def _():
        acc_ref[...] += y

    @pl.when(f == pl.num_programs(1) - 1)
    def _():
        o_ref[...] = _deepnorm(x_ref[...], acc_ref[...], g_ref[...], b_ref[...])


def _swiglu_ln(xh, x2d, w1, w3, w2, g, b):
    n, d = x2d.shape
    dff = w1.shape[1]
    tm = FFN_ROWS
    nf = FFN_CHUNKS
    fc = dff // nf
    row = lambda: pl.BlockSpec((tm, d), lambda i, f: (i, 0))
    vec = pl.BlockSpec((1, d), lambda i, f: (0, 0))
    mode = pl.Buffered(1) if nf == 1 else None
    return pl.pallas_call(
        _swiglu_ln_kernel,
        grid=(n // tm, nf),
        in_specs=[row(), row(),
                  pl.BlockSpec((d, fc), lambda i, f: (0, f), pipeline_mode=mode),
                  pl.BlockSpec((d, fc), lambda i, f: (0, f), pipeline_mode=mode),
                  pl.BlockSpec((fc, d), lambda i, f: (f, 0), pipeline_mode=mode),
                  vec, vec],
        out_specs=row(),
        out_shape=jax.ShapeDtypeStruct((n, d), F32),
        scratch_shapes=[pltpu.VMEM((tm, d), F32)],
        compiler_params=_params(("parallel", "arbitrary")),
    )(xh, x2d, w1, w3, w2, g.reshape(1, d), b.reshape(1, d))


def _add_ln_kernel(x_ref, f_ref, g_ref, b_ref, o_ref):
    o_ref[...] = _deepnorm(x_ref[...], f_ref[...], g_ref[...], b_ref[...])


def _add_ln(x2d, f2d, g, b):
    n, d = x2d.shape
    tm = PROJ_ROWS
    row = pl.BlockSpec((tm, d), lambda i: (i, 0))
    vec = pl.BlockSpec((1, d), lambda i: (0, 0))
    return pl.pallas_call(
        _add_ln_kernel,
        grid=(n // tm,),
        in_specs=[row, row, vec, vec],
        out_specs=row,
        out_shape=jax.ShapeDtypeStruct((n, d), F32),
        compiler_params=_params(("parallel",)),
    )(x2d, f2d, g.reshape(1, d), b.reshape(1, d))


def _router_kernel(xh_ref, rw_ref, gate_ref, lposc_ref, lpost_ref, cum_ref, carry_ref, *, tiles_per_group):
    tm = xh_ref.shape[0]
    i = pl.program_id(0)

    @pl.when(i % tiles_per_group == 0)
    def _():
        carry_ref[...] = jnp.zeros_like(carry_ref)

    lane = lax.broadcasted_iota(I32, (1, LANES), 1)
    logits = jnp.where(lane < N_EXPERTS, _dot(xh_ref[...], rw_ref[...]), NEG_INF)
    m1 = jnp.max(logits, axis=1, keepdims=True)
    i1 = jnp.min(jnp.where(logits == m1, lane, LANES), axis=1, keepdims=True)
    rest = jnp.where(lane == i1, NEG_INF, logits)
    m2 = jnp.max(rest, axis=1, keepdims=True)
    i2 = jnp.min(jnp.where(rest == m2, lane, LANES), axis=1, keepdims=True)
    e2 = jnp.exp(m2 - m1)
    g1 = 1.0 / (1.0 + e2)
    g2 = e2 / (1.0 + e2)
    sel1 = lane == i1
    sel2 = lane == i2
    routed = sel1 | sel2
    gate_ref[...] = jnp.where(sel1, g1, jnp.where(sel2, g2, 0.0))
    routedf = jnp.where(routed, 1.0, 0.0)
    earlier = (lax.broadcasted_iota(I32, (tm, tm), 1) < lax.broadcasted_iota(I32, (tm, tm), 0))
    earlier = jnp.where(earlier, 1.0, 0.0).astype(BF16)
    carry = carry_ref[...]
    lpos = jnp.where(routed, _dot(earlier, routedf.astype(BF16)) + carry, -1.0)
    lposc_ref[...] = lpos
    lpost_ref[...] = jnp.transpose(lpos)[0:N_EXPERTS, :]
    carry = carry + jnp.sum(routedf, axis=0, keepdims=True)
    carry_ref[...] = carry
    cum_ref[...] = carry.astype(I32).reshape(1, 1, LANES)


def _router(xh, rw):
    n, d = xh.shape
    tm = ROUTER_ROWS
    nt = n // tm
    row = pl.BlockSpec((tm, LANES), lambda i: (i, 0))
    return pl.pallas_call(
        functools.partial(_router_kernel, tiles_per_group=MOE_GROUP // tm),
        grid=(nt,),
        in_specs=[pl.BlockSpec((tm, d), lambda i: (i, 0)), pl.BlockSpec(rw.shape, lambda i: (0, 0))],
        out_specs=[row, row,
                   pl.BlockSpec((N_EXPERTS, tm), lambda i: (0, i)),
                   pl.BlockSpec((1, 1, LANES), lambda i: (i, 0, 0))],
        out_shape=[jax.ShapeDtypeStruct((n, LANES), F32), jax.ShapeDtypeStruct((n, LANES), F32),
                   jax.ShapeDtypeStruct((N_EXPERTS, n), F32), jax.ShapeDtypeStruct((nt, 1, LANES), I32)],
        scratch_shapes=[pltpu.VMEM((1, LANES), F32)],
        compiler_params=_params(("arbitrary",)),
    )(xh, rw)


def _moe_kernel(bnd_ref, xh_ref, lpost_ref, lposc_ref, gatec_ref, w1_ref, w3_ref, w2_ref,
                o_ref, xg_ref, yacc_ref):
    tg = xh_ref.shape[0]
    win, ch, fr, align = MOE_WIN, ROUTER_ROWS, MOE_FFN_ROWS, MOE_ROW_ALIGN
    nch = tg // ch
    g, e, f = pl.program_id(0), pl.program_id(1), pl.program_id(2)
    nf = pl.num_programs(2)

    def bound(c):
        return bnd_ref[(g * (nch + 1) + c) * N_EXPERTS + e]

    cnt = bound(nch)
    n_ffn = (cnt + fr - 1) // fr

    def tok_chunk(c):
        return pl.ds(pl.multiple_of(c * ch, ch), ch)

    def windows(c):
        r0, r1 = bound(c), bound(c + 1)
        first = (r0 // align) * align
        return first, jnp.where(r1 > r0, (r1 - first + win - 1) // win, 0)

    one_window = functools.reduce(jnp.logical_and, [windows(c)[1] <= 1 for c in range(nch)])

    @pl.when((e == 0) & (f == 0))
    def _():
        o_ref[...] = jnp.zeros_like(o_ref)

    @pl.when(f == 0)
    def _compact():
        def clear(s, carry):
            rows = pl.ds(pl.multiple_of(s * win, win), win)
            xg_ref[rows, :] = jnp.zeros((win, xg_ref.shape[1]), xg_ref.dtype)
            yacc_ref[rows, :] = jnp.zeros((win, yacc_ref.shape[1]), F32)
            return carry

        lax.fori_loop(0, (cnt + max(fr, win) + win - 1) // win, clear, 0)

        def per_chunk(c, carry):
            first, n_win = windows(c)
            lp = lpost_ref[pl.ds(e, 1), tok_chunk(c)]

            def per_window(w, carry2):
                start = pl.multiple_of(first + w * win, align)
                want = (lax.broadcasted_iota(I32, (win, 1), 0) + start).astype(F32)
                pick = jnp.where(lp == want, 1.0, 0.0).astype(BF16)
                rows = pl.ds(start, win)
                xg_ref[rows, :] += _dot(pick, xh_ref[tok_chunk(c), :]).astype(xg_ref.dtype)
                return carry2

            return lax.fori_loop(0, n_win, per_window, carry)

        @pl.when(one_window)
        def _():
            for c in range(nch):
                toks = slice(c * ch, (c + 1) * ch)
                start = pl.multiple_of(windows(c)[0], align)
                want = (lax.broadcasted_iota(I32, (win, 1), 0) + start).astype(F32)
                pick = jnp.where(lpost_ref[pl.ds(e, 1), toks] == want, 1.0, 0.0).astype(BF16)
                xg_ref[pl.ds(start, win), :] += _dot(pick, xh_ref[toks, :]).astype(xg_ref.dtype)

        @pl.when(jnp.logical_not(one_window))
        def _():
            lax.fori_loop(0, nch, per_chunk, 0)

    def ffn_rows(rows):
        xs = xg_ref[rows, :]
        hid = (_silu(_dot(xs, w1_ref[...])) * _dot(xs, w3_ref[...])).astype(BF16)
        yacc_ref[rows, :] += _dot(hid, w2_ref[...])

    def ffn_pair(p, carry):
        ffn_rows(pl.ds(pl.multiple_of(p * 2 * fr, align), 2 * fr))
        return carry

    lax.fori_loop(0, n_ffn // 2, ffn_pair, 0)

    @pl.when(n_ffn % 2 == 1)
    def _():
        ffn_rows(pl.ds(pl.multiple_of((n_ffn - 1) * fr, align), fr))

    @pl.when(f == nf - 1)
    def _scatter():
        lane = lax.broadcasted_iota(I32, (1, LANES), 1)

        def per_chunk(c, carry):
            first, n_win = windows(c)
            toks = tok_chunk(c)
            lp = jnp.sum(jnp.where(lane == e, lposc_ref[toks, :], 0.0), axis=1, keepdims=True)
            gt = jnp.sum(jnp.where(lane == e, gatec_ref[toks, :], 0.0), axis=1, keepdims=True)

            def per_window(w, carry2):
                start = pl.multiple_of(first + w * win, align)
                y = yacc_ref[pl.ds(start, win), :]
                y_hi = y.astype(BF16)
                y_lo = (y - y_hi.astype(F32)).astype(BF16)
                want = (lax.broadcasted_iota(I32, (1, win), 1) + start).astype(F32)
                place = jnp.where(lp == want, 1.0, 0.0).astype(BF16)
                o_ref[toks, :] += gt * (_dot(place, y_hi) + _dot(place, y_lo))
                return carry2

            return lax.fori_loop(0, n_win, per_window, carry)

        @pl.when(one_window)
        def _():
            for c in range(nch):
                toks = slice(c * ch, (c + 1) * ch)
                start = pl.multiple_of(windows(c)[0], align)
                lp = jnp.sum(jnp.where(lane == e, lposc_ref[toks, :], 0.0), axis=1, keepdims=True)
                gt = jnp.sum(jnp.where(lane == e, gatec_ref[toks, :], 0.0), axis=1, keepdims=True)
                y = yacc_ref[pl.ds(start, win), :]
                y_hi = y.astype(BF16)
                y_lo = (y - y_hi.astype(F32)).astype(BF16)
                want = (lax.broadcasted_iota(I32, (1, win), 1) + start).astype(F32)
                place = jnp.where(lp == want, 1.0, 0.0).astype(BF16)
                o_ref[toks, :] += gt * (_dot(place, y_hi) + _dot(place, y_lo))

        @pl.when(jnp.logical_not(one_window))
        def _():
            lax.fori_loop(0, nch, per_chunk, 0)


def _chunk_bounds(cum):
    nch = MOE_GROUP // ROUTER_ROWS
    per_group = cum[:, 0, :N_EXPERTS].reshape(-1, nch, N_EXPERTS)
    return jnp.pad(per_group, ((0, 0), (1, 0), (0, 0))).reshape(-1)


def _moe(bounds, xh, lpost, lposc, gatec, w1, w3, w2, layer):
    n, d = xh.shape
    tg = MOE_GROUP
    dff = w1.shape[3]
    fc = MOE_FF_CHUNK
    cap = -(-(tg + max(MOE_FFN_ROWS, MOE_WIN)) // MOE_WIN) * MOE_WIN
    grid_spec = pltpu.PrefetchScalarGridSpec(
        num_scalar_prefetch=1,
        grid=(n // tg, N_EXPERTS, dff // fc),
        in_specs=[pl.BlockSpec((tg, d), lambda g, e, f, c: (g, 0)),
                  pl.BlockSpec((N_EXPERTS, tg), lambda g, e, f, c: (0, g)),
                  pl.BlockSpec((tg, LANES), lambda g, e, f, c: (g, 0)),
                  pl.BlockSpec((tg, LANES), lambda g, e, f, c: (g, 0)),
                  pl.BlockSpec((None, None, d, fc), lambda g, e, f, c: (layer, e, 0, f)),
                  pl.BlockSpec((None, None, d, fc), lambda g, e, f, c: (layer, e, 0, f)),
                  pl.BlockSpec((None, None, fc, d), lambda g, e, f, c: (layer, e, f, 0))],
        out_specs=pl.BlockSpec((tg, d), lambda g, e, f, c: (g, 0)),
        scratch_shapes=[pltpu.VMEM((cap, d), BF16), pltpu.VMEM((cap, d), F32)],
    )
    return pl.pallas_call(
        _moe_kernel,
        grid_spec=grid_spec,
        out_shape=jax.ShapeDtypeStruct((n, d), F32),
        compiler_params=_params(("parallel", "arbitrary", "arbitrary")),
    )(bounds, xh, lpost, lposc, gatec, w1, w3, w2)


EVEN_KB_SECTION = 2


def _even_weight_layout(w_in):
    hd = HEAD_DIM
    widths = (A_HEADS * hd, hd, hd, IDX_HEADS * IDX_DIM, IDX_DIM, IDX_HEADS, B_HEADS * hd, B_HEADS * hd, B_HEADS * hd)
    offs = [0]
    for wd in widths:
        offs.append(offs[-1] + wd)
    qa, ka, va, iq, ik, iw, qb, kb, vb = (w_in[:, offs[j]:offs[j + 1]] for j in range(9))
    iw_pad = jnp.pad(iw, ((0, 0), (0, LANES - IDX_HEADS)))
    w = jnp.concatenate([qa, qb, kb, vb, ka, ka, va, va, iq, ik, ik, ik, ik, iw_pad], axis=1)
    mixw = B_HEADS * hd
    sections = ((0, A_HEADS * hd, hd, QK_SCALE),
                (512, mixw, hd, QK_SCALE),
                (1024, mixw, hd, None),
                (1536, mixw, None, None),
                (2048, LANES, hd, None),
                (2176, LANES, None, None),
                (2304, IDX_HEADS * IDX_DIM, IDX_DIM, None),
                (2560, LANES, IDX_DIM, None),
                (2688, LANES, None, None))
    dtypes = (BF16,) * 8 + (F32,)
    return w.astype(BF16), sections, dtypes


def kernel(x, even_w_in, even_w_out, even_ln1_g, even_ln1_b, even_w1, even_w3, even_w2, even_ln2_g, even_ln2_b, odd_w_in, odd_w_out, odd_ln1_g, odd_ln1_b, odd_router, odd_w1, odd_w3, odd_w2, odd_ln2_g, odd_ln2_b):
    batch, seq, d = x.shape
    n = batch * seq
    tables = _rope_tables(seq)
    x2d = x.reshape(n, d)
    odd_w1h, odd_w3h, odd_w2h = odd_w1.astype(BF16), odd_w3.astype(BF16), odd_w2.astype(BF16)
    for layer in range(DEPTH):
        i = layer // 2
        if layer % 2 == 0:
            w, sections, dtypes = _even_weight_layout(even_w_in[i])
            qa, qb, kb, vb, kk, vv, iq, ik4, iw, kb_mean = _project(x2d, w, tables, sections, dtypes, seq,
                                                                    mean_section=EVEN_KB_SECTION)
            o_a = _dsa_attention(iq, iw, ik4, qa, kk, vv, batch, seq)
            o_b = _moba_attention(qb, kb, vb, kb_mean, batch, seq)
            x2d, xh = _outproj_ln([o_a, o_b], even_w_out[i].astype(BF16), x2d, even_ln1_g[i], even_ln1_b[i])
            x2d = _swiglu_ln(xh, x2d, even_w1[i].astype(BF16), even_w3[i].astype(BF16), even_w2[i].astype(BF16),
                             even_ln2_g[i], even_ln2_b[i])
        else:
            mix = C_HEADS * HEAD_DIM
            sections = ((0, mix, HEAD_DIM, QK_SCALE), (mix, mix, HEAD_DIM, None), (2 * mix, mix, None, None))
            q, k, v = _project(x2d, odd_w_in[i].astype(BF16), tables, sections, (F32, F32, F32), seq)
            o = _dilated_attention(q, k, v, batch, seq)
            x2d, xh = _outproj_ln([o], odd_w_out[i].astype(BF16), x2d, odd_ln1_g[i], odd_ln1_b[i])
            rw = jnp.pad(odd_router[i], ((0, 0), (0, LANES - N_EXPERTS))).astype(BF16)
            gatec, lposc, lpost, cum = _router(xh, rw)
            ffn = _moe(_chunk_bounds(cum), xh, lpost, lposc, gatec, odd_w1h, odd_w3h, odd_w2h, i)
            x2d = _add_ln(x2d, ffn, odd_ln2_g[i], odd_ln2_b[i])
    return x2d.reshape(batch, seq, d)
```

```python
import functools

import jax
import jax.numpy as jnp
from jax import lax
from jax.experimental import pallas as pl
from jax.experimental.pallas import tpu as pltpu

F32 = jnp.float32
BF16 = jnp.bfloat16
I32 = jnp.int32

DEPTH = 4
HEAD_DIM = 64
ROPE_THETA = 10000.0
LN_EPS = 1e-5
A_HEADS = 8
IDX_HEADS = 8
IDX_DIM = 32
DSA_TOPK = 256
B_HEADS = 8
MOBA_BLOCK = 256
MOBA_TOPK = 3
C_HEADS = 16
DILATED_CFG = ((128, 1), (512, 4), (2048, 16))
BAND_BLOCK = 128
N_EXPERTS = 8
DEEPNORM_ALPHA = (2 * DEPTH) ** 0.25
QK_SCALE = HEAD_DIM ** -0.5 * 1.4426950408889634

LANES = 128
VMEM_LIMIT_BYTES = 54 * 1024 * 1024

PROJ_ROWS = 1024
FFN_ROWS = 512
FFN_CHUNKS = 1
ROUTER_ROWS = 512
MOE_GROUP = 2048
MOE_WIN = 256
MOE_FFN_ROWS = 272
MOE_ROW_ALIGN = 16
MOE_FF_CHUNK = 512
DSA_ROWS = 256
DSA_SEARCH_GROUPS = 4
DSA_SEARCH_UNROLL = 4
MOBA_GATE_ROWS = 16
DILATED_UNITS = 16
DILATED_SLABS = 4
ATTN_KEY_STEP = 512
MOBA_STEP_BLOCKS = 8

NEG_INF = float("-inf")


def _params(semantics):
    return pltpu.CompilerParams(dimension_semantics=semantics, vmem_limit_bytes=VMEM_LIMIT_BYTES)


def _dot(a, b):
    return jnp.dot(a, b, preferred_element_type=F32)


def _dot_nt(a, b):
    return lax.dot_general(a, b, (((1,), (1,)), ((), ())), preferred_element_type=F32)


def _lane_group(shape, group, idx):
    lane = lax.broadcasted_iota(I32, shape, len(shape) - 1)
    return (lane // group) == idx


def _head_pair_attention(units):
    first = _lane_group((1, LANES), HEAD_DIM, 0)
    raw, unit_masks = [], []
    for q2, k2, _, mask in units:
        raw.append(_dot_nt(jnp.where(first, q2, jnp.zeros_like(q2)), k2))
        unit_masks.append(mask() if callable(mask) else mask)
        raw.append(_dot_nt(jnp.where(first, jnp.zeros_like(q2), q2), k2))
    scores = []
    for u, mask in enumerate(unit_masks):
        masks = mask if isinstance(mask, (tuple, list)) else (mask, mask)
        for h in range(2):
            if masks[h].dtype == jnp.bool_:
                scores.append(jnp.where(masks[h], raw[2 * u + h], NEG_INF))
            else:
                scores.append(raw[2 * u + h] + masks[h])
    maxes = [jnp.max(s, axis=1, keepdims=True) for s in scores]
    weights = [jnp.exp2(s - mx).astype(BF16) for s, mx in zip(scores, maxes)]
    results = []
    for u, (_, _, v2, _) in enumerate(units):
        ones = jnp.ones_like(v2)
        pv0 = _dot(weights[2 * u], jnp.where(first, v2, ones))
        pv1 = _dot(weights[2 * u + 1], jnp.where(first, ones, v2))
        den = jnp.where(first, pv0[:, LANES - 1:LANES], pv1[:, 0:1])
        results.append((jnp.where(first, maxes[2 * u], maxes[2 * u + 1]), den, jnp.where(first, pv0, pv1)))
    return results


def _stacked_head_pair_attention(units):
    first = _lane_group((1, LANES), HEAD_DIM, 0)
    raw = [_dot_nt(jnp.concatenate([jnp.where(first, q2, jnp.zeros_like(q2)),
                                    jnp.where(first, jnp.zeros_like(q2), q2)], axis=0), k2)
           for q2, k2, _, _ in units]
    stats, weights = [], []
    for r, (q2, _, _, mask) in zip(raw, units):
        rows = q2.shape[0]
        halves = [jnp.where(mask, r[h * rows:(h + 1) * rows], NEG_INF) for h in range(2)]
        maxes = [jnp.max(s, axis=1, keepdims=True) for s in halves]
        exps = [jnp.exp2(s - mx) for s, mx in zip(halves, maxes)]
        dens = [jnp.sum(e, axis=1, keepdims=True) for e in exps]
        stats.append((jnp.where(first, maxes[0], maxes[1]), jnp.where(first, dens[0], dens[1])))
        weights.append(jnp.concatenate(exps, axis=0).astype(BF16))
    results = []
    for (mx, den), w, (q2, _, v2, _) in zip(stats, weights, units):
        rows = q2.shape[0]
        pv = _dot(w, v2)
        results.append((mx, den, jnp.where(first, pv[:rows], pv[rows:])))
    return results


def _rope_lanes(t, cosf, sinf, half):
    lane = lax.broadcasted_iota(I32, t.shape, 1)
    first = (lane % (2 * half)) < half
    swapped = jnp.where(first, pltpu.roll(t, LANES - half, 1), pltpu.roll(t, half, 1))
    return t * cosf + swapped * sinf


def _proj_kernel(x_ref, w_ref, c64_ref, s64_ref, c32_ref, s32_ref, *out_refs, sections, mean_section):
    x = x_ref[...].astype(BF16)
    for idx, (o_ref, (start, width, rope, scale)) in enumerate(zip(out_refs, sections)):
        h = _dot(x, w_ref[:, start:start + width])
        if rope is not None:
            cosf = (c64_ref if rope == HEAD_DIM else c32_ref)[...]
            sinf = (s64_ref if rope == HEAD_DIM else s32_ref)[...]
            parts = [_rope_lanes(h[:, g * LANES:(g + 1) * LANES], cosf, sinf, rope // 2)
                     for g in range(width // LANES)]
            h = parts[0] if len(parts) == 1 else jnp.concatenate(parts, axis=1)
        if scale is not None:
            h = h * scale
        o_ref[...] = h.astype(o_ref.dtype)
        if idx == mean_section:
            m_ref = out_refs[len(sections)]
            for blk in range(h.shape[0] // MOBA_BLOCK):
                rows = h[blk * MOBA_BLOCK:(blk + 1) * MOBA_BLOCK]
                m_ref[0, blk:blk + 1, :] = jnp.sum(rows, axis=0, keepdims=True) * (1.0 / MOBA_BLOCK)


def _project(x2d, w, tables, sections, out_dtypes, seq, mean_section=None):
    n, d = x2d.shape
    tm = PROJ_ROWS
    pos_blocks = seq // tm
    tab_spec = pl.BlockSpec((tm, LANES), lambda i: (i % pos_blocks, 0))
    out_specs = [pl.BlockSpec((tm, sec[1]), lambda i: (i, 0)) for sec in sections]
    out_shape = [jax.ShapeDtypeStruct((n, sec[1]), dt) for sec, dt in zip(sections, out_dtypes)]
    if mean_section is not None:
        per_tile, width = tm // MOBA_BLOCK, sections[mean_section][1]
        out_specs.append(pl.BlockSpec((1, per_tile, width), lambda i: (i, 0, 0)))
        out_shape.append(jax.ShapeDtypeStruct((n // tm, per_tile, width), F32))
    outs = pl.pallas_call(
        functools.partial(_proj_kernel, sections=sections, mean_section=mean_section),
        grid=(n // tm,),
        in_specs=[pl.BlockSpec((tm, d), lambda i: (i, 0)),
                  pl.BlockSpec(w.shape, lambda i: (0, 0)),
                  tab_spec, tab_spec, tab_spec, tab_spec],
        out_specs=out_specs,
        out_shape=out_shape,
        compiler_params=_params(("parallel",)),
    )(x2d, w, *tables)
    outs = list(outs)
    if mean_section is not None:
        outs[-1] = outs[-1].reshape(n // MOBA_BLOCK, -1)
    return outs


def _rope_tables(seq):
    out = []
    for dim in (HEAD_DIM, IDX_DIM):
        inv = ROPE_THETA ** (-jnp.arange(0, dim, 2, dtype=F32) / dim)
        ang = jnp.arange(seq, dtype=F32)[:, None] * inv[None, :]
        cos, sin = jnp.cos(ang), jnp.sin(ang)
        reps = LANES // dim
        out.append(jnp.tile(jnp.concatenate([cos, cos], -1), (1, reps)))
        out.append(jnp.tile(jnp.concatenate([-sin, sin], -1), (1, reps)))
    return out


def _dsa_block(i, iq_ref, iw_ref, ik_ref, q_ref, kk_ref, vv_ref, o_ref, key_ref, sel_ref, *, kext, topk):
    qb = DSA_ROWS
    keys = slice(0, kext)
    qpos = i * qb + lax.broadcasted_iota(I32, (qb, 1), 0)
    kpos = lax.broadcasted_iota(I32, (1, kext), 1)
    causal = kpos <= qpos

    ik = ik_ref[keys, :]
    iw = iw_ref[...] * (IDX_DIM ** -0.5 * IDX_HEADS ** -0.5)
    heads_per_group = LANES // IDX_DIM
    score = jnp.zeros((qb, kext), F32)
    for h in range(IDX_HEADS):
        grp, j = divmod(h, heads_per_group)
        iqg = iq_ref[:, grp * LANES:(grp + 1) * LANES]
        iqm = jnp.where(_lane_group((1, LANES), IDX_DIM, j), iqg, jnp.zeros_like(iqg))
        rel = jnp.maximum(_dot_nt(iqm, ik), 0.0)
        score = score + rel * iw[:, h:h + 1]
    key_ref[:, keys] = jnp.where(causal, score, NEG_INF)

    int_min = jnp.int32(-2 ** 31)
    lowest = jnp.float32(-3.4028235e38)

    def as_float(code):
        bits = jnp.where(code >= 0, code, (code - 1) ^ 0x7FFFFFFF)
        return pltpu.bitcast(bits, F32)

    grp_rows = qb // DSA_SEARCH_GROUPS
    groups = [slice(g * grp_rows, (g + 1) * grp_rows) for g in range(DSA_SEARCH_GROUPS)]

    def count_ge(rows, code):
        return jnp.sum(jnp.where(key_ref[rows, keys] >= as_float(code), 1.0, 0.0), axis=1, keepdims=True)

    def search(it, thrs):
        bit = jnp.left_shift(jnp.int32(1), 30 - it)
        return tuple(jnp.where(count_ge(rows, thr | bit) >= topk, thr | bit, thr)
                     for rows, thr in zip(groups, thrs))

    zero = jnp.zeros((grp_rows, 1), I32)
    thrs = tuple(jnp.where(count_ge(rows, zero) >= topk, zero, int_min) for rows in groups)
    thrs = lax.fori_loop(0, 31, search, thrs, unroll=DSA_SEARCH_UNROLL)
    thr = jnp.maximum(as_float(jnp.concatenate(thrs, axis=0)), lowest)

    key = key_ref[:, keys]
    sel_ref[:, keys] = jnp.where(key >= thr, 0.0, NEG_INF)
    n_ge = jnp.sum(jnp.where(key >= thr, 1.0, 0.0), axis=1, keepdims=True)

    @pl.when(jnp.max(jnp.where(n_ge > topk, 1, 0)) > 0)
    def _break_ties():
        upper = (lax.broadcasted_iota(I32, (LANES, LANES), 0)
                 < lax.broadcasted_iota(I32, (LANES, LANES), 1))
        upper = jnp.where(upper, 1.0, 0.0).astype(BF16)
        n_gt = jnp.sum(jnp.where(key_ref[:, keys] > thr, 1.0, 0.0), axis=1, keepdims=True)
        need = topk - n_gt
        before = jnp.zeros((qb, 1), F32)
        for c in range(kext // LANES):
            sl = slice(c * LANES, (c + 1) * LANES)
            keyc = key_ref[:, sl]
            eqc = keyc == thr
            eqf = jnp.where(eqc, 1.0, 0.0)
            rank = _dot(eqf.astype(BF16), upper) + before
            take = (keyc > thr) | (eqc & (rank < need))
            sel_ref[:, sl] = jnp.where(take, 0.0, NEG_INF)
            before = before + jnp.sum(eqf, axis=1, keepdims=True)

    selected = sel_ref[:, keys]
    kk = kk_ref[keys, :]
    vv = vv_ref[keys, :]
    pairs = [(q_ref[:, hp * LANES:(hp + 1) * LANES], kk, vv, selected) for hp in range(A_HEADS // 2)]
    for hp, (_, den, num) in enumerate(_head_pair_attention(pairs)):
        o_ref[:, hp * LANES:(hp + 1) * LANES] = (num / den).astype(o_ref.dtype)


def _dsa_kernel(*refs, seq, topk):
    i = pl.program_id(1)
    blocks_per_step = ATTN_KEY_STEP // DSA_ROWS
    for v in range(seq // ATTN_KEY_STEP):
        @pl.when(i // blocks_per_step == v)
        def _(v=v):
            _dsa_block(i, *refs, kext=(v + 1) * ATTN_KEY_STEP, topk=topk)


def _dsa_attention(iq, iw, ik4, qa, kk, vv, batch, seq):
    n = batch * seq
    qb = DSA_ROWS
    nq = seq // qb
    topk = min(DSA_TOPK, seq // 4)
    rows = lambda width: pl.BlockSpec((qb, width), lambda b, i: (b * nq + i, 0))
    whole = pl.BlockSpec((seq, LANES), lambda b, i: (b, 0))
    return pl.pallas_call(
        functools.partial(_dsa_kernel, seq=seq, topk=topk),
        grid=(batch, nq),
        in_specs=[rows(iq.shape[1]), rows(LANES), whole, rows(qa.shape[1]), whole, whole],
        out_specs=rows(qa.shape[1]),
        out_shape=jax.ShapeDtypeStruct((n, qa.shape[1]), BF16),
        scratch_shapes=[pltpu.VMEM((qb, seq), F32), pltpu.VMEM((qb, seq), F32)],
        compiler_params=_params(("parallel", "parallel")),
    )(iq, iw, ik4, qa, kk, vv)


def _moba_step(step, q_ref, k_ref, v_ref, kmean_ref, o_ref, *, kext, kt):
    bs, per = MOBA_BLOCK, MOBA_STEP_BLOCKS
    nb_all = kmean_ref.shape[0]
    nbp = max(nb_all, MOBA_GATE_ROWS)
    kmean = jnp.concatenate([kmean_ref[...], jnp.zeros((nbp - nb_all, LANES), F32)], axis=0).astype(BF16)
    n_iota = lax.broadcasted_iota(I32, (nbp, 1), 0)
    qrow = lax.broadcasted_iota(I32, (bs, 1), 0)
    units = []
    for j in range(per):
        qi = step * per + j
        ext = kext - (per - 1 - j) * bs
        nb = ext // bs
        q2 = q_ref[j * bs:(j + 1) * bs, :]
        blk_row = lax.broadcasted_iota(I32, (nbp, ext), 0)
        blk_of_key = lax.broadcasted_iota(I32, (nbp, ext), 1) // bs
        member = jnp.where(blk_row == blk_of_key, 1.0, 0.0).astype(BF16)
        past = n_iota < qi
        kcol = lax.broadcasted_iota(I32, (1, ext), 1)
        own = (kcol >= qi * bs) & (kcol <= qi * bs + qrow)
        gates = [jnp.where(past, _dot_nt(kmean, jnp.where(_lane_group((1, LANES), HEAD_DIM, half), q2,
                                                          jnp.zeros_like(q2))), NEG_INF)
                 for half in range(2)]

        def block_masks(gates=gates, past=past, own=own, member=member, nb=nb):
            chosen = []
            for gate in gates:
                rank = jnp.zeros((nbp, bs), I32)
                for m in range(nb):
                    gm = gate[m:m + 1, :]
                    beats = (gm > gate) | ((gm == gate) & (m < n_iota))
                    rank = rank + beats.astype(I32)
                chosen.append(jnp.transpose(jnp.where(past & (rank < kt), 1.0, 0.0)).astype(BF16))
            return [(_dot(c, member) > 0.5) | own for c in chosen]

        units.append((q2, k_ref[0:ext, :], v_ref[0:ext, :], block_masks))
    for j, (_, den, num) in enumerate(_head_pair_attention(units)):
        o_ref[j * bs:(j + 1) * bs, :] = (num / den).astype(o_ref.dtype)


def _moba_kernel(*refs, seq):
    step = pl.program_id(2)
    kt = min(MOBA_TOPK, seq // MOBA_BLOCK - 1)
    rows = MOBA_BLOCK * MOBA_STEP_BLOCKS
    for v in range(seq // rows):
        @pl.when(step == v)
        def _(v=v):
            _moba_step(step, *refs, kext=(v + 1) * rows, kt=kt)


def _moba_attention(q, k, v, kmean, batch, seq):
    n, width = q.shape
    tq = MOBA_BLOCK * MOBA_STEP_BLOCKS
    steps = seq // tq
    rows = pl.BlockSpec((tq, LANES), lambda b, hp, st: (b * steps + st, hp))
    whole = pl.BlockSpec((seq, LANES), lambda b, hp, st: (b, hp))
    means = pl.BlockSpec((seq // MOBA_BLOCK, LANES), lambda b, hp, st: (b, hp))
    return pl.pallas_call(
        functools.partial(_moba_kernel, seq=seq),
        grid=(batch, width // LANES, steps),
        in_specs=[rows, whole, whole, means],
        out_specs=rows,
        out_shape=jax.ShapeDtypeStruct((n, width), BF16),
        compiler_params=_params(("parallel", "parallel", "parallel")),
    )(q, k, v, kmean)


def _rows(start, size, stride):
    return pl.ds(start, size) if stride == 1 else pl.ds(start, size, stride=stride)


def _largest_divisor(n, limit):
    return max(d for d in range(1, limit + 1) if n % d == 0)


def _repeat(trips, body):
    if trips == 1:
        body(0, 0)
    else:
        lax.fori_loop(0, trips, body, 0)


def _dilated_kernel(q_ref, k_ref, v_ref, o_ref, qp_ref, kp_ref, vp_ref, nat_ref, slab_ref, *, seq):
    blk = BAND_BLOCK
    ns = DILATED_SLABS
    slab_len = seq // ns

    for s in range(ns):
        rows = slice(s * slab_len, (s + 1) * slab_len)
        qp_ref[rows, :] = q_ref[pl.ds(s, slab_len, stride=ns), :]
        kp_ref[rows, :] = k_ref[pl.ds(s, slab_len, stride=ns), :]
        vp_ref[rows, :] = v_ref[pl.ds(s, slab_len, stride=ns), :]

    def operands(srcs, qstart, kstart, nk, stride):
        ksl = _rows(kstart, nk, stride)
        i = lax.broadcasted_iota(I32, (blk, 1), 0)
        j = lax.broadcasted_iota(I32, (1, nk), 1)
        if nk == blk:
            mask = j <= i
        else:
            mask = (j >= i) & (j <= i + blk)
        return (srcs[0][_rows(qstart, blk, stride), :].astype(BF16), srcs[1][ksl, :].astype(BF16),
                srcs[2][ksl, :].astype(BF16), mask)

    def merged(old, new):
        (m_old, l_old, a_old), (m_new, l_new, a_new) = old, new
        m_tot = jnp.maximum(m_old, m_new)
        w_old = jnp.exp2(m_old - m_tot)
        w_new = jnp.exp2(m_new - m_tot)
        return m_tot, w_old * l_old + w_new * l_new, w_old * a_old + w_new * a_new

    def run_units(srcs, state_ref, units, nk, stride, merge):
        results = _stacked_head_pair_attention([operands(srcs, qstart, kstart, nk, stride)
                                                for qstart, kstart in units])
        for (qstart, _), new in zip(units, results):
            qsl = _rows(qstart, blk, stride)
            if merge:
                new = merged(tuple(state_ref[t, qsl, :] for t in range(3)), new)
            for t in range(3):
                state_ref[t, qsl, :] = new[t]

    written = set()
    for window, dil in DILATED_CFG:
        assert window // dil == blk
        if dil % ns == 0:
            srcs, state_ref, stride, group_len = (qp_ref, kp_ref, vp_ref), slab_ref, dil // ns, slab_len
        else:
            srcs, state_ref, stride, group_len = (q_ref, k_ref, v_ref), nat_ref, dil, seq
        merge = id(state_ref) in written
        written.add(id(state_ref))
        nblk = seq // dil // blk
        ub = _largest_divisor(dil, DILATED_UNITS)

        def class_start(c, stride=stride, group_len=group_len):
            return (c // stride) * group_len + c % stride if group_len != seq else c

        def first_blocks(it, carry, srcs=srcs, state_ref=state_ref, stride=stride, merge=merge, ub=ub,
                         class_start=class_start):
            starts = [class_start(it * ub + u) for u in range(ub)]
            run_units(srcs, state_ref, [(st, st) for st in starts], blk, stride, merge)
            return carry

        _repeat(dil // ub, first_blocks)
        if nblk > 1:
            later = nblk - 1
            ul = _largest_divisor(dil * later, DILATED_UNITS)

            def later_blocks(it, carry, srcs=srcs, state_ref=state_ref, stride=stride, merge=merge, ul=ul,
                             later=later, class_start=class_start):
                flat = [it * ul + u for u in range(ul)]
                qstarts = [class_start(t // later) + stride * blk * (1 + t % later) for t in flat]
                run_units(srcs, state_ref, [(qs, qs - stride * blk) for qs in qstarts], 2 * blk, stride, merge)
                return carry

            _repeat(dil * later // ul, later_blocks)

    for s in range(ns):
        nat_rows = pl.ds(s, slab_len, stride=ns)
        slab_rows = slice(s * slab_len, (s + 1) * slab_len)
        _, den, num = merged(tuple(nat_ref[t, nat_rows, :] for t in range(3)),
                             tuple(slab_ref[t, slab_rows, :] for t in range(3)))
        nat_ref[2, nat_rows, :] = num / den
    o_ref[...] = nat_ref[2].astype(o_ref.dtype)


def _dilated_attention(q, k, v, batch, seq):
    n, width = q.shape
    for window, dil in DILATED_CFG:
        assert seq % (dil * BAND_BLOCK) == 0
    assert {dil % DILATED_SLABS == 0 for _, dil in DILATED_CFG} == {True, False}
    whole = pl.BlockSpec((seq, LANES), lambda b, hp: (b, hp))
    return pl.pallas_call(
        functools.partial(_dilated_kernel, seq=seq),
        grid=(batch, width // LANES),
        in_specs=[whole, whole, whole],
        out_specs=whole,
        out_shape=jax.ShapeDtypeStruct((n, width), BF16),
        scratch_shapes=[pltpu.VMEM((seq, LANES), F32)] * 3 + [pltpu.VMEM((3, seq, LANES), F32)] * 2,
        compiler_params=_params(("parallel", "parallel")),
    )(q, k, v)


def _deepnorm(x, f, g, b):
    y = DEEPNORM_ALPHA * x + f
    mu = jnp.mean(y, axis=1, keepdims=True)
    yc = y - mu
    var = jnp.mean(yc * yc, axis=1, keepdims=True)
    return yc * lax.rsqrt(var + LN_EPS) * g + b


def _outproj_ln_kernel(*refs, n_parts):
    part_refs = refs[:n_parts]
    w_ref, x_ref, g_ref, b_ref, o_ref, oh_ref = refs[n_parts:]
    mix = None
    start = 0
    for p_ref in part_refs:
        width = p_ref.shape[1]
        t = _dot(p_ref[...], w_ref[start:start + width, :])
        mix = t if mix is None else mix + t
        start += width
    y = _deepnorm(x_ref[...], mix, g_ref[...], b_ref[...])
    o_ref[...] = y
    oh_ref[...] = y.astype(BF16)


def _outproj_ln(parts, w, x2d, g, b):
    n, d = x2d.shape
    tm = PROJ_ROWS
    row = lambda width: pl.BlockSpec((tm, width), lambda i: (i, 0))
    vec = pl.BlockSpec((1, d), lambda i: (0, 0))
    return pl.pallas_call(
        functools.partial(_outproj_ln_kernel, n_parts=len(parts)),
        grid=(n // tm,),
        in_specs=[row(p.shape[1]) for p in parts] + [pl.BlockSpec(w.shape, lambda i: (0, 0)), row(d), vec, vec],
        out_specs=[row(d), row(d)],
        out_shape=[jax.ShapeDtypeStruct((n, d), F32), jax.ShapeDtypeStruct((n, d), BF16)],
        compiler_params=_params(("parallel",)),
    )(*parts, w, x2d, g.reshape(1, d), b.reshape(1, d))


def _silu(h):
    return h * (1.0 / (1.0 + jnp.exp(-h)))


def _swiglu_ln_kernel(xh_ref, x_ref, w1_ref, w3_ref, w2_ref, g_ref, b_ref, o_ref, acc_ref):
    f = pl.program_id(1)
    xh = xh_ref[...]
    hid = (_silu(_dot(xh, w1_ref[...])) * _dot(xh, w3_ref[...])).astype(BF16)
    y = _dot(hid, w2_ref[...])

    @pl.when(f == 0)
    def _():
        acc_ref[...] = y

    @pl.when(f > 0)
    def _():
        acc_ref[...] += y

    @pl.when(f == pl.num_programs(1) - 1)
    def _():
        o_ref[...] = _deepnorm(x_ref[...], acc_ref[...], g_ref[...], b_ref[...])


def _swiglu_ln(xh, x2d, w1, w3, w2, g, b):
    n, d = x2d.shape
    dff = w1.shape[1]
    tm = FFN_ROWS
    nf = FFN_CHUNKS
    fc = dff // nf
    row = lambda: pl.BlockSpec((tm, d), lambda i, f: (i, 0))
    vec = pl.BlockSpec((1, d), lambda i, f: (0, 0))
    mode = pl.Buffered(1) if nf == 1 else None
    return pl.pallas_call(
        _swiglu_ln_kernel,
        grid=(n // tm, nf),
        in_specs=[row(), row(),
                  pl.BlockSpec((d, fc), lambda i, f: (0, f), pipeline_mode=mode),
                  pl.BlockSpec((d, fc), lambda i, f: (0, f), pipeline_mode=mode),
                  pl.BlockSpec((fc, d), lambda i, f: (f, 0), pipeline_mode=mode),
                  vec, vec],
        out_specs=row(),
        out_shape=jax.ShapeDtypeStruct((n, d), F32),
        scratch_shapes=[pltpu.VMEM((tm, d), F32)],
        compiler_params=_params(("parallel", "arbitrary")),
    )(xh, x2d, w1, w3, w2, g.reshape(1, d), b.reshape(1, d))


def _add_ln_kernel(x_ref, f_ref, g_ref, b_ref, o_ref):
    o_ref[...] = _deepnorm(x_ref[...], f_ref[...], g_ref[...], b_ref[...])


def _add_ln(x2d, f2d, g, b):
    n, d = x2d.shape
    tm = PROJ_ROWS
    row = pl.BlockSpec((tm, d), lambda i: (i, 0))
    vec = pl.BlockSpec((1, d), lambda i: (0, 0))
    return pl.pallas_call(
        _add_ln_kernel,
        grid=(n // tm,),
        in_specs=[row, row, vec, vec],
        out_specs=row,
        out_shape=jax.ShapeDtypeStruct((n, d), F32),
        compiler_params=_params(("parallel",)),
    )(x2d, f2d, g.reshape(1, d), b.reshape(1, d))


def _router_kernel(xh_ref, rw_ref, gate_ref, lposc_ref, lpost_ref, cum_ref, carry_ref, *, tiles_per_group):
    tm = xh_ref.shape[0]
    i = pl.program_id(0)

    @pl.when(i % tiles_per_group == 0)
    def _():
        carry_ref[...] = jnp.zeros_like(carry_ref)

    lane = lax.broadcasted_iota(I32, (1, LANES), 1)
    logits = jnp.where(lane < N_EXPERTS, _dot(xh_ref[...], rw_ref[...]), NEG_INF)
    m1 = jnp.max(logits, axis=1, keepdims=True)
    i1 = jnp.min(jnp.where(logits == m1, lane, LANES), axis=1, keepdims=True)
    rest = jnp.where(lane == i1, NEG_INF, logits)
    m2 = jnp.max(rest, axis=1, keepdims=True)
    i2 = jnp.min(jnp.where(rest == m2, lane, LANES), axis=1, keepdims=True)
    e2 = jnp.exp(m2 - m1)
    g1 = 1.0 / (1.0 + e2)
    g2 = e2 / (1.0 + e2)
    sel1 = lane == i1
    sel2 = lane == i2
    routed = sel1 | sel2
    gate_ref[...] = jnp.where(sel1, g1, jnp.where(sel2, g2, 0.0))
    routedf = jnp.where(routed, 1.0, 0.0)
    earlier = (lax.broadcasted_iota(I32, (tm, tm), 1) < lax.broadcasted_iota(I32, (tm, tm), 0))
    earlier = jnp.where(earlier, 1.0, 0.0).astype(BF16)
    carry = carry_ref[...]
    lpos = jnp.where(routed, _dot(earlier, routedf.astype(BF16)) + carry, -1.0)
    lposc_ref[...] = lpos
    lpost_ref[...] = jnp.transpose(lpos)[0:N_EXPERTS, :]
    carry = carry + jnp.sum(routedf, axis=0, keepdims=True)
    carry_ref[...] = carry
    cum_ref[...] = carry.astype(I32).reshape(1, 1, LANES)


def _router(xh, rw):
    n, d = xh.shape
    tm = ROUTER_ROWS
    nt = n // tm
    row = pl.BlockSpec((tm, LANES), lambda i: (i, 0))
    return pl.pallas_call(
        functools.partial(_router_kernel, tiles_per_group=MOE_GROUP // tm),
        grid=(nt,),
        in_specs=[pl.BlockSpec((tm, d), lambda i: (i, 0)), pl.BlockSpec(rw.shape, lambda i: (0, 0))],
        out_specs=[row, row,
                   pl.BlockSpec((N_EXPERTS, tm), lambda i: (0, i)),
                   pl.BlockSpec((1, 1, LANES), lambda i: (i, 0, 0))],
        out_shape=[jax.ShapeDtypeStruct((n, LANES), F32), jax.ShapeDtypeStruct((n, LANES), F32),
                   jax.ShapeDtypeStruct((N_EXPERTS, n), F32), jax.ShapeDtypeStruct((nt, 1, LANES), I32)],
        scratch_shapes=[pltpu.VMEM((1, LANES), F32)],
        compiler_params=_params(("arbitrary",)),
    )(xh, rw)


def _moe_kernel(bnd_ref, xh_ref, lpost_ref, lposc_ref, gatec_ref, w1_ref, w3_ref, w2_ref,
                o_ref, xg_ref, yacc_ref):
    tg = xh_ref.shape[0]
    win, ch, fr, align = MOE_WIN, ROUTER_ROWS, MOE_FFN_ROWS, MOE_ROW_ALIGN
    nch = tg // ch
    g, e, f = pl.program_id(0), pl.program_id(1), pl.program_id(2)
    nf = pl.num_programs(2)

    def bound(c):
        return bnd_ref[(g * (nch + 1) + c) * N_EXPERTS + e]

    cnt = bound(nch)
    n_ffn = (cnt + fr - 1) // fr

    def tok_chunk(c):
        return pl.ds(pl.multiple_of(c * ch, ch), ch)

    def windows(c):
        r0, r1 = bound(c), bound(c + 1)
        first = (r0 // align) * align
        return first, jnp.where(r1 > r0, (r1 - first + win - 1) // win, 0)

    one_window = functools.reduce(jnp.logical_and, [windows(c)[1] <= 1 for c in range(nch)])

    @pl.when((e == 0) & (f == 0))
    def _():
        o_ref[...] = jnp.zeros_like(o_ref)

    @pl.when(f == 0)
    def _compact():
        def clear(s, carry):
            rows = pl.ds(pl.multiple_of(s * win, win), win)
            xg_ref[rows, :] = jnp.zeros((win, xg_ref.shape[1]), xg_ref.dtype)
            yacc_ref[rows, :] = jnp.zeros((win, yacc_ref.shape[1]), F32)
            return carry

        lax.fori_loop(0, (cnt + max(fr, win) + win - 1) // win, clear, 0)

        def per_chunk(c, carry):
            first, n_win = windows(c)
            lp = lpost_ref[pl.ds(e, 1), tok_chunk(c)]

            def per_window(w, carry2):
                start = pl.multiple_of(first + w * win, align)
                want = (lax.broadcasted_iota(I32, (win, 1), 0) + start).astype(F32)
                pick = jnp.where(lp == want, 1.0, 0.0).astype(BF16)
                rows = pl.ds(start, win)
                xg_ref[rows, :] += _dot(pick, xh_ref[tok_chunk(c), :]).astype(xg_ref.dtype)
                return carry2

            return lax.fori_loop(0, n_win, per_window, carry)

        @pl.when(one_window)
        def _():
            for c in range(nch):
                toks = slice(c * ch, (c + 1) * ch)
                start = pl.multiple_of(windows(c)[0], align)
                want = (lax.broadcasted_iota(I32, (win, 1), 0) + start).astype(F32)
                pick = jnp.where(lpost_ref[pl.ds(e, 1), toks] == want, 1.0, 0.0).astype(BF16)
                xg_ref[pl.ds(start, win), :] += _dot(pick, xh_ref[toks, :]).astype(xg_ref.dtype)

        @pl.when(jnp.logical_not(one_window))
        def _():
            lax.fori_loop(0, nch, per_chunk, 0)

    def ffn_rows(rows):
        xs = xg_ref[rows, :]
        hid = (_silu(_dot(xs, w1_ref[...])) * _dot(xs, w3_ref[...])).astype(BF16)
        yacc_ref[rows, :] += _dot(hid, w2_ref[...].astype(BF16))

    def ffn_pair(p, carry):
        ffn_rows(pl.ds(pl.multiple_of(p * 2 * fr, align), 2 * fr))
        return carry

    lax.fori_loop(0, n_ffn // 2, ffn_pair, 0)

    @pl.when(n_ffn % 2 == 1)
    def _():
        ffn_rows(pl.ds(pl.multiple_of((n_ffn - 1) * fr, align), fr))

    @pl.when(f == nf - 1)
    def _scatter():
        lane = lax.broadcasted_iota(I32, (1, LANES), 1)

        def per_chunk(c, carry):
            first, n_win = windows(c)
            toks = tok_chunk(c)
            lp = jnp.sum(jnp.where(lane == e, lposc_ref[toks, :], 0.0), axis=1, keepdims=True)
            gt = jnp.sum(jnp.where(lane == e, gatec_ref[toks, :], 0.0), axis=1, keepdims=True)

            def per_window(w, carry2):
                start = pl.multiple_of(first + w * win, align)
                y = yacc_ref[pl.ds(start, win), :]
                y_hi = y.astype(BF16)
                y_lo = (y - y_hi.astype(F32)).astype(BF16)
                want = (lax.broadcasted_iota(I32, (1, win), 1) + start).astype(F32)
                place = jnp.where(lp == want, 1.0, 0.0).astype(BF16)
                o_ref[toks, :] += gt * (_dot(place, y_hi) + _dot(place, y_lo))
                return carry2

            return lax.fori_loop(0, n_win, per_window, carry)

        @pl.when(one_window)
        def _():
            for c in range(nch):
                toks = slice(c * ch, (c + 1) * ch)
                start = pl.multiple_of(windows(c)[0], align)
                lp = jnp.sum(jnp.where(lane == e, lposc_ref[toks, :], 0.0), axis=1, keepdims=True)
                gt = jnp.sum(jnp.where(lane == e, gatec_ref[toks, :], 0.0), axis=1, keepdims=True)
                y = yacc_ref[pl.ds(start, win), :]
                y_hi = y.astype(BF16)
                y_lo = (y - y_hi.astype(F32)).astype(BF16)
                want = (lax.broadcasted_iota(I32, (1, win), 1) + start).astype(F32)
                place = jnp.where(lp == want, 1.0, 0.0).astype(BF16)
                o_ref[toks, :] += gt * (_dot(place, y_hi) + _dot(place, y_lo))

        @pl.when(jnp.logical_not(one_window))
        def _():
            lax.fori_loop(0, nch, per_chunk, 0)


def _chunk_bounds(cum):
    nch = MOE_GROUP // ROUTER_ROWS
    per_group = cum[:, 0, :N_EXPERTS].reshape(-1, nch, N_EXPERTS)
    return jnp.pad(per_group, ((0, 0), (1, 0), (0, 0))).reshape(-1)


def _moe(bounds, xh, lpost, lposc, gatec, w1, w3, w2, layer):
    n, d = xh.shape
    tg = MOE_GROUP
    dff = w1.shape[3]
    fc = MOE_FF_CHUNK
    cap = -(-(tg + max(MOE_FFN_ROWS, MOE_WIN)) // MOE_WIN) * MOE_WIN
    grid_spec = pltpu.PrefetchScalarGridSpec(
        num_scalar_prefetch=1,
        grid=(n // tg, N_EXPERTS, dff // fc),
        in_specs=[pl.BlockSpec((tg, d), lambda g, e, f, c: (g, 0)),
                  pl.BlockSpec((N_EXPERTS, tg), lambda g, e, f, c: (0, g)),
                  pl.BlockSpec((tg, LANES), lambda g, e, f, c: (g, 0)),
                  pl.BlockSpec((tg, LANES), lambda g, e, f, c: (g, 0)),
                  pl.BlockSpec((None, None, d, fc), lambda g, e, f, c: (layer, e, 0, f)),
                  pl.BlockSpec((None, None, d, fc), lambda g, e, f, c: (layer, e, 0, f)),
                  pl.BlockSpec((None, None, fc, d), lambda g, e, f, c: (layer, e, f, 0))],
        out_specs=pl.BlockSpec((tg, d), lambda g, e, f, c: (g, 0)),
        scratch_shapes=[pltpu.VMEM((cap, d), BF16), pltpu.VMEM((cap, d), F32)],
    )
    return pl.pallas_call(
        _moe_kernel,
        grid_spec=grid_spec,
        out_shape=jax.ShapeDtypeStruct((n, d), F32),
        compiler_params=_params(("parallel", "arbitrary", "arbitrary")),
    )(bounds, xh, lpost, lposc, gatec, w1, w3, w2)


EVEN_KB_SECTION = 2


def _even_weight_layout(w_in):
    hd = HEAD_DIM
    widths = (A_HEADS * hd, hd, hd, IDX_HEADS * IDX_DIM, IDX_DIM, IDX_HEADS, B_HEADS * hd, B_HEADS * hd, B_HEADS * hd)
    offs = [0]
    for wd in widths:
        offs.append(offs[-1] + wd)
    qa, ka, va, iq, ik, iw, qb, kb, vb = (w_in[:, offs[j]:offs[j + 1]] for j in range(9))
    iw_pad = jnp.pad(iw, ((0, 0), (0, LANES - IDX_HEADS)))
    w = jnp.concatenate([qa, qb, kb, vb, ka, ka, va, va, iq, ik, ik, ik, ik, iw_pad], axis=1)
    mixw = B_HEADS * hd
    sections = ((0, A_HEADS * hd, hd, QK_SCALE),
                (512, mixw, hd, QK_SCALE),
                (1024, mixw, hd, None),
                (1536, mixw, None, None),
                (2048, LANES, hd, None),
                (2176, LANES, None, None),
                (2304, IDX_HEADS * IDX_DIM, IDX_DIM, None),
                (2560, LANES, IDX_DIM, None),
                (2688, LANES, None, None))
    dtypes = (BF16,) * 8 + (F32,)
    return w.astype(BF16), sections, dtypes


def kernel(x, even_w_in, even_w_out, even_ln1_g, even_ln1_b, even_w1, even_w3, even_w2, even_ln2_g, even_ln2_b, odd_w_in, odd_w_out, odd_ln1_g, odd_ln1_b, odd_router, odd_w1, odd_w3, odd_w2, odd_ln2_g, odd_ln2_b):
    batch, seq, d = x.shape
    n = batch * seq
    tables = _rope_tables(seq)
    x2d = x.reshape(n, d)
    odd_w1h, odd_w3h = odd_w1.astype(BF16), odd_w3.astype(BF16)
    for layer in range(DEPTH):
        i = layer // 2
        if layer % 2 == 0:
            w, sections, dtypes = _even_weight_layout(even_w_in[i])
            qa, qb, kb, vb, kk, vv, iq, ik4, iw, kb_mean = _project(x2d, w, tables, sections, dtypes, seq,
                                                                    mean_section=EVEN_KB_SECTION)
            o_a = _dsa_attention(iq, iw, ik4, qa, kk, vv, batch, seq)
            o_b = _moba_attention(qb, kb, vb, kb_mean, batch, seq)
            x2d, xh = _outproj_ln([o_a, o_b], even_w_out[i].astype(BF16), x2d, even_ln1_g[i], even_ln1_b[i])
            x2d = _swiglu_ln(xh, x2d, even_w1[i].astype(BF16), even_w3[i].astype(BF16), even_w2[i].astype(BF16),
                             even_ln2_g[i], even_ln2_b[i])
        else:
            mix = C_HEADS * HEAD_DIM
            sections = ((0, mix, HEAD_DIM, QK_SCALE), (mix, mix, HEAD_DIM, None), (2 * mix, mix, None, None))
            q, k, v = _project(x2d, odd_w_in[i].astype(BF16), tables, sections, (F32, F32, F32), seq)
            o = _dilated_attention(q, k, v, batch, seq)
            x2d, xh = _outproj_ln([o], odd_w_out[i].astype(BF16), x2d, odd_ln1_g[i], odd_ln1_b[i])
            rw = jnp.pad(odd_router[i], ((0, 0), (0, LANES - N_EXPERTS))).astype(BF16)
            gatec, lposc, lpost, cum = _router(xh, rw)
            ffn = _moe(_chunk_bounds(cum), xh, lpost, lposc, gatec, odd_w1h, odd_w3h, odd_w2, i)
            x2d = _add_ln(x2d, ffn, odd_ln2_g[i], odd_ln2_b[i])
    return x2d.reshape(batch, seq, d)
```

```python
import functools

import jax
import jax.numpy as jnp
from jax import lax
from jax.experimental import pallas as pl
from jax.experimental.pallas import tpu as pltpu

F32 = jnp.float32
BF16 = jnp.bfloat16
I32 = jnp.int32

DEPTH = 4
HEAD_DIM = 64
ROPE_THETA = 10000.0
LN_EPS = 1e-5
A_HEADS = 8
IDX_HEADS = 8
IDX_DIM = 32
DSA_TOPK = 256
B_HEADS = 8
MOBA_BLOCK = 256
MOBA_TOPK = 3
C_HEADS = 16
DILATED_CFG = ((128, 1), (512, 4), (2048, 16))
BAND_BLOCK = 128
N_EXPERTS = 8
DEEPNORM_ALPHA = (2 * DEPTH) ** 0.25
QK_SCALE = HEAD_DIM ** -0.5 * 1.4426950408889634

LANES = 128
VMEM_LIMIT_BYTES = 54 * 1024 * 1024

PROJ_ROWS = 1024
FUSED_PROJ_ROWS = 512
FFN_ROWS = 512
FFN_CHUNKS = 1
ROUTER_ROWS = 512
MOE_GROUP = 2048
MOE_WIN = 256
MOE_FFN_ROWS = 272
MOE_ROW_ALIGN = 16
MOE_FF_CHUNK = 512
DSA_ROWS = 256
DSA_SEARCH_GROUPS = 4
DSA_SEARCH_UNROLL = 4
MOBA_GATE_ROWS = 16
DILATED_UNITS = 16
DILATED_SLABS = 4
ATTN_KEY_STEP = 512
MOBA_STEP_BLOCKS = 8

NEG_INF = float("-inf")


def _params(semantics):
    return pltpu.CompilerParams(dimension_semantics=semantics, vmem_limit_bytes=VMEM_LIMIT_BYTES)


def _dot(a, b):
    return jnp.dot(a, b, preferred_element_type=F32)


def _dot_nt(a, b):
    return lax.dot_general(a, b, (((1,), (1,)), ((), ())), preferred_element_type=F32)


def _lane_group(shape, group, idx):
    lane = lax.broadcasted_iota(I32, shape, len(shape) - 1)
    return (lane // group) == idx


def _head_pair_attention(units):
    first = _lane_group((1, LANES), HEAD_DIM, 0)
    raw, unit_masks = [], []
    for q2, k2, _, mask in units:
        raw.append(_dot_nt(jnp.where(first, q2, jnp.zeros_like(q2)), k2))
        unit_masks.append(mask() if callable(mask) else mask)
        raw.append(_dot_nt(jnp.where(first, jnp.zeros_like(q2), q2), k2))
    scores = []
    for u, mask in enumerate(unit_masks):
        masks = mask if isinstance(mask, (tuple, list)) else (mask, mask)
        for h in range(2):
            if masks[h].dtype == jnp.bool_:
                scores.append(jnp.where(masks[h], raw[2 * u + h], NEG_INF))
            else:
                scores.append(raw[2 * u + h] + masks[h])
    maxes = [jnp.max(s, axis=1, keepdims=True) for s in scores]
    weights = [jnp.exp2(s - mx).astype(BF16) for s, mx in zip(scores, maxes)]
    results = []
    for u, (_, _, v2, _) in enumerate(units):
        ones = jnp.ones_like(v2)
        pv0 = _dot(weights[2 * u], jnp.where(first, v2, ones))
        pv1 = _dot(weights[2 * u + 1], jnp.where(first, ones, v2))
        den = jnp.where(first, pv0[:, LANES - 1:LANES], pv1[:, 0:1])
        results.append((jnp.where(first, maxes[2 * u], maxes[2 * u + 1]), den, jnp.where(first, pv0, pv1)))
    return results


def _stacked_head_pair_attention(units):
    first = _lane_group((1, LANES), HEAD_DIM, 0)
    raw = [_dot_nt(jnp.concatenate([jnp.where(first, q2, jnp.zeros_like(q2)),
                                    jnp.where(first, jnp.zeros_like(q2), q2)], axis=0), k2)
           for q2, k2, _, _ in units]
    stats, weights = [], []
    for r, (q2, _, _, mask) in zip(raw, units):
        rows = q2.shape[0]
        halves = [jnp.where(mask, r[h * rows:(h + 1) * rows], NEG_INF) for h in range(2)]
        maxes = [jnp.max(s, axis=1, keepdims=True) for s in halves]
        exps = [jnp.exp2(s - mx) for s, mx in zip(halves, maxes)]
        dens = [jnp.sum(e, axis=1, keepdims=True) for e in exps]
        stats.append((jnp.where(first, maxes[0], maxes[1]), jnp.where(first, dens[0], dens[1])))
        weights.append(jnp.concatenate(exps, axis=0).astype(BF16))
    results = []
    for (mx, den), w, (q2, _, v2, _) in zip(stats, weights, units):
        rows = q2.shape[0]
        pv = _dot(w, v2)
        results.append((mx, den, jnp.where(first, pv[:rows], pv[rows:])))
    return results


def _rope_lanes(t, cosf, sinf, half):
    lane = lax.broadcasted_iota(I32, t.shape, 1)
    first = (lane % (2 * half)) < half
    swapped = jnp.where(first, pltpu.roll(t, LANES - half, 1), pltpu.roll(t, half, 1))
    return t * cosf + swapped * sinf


def _proj_kernel(*refs, sections, mean_section, fused_norm):
    if fused_norm:
        x_ref, f_ref, g_ref, b_ref, w_ref, c64_ref, s64_ref, c32_ref, s32_ref, *out_refs = refs
        xf = _deepnorm(x_ref[...], f_ref[...], g_ref[...], b_ref[...])
        out_refs[-1][...] = xf
        x = xf.astype(BF16)
    else:
        x_ref, w_ref, c64_ref, s64_ref, c32_ref, s32_ref, *out_refs = refs
        x = x_ref[...].astype(BF16)
    for idx, (o_ref, (start, width, rope, scale)) in enumerate(zip(out_refs, sections)):
        h = _dot(x, w_ref[:, start:start + width])
        if rope is not None:
            cosf = (c64_ref if rope == HEAD_DIM else c32_ref)[...]
            sinf = (s64_ref if rope == HEAD_DIM else s32_ref)[...]
            parts = [_rope_lanes(h[:, g * LANES:(g + 1) * LANES], cosf, sinf, rope // 2)
                     for g in range(width // LANES)]
            h = parts[0] if len(parts) == 1 else jnp.concatenate(parts, axis=1)
        if scale is not None:
            h = h * scale
        o_ref[...] = h.astype(o_ref.dtype)
        if idx == mean_section:
            m_ref = out_refs[len(sections)]
            for blk in range(h.shape[0] // MOBA_BLOCK):
                rows = h[blk * MOBA_BLOCK:(blk + 1) * MOBA_BLOCK]
                m_ref[0, blk:blk + 1, :] = jnp.sum(rows, axis=0, keepdims=True) * (1.0 / MOBA_BLOCK)


def _project(x2d, w, tables, sections, out_dtypes, seq, mean_section=None, norm_with=None):
    n, d = x2d.shape
    tm = PROJ_ROWS if norm_with is None else FUSED_PROJ_ROWS
    pos_blocks = seq // tm
    row = pl.BlockSpec((tm, d), lambda i: (i, 0))
    tab_spec = pl.BlockSpec((tm, LANES), lambda i: (i % pos_blocks, 0))
    out_specs = [pl.BlockSpec((tm, sec[1]), lambda i: (i, 0)) for sec in sections]
    out_shape = [jax.ShapeDtypeStruct((n, sec[1]), dt) for sec, dt in zip(sections, out_dtypes)]
    if mean_section is not None:
        per_tile, width = tm // MOBA_BLOCK, sections[mean_section][1]
        out_specs.append(pl.BlockSpec((1, per_tile, width), lambda i: (i, 0, 0)))
        out_shape.append(jax.ShapeDtypeStruct((n // tm, per_tile, width), F32))
    operands, in_specs = [x2d], [row]
    if norm_with is not None:
        f2d, g, b = norm_with
        vec = pl.BlockSpec((1, d), lambda i: (0, 0))
        operands += [f2d, g.reshape(1, d), b.reshape(1, d)]
        in_specs += [row, vec, vec]
        out_specs.append(row)
        out_shape.append(jax.ShapeDtypeStruct((n, d), F32))
    outs = pl.pallas_call(
        functools.partial(_proj_kernel, sections=sections, mean_section=mean_section,
                          fused_norm=norm_with is not None),
        grid=(n // tm,),
        in_specs=in_specs + [pl.BlockSpec(w.shape, lambda i: (0, 0)), tab_spec, tab_spec, tab_spec, tab_spec],
        out_specs=out_specs,
        out_shape=out_shape,
        compiler_params=_params(("parallel",)),
    )(*operands, w, *tables)
    outs = list(outs)
    if mean_section is not None:
        outs[len(sections)] = outs[len(sections)].reshape(n // MOBA_BLOCK, -1)
    return outs


def _rope_tables(seq):
    out = []
    for dim in (HEAD_DIM, IDX_DIM):
        inv = ROPE_THETA ** (-jnp.arange(0, dim, 2, dtype=F32) / dim)
        ang = jnp.arange(seq, dtype=F32)[:, None] * inv[None, :]
        cos, sin = jnp.cos(ang), jnp.sin(ang)
        reps = LANES // dim
        out.append(jnp.tile(jnp.concatenate([cos, cos], -1), (1, reps)))
        out.append(jnp.tile(jnp.concatenate([-sin, sin], -1), (1, reps)))
    return out


def _dsa_block(i, iq_ref, iw_ref, ik_ref, q_ref, kk_ref, vv_ref, o_ref, key_ref, sel_ref, *, kext, topk):
    qb = DSA_ROWS
    keys = slice(0, kext)
    qpos = i * qb + lax.broadcasted_iota(I32, (qb, 1), 0)
    kpos = lax.broadcasted_iota(I32, (1, kext), 1)
    causal = kpos <= qpos

    ik = ik_ref[keys, :]
    iw = iw_ref[...] * (IDX_DIM ** -0.5 * IDX_HEADS ** -0.5)
    heads_per_group = LANES // IDX_DIM
    score = jnp.zeros((qb, kext), F32)
    for h in range(IDX_HEADS):
        grp, j = divmod(h, heads_per_group)
        iqg = iq_ref[:, grp * LANES:(grp + 1) * LANES]
        iqm = jnp.where(_lane_group((1, LANES), IDX_DIM, j), iqg, jnp.zeros_like(iqg))
        rel = jnp.maximum(_dot_nt(iqm, ik), 0.0)
        score = score + rel * iw[:, h:h + 1]
    key_ref[:, keys] = jnp.where(causal, score, NEG_INF)

    int_min = jnp.int32(-2 ** 31)
    lowest = jnp.float32(-3.4028235e38)

    def as_float(code):
        bits = jnp.where(code >= 0, code, (code - 1) ^ 0x7FFFFFFF)
        return pltpu.bitcast(bits, F32)

    grp_rows = qb // DSA_SEARCH_GROUPS
    groups = [slice(g * grp_rows, (g + 1) * grp_rows) for g in range(DSA_SEARCH_GROUPS)]

    def count_ge(rows, code):
        return jnp.sum(jnp.where(key_ref[rows, keys] >= as_float(code), 1.0, 0.0), axis=1, keepdims=True)

    def search(it, thrs):
        bit = jnp.left_shift(jnp.int32(1), 30 - it)
        return tuple(jnp.where(count_ge(rows, thr | bit) >= topk, thr | bit, thr)
                     for rows, thr in zip(groups, thrs))

    zero = jnp.zeros((grp_rows, 1), I32)
    thrs = tuple(jnp.where(count_ge(rows, zero) >= topk, zero, int_min) for rows in groups)
    thrs = lax.fori_loop(0, 31, search, thrs, unroll=DSA_SEARCH_UNROLL)
    thr = jnp.maximum(as_float(jnp.concatenate(thrs, axis=0)), lowest)

    key = key_ref[:, keys]
    sel_ref[:, keys] = jnp.where(key >= thr, 0.0, NEG_INF)
    n_ge = jnp.sum(jnp.where(key >= thr, 1.0, 0.0), axis=1, keepdims=True)

    @pl.when(jnp.max(jnp.where(n_ge > topk, 1, 0)) > 0)
    def _break_ties():
        upper = (lax.broadcasted_iota(I32, (LANES, LANES), 0)
                 < lax.broadcasted_iota(I32, (LANES, LANES), 1))
        upper = jnp.where(upper, 1.0, 0.0).astype(BF16)
        n_gt = jnp.sum(jnp.where(key_ref[:, keys] > thr, 1.0, 0.0), axis=1, keepdims=True)
        need = topk - n_gt
        before = jnp.zeros((qb, 1), F32)
        for c in range(kext // LANES):
            sl = slice(c * LANES, (c + 1) * LANES)
            keyc = key_ref[:, sl]
            eqc = keyc == thr
            eqf = jnp.where(eqc, 1.0, 0.0)
            rank = _dot(eqf.astype(BF16), upper) + before
            take = (keyc > thr) | (eqc & (rank < need))
            sel_ref[:, sl] = jnp.where(take, 0.0, NEG_INF)
            before = before + jnp.sum(eqf, axis=1, keepdims=True)

    selected = sel_ref[:, keys]
    kk = kk_ref[keys, :]
    vv = vv_ref[keys, :]
    pairs = [(q_ref[:, hp * LANES:(hp + 1) * LANES], kk, vv, selected) for hp in range(A_HEADS // 2)]
    for hp, (_, den, num) in enumerate(_head_pair_attention(pairs)):
        o_ref[:, hp * LANES:(hp + 1) * LANES] = (num / den).astype(o_ref.dtype)


def _dsa_kernel(*refs, seq, topk):
    i = pl.program_id(1)
    blocks_per_step = ATTN_KEY_STEP // DSA_ROWS
    for v in range(seq // ATTN_KEY_STEP):
        @pl.when(i // blocks_per_step == v)
        def _(v=v):
            _dsa_block(i, *refs, kext=(v + 1) * ATTN_KEY_STEP, topk=topk)


def _dsa_attention(iq, iw, ik4, qa, kk, vv, batch, seq):
    n = batch * seq
    qb = DSA_ROWS
    nq = seq // qb
    topk = min(DSA_TOPK, seq // 4)
    rows = lambda width: pl.BlockSpec((qb, width), lambda b, i: (b * nq + i, 0))
    whole = pl.BlockSpec((seq, LANES), lambda b, i: (b, 0))
    return pl.pallas_call(
        functools.partial(_dsa_kernel, seq=seq, topk=topk),
        grid=(batch, nq),
        in_specs=[rows(iq.shape[1]), rows(LANES), whole, rows(qa.shape[1]), whole, whole],
        out_specs=rows(qa.shape[1]),
        out_shape=jax.ShapeDtypeStruct((n, qa.shape[1]), BF16),
        scratch_shapes=[pltpu.VMEM((qb, seq), F32), pltpu.VMEM((qb, seq), F32)],
        compiler_params=_params(("parallel", "parallel")),
    )(iq, iw, ik4, qa, kk, vv)


def _moba_step(step, q_ref, k_ref, v_ref, kmean_ref, o_ref, *, kext, kt):
    bs, per = MOBA_BLOCK, MOBA_STEP_BLOCKS
    nb_all = kmean_ref.shape[0]
    nbp = max(nb_all, MOBA_GATE_ROWS)
    kmean = jnp.concatenate([kmean_ref[...], jnp.zeros((nbp - nb_all, LANES), F32)], axis=0).astype(BF16)
    n_iota = lax.broadcasted_iota(I32, (nbp, 1), 0)
    qrow = lax.broadcasted_iota(I32, (bs, 1), 0)
    units = []
    for j in range(per):
        qi = step * per + j
        ext = kext - (per - 1 - j) * bs
        nb = ext // bs
        q2 = q_ref[j * bs:(j + 1) * bs, :]
        blk_row = lax.broadcasted_iota(I32, (nbp, ext), 0)
        blk_of_key = lax.broadcasted_iota(I32, (nbp, ext), 1) // bs
        member = jnp.where(blk_row == blk_of_key, 1.0, 0.0).astype(BF16)
        past = n_iota < qi
        kcol = lax.broadcasted_iota(I32, (1, ext), 1)
        own = (kcol >= qi * bs) & (kcol <= qi * bs + qrow)
        gates = [jnp.where(past, _dot_nt(kmean, jnp.where(_lane_group((1, LANES), HEAD_DIM, half), q2,
                                                          jnp.zeros_like(q2))), NEG_INF)
                 for half in range(2)]

        def block_masks(gates=gates, past=past, own=own, member=member, nb=nb):
            chosen = []
            for gate in gates:
                rank = jnp.zeros((nbp, bs), I32)
                for m in range(nb):
                    gm = gate[m:m + 1, :]
                    beats = (gm > gate) | ((gm == gate) & (m < n_iota))
                    rank = rank + beats.astype(I32)
                chosen.append(jnp.transpose(jnp.where(past & (rank < kt), 1.0, 0.0)).astype(BF16))
            return [(_dot(c, member) > 0.5) | own for c in chosen]

        units.append((q2, k_ref[0:ext, :], v_ref[0:ext, :], block_masks))
    for j, (_, den, num) in enumerate(_head_pair_attention(units)):
        o_ref[j * bs:(j + 1) * bs, :] = (num / den).astype(o_ref.dtype)


def _moba_kernel(*refs, seq):
    step = pl.program_id(2)
    kt = min(MOBA_TOPK, seq // MOBA_BLOCK - 1)
    rows = MOBA_BLOCK * MOBA_STEP_BLOCKS
    for v in range(seq // rows):
        @pl.when(step == v)
        def _(v=v):
            _moba_step(step, *refs, kext=(v + 1) * rows, kt=kt)


def _moba_attention(q, k, v, kmean, batch, seq):
    n, width = q.shape
    tq = MOBA_BLOCK * MOBA_STEP_BLOCKS
    steps = seq // tq
    rows = pl.BlockSpec((tq, LANES), lambda b, hp, st: (b * steps + st, hp))
    whole = pl.BlockSpec((seq, LANES), lambda b, hp, st: (b, hp))
    means = pl.BlockSpec((seq // MOBA_BLOCK, LANES), lambda b, hp, st: (b, hp))
    return pl.pallas_call(
        functools.partial(_moba_kernel, seq=seq),
        grid=(batch, width // LANES, steps),
        in_specs=[rows, whole, whole, means],
        out_specs=rows,
        out_shape=jax.ShapeDtypeStruct((n, width), BF16),
        compiler_params=_params(("parallel", "parallel", "parallel")),
    )(q, k, v, kmean)


def _rows(start, size, stride):
    return pl.ds(start, size) if stride == 1 else pl.ds(start, size, stride=stride)


def _largest_divisor(n, limit):
    return max(d for d in range(1, limit + 1) if n % d == 0)


def _repeat(trips, body):
    if trips == 1:
        body(0, 0)
    else:
        lax.fori_loop(0, trips, body, 0)


def _dilated_kernel(q_ref, k_ref, v_ref, o_ref, qp_ref, kp_ref, vp_ref, nat_ref, slab_ref, *, seq):
    blk = BAND_BLOCK
    ns = DILATED_SLABS
    slab_len = seq // ns

    for s in range(ns):
        rows = slice(s * slab_len, (s + 1) * slab_len)
        qp_ref[rows, :] = q_ref[pl.ds(s, slab_len, stride=ns), :]
        kp_ref[rows, :] = k_ref[pl.ds(s, slab_len, stride=ns), :]
        vp_ref[rows, :] = v_ref[pl.ds(s, slab_len, stride=ns), :]

    def operands(srcs, qstart, kstart, nk, stride):
        ksl = _rows(kstart, nk, stride)
        i = lax.broadcasted_iota(I32, (blk, 1), 0)
        j = lax.broadcasted_iota(I32, (1, nk), 1)
        if nk == blk:
            mask = j <= i
        else:
            mask = (j >= i) & (j <= i + blk)
        return (srcs[0][_rows(qstart, blk, stride), :].astype(BF16), srcs[1][ksl, :].astype(BF16),
                srcs[2][ksl, :].astype(BF16), mask)

    def merged(old, new):
        (m_old, l_old, a_old), (m_new, l_new, a_new) = old, new
        m_tot = jnp.maximum(m_old, m_new)
        w_old = jnp.exp2(m_old - m_tot)
        w_new = jnp.exp2(m_new - m_tot)
        return m_tot, w_old * l_old + w_new * l_new, w_old * a_old + w_new * a_new

    def run_units(srcs, state_ref, units, nk, stride, merge):
        results = _stacked_head_pair_attention([operands(srcs, qstart, kstart, nk, stride)
                                                for qstart, kstart in units])
        for (qstart, _), new in zip(units, results):
            qsl = _rows(qstart, blk, stride)
            if merge:
                new = merged(tuple(state_ref[t, qsl, :] for t in range(3)), new)
            for t in range(3):
                state_ref[t, qsl, :] = new[t]

    written = set()
    for window, dil in DILATED_CFG:
        assert window // dil == blk
        if dil % ns == 0:
            srcs, state_ref, stride, group_len = (qp_ref, kp_ref, vp_ref), slab_ref, dil // ns, slab_len
        else:
            srcs, state_ref, stride, group_len = (q_ref, k_ref, v_ref), nat_ref, dil, seq
        merge = id(state_ref) in written
        written.add(id(state_ref))
        nblk = seq // dil // blk
        ub = _largest_divisor(dil, DILATED_UNITS)

        def class_start(c, stride=stride, group_len=group_len):
            return (c // stride) * group_len + c % stride if group_len != seq else c

        def first_blocks(it, carry, srcs=srcs, state_ref=state_ref, stride=stride, merge=merge, ub=ub,
                         class_start=class_start):
            starts = [class_start(it * ub + u) for u in range(ub)]
            run_units(srcs, state_ref, [(st, st) for st in starts], blk, stride, merge)
            return carry

        _repeat(dil // ub, first_blocks)
        if nblk > 1:
            later = nblk - 1
            ul = _largest_divisor(dil * later, DILATED_UNITS)

            def later_blocks(it, carry, srcs=srcs, state_ref=state_ref, stride=stride, merge=merge, ul=ul,
                             later=later, class_start=class_start):
                flat = [it * ul + u for u in range(ul)]
                qstarts = [class_start(t // later) + stride * blk * (1 + t % later) for t in flat]
                run_units(srcs, state_ref, [(qs, qs - stride * blk) for qs in qstarts], 2 * blk, stride, merge)
                return carry

            _repeat(dil * later // ul, later_blocks)

    for s in range(ns):
        nat_rows = pl.ds(s, slab_len, stride=ns)
        slab_rows = slice(s * slab_len, (s + 1) * slab_len)
        _, den, num = merged(tuple(nat_ref[t, nat_rows, :] for t in range(3)),
                             tuple(slab_ref[t, slab_rows, :] for t in range(3)))
        nat_ref[2, nat_rows, :] = num / den
    o_ref[...] = nat_ref[2].astype(o_ref.dtype)


def _dilated_attention(q, k, v, batch, seq):
    n, width = q.shape
    for window, dil in DILATED_CFG:
        assert seq % (dil * BAND_BLOCK) == 0
    assert {dil % DILATED_SLABS == 0 for _, dil in DILATED_CFG} == {True, False}
    whole = pl.BlockSpec((seq, LANES), lambda b, hp: (b, hp))
    return pl.pallas_call(
        functools.partial(_dilated_kernel, seq=seq),
        grid=(batch, width // LANES),
        in_specs=[whole, whole, whole],
        out_specs=whole,
        out_shape=jax.ShapeDtypeStruct((n, width), BF16),
        scratch_shapes=[pltpu.VMEM((seq, LANES), F32)] * 3 + [pltpu.VMEM((3, seq, LANES), F32)] * 2,
        compiler_params=_params(("parallel", "parallel")),
    )(q, k, v)


def _deepnorm(x, f, g, b):
    y = DEEPNORM_ALPHA * x + f
    mu = jnp.mean(y, axis=1, keepdims=True)
    yc = y - mu
    var = jnp.mean(yc * yc, axis=1, keepdims=True)
    return yc * lax.rsqrt(var + LN_EPS) * g + b


def _outproj_ln_kernel(*refs, n_parts):
    part_refs = refs[:n_parts]
    w_ref, x_ref, g_ref, b_ref, o_ref, oh_ref = refs[n_parts:]
    mix = None
    start = 0
    for p_ref in part_refs:
        width = p_ref.shape[1]
        t = _dot(p_ref[...], w_ref[start:start + width, :])
        mix = t if mix is None else mix + t
        start += width
    y = _deepnorm(x_ref[...], mix, g_ref[...], b_ref[...])
    o_ref[...] = y
    oh_ref[...] = y.astype(BF16)


def _outproj_ln(parts, w, x2d, g, b):
    n, d = x2d.shape
    tm = PROJ_ROWS
    row = lambda width: pl.BlockSpec((tm, width), lambda i: (i, 0))
    vec = pl.BlockSpec((1, d), lambda i: (0, 0))
    return pl.pallas_call(
        functools.partial(_outproj_ln_kernel, n_parts=len(parts)),
        grid=(n // tm,),
        in_specs=[row(p.shape[1]) for p in parts] + [pl.BlockSpec(w.shape, lambda i: (0, 0)), row(d), vec, vec],
        out_specs=[row(d), row(d)],
        out_shape=[jax.ShapeDtypeStruct((n, d), F32), jax.ShapeDtypeStruct((n, d), BF16)],
        compiler_params=_params(("parallel",)),
    )(*parts, w, x2d, g.reshape(1, d), b.reshape(1, d))


def _silu(h):
    return h * (1.0 / (1.0 + jnp.exp(-h)))


def _swiglu_ln_kernel(xh_ref, x_ref, w1_ref, w3_ref, w2_ref, g_ref, b_ref, o_ref, acc_ref):
    f = pl.program_id(1)
    xh = xh_ref[...]
    hid = (_silu(_dot(xh, w1_ref[...])) * _dot(xh, w3_ref[...])).astype(BF16)
    y = _dot(hid, w2_ref[...])

    @pl.when(f == 0)
    def _():
        acc_ref[...] = y

    @pl.when(f > 0)
    def _():
        acc_ref[...] += y

    @pl.when(f == pl.num_programs(1) - 1)
    def _():
        o_ref[...] = _deepnorm(x_ref[...], acc_ref[...], g_ref[...], b_ref[...])


def _swiglu_ln(xh, x2d, w1, w3, w2, g, b):
    n, d = x2d.shape
    dff = w1.shape[1]
    tm = FFN_ROWS
    nf = FFN_CHUNKS
    fc = dff // nf
    row = lambda: pl.BlockSpec((tm, d), lambda i, f: (i, 0))
    vec = pl.BlockSpec((1, d), lambda i, f: (0, 0))
    mode = pl.Buffered(1) if nf == 1 else None
    return pl.pallas_call(
        _swiglu_ln_kernel,
        grid=(n // tm, nf),
        in_specs=[row(), row(),
                  pl.BlockSpec((d, fc), lambda i, f: (0, f), pipeline_mode=mode),
                  pl.BlockSpec((d, fc), lambda i, f: (0, f), pipeline_mode=mode),
                  pl.BlockSpec((fc, d), lambda i, f: (f, 0), pipeline_mode=mode),
                  vec, vec],
        out_specs=row(),
        out_shape=jax.ShapeDtypeStruct((n, d), F32),
        scratch_shapes=[pltpu.VMEM((tm, d), F32)],
        compiler_params=_params(("parallel", "arbitrary")),
    )(xh, x2d, w1, w3, w2, g.reshape(1, d), b.reshape(1, d))


def _add_ln_kernel(x_ref, f_ref, g_ref, b_ref, o_ref):
    o_ref[...] = _deepnorm(x_ref[...], f_ref[...], g_ref[...], b_ref[...])


def _add_ln(x2d, f2d, g, b):
    n, d = x2d.shape
    tm = PROJ_ROWS
    row = pl.BlockSpec((tm, d), lambda i: (i, 0))
    vec = pl.BlockSpec((1, d), lambda i: (0, 0))
    return pl.pallas_call(
        _add_ln_kernel,
        grid=(n // tm,),
        in_specs=[row, row, vec, vec],
        out_specs=row,
        out_shape=jax.ShapeDtypeStruct((n, d), F32),
        compiler_params=_params(("parallel",)),
    )(x2d, f2d, g.reshape(1, d), b.reshape(1, d))


def _router_kernel(xh_ref, rw_ref, gate_ref, lposc_ref, lpost_ref, cum_ref, carry_ref, *, tiles_per_group):
    tm = xh_ref.shape[0]
    i = pl.program_id(0)

    @pl.when(i % tiles_per_group == 0)
    def _():
        carry_ref[...] = jnp.zeros_like(carry_ref)

    lane = lax.broadcasted_iota(I32, (1, LANES), 1)
    logits = jnp.where(lane < N_EXPERTS, _dot(xh_ref[...], rw_ref[...]), NEG_INF)
    m1 = jnp.max(logits, axis=1, keepdims=True)
    i1 = jnp.min(jnp.where(logits == m1, lane, LANES), axis=1, keepdims=True)
    rest = jnp.where(lane == i1, NEG_INF, logits)
    m2 = jnp.max(rest, axis=1, keepdims=True)
    i2 = jnp.min(jnp.where(rest == m2, lane, LANES), axis=1, keepdims=True)
    e2 = jnp.exp(m2 - m1)
    g1 = 1.0 / (1.0 + e2)
    g2 = e2 / (1.0 + e2)
    sel1 = lane == i1
    sel2 = lane == i2
    routed = sel1 | sel2
    gate_ref[...] = jnp.where(sel1, g1, jnp.where(sel2, g2, 0.0))
    routedf = jnp.where(routed, 1.0, 0.0)
    earlier = (lax.broadcasted_iota(I32, (tm, tm), 1) < lax.broadcasted_iota(I32, (tm, tm), 0))
    earlier = jnp.where(earlier, 1.0, 0.0).astype(BF16)
    carry = carry_ref[...]
    lpos = jnp.where(routed, _dot(earlier, routedf.astype(BF16)) + carry, -1.0)
    lposc_ref[...] = lpos
    lpost_ref[...] = jnp.transpose(lpos)[0:N_EXPERTS, :]
    carry = carry + jnp.sum(routedf, axis=0, keepdims=True)
    carry_ref[...] = carry
    cum_ref[...] = carry.astype(I32).reshape(1, 1, LANES)


def _router(xh, rw):
    n, d = xh.shape
    tm = ROUTER_ROWS
    nt = n // tm
    row = pl.BlockSpec((tm, LANES), lambda i: (i, 0))
    return pl.pallas_call(
        functools.partial(_router_kernel, tiles_per_group=MOE_GROUP // tm),
        grid=(nt,),
        in_specs=[pl.BlockSpec((tm, d), lambda i: (i, 0)), pl.BlockSpec(rw.shape, lambda i: (0, 0))],
        out_specs=[row, row,
                   pl.BlockSpec((N_EXPERTS, tm), lambda i: (0, i)),
                   pl.BlockSpec((1, 1, LANES), lambda i: (i, 0, 0))],
        out_shape=[jax.ShapeDtypeStruct((n, LANES), F32), jax.ShapeDtypeStruct((n, LANES), F32),
                   jax.ShapeDtypeStruct((N_EXPERTS, n), F32), jax.ShapeDtypeStruct((nt, 1, LANES), I32)],
        scratch_shapes=[pltpu.VMEM((1, LANES), F32)],
        compiler_params=_params(("arbitrary",)),
    )(xh, rw)


def _moe_kernel(bnd_ref, xh_ref, lpost_ref, lposc_ref, gatec_ref, w1_ref, w3_ref, w2_ref,
                o_ref, xg_ref, yacc_ref):
    tg = xh_ref.shape[0]
    win, ch, fr, align = MOE_WIN, ROUTER_ROWS, MOE_FFN_ROWS, MOE_ROW_ALIGN
    nch = tg // ch
    g, e, f = pl.program_id(0), pl.program_id(1), pl.program_id(2)
    nf = pl.num_programs(2)

    def bound(c):
        return bnd_ref[(g * (nch + 1) + c) * N_EXPERTS + e]

    cnt = bound(nch)
    n_ffn = (cnt + fr - 1) // fr

    def tok_chunk(c):
        return pl.ds(pl.multiple_of(c * ch, ch), ch)

    def windows(c):
        r0, r1 = bound(c), bound(c + 1)
        first = (r0 // align) * align
        return first, jnp.where(r1 > r0, (r1 - first + win - 1) // win, 0)

    one_window = functools.reduce(jnp.logical_and, [windows(c)[1] <= 1 for c in range(nch)])

    @pl.when((e == 0) & (f == 0))
    def _():
        o_ref[...] = jnp.zeros_like(o_ref)

    @pl.when(f == 0)
    def _compact():
        def clear(s, carry):
            rows = pl.ds(pl.multiple_of(s * win, win), win)
            xg_ref[rows, :] = jnp.zeros((win, xg_ref.shape[1]), xg_ref.dtype)
            yacc_ref[rows, :] = jnp.zeros((win, yacc_ref.shape[1]), F32)
            return carry

        lax.fori_loop(0, (cnt + max(fr, win) + win - 1) // win, clear, 0)

        def per_chunk(c, carry):
            first, n_win = windows(c)
            lp = lpost_ref[pl.ds(e, 1), tok_chunk(c)]

            def per_window(w, carry2):
                start = pl.multiple_of(first + w * win, align)
                want = (lax.broadcasted_iota(I32, (win, 1), 0) + start).astype(F32)
                pick = jnp.where(lp == want, 1.0, 0.0).astype(BF16)
                rows = pl.ds(start, win)
                xg_ref[rows, :] += _dot(pick, xh_ref[tok_chunk(c), :]).astype(xg_ref.dtype)
                return carry2

            return lax.fori_loop(0, n_win, per_window, carry)

        @pl.when(one_window)
        def _():
            for c in range(nch):
                toks = slice(c * ch, (c + 1) * ch)
                start = pl.multiple_of(windows(c)[0], align)
                want = (lax.broadcasted_iota(I32, (win, 1), 0) + start).astype(F32)
                pick = jnp.where(lpost_ref[pl.ds(e, 1), toks] == want, 1.0, 0.0).astype(BF16)
                xg_ref[pl.ds(start, win), :] += _dot(pick, xh_ref[toks, :]).astype(xg_ref.dtype)

        @pl.when(jnp.logical_not(one_window))
        def _():
            lax.fori_loop(0, nch, per_chunk, 0)

    def ffn_rows(rows):
        xs = xg_ref[rows, :]
        hid = (_silu(_dot(xs, w1_ref[...])) * _dot(xs, w3_ref[...])).astype(BF16)
        yacc_ref[rows, :] += _dot(hid, w2_ref[...].astype(BF16))

    def ffn_pair(p, carry):
        ffn_rows(pl.ds(pl.multiple_of(p * 2 * fr, align), 2 * fr))
        return carry

    lax.fori_loop(0, n_ffn // 2, ffn_pair, 0)

    @pl.when(n_ffn % 2 == 1)
    def _():
        ffn_rows(pl.ds(pl.multiple_of((n_ffn - 1) * fr, align), fr))

    @pl.when(f == nf - 1)
    def _scatter():
        lane = lax.broadcasted_iota(I32, (1, LANES), 1)

        def per_chunk(c, carry):
            first, n_win = windows(c)
            toks = tok_chunk(c)
            lp = jnp.sum(jnp.where(lane == e, lposc_ref[toks, :], 0.0), axis=1, keepdims=True)
            gt = jnp.sum(jnp.where(lane == e, gatec_ref[toks, :], 0.0), axis=1, keepdims=True)

            def per_window(w, carry2):
                start = pl.multiple_of(first + w * win, align)
                y = yacc_ref[pl.ds(start, win), :]
                y_hi = y.astype(BF16)
                y_lo = (y - y_hi.astype(F32)).astype(BF16)
                want = (lax.broadcasted_iota(I32, (1, win), 1) + start).astype(F32)
                place = jnp.where(lp == want, 1.0, 0.0).astype(BF16)
                o_ref[toks, :] += gt * (_dot(place, y_hi) + _dot(place, y_lo))
                return carry2

            return lax.fori_loop(0, n_win, per_window, carry)

        @pl.when(one_window)
        def _():
            for c in range(nch):
                toks = slice(c * ch, (c + 1) * ch)
                start = pl.multiple_of(windows(c)[0], align)
                lp = jnp.sum(jnp.where(lane == e, lposc_ref[toks, :], 0.0), axis=1, keepdims=True)
                gt = jnp.sum(jnp.where(lane == e, gatec_ref[toks, :], 0.0), axis=1, keepdims=True)
                y = yacc_ref[pl.ds(start, win), :]
                y_hi = y.astype(BF16)
                y_lo = (y - y_hi.astype(F32)).astype(BF16)
                want = (lax.broadcasted_iota(I32, (1, win), 1) + start).astype(F32)
                place = jnp.where(lp == want, 1.0, 0.0).astype(BF16)
                o_ref[toks, :] += gt * (_dot(place, y_hi) + _dot(place, y_lo))

        @pl.when(jnp.logical_not(one_window))
        def _():
            lax.fori_loop(0, nch, per_chunk, 0)


def _chunk_bounds(cum):
    nch = MOE_GROUP // ROUTER_ROWS
    per_group = cum[:, 0, :N_EXPERTS].reshape(-1, nch, N_EXPERTS)
    return jnp.pad(per_group, ((0, 0), (1, 0), (0, 0))).reshape(-1)


def _moe(bounds, xh, lpost, lposc, gatec, w1, w3, w2, layer):
    n, d = xh.shape
    tg = MOE_GROUP
    dff = w1.shape[3]
    fc = MOE_FF_CHUNK
    cap = -(-(tg + max(MOE_FFN_ROWS, MOE_WIN)) // MOE_WIN) * MOE_WIN
    grid_spec = pltpu.PrefetchScalarGridSpec(
        num_scalar_prefetch=1,
        grid=(n // tg, N_EXPERTS, dff // fc),
        in_specs=[pl.BlockSpec((tg, d), lambda g, e, f, c: (g, 0)),
                  pl.BlockSpec((N_EXPERTS, tg), lambda g, e, f, c: (0, g)),
                  pl.BlockSpec((tg, LANES), lambda g, e, f, c: (g, 0)),
                  pl.BlockSpec((tg, LANES), lambda g, e, f, c: (g, 0)),
                  pl.BlockSpec((None, None, d, fc), lambda g, e, f, c: (layer, e, 0, f)),
                  pl.BlockSpec((None, None, d, fc), lambda g, e, f, c: (layer, e, 0, f)),
                  pl.BlockSpec((None, None, fc, d), lambda g, e, f, c: (layer, e, f, 0))],
        out_specs=pl.BlockSpec((tg, d), lambda g, e, f, c: (g, 0)),
        scratch_shapes=[pltpu.VMEM((cap, d), BF16), pltpu.VMEM((cap, d), F32)],
    )
    return pl.pallas_call(
        _moe_kernel,
        grid_spec=grid_spec,
        out_shape=jax.ShapeDtypeStruct((n, d), F32),
        compiler_params=_params(("parallel", "arbitrary", "arbitrary")),
    )(bounds, xh, lpost, lposc, gatec, w1, w3, w2)


EVEN_KB_SECTION = 2


def _even_weight_layout(w_in):
    hd = HEAD_DIM
    widths = (A_HEADS * hd, hd, hd, IDX_HEADS * IDX_DIM, IDX_DIM, IDX_HEADS, B_HEADS * hd, B_HEADS * hd, B_HEADS * hd)
    offs = [0]
    for wd in widths:
        offs.append(offs[-1] + wd)
    qa, ka, va, iq, ik, iw, qb, kb, vb = (w_in[:, offs[j]:offs[j + 1]] for j in range(9))
    iw_pad = jnp.pad(iw, ((0, 0), (0, LANES - IDX_HEADS)))
    w = jnp.concatenate([qa, qb, kb, vb, ka, ka, va, va, iq, ik, ik, ik, ik, iw_pad], axis=1)
    mixw = B_HEADS * hd
    sections = ((0, A_HEADS * hd, hd, QK_SCALE),
                (512, mixw, hd, QK_SCALE),
                (1024, mixw, hd, None),
                (1536, mixw, None, None),
                (2048, LANES, hd, None),
                (2176, LANES, None, None),
                (2304, IDX_HEADS * IDX_DIM, IDX_DIM, None),
                (2560, LANES, IDX_DIM, None),
                (2688, LANES, None, None))
    dtypes = (BF16,) * 8 + (F32,)
    return w.astype(BF16), sections, dtypes


def kernel(x, even_w_in, even_w_out, even_ln1_g, even_ln1_b, even_w1, even_w3, even_w2, even_ln2_g, even_ln2_b, odd_w_in, odd_w_out, odd_ln1_g, odd_ln1_b, odd_router, odd_w1, odd_w3, odd_w2, odd_ln2_g, odd_ln2_b):
    batch, seq, d = x.shape
    n = batch * seq
    tables = _rope_tables(seq)
    x2d = x.reshape(n, d)
    odd_w1h, odd_w3h = odd_w1.astype(BF16), odd_w3.astype(BF16)
    pending = None
    for layer in range(DEPTH):
        i = layer // 2
        if layer % 2 == 0:
            w, sections, dtypes = _even_weight_layout(even_w_in[i])
            outs = _project(x2d, w, tables, sections, dtypes, seq, mean_section=EVEN_KB_SECTION, norm_with=pending)
            qa, qb, kb, vb, kk, vv, iq, ik4, iw, kb_mean = outs[:10]
            if pending is not None:
                x2d, pending = outs[10], None
            o_a = _dsa_attention(iq, iw, ik4, qa, kk, vv, batch, seq)
            o_b = _moba_attention(qb, kb, vb, kb_mean, batch, seq)
            x2d, xh = _outproj_ln([o_a, o_b], even_w_out[i].astype(BF16), x2d, even_ln1_g[i], even_ln1_b[i])
            x2d = _swiglu_ln(xh, x2d, even_w1[i].astype(BF16), even_w3[i].astype(BF16), even_w2[i].astype(BF16),
                             even_ln2_g[i], even_ln2_b[i])
        else:
            mix = C_HEADS * HEAD_DIM
            sections = ((0, mix, HEAD_DIM, QK_SCALE), (mix, mix, HEAD_DIM, None), (2 * mix, mix, None, None))
            q, k, v = _project(x2d, odd_w_in[i].astype(BF16), tables, sections, (F32, F32, F32), seq)
            o = _dilated_attention(q, k, v, batch, seq)
            x2d, xh = _outproj_ln([o], odd_w_out[i].astype(BF16), x2d, odd_ln1_g[i], odd_ln1_b[i])
            rw = jnp.pad(odd_router[i], ((0, 0), (0, LANES - N_EXPERTS))).astype(BF16)
            gatec, lposc, lpost, cum = _router(xh, rw)
            ffn = _moe(_chunk_bounds(cum), xh, lpost, lposc, gatec, odd_w1h, odd_w3h, odd_w2, i)
            if layer + 1 < DEPTH:
                pending = (ffn, odd_ln2_g[i], odd_ln2_b[i])
            else:
                x2d = _add_ln(x2d, ffn, odd_ln2_g[i], odd_ln2_b[i])
    return x2d.reshape(batch, seq, d)
```
